```python
import math
import jax
import jax.numpy as jnp
from jax import lax
import numpy as np

D_MODEL = 1024
BATCH = 2
SEQ = 8192
DEPTH = 1

CHUNK = 64
LEFT_CHUNKS = 8
BAND = (LEFT_CHUNKS + 1) * CHUNK

HEAD_DIM = 64
N_HEADS_A = 8
N_HEADS_B = 8
WIDTH_A = N_HEADS_A * HEAD_DIM
WIDTH_B = N_HEADS_B * HEAD_DIM
MIX_WIDTH = WIDTH_A + WIDTH_B
REL_CLIP = 128
N_REL = 2 * REL_CLIP + 1
SB_BLOCK = 128

MEM_LEN = 256
N_HEADS_MEM = 4
HEAD_DIM_MEM = D_MODEL // N_HEADS_MEM

N_EXPERTS = 32
TOP_K = 4
D_FF = D_MODEL
SWIGLU_LIMIT = 7.0
SWIGLU_ALPHA = 1.702
MOE_BLOCK = 128

LN_EPS = 1e-5
RMS_EPS = 1e-6
DEEPNORM_ALPHA = (2.0 * DEPTH) ** 0.25
DEEPNORM_BETA = (8.0 * DEPTH) ** -0.25
NEG_INF = -1e30

kernel_name = "hybrid_chunked_stickbreaking_memxattn_moe"


def layer_norm(x, g, b):
    xf = x.astype(jnp.float32)
    mu = jnp.mean(xf, axis=-1, keepdims=True)
    var = jnp.mean(jnp.square(xf - mu), axis=-1, keepdims=True)
    return ((xf - mu) * lax.rsqrt(var + LN_EPS) * g + b).astype(x.dtype)


def rms_norm(x, g):
    xf = x.astype(jnp.float32)
    return (xf * lax.rsqrt(jnp.mean(jnp.square(xf), axis=-1, keepdims=True) + RMS_EPS) * g).astype(x.dtype)


def chunked_relpos_attention(q, k, v, rel_bias):
    b, h, s, dh = q.shape
    nc = s // CHUNK
    qc = q.reshape(b, h, nc, CHUNK, dh)
    pad = ((0, 0), (0, 0), (LEFT_CHUNKS * CHUNK, 0), (0, 0))
    kc = jnp.pad(k, pad).reshape(b, h, nc + LEFT_CHUNKS, CHUNK, dh)
    vc = jnp.pad(v, pad).reshape(b, h, nc + LEFT_CHUNKS, CHUNK, dh)
    k_band = jnp.concatenate([kc[:, :, o:o + nc] for o in range(LEFT_CHUNKS + 1)], axis=3)
    v_band = jnp.concatenate([vc[:, :, o:o + nc] for o in range(LEFT_CHUNKS + 1)], axis=3)
    band_pos = jnp.arange(BAND)
    key_chunk = jnp.arange(nc)[:, None] - LEFT_CHUNKS + band_pos[None, :] // CHUNK
    valid = key_chunk >= 0
    rel = LEFT_CHUNKS * CHUNK + jnp.arange(CHUNK)[:, None] - band_pos[None, :]
    rel_idx = jnp.clip(rel, -REL_CLIP, REL_CLIP) + REL_CLIP
    bias = rel_bias[:, rel_idx].astype(jnp.float32)
    scores = jnp.einsum('bhcqd,bhckd->bhcqk', qc, k_band).astype(jnp.float32) * (dh ** -0.5)
    scores = scores + bias[None, :, None]
    scores = jnp.where(valid[None, None, :, None, :], scores, NEG_INF)
    p = jax.nn.softmax(scores, axis=-1).astype(v.dtype)
    out = jnp.einsum('bhcqk,bhckd->bhcqd', p, v_band)
    return out.reshape(b, h, s, dh)


def stick_breaking_attention(q, k, v):
    b, h, s, dh = q.shape
    nb = s // SB_BLOCK
    scale = dh ** -0.5
    kf = k.astype(jnp.float32)
    vf = v.astype(jnp.float32)
    key_pos = jnp.arange(s)
    qb = q.reshape(b, h, nb, SB_BLOCK, dh).transpose(2, 0, 1, 3, 4)

    def block(args):
        q_blk, blk_idx = args
        q_pos = blk_idx * SB_BLOCK + jnp.arange(SB_BLOCK)
        causal = key_pos[None, :] < q_pos[:, None]
        z = jnp.einsum('bhqd,bhkd->bhqk', q_blk.astype(jnp.float32), kf) * scale
        log_beta = jax.nn.log_sigmoid(z)
        log_one_minus = jnp.where(causal, jax.nn.log_sigmoid(-z), 0.0)
        between = lax.cumsum(log_one_minus, axis=3, reverse=True) - log_one_minus
        weights = jnp.where(causal, jnp.exp(log_beta + between), 0.0)
        return jnp.einsum('bhqk,bhkd->bhqd', weights, vf)

    out = lax.map(block, (qb, jnp.arange(nb)))
    return out.transpose(1, 2, 0, 3, 4).reshape(b, h, s, dh).astype(q.dtype)


def hybrid_mixer(h, w_in, rel_bias, g_group_a, g_group_b, w_out):
    b, s, _ = h.shape
    proj = h @ w_in
    o1 = WIDTH_A
    o2 = 2 * WIDTH_A
    o3 = 3 * WIDTH_A
    o4 = o3 + WIDTH_B
    o5 = o3 + 2 * WIDTH_B
    qa, ka, va, qb, kb, vb = jnp.split(proj, [o1, o2, o3, o4, o5], axis=-1)

    def heads(t, n):
        return t.reshape(b, s, n, HEAD_DIM).transpose(0, 2, 1, 3)

    def merge(t):
        return t.transpose(0, 2, 1, 3).reshape(b, s, -1)

    out_a = merge(chunked_relpos_attention(heads(qa, N_HEADS_A), heads(ka, N_HEADS_A), heads(va, N_HEADS_A), rel_bias))
    out_b = merge(stick_breaking_attention(heads(qb, N_HEADS_B), heads(kb, N_HEADS_B), heads(vb, N_HEADS_B)))
    out = jnp.concatenate([rms_norm(out_a, g_group_a), rms_norm(out_b, g_group_b)], axis=-1)
    return out @ w_out


def memory_cross_attention(h, mem, w_q_mem, w_kv_mem, w_o_mem):
    b, s, d = h.shape
    m = mem.shape[1]
    q = (h @ w_q_mem).reshape(b, s, N_HEADS_MEM, HEAD_DIM_MEM)
    k, v = jnp.split(mem @ w_kv_mem, 2, axis=-1)
    k = k.reshape(b, m, N_HEADS_MEM, HEAD_DIM_MEM)
    v = v.reshape(b, m, N_HEADS_MEM, HEAD_DIM_MEM)
    scores = jnp.einsum('bqhd,bkhd->bhqk', q, k).astype(jnp.float32) * (HEAD_DIM_MEM ** -0.5)
    p = jax.nn.softmax(scores, axis=-1).astype(v.dtype)
    o = jnp.einsum('bhqk,bkhd->bqhd', p, v).reshape(b, s, d)
    return o @ w_o_mem


def moe_ffn(h, w_router, b_router, w_gate_up, b_gate_up, w_down, b_down):
    b, s, d = h.shape
    n = b * s
    x = h.reshape(n, d)
    logits = (x @ w_router + b_router).astype(jnp.float32)
    top_logits, top_idx = lax.top_k(logits, TOP_K)
    gates = jax.nn.softmax(top_logits, axis=-1)
    n_assign = n * TOP_K
    flat_expert = top_idx.reshape(-1)
    flat_token = jnp.repeat(jnp.arange(n, dtype=jnp.int32), TOP_K)
    flat_gate = gates.reshape(-1)
    order = jnp.argsort(flat_expert)
    sorted_expert = flat_expert[order]
    counts = jnp.bincount(flat_expert, length=N_EXPERTS)
    padded = (counts + MOE_BLOCK - 1) // MOE_BLOCK * MOE_BLOCK
    start = jnp.cumsum(counts) - counts
    pad_end = jnp.cumsum(padded)
    pad_start = pad_end - padded
    dest = pad_start[sorted_expert] + jnp.arange(n_assign) - start[sorted_expert]
    n_rows = n_assign + N_EXPERTS * MOE_BLOCK
    n_blocks = n_rows // MOE_BLOCK
    row_token = jnp.full((n_rows,), n, jnp.int32).at[dest].set(flat_token[order])
    row_gate = jnp.zeros((n_rows,), jnp.float32).at[dest].set(flat_gate[order])
    block_expert = jnp.minimum(
        jnp.searchsorted(pad_end, jnp.arange(n_blocks) * MOE_BLOCK, side='right'), N_EXPERTS - 1)
    x_pad = jnp.concatenate([x, jnp.zeros((1, d), x.dtype)], axis=0)
    x_rows = x_pad[row_token].reshape(n_blocks, MOE_BLOCK, d)

    def expert_block(args):
        xb, e = args
        gu = xb @ w_gate_up[e] + b_gate_up[e]
        gate, up = jnp.split(gu, 2, axis=-1)
        gate = jnp.minimum(gate, SWIGLU_LIMIT)
        up = jnp.clip(up, -SWIGLU_LIMIT, SWIGLU_LIMIT)
        glu = gate * jax.nn.sigmoid(gate * SWIGLU_ALPHA)
        return ((up + 1.0) * glu) @ w_down[e] + b_down[e]

    y_rows = lax.map(expert_block, (x_rows, block_expert)).reshape(n_rows, d)
    y = jax.ops.segment_sum(y_rows * row_gate[:, None].astype(y_rows.dtype), row_token, num_segments=n + 1)[:n]
    return y.reshape(b, s, d)


def setup_inputs(seed: int = 0) -> dict:
    key = jax.random.key(seed)
    ks = jax.random.split(key, 20)

    def normal(k, shape, scale):
        return jax.random.normal(k, shape, jnp.float32) * scale

    beta = DEEPNORM_BETA
    x = normal(ks[0], (BATCH, SEQ, D_MODEL), 1.0)
    mem = normal(ks[1], (BATCH, MEM_LEN, D_MODEL), 1.0)
    col_scale = jnp.concatenate([
        jnp.ones((2 * WIDTH_A,), jnp.float32), jnp.full((WIDTH_A,), beta, jnp.float32),
        jnp.ones((2 * WIDTH_B,), jnp.float32), jnp.full((WIDTH_B,), beta, jnp.float32)])
    w_in = normal(ks[2], (DEPTH, D_MODEL, 3 * MIX_WIDTH), D_MODEL ** -0.5) * col_scale
    rel_bias = normal(ks[3], (DEPTH, N_HEADS_A, N_REL), 0.1)
    g_group_a = 1.0 + normal(ks[4], (DEPTH, WIDTH_A), 0.01)
    g_group_b = 1.0 + normal(ks[5], (DEPTH, WIDTH_B), 0.01)
    w_out = normal(ks[6], (DEPTH, MIX_WIDTH, D_MODEL), MIX_WIDTH ** -0.5 * beta)
    w_q_mem = normal(ks[7], (DEPTH, D_MODEL, D_MODEL), D_MODEL ** -0.5)
    kv_scale = jnp.concatenate([jnp.ones((D_MODEL,), jnp.float32), jnp.full((D_MODEL,), beta, jnp.float32)])
    w_kv_mem = normal(ks[8], (DEPTH, D_MODEL, 2 * D_MODEL), D_MODEL ** -0.5) * kv_scale
    w_o_mem = normal(ks[9], (DEPTH, D_MODEL, D_MODEL), D_MODEL ** -0.5 * beta)
    w_router = normal(ks[10], (DEPTH, D_MODEL, N_EXPERTS), D_MODEL ** -0.5)
    b_router = normal(ks[11], (DEPTH, N_EXPERTS), 0.01)
    w_gate_up = normal(ks[12], (DEPTH, N_EXPERTS, D_MODEL, 2 * D_FF), D_MODEL ** -0.5)
    b_gate_up = normal(ks[13], (DEPTH, N_EXPERTS, 2 * D_FF), 0.01)
    w_down = normal(ks[14], (DEPTH, N_EXPERTS, D_FF, D_MODEL), D_FF ** -0.5 * beta)
    b_down = normal(ks[15], (DEPTH, N_EXPERTS, D_MODEL), 0.01)
    ln_g = 1.0 + normal(ks[16], (DEPTH, 3, D_MODEL), 0.01)
    ln_b = normal(ks[17], (DEPTH, 3, D_MODEL), 0.01)
    return {"x": x, "mem": mem, "w_in": w_in, "rel_bias": rel_bias, "g_group_a": g_group_a,
            "g_group_b": g_group_b, "w_out": w_out, "w_q_mem": w_q_mem, "w_kv_mem": w_kv_mem,
            "w_o_mem": w_o_mem, "w_router": w_router, "b_router": b_router, "w_gate_up": w_gate_up,
            "b_gate_up": b_gate_up, "w_down": w_down, "b_down": b_down, "ln_g": ln_g, "ln_b": ln_b}


def reference(x, mem, w_in, rel_bias, g_group_a, g_group_b, w_out, w_q_mem, w_kv_mem, w_o_mem,
              w_router, b_router, w_gate_up, b_gate_up, w_down, b_down, ln_g, ln_b):
    for l in range(DEPTH):
        mix = hybrid_mixer(x, w_in[l], rel_bias[l], g_group_a[l], g_group_b[l], w_out[l])
        x = layer_norm(DEEPNORM_ALPHA * x + mix, ln_g[l, 0], ln_b[l, 0])
        xa = memory_cross_attention(x, mem, w_q_mem[l], w_kv_mem[l], w_o_mem[l])
        x = layer_norm(DEEPNORM_ALPHA * x + xa, ln_g[l, 1], ln_b[l, 1])
        ff = moe_ffn(x, w_router[l], b_router[l], w_gate_up[l], b_gate_up[l], w_down[l], b_down[l])
        x = layer_norm(DEEPNORM_ALPHA * x + ff, ln_g[l, 2], ln_b[l, 2])
    return x
```

```python
import functools

import jax
import jax.numpy as jnp
from jax import lax
from jax.experimental import pallas as pl
from jax.experimental.pallas import tpu as pltpu

D_MODEL = 1024
CHUNK = 64
LEFT_CHUNKS = 8
LEFT = LEFT_CHUNKS * CHUNK
HEAD_DIM = 64
N_HEADS_A = 8
N_HEADS_B = 8
WIDTH_A = N_HEADS_A * HEAD_DIM
WIDTH_B = N_HEADS_B * HEAD_DIM
MIX_WIDTH = WIDTH_A + WIDTH_B
REL_CLIP = 128
N_HEADS_MEM = 4
HEAD_DIM_MEM = D_MODEL // N_HEADS_MEM
N_EXPERTS = 32
TOP_K = 4
D_FF = D_MODEL
SWIGLU_LIMIT = 7.0
SWIGLU_ALPHA = 1.702
LN_EPS = 1e-5
RMS_EPS = 1e-6
DEEPNORM_ALPHA = 2.0 ** 0.25
NEG_INF = -1e30

LANES = 128
VMEM_LIMIT = 48 * 1024 * 1024

TQ_A = 512
TB_SB = 256
TM_TOK = 512
TM_MOE = 256
TM_COMB = 256

F32 = jnp.float32
BF16 = jnp.bfloat16
_NT = (((1,), (1,)), ((), ()))


def _params(n_axes):
    return pltpu.CompilerParams(dimension_semantics=("arbitrary",) * n_axes,
                                vmem_limit_bytes=VMEM_LIMIT)


def _layer_norm(r, g, b):
    mu = jnp.mean(r, axis=-1, keepdims=True)
    d = r - mu
    var = jnp.mean(d * d, axis=-1, keepdims=True)
    return d * lax.rsqrt(var + LN_EPS) * g + b


def _matmul_kernel(x_ref, w_ref, o_ref):
    o_ref[...] = jnp.dot(x_ref[...].astype(BF16), w_ref[...],
                         preferred_element_type=F32).astype(o_ref.dtype)


def _matmul(x, w, out_dtype, tm, tn):
    m, k = x.shape
    n = w.shape[1]
    return pl.pallas_call(
        _matmul_kernel,
        grid=(m // tm, n // tn),
        in_specs=[pl.BlockSpec((tm, k), lambda i, j: (i, 0)),
                  pl.BlockSpec((k, tn), lambda i, j: (0, j))],
        out_specs=pl.BlockSpec((tm, tn), lambda i, j: (i, j)),
        out_shape=jax.ShapeDtypeStruct((m, n), out_dtype),
        compiler_params=_params(2),
        name="matmul",
    )(x, w)


def _chunk_attn_kernel(q_ref, kp_ref, kc_ref, vp_ref, vc_ref, bias_ref, o_ref):
    i = pl.program_id(2)
    w = TQ_A + LEFT
    col = lax.broadcasted_iota(jnp.int32, (TQ_A, w), 1)
    first_valid = jnp.where(i == 0, LEFT, 0)
    for h in range(2):
        sl = slice(h * HEAD_DIM, (h + 1) * HEAD_DIM)
        q = q_ref[:, sl] * (HEAD_DIM ** -0.5)
        k = jnp.concatenate([kp_ref[:, sl], kc_ref[:, sl]], axis=0)
        v = jnp.concatenate([vp_ref[:, sl], vc_ref[:, sl]], axis=0)
        s = lax.dot_general(q, k, _NT, preferred_element_type=F32) + bias_ref[h]
        s = jnp.where(col >= first_valid, s, NEG_INF)
        m = jnp.max(s, axis=-1, keepdims=True)
        p = jnp.exp(s - m)
        l = jnp.sum(p, axis=-1, keepdims=True)
        o = jnp.dot(p.astype(BF16), v, preferred_element_type=F32) / l
        o_ref[:, sl] = o.astype(o_ref.dtype)


def _chunk_bias(rel_bias):
    w = TQ_A + LEFT
    qi = jnp.arange(TQ_A)[:, None]
    kj = jnp.arange(w)[None, :]
    rel = qi + LEFT - kj
    qc = qi // CHUNK
    kc = kj // CHUNK
    band = (kc >= qc) & (kc <= qc + LEFT_CHUNKS)
    idx = jnp.clip(rel, -REL_CLIP, REL_CLIP) + REL_CLIP
    return jnp.where(band[None], rel_bias[:, idx].astype(F32), NEG_INF)


def _chunk_attention(proj, bias, b, s):
    nq = s // TQ_A
    pairs = WIDTH_A // LANES
    blk = lambda off, prev: pl.BlockSpec(
        (None, TQ_A, LANES),
        (lambda bi, p, i: (bi, jnp.maximum(i - 1, 0), off + p)) if prev
        else (lambda bi, p, i: (bi, i, off + p)))
    return pl.pallas_call(
        _chunk_attn_kernel,
        grid=(b, pairs, nq),
        in_specs=[blk(0, False),
                  blk(pairs, True), blk(pairs, False),
                  blk(2 * pairs, True), blk(2 * pairs, False),
                  pl.BlockSpec((2, TQ_A, TQ_A + LEFT), lambda bi, p, i: (p, 0, 0))],
        out_specs=pl.BlockSpec((None, TQ_A, LANES), lambda bi, p, i: (bi, i, p)),
        out_shape=jax.ShapeDtypeStruct((b, s, WIDTH_A), BF16),
        compiler_params=_params(3),
        name="chunk_attn",
    )(proj, proj, proj, proj, proj, bias)


def _sb_attn_kernel(q_ref, k_ref, v_ref, o_ref):
    i = pl.program_id(2)
    t = TB_SB
    row = lax.broadcasted_iota(jnp.int32, (t, t), 0)
    col = lax.broadcasted_iota(jnp.int32, (t, t), 1)
    tri = jnp.where(row >= col, 1.0, 0.0).astype(BF16)
    causal = col < row

    for h in range(2):
        sl = slice(h * HEAD_DIM, (h + 1) * HEAD_DIM)
        q = q_ref[:, sl] * (HEAD_DIM ** -0.5)

        def step(j, carry, masked):
            later, acc = carry
            start = pl.multiple_of(j * t, t)
            k = k_ref[pl.ds(start, t), sl]
            v = v_ref[pl.ds(start, t), sl]
            z = lax.dot_general(q, k, _NT, preferred_element_type=F32)
            sp = jnp.maximum(z, 0.0) + jnp.log(1.0 + jnp.exp(-jnp.abs(z)))
            if masked:
                sp = jnp.where(causal, sp, 0.0)
            csum = jnp.dot(sp.astype(BF16), tri, preferred_element_type=F32) + later
            wgt = jnp.exp(z - csum)
            if masked:
                wgt = jnp.where(causal, wgt, 0.0)
            acc = acc + jnp.dot(wgt.astype(BF16), v, preferred_element_type=F32)
            later = later + jnp.sum(sp, axis=-1, keepdims=True)
            return later, acc

        carry = (jnp.zeros((t, 1), F32), jnp.zeros((t, HEAD_DIM), F32))
        carry = step(i, carry, True)
        carry = lax.fori_loop(0, i, lambda n, c: step(i - 1 - n, c, False), carry)
        o_ref[:, sl] = carry[1].astype(o_ref.dtype)


def _sb_attention(proj, b, s):
    nq = s // TB_SB
    pairs = WIDTH_B // LANES
    base = 3 * WIDTH_A // LANES
    return pl.pallas_call(
        _sb_attn_kernel,
        grid=(b, pairs, nq),
        in_specs=[pl.BlockSpec((None, TB_SB, LANES), lambda bi, p, i: (bi, i, base + p)),
                  pl.BlockSpec((None, s, LANES), lambda bi, p, i: (bi, 0, base + pairs + p)),
                  pl.BlockSpec((None, s, LANES), lambda bi, p, i: (bi, 0, base + 2 * pairs + p))],
        out_specs=pl.BlockSpec((None, TB_SB, LANES), lambda bi, p, i: (bi, i, p)),
        out_shape=jax.ShapeDtypeStruct((b, s, WIDTH_B), BF16),
        compiler_params=_params(3),
        name="sb_attn",
    )(proj, proj, proj)


def _mixer_out_kernel(oa_ref, ob_ref, x_ref, ga_ref, gb_ref, wout_ref, lng_ref, lnb_ref, wq_ref,
                      x1_ref, qm_ref):
    def rms(ref, g_ref):
        a = ref[...].astype(F32)
        return (a * lax.rsqrt(jnp.mean(a * a, axis=-1, keepdims=True) + RMS_EPS) * g_ref[...]).astype(BF16)

    y = jnp.dot(rms(oa_ref, ga_ref), wout_ref[:WIDTH_A, :], preferred_element_type=F32)
    y = y + jnp.dot(rms(ob_ref, gb_ref), wout_ref[WIDTH_A:, :], preferred_element_type=F32)
    x1 = _layer_norm(DEEPNORM_ALPHA * x_ref[...] + y, lng_ref[...], lnb_ref[...])
    x1_ref[...] = x1
    qm_ref[...] = jnp.dot(x1.astype(BF16), wq_ref[...], preferred_element_type=F32).astype(BF16)


def _mixer_out(out_a, out_b, x, g_a, g_b, w_out, ln_g, ln_b, w_q):
    n = x.shape[0]
    tm = TM_TOK
    row = lambda width: pl.BlockSpec((tm, width), lambda i: (i, 0))
    full = lambda r, c: pl.BlockSpec((r, c), lambda i: (0, 0))
    return pl.pallas_call(
        _mixer_out_kernel,
        grid=(n // tm,),
        in_specs=[row(WIDTH_A), row(WIDTH_B), row(D_MODEL), full(1, WIDTH_A), full(1, WIDTH_B),
                  full(MIX_WIDTH, D_MODEL), full(1, D_MODEL), full(1, D_MODEL), full(D_MODEL, D_MODEL)],
        out_specs=[row(D_MODEL), row(D_MODEL)],
        out_shape=[jax.ShapeDtypeStruct((n, D_MODEL), F32), jax.ShapeDtypeStruct((n, D_MODEL), BF16)],
        compiler_params=_params(1),
        name="mixer_out",
    )(out_a, out_b, x, g_a, g_b, w_out, ln_g, ln_b, w_q)


def _mem_attn_kernel(q_ref, kv_ref, x1_ref, wo_ref, lng_ref, lnb_ref, wr_ref, br_ref,
                     x2_ref, idx_ref, gate_ref):
    heads = []
    for h in range(N_HEADS_MEM):
        sl = slice(h * HEAD_DIM_MEM, (h + 1) * HEAD_DIM_MEM)
        q = q_ref[:, sl] * (HEAD_DIM_MEM ** -0.5)
        k = kv_ref[:, sl]
        v = kv_ref[:, D_MODEL + h * HEAD_DIM_MEM:D_MODEL + (h + 1) * HEAD_DIM_MEM]
        s = lax.dot_general(q, k, _NT, preferred_element_type=F32)
        m = jnp.max(s, axis=-1, keepdims=True)
        p = jnp.exp(s - m)
        l = jnp.sum(p, axis=-1, keepdims=True)
        heads.append((jnp.dot(p.astype(BF16), v, preferred_element_type=F32) / l).astype(BF16))
    o = jnp.concatenate(heads, axis=-1)
    y = jnp.dot(o, wo_ref[...], preferred_element_type=F32)
    x2 = _layer_norm(DEEPNORM_ALPHA * x1_ref[...] + y, lng_ref[...], lnb_ref[...])
    x2_ref[...] = x2

    logits = jnp.dot(x2.astype(BF16), wr_ref[...], preferred_element_type=F32) + br_ref[...]
    tm = logits.shape[0]
    e_iota = lax.broadcasted_iota(jnp.int32, (tm, N_EXPERTS), 1)
    lane = lax.broadcasted_iota(jnp.int32, (tm, LANES), 1)
    idx_out = jnp.zeros((tm, LANES), jnp.int32)
    val_out = jnp.zeros((tm, LANES), F32)
    top = None
    denom = jnp.zeros((tm, 1), F32)
    for kk in range(TOP_K):
        m = jnp.max(logits, axis=-1, keepdims=True)
        sel = jnp.min(jnp.where(logits == m, e_iota, N_EXPERTS), axis=-1, keepdims=True)
        if top is None:
            top = m
        e = jnp.exp(m - top)
        denom = denom + e
        idx_out = jnp.where(lane == kk, sel, idx_out)
        val_out = jnp.where(lane == kk, e, val_out)
        logits = jnp.where(e_iota == sel, -jnp.inf, logits)
    idx_ref[...] = idx_out
    gate_ref[...] = val_out / denom


def _mem_attn(qm, kv, x1, w_o, ln_g, ln_b, w_r, b_r, b, s):
    tm = TM_TOK
    nt = s // tm
    mem_len = kv.shape[1]
    row = lambda width: pl.BlockSpec((tm, width), lambda bi, i: (bi * nt + i, 0))
    full = lambda r, c: pl.BlockSpec((r, c), lambda bi, i: (0, 0))
    n = b * s
    return pl.pallas_call(
        _mem_attn_kernel,
        grid=(b, nt),
        in_specs=[row(D_MODEL), pl.BlockSpec((None, mem_len, 2 * D_MODEL), lambda bi, i: (bi, 0, 0)),
                  row(D_MODEL), full(D_MODEL, D_MODEL), full(1, D_MODEL), full(1, D_MODEL),
                  full(D_MODEL, N_EXPERTS), full(1, N_EXPERTS)],
        out_specs=[row(D_MODEL), row(LANES), row(LANES)],
        out_shape=[jax.ShapeDtypeStruct((n, D_MODEL), F32),
                   jax.ShapeDtypeStruct((n, LANES), jnp.int32),
                   jax.ShapeDtypeStruct((n, LANES), F32)],
        compiler_params=_params(2),
        name="mem_attn",
    )(qm, kv, x1, w_o, ln_g, ln_b, w_r, b_r)


def _gather_rows(idx_ref, count, src_hbm, dst_ref, sem):
    def body(r, carry):
        pltpu.make_async_copy(src_hbm.at[pl.ds(idx_ref[0, 0, r], 1)], dst_ref.at[pl.ds(r, 1)], sem).start()
        return carry
    lax.fori_loop(0, count, body, 0, unroll=8)


def _wait_rows(count, src_hbm, dst_ref, sem):
    pltpu.make_async_copy(src_hbm.at[pl.ds(0, count)], dst_ref, sem).wait()


def _moe_kernel(bexp_ref, nused_ref, tok_ref, tok_next_ref, x_hbm, wgu_ref, bgu_ref, wd_ref, bd_ref,
                y_ref, xbuf, sem):
    i = pl.program_id(0)
    nb = pl.num_programs(0)
    slot = i % 2

    @pl.when(i == 0)
    def _():
        _gather_rows(tok_ref, TM_MOE, x_hbm, xbuf.at[0], sem.at[0])

    @pl.when(i + 1 < nb)
    def _():
        _gather_rows(tok_next_ref, TM_MOE, x_hbm, xbuf.at[1 - slot], sem.at[1 - slot])

    _wait_rows(TM_MOE, x_hbm, xbuf.at[slot], sem.at[slot])

    @pl.when(i < nused_ref[0])
    def _():
        x = xbuf[slot].astype(BF16)
        gu = jnp.dot(x, wgu_ref[0], preferred_element_type=F32) + bgu_ref[0]
        gate = jnp.minimum(gu[:, :D_FF], SWIGLU_LIMIT)
        up = jnp.clip(gu[:, D_FF:], -SWIGLU_LIMIT, SWIGLU_LIMIT)
        glu = gate * jax.nn.sigmoid(gate * SWIGLU_ALPHA)
        hid = ((up + 1.0) * glu).astype(BF16)
        y_ref[...] = jnp.dot(hid, wd_ref[0], preferred_element_type=F32) + bd_ref[0]

    @pl.when(i >= nused_ref[0])
    def _():
        y_ref[...] = jnp.zeros_like(y_ref)


def _moe_experts(block_expert, n_used, row_token, x2, w_gu, b_gu, w_d, b_d):
    nb = block_expert.shape[0]
    tok3 = row_token.reshape(nb, 1, TM_MOE)
    smem_blk = lambda f: pl.BlockSpec((1, 1, TM_MOE), f, memory_space=pltpu.SMEM)
    grid_spec = pltpu.PrefetchScalarGridSpec(
        num_scalar_prefetch=2,
        grid=(nb,),
        in_specs=[smem_blk(lambda i, be, nu: (i, 0, 0)),
                  smem_blk(lambda i, be, nu: (jnp.minimum(i + 1, nb - 1), 0, 0)),
                  pl.BlockSpec(memory_space=pl.ANY),
                  pl.BlockSpec((1, D_MODEL, 2 * D_FF), lambda i, be, nu: (be[i], 0, 0)),
                  pl.BlockSpec((1, 1, 2 * D_FF), lambda i, be, nu: (be[i], 0, 0)),
                  pl.BlockSpec((1, D_FF, D_MODEL), lambda i, be, nu: (be[i], 0, 0)),
                  pl.BlockSpec((1, 1, D_MODEL), lambda i, be, nu: (be[i], 0, 0))],
        out_specs=pl.BlockSpec((TM_MOE, D_MODEL), lambda i, be, nu: (i, 0)),
        scratch_shapes=[pltpu.VMEM((2, TM_MOE, D_MODEL), F32), pltpu.SemaphoreType.DMA((2,))],
    )
    return pl.pallas_call(
        _moe_kernel,
        grid_spec=grid_spec,
        out_shape=jax.ShapeDtypeStruct((nb * TM_MOE, D_MODEL), F32),
        compiler_params=_params(1),
        name="moe_experts",
    )(block_expert, n_used, tok3, tok3, x2, w_gu, b_gu, w_d, b_d)


def _combine_kernel(idx_ref, idx_next_ref, y_hbm, x2_ref, gate_ref, lng_ref, lnb_ref, o_ref, ybuf, sem):
    i = pl.program_id(0)
    nt = pl.num_programs(0)
    slot = i % 2
    rows = TOP_K * TM_COMB

    @pl.when(i == 0)
    def _():
        _gather_rows(idx_ref, rows, y_hbm, ybuf.at[0], sem.at[0])

    @pl.when(i + 1 < nt)
    def _():
        _gather_rows(idx_next_ref, rows, y_hbm, ybuf.at[1 - slot], sem.at[1 - slot])

    _wait_rows(rows, y_hbm, ybuf.at[slot], sem.at[slot])

    g = gate_ref[...]
    ff = jnp.zeros((TM_COMB, D_MODEL), F32)
    for kk in range(TOP_K):
        ff = ff + g[:, kk:kk + 1] * ybuf[slot, kk * TM_COMB:(kk + 1) * TM_COMB, :]
    o_ref[...] = _layer_norm(DEEPNORM_ALPHA * x2_ref[...] + ff, lng_ref[...], lnb_ref[...])


def _combine(dest, y_rows, x2, gates, ln_g, ln_b):
    n = x2.shape[0]
    nt = n // TM_COMB
    rows = TOP_K * TM_COMB
    idx3 = dest.reshape(nt, TM_COMB, TOP_K).transpose(0, 2, 1).reshape(nt, 1, rows)
    smem_blk = lambda f: pl.BlockSpec((1, 1, rows), f, memory_space=pltpu.SMEM)
    row = lambda width: pl.BlockSpec((TM_COMB, width), lambda i: (i, 0))
    full = lambda r, c: pl.BlockSpec((r, c), lambda i: (0, 0))
    return pl.pallas_call(
        _combine_kernel,
        grid=(nt,),
        in_specs=[smem_blk(lambda i: (i, 0, 0)),
                  smem_blk(lambda i: (jnp.minimum(i + 1, nt - 1), 0, 0)),
                  pl.BlockSpec(memory_space=pl.ANY),
                  row(D_MODEL), row(LANES), full(1, D_MODEL), full(1, D_MODEL)],
        out_specs=row(D_MODEL),
        out_shape=jax.ShapeDtypeStruct((n, D_MODEL), F32),
        scratch_shapes=[pltpu.VMEM((2, rows, D_MODEL), F32), pltpu.SemaphoreType.DMA((2,))],
        compiler_params=_params(1),
        name="combine",
    )(idx3, idx3, y_rows, x2, gates, ln_g, ln_b)


def _dispatch_plan(top_idx, n):
    n_assign = n * TOP_K
    nb = n_assign // TM_MOE + N_EXPERTS
    flat_e = top_idx.reshape(-1)
    onehot = (flat_e[:, None] == jnp.arange(N_EXPERTS, dtype=jnp.int32)[None, :]).astype(jnp.int32)
    csum = jnp.cumsum(onehot, axis=0)
    counts = csum[-1]
    padded = (counts + TM_MOE - 1) // TM_MOE * TM_MOE
    pad_end = jnp.cumsum(padded)
    pad_start = pad_end - padded
    dest = jnp.sum(onehot * (csum - 1 + pad_start[None, :]), axis=1).astype(jnp.int32)
    row_token = jnp.zeros((nb * TM_MOE,), jnp.int32).at[dest].set(
        jnp.arange(n_assign, dtype=jnp.int32) // TOP_K)
    block_expert = jnp.minimum(
        jnp.searchsorted(pad_end, jnp.arange(nb, dtype=jnp.int32) * TM_MOE, side='right'),
        N_EXPERTS - 1).astype(jnp.int32)
    n_used = (pad_end[-1:] // TM_MOE).astype(jnp.int32)
    return dest, row_token, block_expert, n_used


def kernel(x, mem, w_in, rel_bias, g_group_a, g_group_b, w_out, w_q_mem, w_kv_mem, w_o_mem, w_router, b_router, w_gate_up, b_gate_up, w_down, b_down, ln_g, ln_b):
    b, s, d = x.shape
    n = b * s
    depth = w_in.shape[0]
    xf = x.reshape(n, d)
    for l in range(depth):
        proj = _matmul(xf, w_in[l].astype(BF16), BF16, TM_TOK, 1024).reshape(b, s, 3 * MIX_WIDTH)
        out_a = _chunk_attention(proj, _chunk_bias(rel_bias[l]), b, s).reshape(n, WIDTH_A)
        out_b = _sb_attention(proj, b, s).reshape(n, WIDTH_B)
        x1, qm = _mixer_out(out_a, out_b, xf, g_group_a[l][None], g_group_b[l][None],
                            w_out[l].astype(BF16), ln_g[l, 0][None], ln_b[l, 0][None],
                            w_q_mem[l].astype(BF16))
        mem_len = mem.shape[1]
        kv = _matmul(mem.reshape(b * mem_len, d), w_kv_mem[l].astype(BF16), BF16, b * mem_len, 1024)
        x2, top_idx, gates = _mem_attn(qm, kv.reshape(b, mem_len, 2 * d), x1, w_o_mem[l].astype(BF16),
                                       ln_g[l, 1][None], ln_b[l, 1][None],
                                       w_router[l].astype(BF16), b_router[l][None], b, s)
        dest, row_token, block_expert, n_used = _dispatch_plan(top_idx[:, :TOP_K], n)
        y_rows = _moe_experts(block_expert, n_used, row_token, x2,
                              w_gate_up[l].astype(BF16), b_gate_up[l][:, None, :],
                              w_down[l].astype(BF16), b_down[l][:, None, :])
        xf = _combine(dest, y_rows, x2, gates, ln_g[l, 2][None], ln_b[l, 2][None])
    return xf.reshape(b, s, d)
```

```python
import functools

import jax
import jax.numpy as jnp
from jax import lax
from jax.experimental import pallas as pl
from jax.experimental.pallas import tpu as pltpu

D_MODEL = 1024
CHUNK = 64
LEFT_CHUNKS = 8
LEFT = LEFT_CHUNKS * CHUNK
HEAD_DIM = 64
N_HEADS_A = 8
N_HEADS_B = 8
WIDTH_A = N_HEADS_A * HEAD_DIM
WIDTH_B = N_HEADS_B * HEAD_DIM
MIX_WIDTH = WIDTH_A + WIDTH_B
REL_CLIP = 128
N_HEADS_MEM = 4
HEAD_DIM_MEM = D_MODEL // N_HEADS_MEM
N_EXPERTS = 32
TOP_K = 4
D_FF = D_MODEL
SWIGLU_LIMIT = 7.0
SWIGLU_ALPHA = 1.702
LN_EPS = 1e-5
RMS_EPS = 1e-6
DEEPNORM_ALPHA = 2.0 ** 0.25
NEG_INF = -1e30

LANES = 128
VMEM_LIMIT = 48 * 1024 * 1024

TQ_A = 512
TB_SB = 256
TM_TOK = 512
TM_MOE = 256
TM_COMB = 256

F32 = jnp.float32
BF16 = jnp.bfloat16
_NT = (((1,), (1,)), ((), ()))


def _params(n_axes):
    return pltpu.CompilerParams(dimension_semantics=("arbitrary",) * n_axes,
                                vmem_limit_bytes=VMEM_LIMIT)


def _layer_norm(r, g, b):
    mu = jnp.mean(r, axis=-1, keepdims=True)
    d = r - mu
    var = jnp.mean(d * d, axis=-1, keepdims=True)
    return d * lax.rsqrt(var + LN_EPS) * g + b


def _matmul_kernel(x_ref, w_ref, o_ref):
    o_ref[...] = jnp.dot(x_ref[...].astype(BF16), w_ref[...],
                         preferred_element_type=F32).astype(o_ref.dtype)


def _matmul(x, w, out_dtype, tm, tn):
    m, k = x.shape
    n = w.shape[1]
    return pl.pallas_call(
        _matmul_kernel,
        grid=(m // tm, n // tn),
        in_specs=[pl.BlockSpec((tm, k), lambda i, j: (i, 0)),
                  pl.BlockSpec((k, tn), lambda i, j: (0, j))],
        out_specs=pl.BlockSpec((tm, tn), lambda i, j: (i, j)),
        out_shape=jax.ShapeDtypeStruct((m, n), out_dtype),
        compiler_params=_params(2),
        name="matmul",
    )(x, w)


def _chunk_attn_kernel(q_ref, kp_ref, kc_ref, vp_ref, vc_ref, bias_ref, o_ref):
    i = pl.program_id(2)
    w = TQ_A + LEFT
    col = lax.broadcasted_iota(jnp.int32, (TQ_A, w), 1)
    first_valid = jnp.where(i == 0, LEFT, 0)
    for h in range(2):
        sl = slice(h * HEAD_DIM, (h + 1) * HEAD_DIM)
        q = q_ref[:, sl] * (HEAD_DIM ** -0.5)
        k = jnp.concatenate([kp_ref[:, sl], kc_ref[:, sl]], axis=0)
        v = jnp.concatenate([vp_ref[:, sl], vc_ref[:, sl]], axis=0)
        s = lax.dot_general(q, k, _NT, preferred_element_type=F32) + bias_ref[h]
        s = jnp.where(col >= first_valid, s, NEG_INF)
        m = jnp.max(s, axis=-1, keepdims=True)
        p = jnp.exp(s - m)
        l = jnp.sum(p, axis=-1, keepdims=True)
        o = jnp.dot(p.astype(BF16), v, preferred_element_type=F32) / l
        o_ref[:, sl] = o.astype(o_ref.dtype)


def _chunk_bias(rel_bias):
    w = TQ_A + LEFT
    n_diag = w + TQ_A - 1
    heads = rel_bias.shape[0]
    rel = jnp.arange(n_diag) - (w - 1) + LEFT
    u = rel_bias[:, jnp.clip(rel, -REL_CLIP, REL_CLIP) + REL_CLIP].astype(F32)
    skew = jnp.tile(u, (1, TQ_A + 1))[:, :TQ_A * (n_diag + 1)].reshape(heads, TQ_A, n_diag + 1)
    table = skew[:, :, :w][:, :, ::-1]
    qc = jnp.arange(TQ_A)[:, None] // CHUNK
    kc = jnp.arange(w)[None, :] // CHUNK
    band = (kc >= qc) & (kc <= qc + LEFT_CHUNKS)
    return jnp.where(band[None], table, NEG_INF)


def _chunk_attention(proj, bias, b, s):
    nq = s // TQ_A
    pairs = WIDTH_A // LANES
    blk = lambda off, prev: pl.BlockSpec(
        (None, TQ_A, LANES),
        (lambda bi, p, i: (bi, jnp.maximum(i - 1, 0), off + p)) if prev
        else (lambda bi, p, i: (bi, i, off + p)))
    return pl.pallas_call(
        _chunk_attn_kernel,
        grid=(b, pairs, nq),
        in_specs=[blk(0, False),
                  blk(pairs, True), blk(pairs, False),
                  blk(2 * pairs, True), blk(2 * pairs, False),
                  pl.BlockSpec((2, TQ_A, TQ_A + LEFT), lambda bi, p, i: (p, 0, 0))],
        out_specs=pl.BlockSpec((None, TQ_A, LANES), lambda bi, p, i: (bi, i, p)),
        out_shape=jax.ShapeDtypeStruct((b, s, WIDTH_A), BF16),
        compiler_params=_params(3),
        name="chunk_attn",
    )(proj, proj, proj, proj, proj, bias)


def _sb_block(q, k_ref, v_ref, sl, j, later, tri, causal):
    t = TB_SB
    start = pl.multiple_of(j * t, t)
    k = k_ref[pl.ds(start, t), sl]
    v = v_ref[pl.ds(start, t), sl]
    z = lax.dot_general(q, k, _NT, preferred_element_type=F32)
    sp = jnp.maximum(z, 0.0) + jnp.log(1.0 + jnp.exp(-jnp.abs(z)))
    if causal is not None:
        sp = jnp.where(causal, sp, 0.0)
    csum = jnp.dot(sp.astype(BF16), tri, preferred_element_type=F32) + later
    wgt = jnp.exp(z - csum)
    if causal is not None:
        wgt = jnp.where(causal, wgt, 0.0)
    pv = jnp.dot(wgt.astype(BF16), v, preferred_element_type=F32)
    return pv, csum[:, 0:1]


def _sb_attn_kernel(q_ref, k_ref, v_ref, o_ref):
    i = pl.program_id(2)
    t = TB_SB
    row = lax.broadcasted_iota(jnp.int32, (t, t), 0)
    col = lax.broadcasted_iota(jnp.int32, (t, t), 1)
    tri = jnp.where(row >= col, 1.0, 0.0).astype(BF16)
    causal = col < row
    sls = [slice(h * HEAD_DIM, (h + 1) * HEAD_DIM) for h in range(2)]
    qs = [q_ref[:, sl] * (HEAD_DIM ** -0.5) for sl in sls]
    block = functools.partial(_sb_block, tri=tri)

    carry = []
    for h in range(2):
        pv, later = block(qs[h], k_ref, v_ref, sls[h], i, jnp.zeros((t, 1), F32), causal=causal)
        carry += [later, pv]
    odd = i % 2

    def single(n, c):
        out = []
        for h in range(2):
            pv, later = block(qs[h], k_ref, v_ref, sls[h], i - 1, c[2 * h], causal=None)
            out += [later, c[2 * h + 1] + pv]
        return tuple(out)

    def pair(n, c):
        j = i - 1 - odd - 2 * n
        out = []
        for h in range(2):
            pv1, later = block(qs[h], k_ref, v_ref, sls[h], j, c[2 * h], causal=None)
            pv2, later = block(qs[h], k_ref, v_ref, sls[h], j - 1, later, causal=None)
            out += [later, c[2 * h + 1] + pv1 + pv2]
        return tuple(out)

    carry = lax.fori_loop(0, odd, single, tuple(carry))
    carry = lax.fori_loop(0, i // 2, pair, carry)
    for h in range(2):
        o_ref[:, sls[h]] = carry[2 * h + 1].astype(o_ref.dtype)


def _sb_attention(proj, b, s):
    nq = s // TB_SB
    pairs = WIDTH_B // LANES
    base = 3 * WIDTH_A // LANES
    return pl.pallas_call(
        _sb_attn_kernel,
        grid=(b, pairs, nq),
        in_specs=[pl.BlockSpec((None, TB_SB, LANES), lambda bi, p, i: (bi, i, base + p)),
                  pl.BlockSpec((None, s, LANES), lambda bi, p, i: (bi, 0, base + pairs + p)),
                  pl.BlockSpec((None, s, LANES), lambda bi, p, i: (bi, 0, base + 2 * pairs + p))],
        out_specs=pl.BlockSpec((None, TB_SB, LANES), lambda bi, p, i: (bi, i, p)),
        out_shape=jax.ShapeDtypeStruct((b, s, WIDTH_B), BF16),
        compiler_params=_params(3),
        name="sb_attn",
    )(proj, proj, proj)


def _mixer_out_kernel(oa_ref, ob_ref, x_ref, ga_ref, gb_ref, wout_ref, lng_ref, lnb_ref, wq_ref,
                      x1_ref, qm_ref):
    def rms(ref, g_ref):
        a = ref[...].astype(F32)
        return (a * lax.rsqrt(jnp.mean(a * a, axis=-1, keepdims=True) + RMS_EPS) * g_ref[...]).astype(BF16)

    y = jnp.dot(rms(oa_ref, ga_ref), wout_ref[:WIDTH_A, :], preferred_element_type=F32)
    y = y + jnp.dot(rms(ob_ref, gb_ref), wout_ref[WIDTH_A:, :], preferred_element_type=F32)
    x1 = _layer_norm(DEEPNORM_ALPHA * x_ref[...] + y, lng_ref[...], lnb_ref[...])
    x1_ref[...] = x1
    qm_ref[...] = jnp.dot(x1.astype(BF16), wq_ref[...], preferred_element_type=F32).astype(BF16)


def _mixer_out(out_a, out_b, x, g_a, g_b, w_out, ln_g, ln_b, w_q):
    n = x.shape[0]
    tm = TM_TOK
    row = lambda width: pl.BlockSpec((tm, width), lambda i: (i, 0))
    full = lambda r, c: pl.BlockSpec((r, c), lambda i: (0, 0))
    return pl.pallas_call(
        _mixer_out_kernel,
        grid=(n // tm,),
        in_specs=[row(WIDTH_A), row(WIDTH_B), row(D_MODEL), full(1, WIDTH_A), full(1, WIDTH_B),
                  full(MIX_WIDTH, D_MODEL), full(1, D_MODEL), full(1, D_MODEL), full(D_MODEL, D_MODEL)],
        out_specs=[row(D_MODEL), row(D_MODEL)],
        out_shape=[jax.ShapeDtypeStruct((n, D_MODEL), F32), jax.ShapeDtypeStruct((n, D_MODEL), BF16)],
        compiler_params=_params(1),
        name="mixer_out",
    )(out_a, out_b, x, g_a, g_b, w_out, ln_g, ln_b, w_q)


def _mem_attn_kernel(q_ref, kv_ref, x1_ref, wo_ref, lng_ref, lnb_ref, wr_ref, br_ref,
                     x2_ref, idx_ref, gate_ref):
    heads = []
    for h in range(N_HEADS_MEM):
        sl = slice(h * HEAD_DIM_MEM, (h + 1) * HEAD_DIM_MEM)
        q = q_ref[:, sl] * (HEAD_DIM_MEM ** -0.5)
        k = kv_ref[:, sl]
        v = kv_ref[:, D_MODEL + h * HEAD_DIM_MEM:D_MODEL + (h + 1) * HEAD_DIM_MEM]
        s = lax.dot_general(q, k, _NT, preferred_element_type=F32)
        m = jnp.max(s, axis=-1, keepdims=True)
        p = jnp.exp(s - m)
        l = jnp.sum(p, axis=-1, keepdims=True)
        heads.append((jnp.dot(p.astype(BF16), v, preferred_element_type=F32) / l).astype(BF16))
    o = jnp.concatenate(heads, axis=-1)
    y = jnp.dot(o, wo_ref[...], preferred_element_type=F32)
    x2 = _layer_norm(DEEPNORM_ALPHA * x1_ref[...] + y, lng_ref[...], lnb_ref[...])
    x2_ref[...] = x2

    logits = jnp.dot(x2.astype(BF16), wr_ref[...], preferred_element_type=F32) + br_ref[...]
    tm = logits.shape[0]
    e_iota = lax.broadcasted_iota(jnp.int32, (tm, N_EXPERTS), 1)
    lane = lax.broadcasted_iota(jnp.int32, (tm, LANES), 1)
    idx_out = jnp.zeros((tm, LANES), jnp.int32)
    val_out = jnp.zeros((tm, LANES), F32)
    top = None
    denom = jnp.zeros((tm, 1), F32)
    for kk in range(TOP_K):
        m = jnp.max(logits, axis=-1, keepdims=True)
        sel = jnp.min(jnp.where(logits == m, e_iota, N_EXPERTS), axis=-1, keepdims=True)
        if top is None:
            top = m
        e = jnp.exp(m - top)
        denom = denom + e
        idx_out = jnp.where(lane == kk, sel, idx_out)
        val_out = jnp.where(lane == kk, e, val_out)
        logits = jnp.where(e_iota == sel, -jnp.inf, logits)
    idx_ref[...] = idx_out
    gate_ref[...] = val_out / denom


def _mem_attn(qm, kv, x1, w_o, ln_g, ln_b, w_r, b_r, b, s):
    tm = TM_TOK
    nt = s // tm
    mem_len = kv.shape[1]
    row = lambda width: pl.BlockSpec((tm, width), lambda bi, i: (bi * nt + i, 0))
    full = lambda r, c: pl.BlockSpec((r, c), lambda bi, i: (0, 0))
    n = b * s
    return pl.pallas_call(
        _mem_attn_kernel,
        grid=(b, nt),
        in_specs=[row(D_MODEL), pl.BlockSpec((None, mem_len, 2 * D_MODEL), lambda bi, i: (bi, 0, 0)),
                  row(D_MODEL), full(D_MODEL, D_MODEL), full(1, D_MODEL), full(1, D_MODEL),
                  full(D_MODEL, N_EXPERTS), full(1, N_EXPERTS)],
        out_specs=[row(D_MODEL), row(LANES), row(LANES)],
        out_shape=[jax.ShapeDtypeStruct((n, D_MODEL), F32),
                   jax.ShapeDtypeStruct((n, LANES), jnp.int32),
                   jax.ShapeDtypeStruct((n, LANES), F32)],
        compiler_params=_params(2),
        name="mem_attn",
    )(qm, kv, x1, w_o, ln_g, ln_b, w_r, b_r)


def _gather_rows(idx_ref, count, src_hbm, dst_ref, sem):
    def body(r, carry):
        pltpu.make_async_copy(src_hbm.at[pl.ds(idx_ref[0, 0, r], 1)], dst_ref.at[pl.ds(r, 1)], sem).start()
        return carry
    lax.fori_loop(0, count, body, 0, unroll=8)


def _wait_rows(count, src_hbm, dst_ref, sem):
    pltpu.make_async_copy(src_hbm.at[pl.ds(0, count)], dst_ref, sem).wait()


def _moe_kernel(bexp_ref, nused_ref, tok_ref, tok_next_ref, x_hbm, wgu_ref, bgu_ref, wd_ref, bd_ref,
                y_ref, xbuf, sem):
    i = pl.program_id(0)
    nb = pl.num_programs(0)
    slot = i % 2

    @pl.when(i == 0)
    def _():
        _gather_rows(tok_ref, TM_MOE, x_hbm, xbuf.at[0], sem.at[0])

    @pl.when(i + 1 < nb)
    def _():
        _gather_rows(tok_next_ref, TM_MOE, x_hbm, xbuf.at[1 - slot], sem.at[1 - slot])

    _wait_rows(TM_MOE, x_hbm, xbuf.at[slot], sem.at[slot])

    @pl.when(i < nused_ref[0])
    def _():
        x = xbuf[slot].astype(BF16)
        gu = jnp.dot(x, wgu_ref[0], preferred_element_type=F32) + bgu_ref[0]
        gate = jnp.minimum(gu[:, :D_FF], SWIGLU_LIMIT)
        up = jnp.clip(gu[:, D_FF:], -SWIGLU_LIMIT, SWIGLU_LIMIT)
        glu = gate * jax.nn.sigmoid(gate * SWIGLU_ALPHA)
        hid = ((up + 1.0) * glu).astype(BF16)
        y_ref[...] = jnp.dot(hid, wd_ref[0], preferred_element_type=F32) + bd_ref[0]

    @pl.when(i >= nused_ref[0])
    def _():
        y_ref[...] = jnp.zeros_like(y_ref)


def _moe_experts(block_expert, n_used, row_token, x2, w_gu, b_gu, w_d, b_d):
    nb = block_expert.shape[0]
    tok3 = row_token.reshape(nb, 1, TM_MOE)
    smem_blk = lambda f: pl.BlockSpec((1, 1, TM_MOE), f, memory_space=pltpu.SMEM)
    grid_spec = pltpu.PrefetchScalarGridSpec(
        num_scalar_prefetch=2,
        grid=(nb,),
        in_specs=[smem_blk(lambda i, be, nu: (i, 0, 0)),
                  smem_blk(lambda i, be, nu: (jnp.minimum(i + 1, nb - 1), 0, 0)),
                  pl.BlockSpec(memory_space=pl.ANY),
                  pl.BlockSpec((1, D_MODEL, 2 * D_FF), lambda i, be, nu: (be[i], 0, 0)),
                  pl.BlockSpec((1, 1, 2 * D_FF), lambda i, be, nu: (be[i], 0, 0)),
                  pl.BlockSpec((1, D_FF, D_MODEL), lambda i, be, nu: (be[i], 0, 0)),
                  pl.BlockSpec((1, 1, D_MODEL), lambda i, be, nu: (be[i], 0, 0))],
        out_specs=pl.BlockSpec((TM_MOE, D_MODEL), lambda i, be, nu: (i, 0)),
        scratch_shapes=[pltpu.VMEM((2, TM_MOE, D_MODEL), F32), pltpu.SemaphoreType.DMA((2,))],
    )
    return pl.pallas_call(
        _moe_kernel,
        grid_spec=grid_spec,
        out_shape=jax.ShapeDtypeStruct((nb * TM_MOE, D_MODEL), F32),
        compiler_params=_params(1),
        name="moe_experts",
    )(block_expert, n_used, tok3, tok3, x2, w_gu, b_gu, w_d, b_d)


def _combine_kernel(idx_ref, idx_next_ref, y_hbm, x2_ref, gate_ref, lng_ref, lnb_ref, o_ref, ybuf, sem):
    i = pl.program_id(0)
    nt = pl.num_programs(0)
    slot = i % 2
    rows = TOP_K * TM_COMB

    @pl.when(i == 0)
    def _():
        _gather_rows(idx_ref, rows, y_hbm, ybuf.at[0], sem.at[0])

    @pl.when(i + 1 < nt)
    def _():
        _gather_rows(idx_next_ref, rows, y_hbm, ybuf.at[1 - slot], sem.at[1 - slot])

    _wait_rows(rows, y_hbm, ybuf.at[slot], sem.at[slot])

    g = gate_ref[...]
    ff = jnp.zeros((TM_COMB, D_MODEL), F32)
    for kk in range(TOP_K):
        ff = ff + g[:, kk:kk + 1] * ybuf[slot, kk * TM_COMB:(kk + 1) * TM_COMB, :]
    o_ref[...] = _layer_norm(DEEPNORM_ALPHA * x2_ref[...] + ff, lng_ref[...], lnb_ref[...])


def _combine(dest, y_rows, x2, gates, ln_g, ln_b):
    n = x2.shape[0]
    nt = n // TM_COMB
    rows = TOP_K * TM_COMB
    idx3 = dest.reshape(nt, TM_COMB, TOP_K).transpose(0, 2, 1).reshape(nt, 1, rows)
    smem_blk = lambda f: pl.BlockSpec((1, 1, rows), f, memory_space=pltpu.SMEM)
    row = lambda width: pl.BlockSpec((TM_COMB, width), lambda i: (i, 0))
    full = lambda r, c: pl.BlockSpec((r, c), lambda i: (0, 0))
    return pl.pallas_call(
        _combine_kernel,
        grid=(nt,),
        in_specs=[smem_blk(lambda i: (i, 0, 0)),
                  smem_blk(lambda i: (jnp.minimum(i + 1, nt - 1), 0, 0)),
                  pl.BlockSpec(memory_space=pl.ANY),
                  row(D_MODEL), row(LANES), full(1, D_MODEL), full(1, D_MODEL)],
        out_specs=row(D_MODEL),
        out_shape=jax.ShapeDtypeStruct((n, D_MODEL), F32),
        scratch_shapes=[pltpu.VMEM((2, rows, D_MODEL), F32), pltpu.SemaphoreType.DMA((2,))],
        compiler_params=_params(1),
        name="combine",
    )(idx3, idx3, y_rows, x2, gates, ln_g, ln_b)


def _dispatch_plan(top_idx, n):
    n_assign = n * TOP_K
    nb = n_assign // TM_MOE + N_EXPERTS
    flat_e = top_idx.reshape(-1)
    onehot = (flat_e[:, None] == jnp.arange(N_EXPERTS, dtype=jnp.int32)[None, :]).astype(jnp.int32)
    csum = jnp.cumsum(onehot, axis=0)
    counts = csum[-1]
    padded = (counts + TM_MOE - 1) // TM_MOE * TM_MOE
    pad_end = jnp.cumsum(padded)
    pad_start = pad_end - padded
    dest = jnp.sum(onehot * (csum - 1 + pad_start[None, :]), axis=1).astype(jnp.int32)
    row_token = jnp.zeros((nb * TM_MOE,), jnp.int32).at[dest].set(
        jnp.arange(n_assign, dtype=jnp.int32) // TOP_K)
    block_expert = jnp.minimum(
        jnp.searchsorted(pad_end, jnp.arange(nb, dtype=jnp.int32) * TM_MOE, side='right'),
        N_EXPERTS - 1).astype(jnp.int32)
    n_used = (pad_end[-1:] // TM_MOE).astype(jnp.int32)
    return dest, row_token, block_expert, n_used


def kernel(x, mem, w_in, rel_bias, g_group_a, g_group_b, w_out, w_q_mem, w_kv_mem, w_o_mem, w_router, b_router, w_gate_up, b_gate_up, w_down, b_down, ln_g, ln_b):
    b, s, d = x.shape
    n = b * s
    depth = w_in.shape[0]
    xf = x.reshape(n, d)
    for l in range(depth):
        proj = _matmul(xf, w_in[l].astype(BF16), BF16, TM_TOK, 1024).reshape(b, s, 3 * MIX_WIDTH)
        out_a = _chunk_attention(proj, _chunk_bias(rel_bias[l]), b, s).reshape(n, WIDTH_A)
        out_b = _sb_attention(proj, b, s).reshape(n, WIDTH_B)
        x1, qm = _mixer_out(out_a, out_b, xf, g_group_a[l][None], g_group_b[l][None],
                            w_out[l].astype(BF16), ln_g[l, 0][None], ln_b[l, 0][None],
                            w_q_mem[l].astype(BF16))
        mem_len = mem.shape[1]
        kv = _matmul(mem.reshape(b * mem_len, d), w_kv_mem[l].astype(BF16), BF16, b * mem_len, 1024)
        x2, top_idx, gates = _mem_attn(qm, kv.reshape(b, mem_len, 2 * d), x1, w_o_mem[l].astype(BF16),
                                       ln_g[l, 1][None], ln_b[l, 1][None],
                                       w_router[l].astype(BF16), b_router[l][None], b, s)
        dest, row_token, block_expert, n_used = _dispatch_plan(top_idx[:, :TOP_K], n)
        y_rows = _moe_experts(block_expert, n_used, row_token, x2,
                              w_gate_up[l].astype(BF16), b_gate_up[l][:, None, :],
                              w_down[l].astype(BF16), b_down[l][:, None, :])
        xf = _combine(dest, y_rows, x2, gates, ln_g[l, 2][None], ln_b[l, 2][None])
    return xf.reshape(b, s, d)
```

```python
import functools

import jax
import jax.numpy as jnp
from jax import lax
from jax.experimental import pallas as pl
from jax.experimental.pallas import tpu as pltpu

D_MODEL = 1024
CHUNK = 64
LEFT_CHUNKS = 8
LEFT = LEFT_CHUNKS * CHUNK
HEAD_DIM = 64
N_HEADS_A = 8
N_HEADS_B = 8
WIDTH_A = N_HEADS_A * HEAD_DIM
WIDTH_B = N_HEADS_B * HEAD_DIM
MIX_WIDTH = WIDTH_A + WIDTH_B
REL_CLIP = 128
N_HEADS_MEM = 4
HEAD_DIM_MEM = D_MODEL // N_HEADS_MEM
N_EXPERTS = 32
TOP_K = 4
D_FF = D_MODEL
SWIGLU_LIMIT = 7.0
SWIGLU_ALPHA = 1.702
LN_EPS = 1e-5
RMS_EPS = 1e-6
DEEPNORM_ALPHA = 2.0 ** 0.25
NEG_INF = -1e30
LOG2E = 1.4426950408889634

LANES = 128
VMEM_LIMIT = 48 * 1024 * 1024

TQ_A = 512
TB_SB = 256
SB_UNROLL = 4
TM_TOK = 512
TM_MOE = 256
TM_COMB = 256

F32 = jnp.float32
BF16 = jnp.bfloat16
_NT = (((1,), (1,)), ((), ()))


def _params(n_axes):
    return pltpu.CompilerParams(dimension_semantics=("arbitrary",) * n_axes,
                                vmem_limit_bytes=VMEM_LIMIT)


def _layer_norm(r, g, b):
    mu = jnp.mean(r, axis=-1, keepdims=True)
    d = r - mu
    var = jnp.mean(d * d, axis=-1, keepdims=True)
    return d * lax.rsqrt(var + LN_EPS) * g + b


def _matmul_kernel(x_ref, w_ref, o_ref):
    o_ref[...] = jnp.dot(x_ref[...].astype(BF16), w_ref[...],
                         preferred_element_type=F32).astype(o_ref.dtype)


def _matmul(x, w, out_dtype, tm, tn):
    m, k = x.shape
    n = w.shape[1]
    return pl.pallas_call(
        _matmul_kernel,
        grid=(m // tm, n // tn),
        in_specs=[pl.BlockSpec((tm, k), lambda i, j: (i, 0)),
                  pl.BlockSpec((k, tn), lambda i, j: (0, j))],
        out_specs=pl.BlockSpec((tm, tn), lambda i, j: (i, j)),
        out_shape=jax.ShapeDtypeStruct((m, n), out_dtype),
        compiler_params=_params(2),
        name="matmul",
    )(x, w)


def _chunk_attn_kernel(q_ref, kp_ref, kc_ref, vp_ref, vc_ref, bias_ref, o_ref):
    i = pl.program_id(2)
    w = TQ_A + LEFT
    col = lax.broadcasted_iota(jnp.int32, (TQ_A, w), 1)
    first_valid = jnp.where(i == 0, LEFT, 0)
    for h in range(2):
        sl = slice(h * HEAD_DIM, (h + 1) * HEAD_DIM)
        q = q_ref[:, sl] * (HEAD_DIM ** -0.5)
        k = jnp.concatenate([kp_ref[:, sl], kc_ref[:, sl]], axis=0)
        v = jnp.concatenate([vp_ref[:, sl], vc_ref[:, sl]], axis=0)
        s = lax.dot_general(q, k, _NT, preferred_element_type=F32) + bias_ref[h]
        s = jnp.where(col >= first_valid, s, NEG_INF)
        m = jnp.max(s, axis=-1, keepdims=True)
        p = jnp.exp(s - m)
        l = jnp.sum(p, axis=-1, keepdims=True)
        o = jnp.dot(p.astype(BF16), v, preferred_element_type=F32) / l
        o_ref[:, sl] = o.astype(o_ref.dtype)


def _chunk_bias(rel_bias):
    w = TQ_A + LEFT
    heads = rel_bias.shape[0]
    m = jnp.arange(2 * w)
    rel = jnp.where(m < w, LEFT - m, LEFT + 2 * w - m)
    diag = rel_bias[:, jnp.clip(rel, -REL_CLIP, REL_CLIP) + REL_CLIP].astype(F32)[:, None, :]
    return pl.pallas_call(
        _chunk_bias_kernel,
        grid=(heads,),
        in_specs=[pl.BlockSpec((1, 1, 2 * w), lambda h: (h, 0, 0))],
        out_specs=pl.BlockSpec((1, TQ_A, w), lambda h: (h, 0, 0)),
        out_shape=jax.ShapeDtypeStruct((heads, TQ_A, w), F32),
        compiler_params=_params(1),
        name="chunk_bias",
    )(diag)


def _chunk_bias_kernel(diag_ref, o_ref):
    w = TQ_A + LEFT
    rolled = pltpu.roll(jnp.broadcast_to(diag_ref[0], (TQ_A, 2 * w)), 0, 1, stride=1, stride_axis=0)
    qc = lax.broadcasted_iota(jnp.int32, (TQ_A, w), 0) // CHUNK
    kc = lax.broadcasted_iota(jnp.int32, (TQ_A, w), 1) // CHUNK
    band = (kc >= qc) & (kc <= qc + LEFT_CHUNKS)
    o_ref[0] = jnp.where(band, rolled[:, :w], NEG_INF)


def _chunk_attention(proj, bias, b, s):
    nq = s // TQ_A
    pairs = WIDTH_A // LANES
    blk = lambda off, prev: pl.BlockSpec(
        (None, TQ_A, LANES),
        (lambda bi, p, i: (bi, jnp.maximum(i - 1, 0), off + p)) if prev
        else (lambda bi, p, i: (bi, i, off + p)))
    return pl.pallas_call(
        _chunk_attn_kernel,
        grid=(b, pairs, nq),
        in_specs=[blk(0, False),
                  blk(pairs, True), blk(pairs, False),
                  blk(2 * pairs, True), blk(2 * pairs, False),
                  pl.BlockSpec((2, TQ_A, TQ_A + LEFT), lambda bi, p, i: (p, 0, 0))],
        out_specs=pl.BlockSpec((None, TQ_A, LANES), lambda bi, p, i: (bi, i, p)),
        out_shape=jax.ShapeDtypeStruct((b, s, WIDTH_A), BF16),
        compiler_params=_params(3),
        name="chunk_attn",
    )(proj, proj, proj, proj, proj, bias)


def _sb_block(q, k_ref, v_ref, sl, j, later, tri, causal):
    t = TB_SB
    start = pl.multiple_of(j * t, t)
    k = k_ref[pl.ds(start, t), sl]
    v = v_ref[pl.ds(start, t), sl]
    z = lax.dot_general(q, k, _NT, preferred_element_type=F32)
    sp = jnp.maximum(z, 0.0) + jnp.log(1.0 + jnp.exp2(jnp.abs(z) * (-LOG2E)))
    if causal is not None:
        sp = jnp.where(causal, sp, 0.0)
    csum = jnp.dot(sp.astype(BF16), tri, preferred_element_type=F32) + later
    wgt = jnp.exp(z - csum)
    if causal is not None:
        wgt = jnp.where(causal, wgt, 0.0)
    pv = jnp.dot(wgt.astype(BF16), v, preferred_element_type=F32)
    return pv, csum[:, 0:1]


def _sb_attn_kernel(q_ref, k_ref, v_ref, o_ref):
    i = pl.program_id(2)
    t = TB_SB
    row = lax.broadcasted_iota(jnp.int32, (t, t), 0)
    col = lax.broadcasted_iota(jnp.int32, (t, t), 1)
    tri = jnp.where(row >= col, 1.0, 0.0).astype(BF16)
    causal = col < row
    sls = [slice(h * HEAD_DIM, (h + 1) * HEAD_DIM) for h in range(2)]
    qs = [q_ref[:, sl] * (HEAD_DIM ** -0.5) for sl in sls]
    block = functools.partial(_sb_block, tri=tri)

    carry = []
    for h in range(2):
        pv, later = block(qs[h], k_ref, v_ref, sls[h], i, jnp.zeros((t, 1), F32), causal=causal)
        carry += [later, pv]
    rem = i % SB_UNROLL

    def run(j_first, n_blocks, c):
        out = []
        for h in range(2):
            later, acc = c[2 * h], c[2 * h + 1]
            for u in range(n_blocks):
                pv, later = block(qs[h], k_ref, v_ref, sls[h], j_first - u, later, causal=None)
                acc = acc + pv
            out += [later, acc]
        return tuple(out)

    carry = lax.fori_loop(0, rem, lambda n, c: run(i - 1 - n, 1, c), tuple(carry))
    carry = lax.fori_loop(0, i // SB_UNROLL,
                          lambda n, c: run(i - 1 - rem - SB_UNROLL * n, SB_UNROLL, c), carry)
    for h in range(2):
        o_ref[:, sls[h]] = carry[2 * h + 1].astype(o_ref.dtype)


def _sb_attention(proj, b, s):
    nq = s // TB_SB
    pairs = WIDTH_B // LANES
    base = 3 * WIDTH_A // LANES
    return pl.pallas_call(
        _sb_attn_kernel,
        grid=(b, pairs, nq),
        in_specs=[pl.BlockSpec((None, TB_SB, LANES), lambda bi, p, i: (bi, i, base + p)),
                  pl.BlockSpec((None, s, LANES), lambda bi, p, i: (bi, 0, base + pairs + p)),
                  pl.BlockSpec((None, s, LANES), lambda bi, p, i: (bi, 0, base + 2 * pairs + p))],
        out_specs=pl.BlockSpec((None, TB_SB, LANES), lambda bi, p, i: (bi, i, p)),
        out_shape=jax.ShapeDtypeStruct((b, s, WIDTH_B), BF16),
        compiler_params=_params(3),
        name="sb_attn",
    )(proj, proj, proj)


def _mixer_out_kernel(oa_ref, ob_ref, x_ref, ga_ref, gb_ref, wout_ref, lng_ref, lnb_ref, wq_ref,
                      x1_ref, qm_ref):
    def rms(ref, g_ref):
        a = ref[...].astype(F32)
        return (a * lax.rsqrt(jnp.mean(a * a, axis=-1, keepdims=True) + RMS_EPS) * g_ref[...]).astype(BF16)

    y = jnp.dot(rms(oa_ref, ga_ref), wout_ref[:WIDTH_A, :], preferred_element_type=F32)
    y = y + jnp.dot(rms(ob_ref, gb_ref), wout_ref[WIDTH_A:, :], preferred_element_type=F32)
    x1 = _layer_norm(DEEPNORM_ALPHA * x_ref[...] + y, lng_ref[...], lnb_ref[...])
    x1_ref[...] = x1
    qm_ref[...] = jnp.dot(x1.astype(BF16), wq_ref[...], preferred_element_type=F32).astype(BF16)


def _mixer_out(out_a, out_b, x, g_a, g_b, w_out, ln_g, ln_b, w_q):
    n = x.shape[0]
    tm = TM_TOK
    row = lambda width: pl.BlockSpec((tm, width), lambda i: (i, 0))
    full = lambda r, c: pl.BlockSpec((r, c), lambda i: (0, 0))
    return pl.pallas_call(
        _mixer_out_kernel,
        grid=(n // tm,),
        in_specs=[row(WIDTH_A), row(WIDTH_B), row(D_MODEL), full(1, WIDTH_A), full(1, WIDTH_B),
                  full(MIX_WIDTH, D_MODEL), full(1, D_MODEL), full(1, D_MODEL), full(D_MODEL, D_MODEL)],
        out_specs=[row(D_MODEL), row(D_MODEL)],
        out_shape=[jax.ShapeDtypeStruct((n, D_MODEL), F32), jax.ShapeDtypeStruct((n, D_MODEL), BF16)],
        compiler_params=_params(1),
        name="mixer_out",
    )(out_a, out_b, x, g_a, g_b, w_out, ln_g, ln_b, w_q)


def _mem_attn_kernel(q_ref, kv_ref, x1_ref, wo_ref, lng_ref, lnb_ref, wr_ref, br_ref,
                     x2_ref, idx_ref, gate_ref):
    heads = []
    for h in range(N_HEADS_MEM):
        sl = slice(h * HEAD_DIM_MEM, (h + 1) * HEAD_DIM_MEM)
        q = q_ref[:, sl] * (HEAD_DIM_MEM ** -0.5)
        k = kv_ref[:, sl]
        v = kv_ref[:, D_MODEL + h * HEAD_DIM_MEM:D_MODEL + (h + 1) * HEAD_DIM_MEM]
        s = lax.dot_general(q, k, _NT, preferred_element_type=F32)
        m = jnp.max(s, axis=-1, keepdims=True)
        p = jnp.exp(s - m)
        l = jnp.sum(p, axis=-1, keepdims=True)
        heads.append((jnp.dot(p.astype(BF16), v, preferred_element_type=F32) / l).astype(BF16))
    o = jnp.concatenate(heads, axis=-1)
    y = jnp.dot(o, wo_ref[...], preferred_element_type=F32)
    x2 = _layer_norm(DEEPNORM_ALPHA * x1_ref[...] + y, lng_ref[...], lnb_ref[...])
    x2_ref[...] = x2

    logits = jnp.dot(x2.astype(BF16), wr_ref[...], preferred_element_type=F32) + br_ref[...]
    tm = logits.shape[0]
    e_iota = lax.broadcasted_iota(jnp.int32, (tm, N_EXPERTS), 1)
    lane = lax.broadcasted_iota(jnp.int32, (tm, LANES), 1)
    idx_out = jnp.zeros((tm, LANES), jnp.int32)
    val_out = jnp.zeros((tm, LANES), F32)
    top = None
    denom = jnp.zeros((tm, 1), F32)
    for kk in range(TOP_K):
        m = jnp.max(logits, axis=-1, keepdims=True)
        sel = jnp.min(jnp.where(logits == m, e_iota, N_EXPERTS), axis=-1, keepdims=True)
        if top is None:
            top = m
        e = jnp.exp(m - top)
        denom = denom + e
        idx_out = jnp.where(lane == kk, sel, idx_out)
        val_out = jnp.where(lane == kk, e, val_out)
        logits = jnp.where(e_iota == sel, -jnp.inf, logits)
    idx_ref[...] = idx_out
    gate_ref[...] = val_out / denom


def _mem_attn(qm, kv, x1, w_o, ln_g, ln_b, w_r, b_r, b, s):
    tm = TM_TOK
    nt = s // tm
    mem_len = kv.shape[1]
    row = lambda width: pl.BlockSpec((tm, width), lambda bi, i: (bi * nt + i, 0))
    full = lambda r, c: pl.BlockSpec((r, c), lambda bi, i: (0, 0))
    n = b * s
    return pl.pallas_call(
        _mem_attn_kernel,
        grid=(b, nt),
        in_specs=[row(D_MODEL), pl.BlockSpec((None, mem_len, 2 * D_MODEL), lambda bi, i: (bi, 0, 0)),
                  row(D_MODEL), full(D_MODEL, D_MODEL), full(1, D_MODEL), full(1, D_MODEL),
                  full(D_MODEL, N_EXPERTS), full(1, N_EXPERTS)],
        out_specs=[row(D_MODEL), row(LANES), row(LANES)],
        out_shape=[jax.ShapeDtypeStruct((n, D_MODEL), F32),
                   jax.ShapeDtypeStruct((n, LANES), jnp.int32),
                   jax.ShapeDtypeStruct((n, LANES), F32)],
        compiler_params=_params(2),
        name="mem_attn",
    )(qm, kv, x1, w_o, ln_g, ln_b, w_r, b_r)


def _gather_rows(idx_ref, count, src_hbm, dst_ref, sem):
    def body(r, carry):
        pltpu.make_async_copy(src_hbm.at[pl.ds(idx_ref[0, 0, r], 1)], dst_ref.at[pl.ds(r, 1)], sem).start()
        return carry
    lax.fori_loop(0, count, body, 0, unroll=8)


def _wait_rows(count, src_hbm, dst_ref, sem):
    pltpu.make_async_copy(src_hbm.at[pl.ds(0, count)], dst_ref, sem).wait()


def _moe_kernel(bexp_ref, nused_ref, tok_ref, tok_next_ref, x_hbm, wgu_ref, bgu_ref, wd_ref, bd_ref,
                y_ref, xbuf, sem):
    i = pl.program_id(0)
    nb = pl.num_programs(0)
    slot = i % 2

    @pl.when(i == 0)
    def _():
        _gather_rows(tok_ref, TM_MOE, x_hbm, xbuf.at[0], sem.at[0])

    @pl.when(i + 1 < nb)
    def _():
        _gather_rows(tok_next_ref, TM_MOE, x_hbm, xbuf.at[1 - slot], sem.at[1 - slot])

    _wait_rows(TM_MOE, x_hbm, xbuf.at[slot], sem.at[slot])

    @pl.when(i < nused_ref[0])
    def _():
        x = xbuf[slot].astype(BF16)
        gu = jnp.dot(x, wgu_ref[0], preferred_element_type=F32) + bgu_ref[0]
        gate = jnp.minimum(gu[:, :D_FF], SWIGLU_LIMIT)
        up = jnp.clip(gu[:, D_FF:], -SWIGLU_LIMIT, SWIGLU_LIMIT)
        glu = gate * jax.nn.sigmoid(gate * SWIGLU_ALPHA)
        hid = ((up + 1.0) * glu).astype(BF16)
        y_ref[...] = jnp.dot(hid, wd_ref[0], preferred_element_type=F32) + bd_ref[0]

    @pl.when(i >= nused_ref[0])
    def _():
        y_ref[...] = jnp.zeros_like(y_ref)


def _moe_experts(block_expert, n_used, row_token, x2, w_gu, b_gu, w_d, b_d):
    nb = block_expert.shape[0]
    tok3 = row_token.reshape(nb, 1, TM_MOE)
    smem_blk = lambda f: pl.BlockSpec((1, 1, TM_MOE), f, memory_space=pltpu.SMEM)
    grid_spec = pltpu.PrefetchScalarGridSpec(
        num_scalar_prefetch=2,
        grid=(nb,),
        in_specs=[smem_blk(lambda i, be, nu: (i, 0, 0)),
                  smem_blk(lambda i, be, nu: (jnp.minimum(i + 1, nb - 1), 0, 0)),
                  pl.BlockSpec(memory_space=pl.ANY),
                  pl.BlockSpec((1, D_MODEL, 2 * D_FF), lambda i, be, nu: (be[i], 0, 0)),
                  pl.BlockSpec((1, 1, 2 * D_FF), lambda i, be, nu: (be[i], 0, 0)),
                  pl.BlockSpec((1, D_FF, D_MODEL), lambda i, be, nu: (be[i], 0, 0)),
                  pl.BlockSpec((1, 1, D_MODEL), lambda i, be, nu: (be[i], 0, 0))],
        out_specs=pl.BlockSpec((TM_MOE, D_MODEL), lambda i, be, nu: (i, 0)),
        scratch_shapes=[pltpu.VMEM((2, TM_MOE, D_MODEL), F32), pltpu.SemaphoreType.DMA((2,))],
    )
    return pl.pallas_call(
        _moe_kernel,
        grid_spec=grid_spec,
        out_shape=jax.ShapeDtypeStruct((nb * TM_MOE, D_MODEL), F32),
        compiler_params=_params(1),
        name="moe_experts",
    )(block_expert, n_used, tok3, tok3, x2, w_gu, b_gu, w_d, b_d)


def _combine_kernel(idx_ref, idx_next_ref, y_hbm, x2_ref, gate_ref, lng_ref, lnb_ref, o_ref, ybuf, sem):
    i = pl.program_id(0)
    nt = pl.num_programs(0)
    slot = i % 2
    rows = TOP_K * TM_COMB

    @pl.when(i == 0)
    def _():
        _gather_rows(idx_ref, rows, y_hbm, ybuf.at[0], sem.at[0])

    @pl.when(i + 1 < nt)
    def _():
        _gather_rows(idx_next_ref, rows, y_hbm, ybuf.at[1 - slot], sem.at[1 - slot])

    _wait_rows(rows, y_hbm, ybuf.at[slot], sem.at[slot])

    g = gate_ref[...]
    ff = jnp.zeros((TM_COMB, D_MODEL), F32)
    for kk in range(TOP_K):
        ff = ff + g[:, kk:kk + 1] * ybuf[slot, kk * TM_COMB:(kk + 1) * TM_COMB, :]
    o_ref[...] = _layer_norm(DEEPNORM_ALPHA * x2_ref[...] + ff, lng_ref[...], lnb_ref[...])


def _combine(dest, y_rows, x2, gates, ln_g, ln_b):
    n = x2.shape[0]
    nt = n // TM_COMB
    rows = TOP_K * TM_COMB
    idx3 = dest.reshape(nt, TM_COMB, TOP_K).transpose(0, 2, 1).reshape(nt, 1, rows)
    smem_blk = lambda f: pl.BlockSpec((1, 1, rows), f, memory_space=pltpu.SMEM)
    row = lambda width: pl.BlockSpec((TM_COMB, width), lambda i: (i, 0))
    full = lambda r, c: pl.BlockSpec((r, c), lambda i: (0, 0))
    return pl.pallas_call(
        _combine_kernel,
        grid=(nt,),
        in_specs=[smem_blk(lambda i: (i, 0, 0)),
                  smem_blk(lambda i: (jnp.minimum(i + 1, nt - 1), 0, 0)),
                  pl.BlockSpec(memory_space=pl.ANY),
                  row(D_MODEL), row(LANES), full(1, D_MODEL), full(1, D_MODEL)],
        out_specs=row(D_MODEL),
        out_shape=jax.ShapeDtypeStruct((n, D_MODEL), F32),
        scratch_shapes=[pltpu.VMEM((2, rows, D_MODEL), F32), pltpu.SemaphoreType.DMA((2,))],
        compiler_params=_params(1),
        name="combine",
    )(idx3, idx3, y_rows, x2, gates, ln_g, ln_b)


def _dispatch_plan(top_idx, n):
    n_assign = n * TOP_K
    nb = n_assign // TM_MOE + N_EXPERTS
    flat_e = top_idx.reshape(-1)
    onehot = (flat_e[:, None] == jnp.arange(N_EXPERTS, dtype=jnp.int32)[None, :]).astype(jnp.int32)
    csum = jnp.cumsum(onehot, axis=0)
    counts = csum[-1]
    padded = (counts + TM_MOE - 1) // TM_MOE * TM_MOE
    pad_end = jnp.cumsum(padded)
    pad_start = pad_end - padded
    dest = jnp.sum(onehot * (csum - 1 + pad_start[None, :]), axis=1).astype(jnp.int32)
    row_token = jnp.zeros((nb * TM_MOE,), jnp.int32).at[dest].set(
        jnp.arange(n_assign, dtype=jnp.int32) // TOP_K)
    block_expert = jnp.minimum(
        jnp.searchsorted(pad_end, jnp.arange(nb, dtype=jnp.int32) * TM_MOE, side='right'),
        N_EXPERTS - 1).astype(jnp.int32)
    n_used = (pad_end[-1:] // TM_MOE).astype(jnp.int32)
    return dest, row_token, block_expert, n_used


def kernel(x, mem, w_in, rel_bias, g_group_a, g_group_b, w_out, w_q_mem, w_kv_mem, w_o_mem, w_router, b_router, w_gate_up, b_gate_up, w_down, b_down, ln_g, ln_b):
    b, s, d = x.shape
    n = b * s
    depth = w_in.shape[0]
    xf = x.reshape(n, d)
    for l in range(depth):
        proj = _matmul(xf, w_in[l].astype(BF16), BF16, TM_TOK, 1024).reshape(b, s, 3 * MIX_WIDTH)
        out_a = _chunk_attention(proj, _chunk_bias(rel_bias[l]), b, s).reshape(n, WIDTH_A)
        out_b = _sb_attention(proj, b, s).reshape(n, WIDTH_B)
        x1, qm = _mixer_out(out_a, out_b, xf, g_group_a[l][None], g_group_b[l][None],
                            w_out[l].astype(BF16), ln_g[l, 0][None], ln_b[l, 0][None],
                            w_q_mem[l].astype(BF16))
        mem_len = mem.shape[1]
        kv = _matmul(mem.reshape(b * mem_len, d), w_kv_mem[l].astype(BF16), BF16, b * mem_len, 1024)
        x2, top_idx, gates = _mem_attn(qm, kv.reshape(b, mem_len, 2 * d), x1, w_o_mem[l].astype(BF16),
                                       ln_g[l, 1][None], ln_b[l, 1][None],
                                       w_router[l].astype(BF16), b_router[l][None], b, s)
        dest, row_token, block_expert, n_used = _dispatch_plan(top_idx[:, :TOP_K], n)
        y_rows = _moe_experts(block_expert, n_used, row_token, x2,
                              w_gate_up[l].astype(BF16), b_gate_up[l][:, None, :],
                              w_down[l].astype(BF16), b_down[l][:, None, :])
        xf = _combine(dest, y_rows, x2, gates, ln_g[l, 2][None], ln_b[l, 2][None])
    return xf.reshape(b, s, d)
```

```python
import functools

import jax
import jax.numpy as jnp
from jax import lax
from jax.experimental import pallas as pl
from jax.experimental.pallas import tpu as pltpu

D_MODEL = 1024
CHUNK = 64
LEFT_CHUNKS = 8
LEFT = LEFT_CHUNKS * CHUNK
HEAD_DIM = 64
N_HEADS_A = 8
N_HEADS_B = 8
WIDTH_A = N_HEADS_A * HEAD_DIM
WIDTH_B = N_HEADS_B * HEAD_DIM
MIX_WIDTH = WIDTH_A + WIDTH_B
REL_CLIP = 128
N_HEADS_MEM = 4
HEAD_DIM_MEM = D_MODEL // N_HEADS_MEM
N_EXPERTS = 32
TOP_K = 4
D_FF = D_MODEL
SWIGLU_LIMIT = 7.0
SWIGLU_ALPHA = 1.702
LN_EPS = 1e-5
RMS_EPS = 1e-6
DEEPNORM_ALPHA = 2.0 ** 0.25
NEG_INF = -1e30
LOG2E = 1.4426950408889634

LANES = 128
VMEM_LIMIT = 48 * 1024 * 1024

TQ_A = 512
TB_SB = 256
SB_UNDERFLOW = 110.0
TM_TOK = 512
TM_MOE = 256
TM_COMB = 256

F32 = jnp.float32
BF16 = jnp.bfloat16
_NT = (((1,), (1,)), ((), ()))


def _params(n_axes):
    return pltpu.CompilerParams(dimension_semantics=("arbitrary",) * n_axes,
                                vmem_limit_bytes=VMEM_LIMIT)


def _layer_norm(r, g, b):
    mu = jnp.mean(r, axis=-1, keepdims=True)
    d = r - mu
    var = jnp.mean(d * d, axis=-1, keepdims=True)
    return d * lax.rsqrt(var + LN_EPS) * g + b


def _matmul_kernel(x_ref, w_ref, o_ref):
    o_ref[...] = jnp.dot(x_ref[...].astype(BF16), w_ref[...],
                         preferred_element_type=F32).astype(o_ref.dtype)


def _matmul(x, w, out_dtype, tm, tn):
    m, k = x.shape
    n = w.shape[1]
    return pl.pallas_call(
        _matmul_kernel,
        grid=(m // tm, n // tn),
        in_specs=[pl.BlockSpec((tm, k), lambda i, j: (i, 0)),
                  pl.BlockSpec((k, tn), lambda i, j: (0, j))],
        out_specs=pl.BlockSpec((tm, tn), lambda i, j: (i, j)),
        out_shape=jax.ShapeDtypeStruct((m, n), out_dtype),
        compiler_params=_params(2),
        name="matmul",
    )(x, w)


def _chunk_attn_kernel(q_ref, kp_ref, kc_ref, vp_ref, vc_ref, bias_ref, o_ref):
    i = pl.program_id(2)
    w = TQ_A + LEFT
    col = lax.broadcasted_iota(jnp.int32, (TQ_A, w), 1)
    first_valid = jnp.where(i == 0, LEFT, 0)
    for h in range(2):
        sl = slice(h * HEAD_DIM, (h + 1) * HEAD_DIM)
        q = q_ref[:, sl]
        k = jnp.concatenate([kp_ref[:, sl], kc_ref[:, sl]], axis=0)
        v = jnp.concatenate([vp_ref[:, sl], vc_ref[:, sl]], axis=0)
        s = lax.dot_general(q, k, _NT, preferred_element_type=F32) + bias_ref[h]
        s = jnp.where(col >= first_valid, s, NEG_INF)
        m = jnp.max(s, axis=-1, keepdims=True)
        p = jnp.exp(s - m)
        l = jnp.sum(p, axis=-1, keepdims=True)
        o = jnp.dot(p.astype(BF16), v, preferred_element_type=F32) / l
        o_ref[:, sl] = o.astype(o_ref.dtype)


def _chunk_bias(rel_bias):
    w = TQ_A + LEFT
    heads = rel_bias.shape[0]
    m = jnp.arange(2 * w)
    rel = jnp.where(m < w, LEFT - m, LEFT + 2 * w - m)
    diag = rel_bias[:, jnp.clip(rel, -REL_CLIP, REL_CLIP) + REL_CLIP].astype(F32)[:, None, :]
    return pl.pallas_call(
        _chunk_bias_kernel,
        grid=(heads,),
        in_specs=[pl.BlockSpec((1, 1, 2 * w), lambda h: (h, 0, 0))],
        out_specs=pl.BlockSpec((1, TQ_A, w), lambda h: (h, 0, 0)),
        out_shape=jax.ShapeDtypeStruct((heads, TQ_A, w), F32),
        compiler_params=_params(1),
        name="chunk_bias",
    )(diag)


def _chunk_bias_kernel(diag_ref, o_ref):
    w = TQ_A + LEFT
    rolled = pltpu.roll(jnp.broadcast_to(diag_ref[0], (TQ_A, 2 * w)), 0, 1, stride=1, stride_axis=0)
    qc = lax.broadcasted_iota(jnp.int32, (TQ_A, w), 0) // CHUNK
    kc = lax.broadcasted_iota(jnp.int32, (TQ_A, w), 1) // CHUNK
    band = (kc >= qc) & (kc <= qc + LEFT_CHUNKS)
    o_ref[0] = jnp.where(band, rolled[:, :w], NEG_INF)


def _chunk_attention(proj, bias, b, s):
    nq = s // TQ_A
    pairs = WIDTH_A // LANES
    blk = lambda off, prev: pl.BlockSpec(
        (None, TQ_A, LANES),
        (lambda bi, p, i: (bi, jnp.maximum(i - 1, 0), off + p)) if prev
        else (lambda bi, p, i: (bi, i, off + p)))
    return pl.pallas_call(
        _chunk_attn_kernel,
        grid=(b, pairs, nq),
        in_specs=[blk(0, False),
                  blk(pairs, True), blk(pairs, False),
                  blk(2 * pairs, True), blk(2 * pairs, False),
                  pl.BlockSpec((2, TQ_A, TQ_A + LEFT), lambda bi, p, i: (p, 0, 0))],
        out_specs=pl.BlockSpec((None, TQ_A, LANES), lambda bi, p, i: (bi, i, p)),
        out_shape=jax.ShapeDtypeStruct((b, s, WIDTH_A), BF16),
        compiler_params=_params(3),
        name="chunk_attn",
    )(proj, proj, proj, proj, proj, bias)


def _sb_block(q, k_ref, v_ref, sl, j, later, tri, causal):
    t = TB_SB
    start = pl.multiple_of(j * t, t)
    k = k_ref[pl.ds(start, t), sl]
    v = v_ref[pl.ds(start, t), sl]
    z = lax.dot_general(q, k, _NT, preferred_element_type=F32)
    sp = jnp.maximum(z, 0.0) + jnp.log(1.0 + jnp.exp2(jnp.abs(z) * (-LOG2E)))
    if causal is not None:
        sp = jnp.where(causal, sp, 0.0)
    csum = jnp.dot(sp.astype(BF16), tri, preferred_element_type=F32) + later
    wgt = jnp.exp(z - csum)
    if causal is not None:
        wgt = jnp.where(causal, wgt, 0.0)
    pv = jnp.dot(wgt.astype(BF16), v, preferred_element_type=F32)
    return pv, csum[:, 0:1]


def _sb_attn_kernel(q_ref, k_ref, v_ref, o_ref):
    i = pl.program_id(2)
    t = TB_SB
    row = lax.broadcasted_iota(jnp.int32, (t, t), 0)
    col = lax.broadcasted_iota(jnp.int32, (t, t), 1)
    tri = jnp.where(row >= col, 1.0, 0.0).astype(BF16)
    causal = col < row
    sls = [slice(h * HEAD_DIM, (h + 1) * HEAD_DIM) for h in range(2)]
    qs = [q_ref[:, sl] for sl in sls]
    block = functools.partial(_sb_block, tri=tri)

    carry = []
    for h in range(2):
        pv, later = block(qs[h], k_ref, v_ref, sls[h], i, jnp.zeros((t, 1), F32), causal=causal)
        carry += [later, pv]

    def pending(c):
        return (c[0] >= 0) & (jnp.minimum(jnp.min(c[1]), jnp.min(c[3])) < SB_UNDERFLOW)

    def step(c):
        out = [c[0] - 1]
        for h in range(2):
            pv, later = block(qs[h], k_ref, v_ref, sls[h], c[0], c[1 + 2 * h], causal=None)
            out += [later, c[2 + 2 * h] + pv]
        return tuple(out)

    carry = lax.while_loop(pending, step, (i - 1, *carry))
    for h in range(2):
        o_ref[:, sls[h]] = carry[2 + 2 * h].astype(o_ref.dtype)


def _sb_attention(proj, b, s):
    nq = s // TB_SB
    pairs = WIDTH_B // LANES
    base = 3 * WIDTH_A // LANES
    return pl.pallas_call(
        _sb_attn_kernel,
        grid=(b, pairs, nq),
        in_specs=[pl.BlockSpec((None, TB_SB, LANES), lambda bi, p, i: (bi, i, base + p)),
                  pl.BlockSpec((None, s, LANES), lambda bi, p, i: (bi, 0, base + pairs + p)),
                  pl.BlockSpec((None, s, LANES), lambda bi, p, i: (bi, 0, base + 2 * pairs + p))],
        out_specs=pl.BlockSpec((None, TB_SB, LANES), lambda bi, p, i: (bi, i, p)),
        out_shape=jax.ShapeDtypeStruct((b, s, WIDTH_B), BF16),
        compiler_params=_params(3),
        name="sb_attn",
    )(proj, proj, proj)


def _mixer_out_kernel(oa_ref, ob_ref, x_ref, ga_ref, gb_ref, wout_ref, lng_ref, lnb_ref, wq_ref,
                      x1_ref, qm_ref):
    def rms(ref, g_ref):
        a = ref[...].astype(F32)
        return (a * lax.rsqrt(jnp.mean(a * a, axis=-1, keepdims=True) + RMS_EPS) * g_ref[...]).astype(BF16)

    y = jnp.dot(rms(oa_ref, ga_ref), wout_ref[:WIDTH_A, :], preferred_element_type=F32)
    y = y + jnp.dot(rms(ob_ref, gb_ref), wout_ref[WIDTH_A:, :], preferred_element_type=F32)
    x1 = _layer_norm(DEEPNORM_ALPHA * x_ref[...] + y, lng_ref[...], lnb_ref[...])
    x1_ref[...] = x1
    qm_ref[...] = jnp.dot(x1.astype(BF16), wq_ref[...], preferred_element_type=F32).astype(BF16)


def _mixer_out(out_a, out_b, x, g_a, g_b, w_out, ln_g, ln_b, w_q):
    n = x.shape[0]
    tm = TM_TOK
    row = lambda width: pl.BlockSpec((tm, width), lambda i: (i, 0))
    full = lambda r, c: pl.BlockSpec((r, c), lambda i: (0, 0))
    return pl.pallas_call(
        _mixer_out_kernel,
        grid=(n // tm,),
        in_specs=[row(WIDTH_A), row(WIDTH_B), row(D_MODEL), full(1, WIDTH_A), full(1, WIDTH_B),
                  full(MIX_WIDTH, D_MODEL), full(1, D_MODEL), full(1, D_MODEL), full(D_MODEL, D_MODEL)],
        out_specs=[row(D_MODEL), row(D_MODEL)],
        out_shape=[jax.ShapeDtypeStruct((n, D_MODEL), F32), jax.ShapeDtypeStruct((n, D_MODEL), BF16)],
        compiler_params=_params(1),
        name="mixer_out",
    )(out_a, out_b, x, g_a, g_b, w_out, ln_g, ln_b, w_q)


def _mem_attn_kernel(q_ref, kv_ref, x1_ref, wo_ref, lng_ref, lnb_ref, wr_ref, br_ref,
                     x2_ref, idx_ref, gate_ref):
    heads = []
    for h in range(N_HEADS_MEM):
        sl = slice(h * HEAD_DIM_MEM, (h + 1) * HEAD_DIM_MEM)
        q = q_ref[:, sl] * (HEAD_DIM_MEM ** -0.5)
        k = kv_ref[:, sl]
        v = kv_ref[:, D_MODEL + h * HEAD_DIM_MEM:D_MODEL + (h + 1) * HEAD_DIM_MEM]
        s = lax.dot_general(q, k, _NT, preferred_element_type=F32)
        m = jnp.max(s, axis=-1, keepdims=True)
        p = jnp.exp(s - m)
        l = jnp.sum(p, axis=-1, keepdims=True)
        heads.append((jnp.dot(p.astype(BF16), v, preferred_element_type=F32) / l).astype(BF16))
    o = jnp.concatenate(heads, axis=-1)
    y = jnp.dot(o, wo_ref[...], preferred_element_type=F32)
    x2 = _layer_norm(DEEPNORM_ALPHA * x1_ref[...] + y, lng_ref[...], lnb_ref[...])
    x2_ref[...] = x2

    logits = jnp.dot(x2.astype(BF16), wr_ref[...], preferred_element_type=F32) + br_ref[...]
    tm = logits.shape[0]
    e_iota = lax.broadcasted_iota(jnp.int32, (tm, N_EXPERTS), 1)
    lane = lax.broadcasted_iota(jnp.int32, (tm, LANES), 1)
    idx_out = jnp.zeros((tm, LANES), jnp.int32)
    val_out = jnp.zeros((tm, LANES), F32)
    top = None
    denom = jnp.zeros((tm, 1), F32)
    for kk in range(TOP_K):
        m = jnp.max(logits, axis=-1, keepdims=True)
        sel = jnp.min(jnp.where(logits == m, e_iota, N_EXPERTS), axis=-1, keepdims=True)
        if top is None:
            top = m
        e = jnp.exp(m - top)
        denom = denom + e
        idx_out = jnp.where(lane == kk, sel, idx_out)
        val_out = jnp.where(lane == kk, e, val_out)
        logits = jnp.where(e_iota == sel, -jnp.inf, logits)
    idx_ref[...] = idx_out
    gate_ref[...] = val_out / denom


def _mem_attn(qm, kv, x1, w_o, ln_g, ln_b, w_r, b_r, b, s):
    tm = TM_TOK
    nt = s // tm
    mem_len = kv.shape[1]
    row = lambda width: pl.BlockSpec((tm, width), lambda bi, i: (bi * nt + i, 0))
    full = lambda r, c: pl.BlockSpec((r, c), lambda bi, i: (0, 0))
    n = b * s
    return pl.pallas_call(
        _mem_attn_kernel,
        grid=(b, nt),
        in_specs=[row(D_MODEL), pl.BlockSpec((None, mem_len, 2 * D_MODEL), lambda bi, i: (bi, 0, 0)),
                  row(D_MODEL), full(D_MODEL, D_MODEL), full(1, D_MODEL), full(1, D_MODEL),
                  full(D_MODEL, N_EXPERTS), full(1, N_EXPERTS)],
        out_specs=[row(D_MODEL), row(LANES), row(LANES)],
        out_shape=[jax.ShapeDtypeStruct((n, D_MODEL), F32),
                   jax.ShapeDtypeStruct((n, LANES), jnp.int32),
                   jax.ShapeDtypeStruct((n, LANES), F32)],
        compiler_params=_params(2),
        name="mem_attn",
    )(qm, kv, x1, w_o, ln_g, ln_b, w_r, b_r)


def _gather_rows(idx_ref, count, src_hbm, dst_ref, sem):
    def body(r, carry):
        pltpu.make_async_copy(src_hbm.at[pl.ds(idx_ref[0, 0, r], 1)], dst_ref.at[pl.ds(r, 1)], sem).start()
        return carry
    lax.fori_loop(0, count, body, 0, unroll=8)


def _wait_rows(count, src_hbm, dst_ref, sem):
    pltpu.make_async_copy(src_hbm.at[pl.ds(0, count)], dst_ref, sem).wait()


def _moe_kernel(bexp_ref, nused_ref, tok_ref, tok_next_ref, x_hbm, wgu_ref, bgu_ref, wd_ref, bd_ref,
                y_ref, xbuf, sem):
    i = pl.program_id(0)
    nb = pl.num_programs(0)
    slot = i % 2

    @pl.when(i == 0)
    def _():
        _gather_rows(tok_ref, TM_MOE, x_hbm, xbuf.at[0], sem.at[0])

    @pl.when(i + 1 < nb)
    def _():
        _gather_rows(tok_next_ref, TM_MOE, x_hbm, xbuf.at[1 - slot], sem.at[1 - slot])

    _wait_rows(TM_MOE, x_hbm, xbuf.at[slot], sem.at[slot])

    @pl.when(i < nused_ref[0])
    def _():
        x = xbuf[slot].astype(BF16)
        gu = jnp.dot(x, wgu_ref[0], preferred_element_type=F32) + bgu_ref[0]
        gate = jnp.minimum(gu[:, :D_FF], SWIGLU_LIMIT)
        up = jnp.clip(gu[:, D_FF:], -SWIGLU_LIMIT, SWIGLU_LIMIT)
        glu = gate * jax.nn.sigmoid(gate * SWIGLU_ALPHA)
        hid = ((up + 1.0) * glu).astype(BF16)
        y_ref[...] = jnp.dot(hid, wd_ref[0], preferred_element_type=F32) + bd_ref[0]

    @pl.when(i >= nused_ref[0])
    def _():
        y_ref[...] = jnp.zeros_like(y_ref)


def _moe_experts(block_expert, n_used, row_token, x2, w_gu, b_gu, w_d, b_d):
    nb = block_expert.shape[0]
    tok3 = row_token.reshape(nb, 1, TM_MOE)
    smem_blk = lambda f: pl.BlockSpec((1, 1, TM_MOE), f, memory_space=pltpu.SMEM)
    grid_spec = pltpu.PrefetchScalarGridSpec(
        num_scalar_prefetch=2,
        grid=(nb,),
        in_specs=[smem_blk(lambda i, be, nu: (i, 0, 0)),
                  smem_blk(lambda i, be, nu: (jnp.minimum(i + 1, nb - 1), 0, 0)),
                  pl.BlockSpec(memory_space=pl.ANY),
                  pl.BlockSpec((1, D_MODEL, 2 * D_FF), lambda i, be, nu: (be[i], 0, 0)),
                  pl.BlockSpec((1, 1, 2 * D_FF), lambda i, be, nu: (be[i], 0, 0)),
                  pl.BlockSpec((1, D_FF, D_MODEL), lambda i, be, nu: (be[i], 0, 0)),
                  pl.BlockSpec((1, 1, D_MODEL), lambda i, be, nu: (be[i], 0, 0))],
        out_specs=pl.BlockSpec((TM_MOE, D_MODEL), lambda i, be, nu: (i, 0)),
        scratch_shapes=[pltpu.VMEM((2, TM_MOE, D_MODEL), F32), pltpu.SemaphoreType.DMA((2,))],
    )
    return pl.pallas_call(
        _moe_kernel,
        grid_spec=grid_spec,
        out_shape=jax.ShapeDtypeStruct((nb * TM_MOE, D_MODEL), F32),
        compiler_params=_params(1),
        name="moe_experts",
    )(block_expert, n_used, tok3, tok3, x2, w_gu, b_gu, w_d, b_d)


def _combine_kernel(idx_ref, idx_next_ref, y_hbm, x2_ref, gate_ref, lng_ref, lnb_ref, o_ref, ybuf, sem):
    i = pl.program_id(0)
    nt = pl.num_programs(0)
    slot = i % 2
    rows = TOP_K * TM_COMB

    @pl.when(i == 0)
    def _():
        _gather_rows(idx_ref, rows, y_hbm, ybuf.at[0], sem.at[0])

    @pl.when(i + 1 < nt)
    def _():
        _gather_rows(idx_next_ref, rows, y_hbm, ybuf.at[1 - slot], sem.at[1 - slot])

    _wait_rows(rows, y_hbm, ybuf.at[slot], sem.at[slot])

    g = gate_ref[...]
    ff = jnp.zeros((TM_COMB, D_MODEL), F32)
    for kk in range(TOP_K):
        ff = ff + g[:, kk:kk + 1] * ybuf[slot, kk * TM_COMB:(kk + 1) * TM_COMB, :]
    o_ref[...] = _layer_norm(DEEPNORM_ALPHA * x2_ref[...] + ff, lng_ref[...], lnb_ref[...])


def _combine(dest, y_rows, x2, gates, ln_g, ln_b):
    n = x2.shape[0]
    nt = n // TM_COMB
    rows = TOP_K * TM_COMB
    idx3 = dest.reshape(nt, TM_COMB, TOP_K).transpose(0, 2, 1).reshape(nt, 1, rows)
    smem_blk = lambda f: pl.BlockSpec((1, 1, rows), f, memory_space=pltpu.SMEM)
    row = lambda width: pl.BlockSpec((TM_COMB, width), lambda i: (i, 0))
    full = lambda r, c: pl.BlockSpec((r, c), lambda i: (0, 0))
    return pl.pallas_call(
        _combine_kernel,
        grid=(nt,),
        in_specs=[smem_blk(lambda i: (i, 0, 0)),
                  smem_blk(lambda i: (jnp.minimum(i + 1, nt - 1), 0, 0)),
                  pl.BlockSpec(memory_space=pl.ANY),
                  row(D_MODEL), row(LANES), full(1, D_MODEL), full(1, D_MODEL)],
        out_specs=row(D_MODEL),
        out_shape=jax.ShapeDtypeStruct((n, D_MODEL), F32),
        scratch_shapes=[pltpu.VMEM((2, rows, D_MODEL), F32), pltpu.SemaphoreType.DMA((2,))],
        compiler_params=_params(1),
        name="combine",
    )(idx3, idx3, y_rows, x2, gates, ln_g, ln_b)


def _dispatch_plan(top_idx, n):
    n_assign = n * TOP_K
    nb = n_assign // TM_MOE + N_EXPERTS
    flat_e = top_idx.reshape(-1)
    onehot = (flat_e[:, None] == jnp.arange(N_EXPERTS, dtype=jnp.int32)[None, :]).astype(jnp.int32)
    csum = jnp.cumsum(onehot, axis=0)
    counts = csum[-1]
    padded = (counts + TM_MOE - 1) // TM_MOE * TM_MOE
    pad_end = jnp.cumsum(padded)
    pad_start = pad_end - padded
    dest = jnp.sum(onehot * (csum - 1 + pad_start[None, :]), axis=1).astype(jnp.int32)
    row_token = jnp.zeros((nb * TM_MOE,), jnp.int32).at[dest].set(
        jnp.arange(n_assign, dtype=jnp.int32) // TOP_K)
    block_expert = jnp.minimum(
        jnp.searchsorted(pad_end, jnp.arange(nb, dtype=jnp.int32) * TM_MOE, side='right'),
        N_EXPERTS - 1).astype(jnp.int32)
    n_used = (pad_end[-1:] // TM_MOE).astype(jnp.int32)
    return dest, row_token, block_expert, n_used


def kernel(x, mem, w_in, rel_bias, g_group_a, g_group_b, w_out, w_q_mem, w_kv_mem, w_o_mem, w_router, b_router, w_gate_up, b_gate_up, w_down, b_down, ln_g, ln_b):
    b, s, d = x.shape
    n = b * s
    depth = w_in.shape[0]
    xf = x.reshape(n, d)
    for l in range(depth):
        q_scale = jnp.ones((3 * MIX_WIDTH,), F32).at[:WIDTH_A].set(HEAD_DIM ** -0.5)
        q_scale = q_scale.at[3 * WIDTH_A:3 * WIDTH_A + WIDTH_B].set(HEAD_DIM ** -0.5)
        proj = _matmul(xf, (w_in[l] * q_scale).astype(BF16), BF16, TM_TOK, 1024).reshape(b, s, 3 * MIX_WIDTH)
        out_a = _chunk_attention(proj, _chunk_bias(rel_bias[l]), b, s).reshape(n, WIDTH_A)
        out_b = _sb_attention(proj, b, s).reshape(n, WIDTH_B)
        x1, qm = _mixer_out(out_a, out_b, xf, g_group_a[l][None], g_group_b[l][None],
                            w_out[l].astype(BF16), ln_g[l, 0][None], ln_b[l, 0][None],
                            w_q_mem[l].astype(BF16))
        mem_len = mem.shape[1]
        kv = _matmul(mem.reshape(b * mem_len, d), w_kv_mem[l].astype(BF16), BF16, b * mem_len, 1024)
        x2, top_idx, gates = _mem_attn(qm, kv.reshape(b, mem_len, 2 * d), x1, w_o_mem[l].astype(BF16),
                                       ln_g[l, 1][None], ln_b[l, 1][None],
                                       w_router[l].astype(BF16), b_router[l][None], b, s)
        dest, row_token, block_expert, n_used = _dispatch_plan(top_idx[:, :TOP_K], n)
        y_rows = _moe_experts(block_expert, n_used, row_token, x2,
                              w_gate_up[l].astype(BF16), b_gate_up[l][:, None, :],
                              w_down[l].astype(BF16), b_down[l][:, None, :])
        xf = _combine(dest, y_rows, x2, gates, ln_g[l, 2][None], ln_b[l, 2][None])
    return xf.reshape(b, s, d)
```

```python
import functools

import jax
import jax.numpy as jnp
from jax import lax
from jax.experimental import pallas as pl
from jax.experimental.pallas import tpu as pltpu

D_MODEL = 1024
CHUNK = 64
LEFT_CHUNKS = 8
LEFT = LEFT_CHUNKS * CHUNK
HEAD_DIM = 64
N_HEADS_A = 8
N_HEADS_B = 8
WIDTH_A = N_HEADS_A * HEAD_DIM
WIDTH_B = N_HEADS_B * HEAD_DIM
MIX_WIDTH = WIDTH_A + WIDTH_B
REL_CLIP = 128
N_HEADS_MEM = 4
HEAD_DIM_MEM = D_MODEL // N_HEADS_MEM
N_EXPERTS = 32
TOP_K = 4
D_FF = D_MODEL
SWIGLU_LIMIT = 7.0
SWIGLU_ALPHA = 1.702
LN_EPS = 1e-5
RMS_EPS = 1e-6
DEEPNORM_ALPHA = 2.0 ** 0.25
NEG_INF = -1e30
LOG2E = 1.4426950408889634

LANES = 128
VMEM_LIMIT = 48 * 1024 * 1024

TQ_A = 512
TB_SB = 256
SB_UNDERFLOW = 110.0
TM_TOK = 512
TM_MOE = 256
TM_COMB = 256
TM_RANK = 256

F32 = jnp.float32
BF16 = jnp.bfloat16
_NT = (((1,), (1,)), ((), ()))


def _params(n_axes):
    return pltpu.CompilerParams(dimension_semantics=("arbitrary",) * n_axes,
                                vmem_limit_bytes=VMEM_LIMIT)


def _layer_norm(r, g, b):
    mu = jnp.mean(r, axis=-1, keepdims=True)
    d = r - mu
    var = jnp.mean(d * d, axis=-1, keepdims=True)
    return d * lax.rsqrt(var + LN_EPS) * g + b


def _matmul_kernel(x_ref, w_ref, o_ref):
    o_ref[...] = jnp.dot(x_ref[...].astype(BF16), w_ref[...],
                         preferred_element_type=F32).astype(o_ref.dtype)


def _matmul(x, w, out_dtype, tm, tn):
    m, k = x.shape
    n = w.shape[1]
    return pl.pallas_call(
        _matmul_kernel,
        grid=(m // tm, n // tn),
        in_specs=[pl.BlockSpec((tm, k), lambda i, j: (i, 0)),
                  pl.BlockSpec((k, tn), lambda i, j: (0, j))],
        out_specs=pl.BlockSpec((tm, tn), lambda i, j: (i, j)),
        out_shape=jax.ShapeDtypeStruct((m, n), out_dtype),
        compiler_params=_params(2),
        name="matmul",
    )(x, w)


def _chunk_attn_kernel(q_ref, kp_ref, kc_ref, vp_ref, vc_ref, bias_ref, o_ref):
    i = pl.program_id(2)
    w = TQ_A + LEFT
    col = lax.broadcasted_iota(jnp.int32, (TQ_A, w), 1)
    first_valid = jnp.where(i == 0, LEFT, 0)
    for h in range(2):
        sl = slice(h * HEAD_DIM, (h + 1) * HEAD_DIM)
        q = q_ref[:, sl]
        k = jnp.concatenate([kp_ref[:, sl], kc_ref[:, sl]], axis=0)
        v = jnp.concatenate([vp_ref[:, sl], vc_ref[:, sl]], axis=0)
        s = lax.dot_general(q, k, _NT, preferred_element_type=F32) + bias_ref[h]
        s = jnp.where(col >= first_valid, s, NEG_INF)
        m = jnp.max(s, axis=-1, keepdims=True)
        p = jnp.exp(s - m)
        l = jnp.sum(p, axis=-1, keepdims=True)
        o = jnp.dot(p.astype(BF16), v, preferred_element_type=F32) / l
        o_ref[:, sl] = o.astype(o_ref.dtype)


def _chunk_bias(rel_bias):
    w = TQ_A + LEFT
    heads = rel_bias.shape[0]
    m = jnp.arange(2 * w)
    rel = jnp.where(m < w, LEFT - m, LEFT + 2 * w - m)
    diag = rel_bias[:, jnp.clip(rel, -REL_CLIP, REL_CLIP) + REL_CLIP].astype(F32)[:, None, :]
    return pl.pallas_call(
        _chunk_bias_kernel,
        grid=(heads,),
        in_specs=[pl.BlockSpec((1, 1, 2 * w), lambda h: (h, 0, 0))],
        out_specs=pl.BlockSpec((1, TQ_A, w), lambda h: (h, 0, 0)),
        out_shape=jax.ShapeDtypeStruct((heads, TQ_A, w), F32),
        compiler_params=_params(1),
        name="chunk_bias",
    )(diag)


def _chunk_bias_kernel(diag_ref, o_ref):
    w = TQ_A + LEFT
    rolled = pltpu.roll(jnp.broadcast_to(diag_ref[0], (TQ_A, 2 * w)), 0, 1, stride=1, stride_axis=0)
    qc = lax.broadcasted_iota(jnp.int32, (TQ_A, w), 0) // CHUNK
    kc = lax.broadcasted_iota(jnp.int32, (TQ_A, w), 1) // CHUNK
    band = (kc >= qc) & (kc <= qc + LEFT_CHUNKS)
    o_ref[0] = jnp.where(band, rolled[:, :w], NEG_INF)


def _chunk_attention(proj, bias, b, s):
    nq = s // TQ_A
    pairs = WIDTH_A // LANES
    blk = lambda off, prev: pl.BlockSpec(
        (None, TQ_A, LANES),
        (lambda bi, p, i: (bi, jnp.maximum(i - 1, 0), off + p)) if prev
        else (lambda bi, p, i: (bi, i, off + p)))
    return pl.pallas_call(
        _chunk_attn_kernel,
        grid=(b, pairs, nq),
        in_specs=[blk(0, False),
                  blk(pairs, True), blk(pairs, False),
                  blk(2 * pairs, True), blk(2 * pairs, False),
                  pl.BlockSpec((2, TQ_A, TQ_A + LEFT), lambda bi, p, i: (p, 0, 0))],
        out_specs=pl.BlockSpec((None, TQ_A, LANES), lambda bi, p, i: (bi, i, p)),
        out_shape=jax.ShapeDtypeStruct((b, s, WIDTH_A), BF16),
        compiler_params=_params(3),
        name="chunk_attn",
    )(proj, proj, proj, proj, proj, bias)


def _sb_block(q, k_ref, v_ref, sl, j, later, tri, causal):
    t = TB_SB
    start = pl.multiple_of(j * t, t)
    k = k_ref[pl.ds(start, t), sl]
    v = v_ref[pl.ds(start, t), sl]
    z = lax.dot_general(q, k, _NT, preferred_element_type=F32)
    sp = jnp.maximum(z, 0.0) + jnp.log(1.0 + jnp.exp2(jnp.abs(z) * (-LOG2E)))
    if causal is not None:
        sp = jnp.where(causal, sp, 0.0)
    csum = jnp.dot(sp.astype(BF16), tri, preferred_element_type=F32) + later
    wgt = jnp.exp(z - csum)
    if causal is not None:
        wgt = jnp.where(causal, wgt, 0.0)
    pv = jnp.dot(wgt.astype(BF16), v, preferred_element_type=F32)
    return pv, csum[:, 0:1]


def _sb_attn_kernel(q_ref, k_ref, v_ref, o_ref):
    i = pl.program_id(2)
    t = TB_SB
    row = lax.broadcasted_iota(jnp.int32, (t, t), 0)
    col = lax.broadcasted_iota(jnp.int32, (t, t), 1)
    tri = jnp.where(row >= col, 1.0, 0.0).astype(BF16)
    causal = col < row
    sls = [slice(h * HEAD_DIM, (h + 1) * HEAD_DIM) for h in range(2)]
    qs = [q_ref[:, sl] for sl in sls]
    block = functools.partial(_sb_block, tri=tri)

    carry = []
    for h in range(2):
        pv, later = block(qs[h], k_ref, v_ref, sls[h], i, jnp.zeros((t, 1), F32), causal=causal)
        carry += [later, pv]

    def pending(c):
        return (c[0] >= 0) & (jnp.minimum(jnp.min(c[1]), jnp.min(c[3])) < SB_UNDERFLOW)

    def step(c):
        out = [c[0] - 1]
        for h in range(2):
            pv, later = block(qs[h], k_ref, v_ref, sls[h], c[0], c[1 + 2 * h], causal=None)
            out += [later, c[2 + 2 * h] + pv]
        return tuple(out)

    carry = lax.while_loop(pending, step, (i - 1, *carry))
    for h in range(2):
        o_ref[:, sls[h]] = carry[2 + 2 * h].astype(o_ref.dtype)


def _sb_attention(proj, b, s):
    nq = s // TB_SB
    pairs = WIDTH_B // LANES
    base = 3 * WIDTH_A // LANES
    return pl.pallas_call(
        _sb_attn_kernel,
        grid=(b, pairs, nq),
        in_specs=[pl.BlockSpec((None, TB_SB, LANES), lambda bi, p, i: (bi, i, base + p)),
                  pl.BlockSpec((None, s, LANES), lambda bi, p, i: (bi, 0, base + pairs + p)),
                  pl.BlockSpec((None, s, LANES), lambda bi, p, i: (bi, 0, base + 2 * pairs + p))],
        out_specs=pl.BlockSpec((None, TB_SB, LANES), lambda bi, p, i: (bi, i, p)),
        out_shape=jax.ShapeDtypeStruct((b, s, WIDTH_B), BF16),
        compiler_params=_params(3),
        name="sb_attn",
    )(proj, proj, proj)


def _mixer_out_kernel(oa_ref, ob_ref, x_ref, ga_ref, gb_ref, wout_ref, lng_ref, lnb_ref, wq_ref,
                      x1_ref, qm_ref):
    def rms(ref, g_ref):
        a = ref[...].astype(F32)
        return (a * lax.rsqrt(jnp.mean(a * a, axis=-1, keepdims=True) + RMS_EPS) * g_ref[...]).astype(BF16)

    y = jnp.dot(rms(oa_ref, ga_ref), wout_ref[:WIDTH_A, :], preferred_element_type=F32)
    y = y + jnp.dot(rms(ob_ref, gb_ref), wout_ref[WIDTH_A:, :], preferred_element_type=F32)
    x1 = _layer_norm(DEEPNORM_ALPHA * x_ref[...] + y, lng_ref[...], lnb_ref[...])
    x1_ref[...] = x1
    qm_ref[...] = jnp.dot(x1.astype(BF16), wq_ref[...], preferred_element_type=F32).astype(BF16)


def _mixer_out(out_a, out_b, x, g_a, g_b, w_out, ln_g, ln_b, w_q):
    n = x.shape[0]
    tm = TM_TOK
    row = lambda width: pl.BlockSpec((tm, width), lambda i: (i, 0))
    full = lambda r, c: pl.BlockSpec((r, c), lambda i: (0, 0))
    return pl.pallas_call(
        _mixer_out_kernel,
        grid=(n // tm,),
        in_specs=[row(WIDTH_A), row(WIDTH_B), row(D_MODEL), full(1, WIDTH_A), full(1, WIDTH_B),
                  full(MIX_WIDTH, D_MODEL), full(1, D_MODEL), full(1, D_MODEL), full(D_MODEL, D_MODEL)],
        out_specs=[row(D_MODEL), row(D_MODEL)],
        out_shape=[jax.ShapeDtypeStruct((n, D_MODEL), F32), jax.ShapeDtypeStruct((n, D_MODEL), BF16)],
        compiler_params=_params(1),
        name="mixer_out",
    )(out_a, out_b, x, g_a, g_b, w_out, ln_g, ln_b, w_q)


def _mem_attn_kernel(q_ref, kv_ref, x1_ref, wo_ref, lng_ref, lnb_ref, wr_ref, br_ref,
                     x2_ref, idx_ref, gate_ref):
    heads = []
    for h in range(N_HEADS_MEM):
        sl = slice(h * HEAD_DIM_MEM, (h + 1) * HEAD_DIM_MEM)
        q = q_ref[:, sl] * (HEAD_DIM_MEM ** -0.5)
        k = kv_ref[:, sl]
        v = kv_ref[:, D_MODEL + h * HEAD_DIM_MEM:D_MODEL + (h + 1) * HEAD_DIM_MEM]
        s = lax.dot_general(q, k, _NT, preferred_element_type=F32)
        m = jnp.max(s, axis=-1, keepdims=True)
        p = jnp.exp(s - m)
        l = jnp.sum(p, axis=-1, keepdims=True)
        heads.append((jnp.dot(p.astype(BF16), v, preferred_element_type=F32) / l).astype(BF16))
    o = jnp.concatenate(heads, axis=-1)
    y = jnp.dot(o, wo_ref[...], preferred_element_type=F32)
    x2 = _layer_norm(DEEPNORM_ALPHA * x1_ref[...] + y, lng_ref[...], lnb_ref[...])
    x2_ref[...] = x2

    logits = jnp.dot(x2.astype(BF16), wr_ref[...], preferred_element_type=F32) + br_ref[...]
    tm = logits.shape[0]
    e_iota = lax.broadcasted_iota(jnp.int32, (tm, N_EXPERTS), 1)
    lane = lax.broadcasted_iota(jnp.int32, (tm, LANES), 1)
    idx_out = jnp.zeros((tm, LANES), jnp.int32)
    val_out = jnp.zeros((tm, LANES), F32)
    top = None
    denom = jnp.zeros((tm, 1), F32)
    for kk in range(TOP_K):
        m = jnp.max(logits, axis=-1, keepdims=True)
        sel = jnp.min(jnp.where(logits == m, e_iota, N_EXPERTS), axis=-1, keepdims=True)
        if top is None:
            top = m
        e = jnp.exp(m - top)
        denom = denom + e
        idx_out = jnp.where(lane == kk, sel, idx_out)
        val_out = jnp.where(lane == kk, e, val_out)
        logits = jnp.where(e_iota == sel, -jnp.inf, logits)
    idx_ref[...] = idx_out
    gate_ref[...] = val_out / denom


def _mem_attn(qm, kv, x1, w_o, ln_g, ln_b, w_r, b_r, b, s):
    tm = TM_TOK
    nt = s // tm
    mem_len = kv.shape[1]
    row = lambda width: pl.BlockSpec((tm, width), lambda bi, i: (bi * nt + i, 0))
    full = lambda r, c: pl.BlockSpec((r, c), lambda bi, i: (0, 0))
    n = b * s
    return pl.pallas_call(
        _mem_attn_kernel,
        grid=(b, nt),
        in_specs=[row(D_MODEL), pl.BlockSpec((None, mem_len, 2 * D_MODEL), lambda bi, i: (bi, 0, 0)),
                  row(D_MODEL), full(D_MODEL, D_MODEL), full(1, D_MODEL), full(1, D_MODEL),
                  full(D_MODEL, N_EXPERTS), full(1, N_EXPERTS)],
        out_specs=[row(D_MODEL), row(LANES), row(LANES)],
        out_shape=[jax.ShapeDtypeStruct((n, D_MODEL), F32),
                   jax.ShapeDtypeStruct((n, LANES), jnp.int32),
                   jax.ShapeDtypeStruct((n, LANES), F32)],
        compiler_params=_params(2),
        name="mem_attn",
    )(qm, kv, x1, w_o, ln_g, ln_b, w_r, b_r)


def _gather_rows(idx_ref, count, src_hbm, dst_ref, sem):
    for r in range(count):
        pltpu.make_async_copy(src_hbm.at[pl.ds(idx_ref[0, 0, r], 1)], dst_ref.at[pl.ds(r, 1)], sem).start()


def _wait_rows(count, src_hbm, dst_ref, sem):
    pltpu.make_async_copy(src_hbm.at[pl.ds(0, count)], dst_ref, sem).wait()


def _moe_kernel(bexp_ref, nused_ref, tok_ref, tok_next_ref, x_hbm, wgu_ref, bgu_ref, wd_ref, bd_ref,
                y_ref, xbuf, sem):
    i = pl.program_id(0)
    slot = i % 2
    n_used = nused_ref[0]

    @pl.when(i == 0)
    def _():
        _gather_rows(tok_ref, TM_MOE, x_hbm, xbuf.at[0], sem.at[0])

    @pl.when(i <= n_used)
    def _():
        _wait_rows(TM_MOE, x_hbm, xbuf.at[slot], sem.at[slot])

    @pl.when(i < n_used)
    def _():
        x = xbuf[slot].astype(BF16)
        _gather_rows(tok_next_ref, TM_MOE, x_hbm, xbuf.at[1 - slot], sem.at[1 - slot])
        gu = jnp.dot(x, wgu_ref[0], preferred_element_type=F32) + bgu_ref[0]
        gate = jnp.minimum(gu[:, :D_FF], SWIGLU_LIMIT)
        up = jnp.clip(gu[:, D_FF:], -SWIGLU_LIMIT, SWIGLU_LIMIT)
        glu = gate * jax.nn.sigmoid(gate * SWIGLU_ALPHA)
        hid = ((up + 1.0) * glu).astype(BF16)
        y_ref[...] = jnp.dot(hid, wd_ref[0], preferred_element_type=F32) + bd_ref[0]

    @pl.when(i >= n_used)
    def _():
        y_ref[...] = jnp.zeros_like(y_ref)


def _moe_experts(block_expert, n_used, row_token, x2, w_gu, b_gu, w_d, b_d):
    nb = block_expert.shape[0]
    tok3 = row_token.reshape(nb, 1, TM_MOE)
    smem_blk = lambda f: pl.BlockSpec((1, 1, TM_MOE), f, memory_space=pltpu.SMEM)
    grid_spec = pltpu.PrefetchScalarGridSpec(
        num_scalar_prefetch=2,
        grid=(nb,),
        in_specs=[smem_blk(lambda i, be, nu: (i, 0, 0)),
                  smem_blk(lambda i, be, nu: (jnp.minimum(i + 1, nb - 1), 0, 0)),
                  pl.BlockSpec(memory_space=pl.ANY),
                  pl.BlockSpec((1, D_MODEL, 2 * D_FF), lambda i, be, nu: (be[i], 0, 0)),
                  pl.BlockSpec((1, 1, 2 * D_FF), lambda i, be, nu: (be[i], 0, 0)),
                  pl.BlockSpec((1, D_FF, D_MODEL), lambda i, be, nu: (be[i], 0, 0)),
                  pl.BlockSpec((1, 1, D_MODEL), lambda i, be, nu: (be[i], 0, 0))],
        out_specs=pl.BlockSpec((TM_MOE, D_MODEL), lambda i, be, nu: (i, 0)),
        scratch_shapes=[pltpu.VMEM((2, TM_MOE, D_MODEL), F32), pltpu.SemaphoreType.DMA((2,))],
    )
    return pl.pallas_call(
        _moe_kernel,
        grid_spec=grid_spec,
        out_shape=jax.ShapeDtypeStruct((nb * TM_MOE, D_MODEL), F32),
        compiler_params=_params(1),
        name="moe_experts",
    )(block_expert, n_used, tok3, tok3, x2, w_gu, b_gu, w_d, b_d)


def _combine_kernel(idx_ref, idx_next_ref, y_hbm, x2_ref, gate_ref, lng_ref, lnb_ref, o_ref, ybuf, sem):
    i = pl.program_id(0)
    nt = pl.num_programs(0)
    slot = i % 2
    rows = TOP_K * TM_COMB

    @pl.when(i == 0)
    def _():
        _gather_rows(idx_ref, rows, y_hbm, ybuf.at[0], sem.at[0])

    @pl.when(i + 1 < nt)
    def _():
        _gather_rows(idx_next_ref, rows, y_hbm, ybuf.at[1 - slot], sem.at[1 - slot])

    _wait_rows(rows, y_hbm, ybuf.at[slot], sem.at[slot])

    g = gate_ref[...]
    ff = jnp.zeros((TM_COMB, D_MODEL), F32)
    for kk in range(TOP_K):
        ff = ff + g[:, kk:kk + 1] * ybuf[slot, kk * TM_COMB:(kk + 1) * TM_COMB, :]
    o_ref[...] = _layer_norm(DEEPNORM_ALPHA * x2_ref[...] + ff, lng_ref[...], lnb_ref[...])


def _combine(dest, y_rows, x2, gates, ln_g, ln_b):
    n = x2.shape[0]
    nt = n // TM_COMB
    rows = TOP_K * TM_COMB
    idx3 = dest.reshape(nt, TM_COMB, TOP_K).transpose(0, 2, 1).reshape(nt, 1, rows)
    smem_blk = lambda f: pl.BlockSpec((1, 1, rows), f, memory_space=pltpu.SMEM)
    row = lambda width: pl.BlockSpec((TM_COMB, width), lambda i: (i, 0))
    full = lambda r, c: pl.BlockSpec((r, c), lambda i: (0, 0))
    return pl.pallas_call(
        _combine_kernel,
        grid=(nt,),
        in_specs=[smem_blk(lambda i: (i, 0, 0)),
                  smem_blk(lambda i: (jnp.minimum(i + 1, nt - 1), 0, 0)),
                  pl.BlockSpec(memory_space=pl.ANY),
                  row(D_MODEL), row(LANES), full(1, D_MODEL), full(1, D_MODEL)],
        out_specs=row(D_MODEL),
        out_shape=jax.ShapeDtypeStruct((n, D_MODEL), F32),
        scratch_shapes=[pltpu.VMEM((2, rows, D_MODEL), F32), pltpu.SemaphoreType.DMA((2,))],
        compiler_params=_params(1),
        name="combine",
    )(idx3, idx3, y_rows, x2, gates, ln_g, ln_b)


def _rank_kernel(idx_ref, dest_ref, counts_ref, running):
    p = pl.program_id(0)
    j = pl.program_id(1)
    t = TM_RANK
    idx = idx_ref[...]
    e_iota = lax.broadcasted_iota(jnp.int32, (t, LANES), 1)
    hot = [idx[:, k:k + 1] == e_iota for k in range(TOP_K)]
    chosen = sum(jnp.where(h, 1.0, 0.0) for h in hot)
    tile_counts = jnp.sum(chosen, axis=0, keepdims=True)

    @pl.when((p == 0) & (j == 0))
    def _():
        running[...] = jnp.zeros_like(running)

    @pl.when(p == 0)
    def _():
        running[...] += tile_counts
        dest_ref[...] = jnp.zeros_like(dest_ref)

    @pl.when((p == 1) & (j == 0))
    def _():
        counts = running[...]
        counts_ref[...] = counts
        padded = jnp.ceil(counts * (1.0 / TM_MOE)) * TM_MOE
        lane = lax.broadcasted_iota(jnp.int32, (1, LANES), 1)
        scan = padded
        shift = 1
        while shift < N_EXPERTS:
            scan = scan + jnp.where(lane >= shift, pltpu.roll(scan, shift, 1), 0.0)
            shift *= 2
        running[...] = scan - padded

    @pl.when(p == 1)
    def _():
        r_iota = lax.broadcasted_iota(jnp.int32, (t, t), 0)
        c_iota = lax.broadcasted_iota(jnp.int32, (t, t), 1)
        earlier = jnp.where(c_iota < r_iota, 1.0, 0.0).astype(BF16)
        base = jnp.dot(earlier, chosen.astype(BF16), preferred_element_type=F32) + running[...]
        out = jnp.zeros((t, LANES), jnp.int32)
        for k in range(TOP_K):
            row = jnp.sum(jnp.where(hot[k], base, 0.0), axis=-1, keepdims=True).astype(jnp.int32)
            out = jnp.where(e_iota == k, row, out)
        dest_ref[...] = out
        running[...] += tile_counts


def _dispatch_plan(top_idx, n):
    n_assign = n * TOP_K
    nb = n_assign // TM_MOE + N_EXPERTS
    dest, counts = pl.pallas_call(
        _rank_kernel,
        grid=(2, n // TM_RANK),
        in_specs=[pl.BlockSpec((TM_RANK, LANES), lambda p, j: (j, 0))],
        out_specs=[pl.BlockSpec((TM_RANK, LANES), lambda p, j: (p * j, 0)),
                   pl.BlockSpec((1, LANES), lambda p, j: (0, 0))],
        out_shape=[jax.ShapeDtypeStruct((n, LANES), jnp.int32), jax.ShapeDtypeStruct((1, LANES), F32)],
        scratch_shapes=[pltpu.VMEM((1, LANES), F32)],
        compiler_params=_params(2),
        name="expert_rank",
    )(top_idx)
    dest = dest[:, :TOP_K].reshape(-1)
    counts = counts[0, :N_EXPERTS].astype(jnp.int32)
    padded = (counts + TM_MOE - 1) // TM_MOE * TM_MOE
    pad_end = jnp.cumsum(padded)
    row_token = jnp.zeros((nb * TM_MOE,), jnp.int32).at[dest].set(
        jnp.arange(n_assign, dtype=jnp.int32) // TOP_K, unique_indices=True, mode='promise_in_bounds')
    block_expert = jnp.minimum(
        jnp.searchsorted(pad_end, jnp.arange(nb, dtype=jnp.int32) * TM_MOE, side='right'),
        N_EXPERTS - 1).astype(jnp.int32)
    n_used = (pad_end[-1:] // TM_MOE).astype(jnp.int32)
    return dest, row_token, block_expert, n_used


def kernel(x, mem, w_in, rel_bias, g_group_a, g_group_b, w_out, w_q_mem, w_kv_mem, w_o_mem, w_router, b_router, w_gate_up, b_gate_up, w_down, b_down, ln_g, ln_b):
    b, s, d = x.shape
    n = b * s
    depth = w_in.shape[0]
    xf = x.reshape(n, d)
    for l in range(depth):
        q_scale = jnp.ones((3 * MIX_WIDTH,), F32).at[:WIDTH_A].set(HEAD_DIM ** -0.5)
        q_scale = q_scale.at[3 * WIDTH_A:3 * WIDTH_A + WIDTH_B].set(HEAD_DIM ** -0.5)
        proj = _matmul(xf, (w_in[l] * q_scale).astype(BF16), BF16, TM_TOK, 1024).reshape(b, s, 3 * MIX_WIDTH)
        out_a = _chunk_attention(proj, _chunk_bias(rel_bias[l]), b, s).reshape(n, WIDTH_A)
        out_b = _sb_attention(proj, b, s).reshape(n, WIDTH_B)
        x1, qm = _mixer_out(out_a, out_b, xf, g_group_a[l][None], g_group_b[l][None],
                            w_out[l].astype(BF16), ln_g[l, 0][None], ln_b[l, 0][None],
                            w_q_mem[l].astype(BF16))
        mem_len = mem.shape[1]
        kv = _matmul(mem.reshape(b * mem_len, d), w_kv_mem[l].astype(BF16), BF16, b * mem_len, 1024)
        x2, top_idx, gates = _mem_attn(qm, kv.reshape(b, mem_len, 2 * d), x1, w_o_mem[l].astype(BF16),
                                       ln_g[l, 1][None], ln_b[l, 1][None],
                                       w_router[l].astype(BF16), b_router[l][None], b, s)
        dest, row_token, block_expert, n_used = _dispatch_plan(top_idx, n)
        y_rows = _moe_experts(block_expert, n_used, row_token, x2,
                              w_gate_up[l].astype(BF16), b_gate_up[l][:, None, :],
                              w_down[l].astype(BF16), b_down[l][:, None, :])
        xf = _combine(dest, y_rows, x2, gates, ln_g[l, 2][None], ln_b[l, 2][None])
    return xf.reshape(b, s, d)
```

```python
import functools

import jax
import jax.numpy as jnp
from jax import lax
from jax.experimental import pallas as pl
from jax.experimental.pallas import tpu as pltpu

D_MODEL = 1024
CHUNK = 64
LEFT_CHUNKS = 8
LEFT = LEFT_CHUNKS * CHUNK
HEAD_DIM = 64
N_HEADS_A = 8
N_HEADS_B = 8
WIDTH_A = N_HEADS_A * HEAD_DIM
WIDTH_B = N_HEADS_B * HEAD_DIM
MIX_WIDTH = WIDTH_A + WIDTH_B
REL_CLIP = 128
N_HEADS_MEM = 4
HEAD_DIM_MEM = D_MODEL // N_HEADS_MEM
N_EXPERTS = 32
TOP_K = 4
D_FF = D_MODEL
SWIGLU_LIMIT = 7.0
SWIGLU_ALPHA = 1.702
LN_EPS = 1e-5
RMS_EPS = 1e-6
DEEPNORM_ALPHA = 2.0 ** 0.25
NEG_INF = -1e30
LOG2E = 1.4426950408889634

LANES = 128
LANE_BLOCKS = D_MODEL // LANES
VMEM_LIMIT = 48 * 1024 * 1024

TQ_A = 512
TB_SB = 256
SB_UNDERFLOW = 110.0
TM_TOK = 512
TM_MOE = 256
TM_COMB = 256
TM_RANK = 512

F32 = jnp.float32
BF16 = jnp.bfloat16
_NT = (((1,), (1,)), ((), ()))


def _params(n_axes):
    return pltpu.CompilerParams(dimension_semantics=("arbitrary",) * n_axes,
                                vmem_limit_bytes=VMEM_LIMIT)


def _store_token_major(ref, value):
    rows = value.shape[0]
    for c in range(LANE_BLOCKS):
        ref[pl.ds(c, rows, stride=LANE_BLOCKS), :] = value[:, c * LANES:(c + 1) * LANES]


def _load_token_major(ref, first_row, rows):
    return jnp.concatenate(
        [ref[pl.ds(first_row * LANE_BLOCKS + c, rows, stride=LANE_BLOCKS), :] for c in range(LANE_BLOCKS)],
        axis=1)


def _layer_norm(r, g, b):
    mu = jnp.mean(r, axis=-1, keepdims=True)
    d = r - mu
    var = jnp.mean(d * d, axis=-1, keepdims=True)
    return d * lax.rsqrt(var + LN_EPS) * g + b


def _matmul_kernel(x_ref, w_ref, o_ref):
    o_ref[...] = jnp.dot(x_ref[...].astype(BF16), w_ref[...],
                         preferred_element_type=F32).astype(o_ref.dtype)


def _matmul(x, w, out_dtype, tm, tn):
    m, k = x.shape
    n = w.shape[1]
    return pl.pallas_call(
        _matmul_kernel,
        grid=(m // tm, n // tn),
        in_specs=[pl.BlockSpec((tm, k), lambda i, j: (i, 0)),
                  pl.BlockSpec((k, tn), lambda i, j: (0, j))],
        out_specs=pl.BlockSpec((tm, tn), lambda i, j: (i, j)),
        out_shape=jax.ShapeDtypeStruct((m, n), out_dtype),
        compiler_params=_params(2),
        name="matmul",
    )(x, w)


def _chunk_attn_kernel(q_ref, kp_ref, kc_ref, vp_ref, vc_ref, bias_ref, o_ref):
    i = pl.program_id(2)
    w = TQ_A + LEFT
    col = lax.broadcasted_iota(jnp.int32, (TQ_A, w), 1)
    first_valid = jnp.where(i == 0, LEFT, 0)
    for h in range(2):
        sl = slice(h * HEAD_DIM, (h + 1) * HEAD_DIM)
        q = q_ref[:, sl]
        k = jnp.concatenate([kp_ref[:, sl], kc_ref[:, sl]], axis=0)
        v = jnp.concatenate([vp_ref[:, sl], vc_ref[:, sl]], axis=0)
        s = lax.dot_general(q, k, _NT, preferred_element_type=F32) + bias_ref[h]
        s = jnp.where(col >= first_valid, s, NEG_INF)
        m = jnp.max(s, axis=-1, keepdims=True)
        p = jnp.exp(s - m)
        l = jnp.sum(p, axis=-1, keepdims=True)
        o = jnp.dot(p.astype(BF16), v, preferred_element_type=F32) / l
        o_ref[:, sl] = o.astype(o_ref.dtype)


def _chunk_bias(rel_bias):
    w = TQ_A + LEFT
    heads = rel_bias.shape[0]
    m = jnp.arange(2 * w)
    rel = jnp.where(m < w, LEFT - m, LEFT + 2 * w - m)
    diag = rel_bias[:, jnp.clip(rel, -REL_CLIP, REL_CLIP) + REL_CLIP].astype(F32)[:, None, :]
    return pl.pallas_call(
        _chunk_bias_kernel,
        grid=(heads,),
        in_specs=[pl.BlockSpec((1, 1, 2 * w), lambda h: (h, 0, 0))],
        out_specs=pl.BlockSpec((1, TQ_A, w), lambda h: (h, 0, 0)),
        out_shape=jax.ShapeDtypeStruct((heads, TQ_A, w), F32),
        compiler_params=_params(1),
        name="chunk_bias",
    )(diag)


def _chunk_bias_kernel(diag_ref, o_ref):
    w = TQ_A + LEFT
    rolled = pltpu.roll(jnp.broadcast_to(diag_ref[0], (TQ_A, 2 * w)), 0, 1, stride=1, stride_axis=0)
    qc = lax.broadcasted_iota(jnp.int32, (TQ_A, w), 0) // CHUNK
    kc = lax.broadcasted_iota(jnp.int32, (TQ_A, w), 1) // CHUNK
    band = (kc >= qc) & (kc <= qc + LEFT_CHUNKS)
    o_ref[0] = jnp.where(band, rolled[:, :w], NEG_INF)


def _chunk_attention(proj, bias, b, s):
    nq = s // TQ_A
    pairs = WIDTH_A // LANES
    blk = lambda off, prev: pl.BlockSpec(
        (None, TQ_A, LANES),
        (lambda bi, p, i: (bi, jnp.maximum(i - 1, 0), off + p)) if prev
        else (lambda bi, p, i: (bi, i, off + p)))
    return pl.pallas_call(
        _chunk_attn_kernel,
        grid=(b, pairs, nq),
        in_specs=[blk(0, False),
                  blk(pairs, True), blk(pairs, False),
                  blk(2 * pairs, True), blk(2 * pairs, False),
                  pl.BlockSpec((2, TQ_A, TQ_A + LEFT), lambda bi, p, i: (p, 0, 0))],
        out_specs=pl.BlockSpec((None, TQ_A, LANES), lambda bi, p, i: (bi, i, p)),
        out_shape=jax.ShapeDtypeStruct((b, s, WIDTH_A), BF16),
        compiler_params=_params(3),
        name="chunk_attn",
    )(proj, proj, proj, proj, proj, bias)


def _sb_block(q, k_ref, v_ref, sl, j, later, tri, causal):
    t = TB_SB
    start = pl.multiple_of(j * t, t)
    k = k_ref[pl.ds(start, t), sl]
    v = v_ref[pl.ds(start, t), sl]
    z = lax.dot_general(q, k, _NT, preferred_element_type=F32)
    sp = jnp.maximum(z, 0.0) + jnp.log(1.0 + jnp.exp2(jnp.abs(z) * (-LOG2E)))
    if causal is not None:
        sp = jnp.where(causal, sp, 0.0)
    csum = jnp.dot(sp.astype(BF16), tri, preferred_element_type=F32) + later
    wgt = jnp.exp(z - csum)
    if causal is not None:
        wgt = jnp.where(causal, wgt, 0.0)
    pv = jnp.dot(wgt.astype(BF16), v, preferred_element_type=F32)
    return pv, csum[:, 0:1]


def _sb_attn_kernel(q_ref, k_ref, v_ref, o_ref):
    i = pl.program_id(2)
    t = TB_SB
    row = lax.broadcasted_iota(jnp.int32, (t, t), 0)
    col = lax.broadcasted_iota(jnp.int32, (t, t), 1)
    tri = jnp.where(row >= col, 1.0, 0.0).astype(BF16)
    causal = col < row
    sls = [slice(h * HEAD_DIM, (h + 1) * HEAD_DIM) for h in range(2)]
    qs = [q_ref[:, sl] for sl in sls]
    block = functools.partial(_sb_block, tri=tri)

    carry = []
    for h in range(2):
        pv, later = block(qs[h], k_ref, v_ref, sls[h], i, jnp.zeros((t, 1), F32), causal=causal)
        carry += [later, pv]

    def pending(c):
        return (c[0] >= 0) & (jnp.minimum(jnp.min(c[1]), jnp.min(c[3])) < SB_UNDERFLOW)

    def step(c):
        out = [c[0] - 1]
        for h in range(2):
            pv, later = block(qs[h], k_ref, v_ref, sls[h], c[0], c[1 + 2 * h], causal=None)
            out += [later, c[2 + 2 * h] + pv]
        return tuple(out)

    carry = lax.while_loop(pending, step, (i - 1, *carry))
    for h in range(2):
        o_ref[:, sls[h]] = carry[2 + 2 * h].astype(o_ref.dtype)


def _sb_attention(proj, b, s):
    nq = s // TB_SB
    pairs = WIDTH_B // LANES
    base = 3 * WIDTH_A // LANES
    return pl.pallas_call(
        _sb_attn_kernel,
        grid=(b, pairs, nq),
        in_specs=[pl.BlockSpec((None, TB_SB, LANES), lambda bi, p, i: (bi, i, base + p)),
                  pl.BlockSpec((None, s, LANES), lambda bi, p, i: (bi, 0, base + pairs + p)),
                  pl.BlockSpec((None, s, LANES), lambda bi, p, i: (bi, 0, base + 2 * pairs + p))],
        out_specs=pl.BlockSpec((None, TB_SB, LANES), lambda bi, p, i: (bi, i, p)),
        out_shape=jax.ShapeDtypeStruct((b, s, WIDTH_B), BF16),
        compiler_params=_params(3),
        name="sb_attn",
    )(proj, proj, proj)


def _mixer_out_kernel(oa_ref, ob_ref, x_ref, ga_ref, gb_ref, wout_ref, lng_ref, lnb_ref, wq_ref,
                      x1_ref, qm_ref):
    def rms(ref, g_ref):
        a = ref[...].astype(F32)
        return (a * lax.rsqrt(jnp.mean(a * a, axis=-1, keepdims=True) + RMS_EPS) * g_ref[...]).astype(BF16)

    y = jnp.dot(rms(oa_ref, ga_ref), wout_ref[:WIDTH_A, :], preferred_element_type=F32)
    y = y + jnp.dot(rms(ob_ref, gb_ref), wout_ref[WIDTH_A:, :], preferred_element_type=F32)
    x1 = _layer_norm(DEEPNORM_ALPHA * x_ref[...] + y, lng_ref[...], lnb_ref[...])
    x1_ref[...] = x1
    qm_ref[...] = jnp.dot(x1.astype(BF16), wq_ref[...], preferred_element_type=F32).astype(BF16)


def _mixer_out(out_a, out_b, x, g_a, g_b, w_out, ln_g, ln_b, w_q):
    n = x.shape[0]
    tm = TM_TOK
    row = lambda width: pl.BlockSpec((tm, width), lambda i: (i, 0))
    full = lambda r, c: pl.BlockSpec((r, c), lambda i: (0, 0))
    return pl.pallas_call(
        _mixer_out_kernel,
        grid=(n // tm,),
        in_specs=[row(WIDTH_A), row(WIDTH_B), row(D_MODEL), full(1, WIDTH_A), full(1, WIDTH_B),
                  full(MIX_WIDTH, D_MODEL), full(1, D_MODEL), full(1, D_MODEL), full(D_MODEL, D_MODEL)],
        out_specs=[row(D_MODEL), row(D_MODEL)],
        out_shape=[jax.ShapeDtypeStruct((n, D_MODEL), F32), jax.ShapeDtypeStruct((n, D_MODEL), BF16)],
        compiler_params=_params(1),
        name="mixer_out",
    )(out_a, out_b, x, g_a, g_b, w_out, ln_g, ln_b, w_q)


def _mem_attn_kernel(q_ref, kv_ref, x1_ref, wo_ref, lng_ref, lnb_ref, wr_ref, br_ref,
                     x2_ref, idx_ref, gate_ref):
    heads = []
    for h in range(N_HEADS_MEM):
        sl = slice(h * HEAD_DIM_MEM, (h + 1) * HEAD_DIM_MEM)
        q = q_ref[:, sl] * (HEAD_DIM_MEM ** -0.5)
        k = kv_ref[:, sl]
        v = kv_ref[:, D_MODEL + h * HEAD_DIM_MEM:D_MODEL + (h + 1) * HEAD_DIM_MEM]
        s = lax.dot_general(q, k, _NT, preferred_element_type=F32)
        m = jnp.max(s, axis=-1, keepdims=True)
        p = jnp.exp(s - m)
        l = jnp.sum(p, axis=-1, keepdims=True)
        heads.append((jnp.dot(p.astype(BF16), v, preferred_element_type=F32) / l).astype(BF16))
    o = jnp.concatenate(heads, axis=-1)
    y = jnp.dot(o, wo_ref[...], preferred_element_type=F32)
    x2 = _layer_norm(DEEPNORM_ALPHA * x1_ref[...] + y, lng_ref[...], lnb_ref[...])
    _store_token_major(x2_ref, x2)

    logits = jnp.dot(x2.astype(BF16), wr_ref[...], preferred_element_type=F32) + br_ref[...]
    tm = logits.shape[0]
    e_iota = lax.broadcasted_iota(jnp.int32, (tm, N_EXPERTS), 1)
    lane = lax.broadcasted_iota(jnp.int32, (tm, LANES), 1)
    idx_out = jnp.zeros((tm, LANES), jnp.int32)
    val_out = jnp.zeros((tm, LANES), F32)
    top = None
    denom = jnp.zeros((tm, 1), F32)
    for kk in range(TOP_K):
        m = jnp.max(logits, axis=-1, keepdims=True)
        sel = jnp.min(jnp.where(logits == m, e_iota, N_EXPERTS), axis=-1, keepdims=True)
        if top is None:
            top = m
        e = jnp.exp(m - top)
        denom = denom + e
        idx_out = jnp.where(lane == kk, sel, idx_out)
        val_out = jnp.where(lane == kk, e, val_out)
        logits = jnp.where(e_iota == sel, -jnp.inf, logits)
    idx_ref[...] = idx_out
    gate_ref[...] = val_out / denom


def _mem_attn(qm, kv, x1, w_o, ln_g, ln_b, w_r, b_r, b, s):
    tm = TM_TOK
    nt = s // tm
    mem_len = kv.shape[1]
    row = lambda width: pl.BlockSpec((tm, width), lambda bi, i: (bi * nt + i, 0))
    full = lambda r, c: pl.BlockSpec((r, c), lambda bi, i: (0, 0))
    n = b * s
    return pl.pallas_call(
        _mem_attn_kernel,
        grid=(b, nt),
        in_specs=[row(D_MODEL), pl.BlockSpec((None, mem_len, 2 * D_MODEL), lambda bi, i: (bi, 0, 0)),
                  row(D_MODEL), full(D_MODEL, D_MODEL), full(1, D_MODEL), full(1, D_MODEL),
                  full(D_MODEL, N_EXPERTS), full(1, N_EXPERTS)],
        out_specs=[pl.BlockSpec((tm * LANE_BLOCKS, LANES), lambda bi, i: (bi * nt + i, 0)), row(LANES), row(LANES)],
        out_shape=[jax.ShapeDtypeStruct((n * LANE_BLOCKS, LANES), F32),
                   jax.ShapeDtypeStruct((n, LANES), jnp.int32),
                   jax.ShapeDtypeStruct((n, LANES), F32)],
        compiler_params=_params(2),
        name="mem_attn",
    )(qm, kv, x1, w_o, ln_g, ln_b, w_r, b_r)


def _gather_rows(idx_ref, count, src_hbm, dst_ref, sem):
    for r in range(count):
        src = pl.ds(pl.multiple_of(idx_ref[0, 0, r], LANE_BLOCKS), LANE_BLOCKS)
        pltpu.make_async_copy(src_hbm.at[src], dst_ref.at[pl.ds(r * LANE_BLOCKS, LANE_BLOCKS)], sem).start()


def _wait_rows(count, src_hbm, dst_ref, sem):
    pltpu.make_async_copy(src_hbm.at[pl.ds(0, count * LANE_BLOCKS)], dst_ref, sem).wait()


def _moe_kernel(bexp_ref, nused_ref, tok_ref, tok_next_ref, x_hbm, wgu_ref, bgu_ref, wd_ref, bd_ref,
                y_ref, xbuf, sem):
    i = pl.program_id(0)
    slot = i % 2
    n_used = nused_ref[0]

    @pl.when(i == 0)
    def _():
        _gather_rows(tok_ref, TM_MOE, x_hbm, xbuf.at[0], sem.at[0])

    @pl.when(i <= n_used)
    def _():
        _wait_rows(TM_MOE, x_hbm, xbuf.at[slot], sem.at[slot])

    @pl.when(i < n_used)
    def _():
        x = _load_token_major(xbuf.at[slot], 0, TM_MOE).astype(BF16)
        _gather_rows(tok_next_ref, TM_MOE, x_hbm, xbuf.at[1 - slot], sem.at[1 - slot])
        gu = jnp.dot(x, wgu_ref[0], preferred_element_type=F32) + bgu_ref[0]
        gate = jnp.minimum(gu[:, :D_FF], SWIGLU_LIMIT)
        up = jnp.clip(gu[:, D_FF:], -SWIGLU_LIMIT, SWIGLU_LIMIT)
        glu = gate * jax.nn.sigmoid(gate * SWIGLU_ALPHA)
        hid = ((up + 1.0) * glu).astype(BF16)
        _store_token_major(y_ref, jnp.dot(hid, wd_ref[0], preferred_element_type=F32) + bd_ref[0])

    @pl.when(i >= n_used)
    def _():
        y_ref[...] = jnp.zeros_like(y_ref)


def _moe_experts(block_expert, n_used, row_token, x2, w_gu, b_gu, w_d, b_d):
    nb = block_expert.shape[0]
    tok3 = (row_token * LANE_BLOCKS).reshape(nb, 1, TM_MOE)
    smem_blk = lambda f: pl.BlockSpec((1, 1, TM_MOE), f, memory_space=pltpu.SMEM)
    grid_spec = pltpu.PrefetchScalarGridSpec(
        num_scalar_prefetch=2,
        grid=(nb,),
        in_specs=[smem_blk(lambda i, be, nu: (i, 0, 0)),
                  smem_blk(lambda i, be, nu: (jnp.minimum(i + 1, nb - 1), 0, 0)),
                  pl.BlockSpec(memory_space=pl.ANY),
                  pl.BlockSpec((1, D_MODEL, 2 * D_FF), lambda i, be, nu: (be[i], 0, 0)),
                  pl.BlockSpec((1, 1, 2 * D_FF), lambda i, be, nu: (be[i], 0, 0)),
                  pl.BlockSpec((1, D_FF, D_MODEL), lambda i, be, nu: (be[i], 0, 0)),
                  pl.BlockSpec((1, 1, D_MODEL), lambda i, be, nu: (be[i], 0, 0))],
        out_specs=pl.BlockSpec((TM_MOE * LANE_BLOCKS, LANES), lambda i, be, nu: (i, 0)),
        scratch_shapes=[pltpu.VMEM((2, TM_MOE * LANE_BLOCKS, LANES), F32), pltpu.SemaphoreType.DMA((2,))],
    )
    return pl.pallas_call(
        _moe_kernel,
        grid_spec=grid_spec,
        out_shape=jax.ShapeDtypeStruct((nb * TM_MOE * LANE_BLOCKS, LANES), F32),
        compiler_params=_params(1),
        name="moe_experts",
    )(block_expert, n_used, tok3, tok3, x2, w_gu, b_gu, w_d, b_d)


def _combine_kernel(idx_ref, idx_next_ref, y_hbm, x2_ref, gate_ref, lng_ref, lnb_ref, o_ref, ybuf, sem):
    i = pl.program_id(0)
    nt = pl.num_programs(0)
    slot = i % 2
    rows = TOP_K * TM_COMB

    @pl.when(i == 0)
    def _():
        _gather_rows(idx_ref, rows, y_hbm, ybuf.at[0], sem.at[0])

    @pl.when(i + 1 < nt)
    def _():
        _gather_rows(idx_next_ref, rows, y_hbm, ybuf.at[1 - slot], sem.at[1 - slot])

    _wait_rows(rows, y_hbm, ybuf.at[slot], sem.at[slot])

    g = gate_ref[...]
    ff = jnp.zeros((TM_COMB, D_MODEL), F32)
    for kk in range(TOP_K):
        ff = ff + g[:, kk:kk + 1] * _load_token_major(ybuf.at[slot], kk * TM_COMB, TM_COMB)
    x2 = _load_token_major(x2_ref, 0, TM_COMB)
    o_ref[...] = _layer_norm(DEEPNORM_ALPHA * x2 + ff, lng_ref[...], lnb_ref[...])


def _combine(dest, y_rows, x2, gates, ln_g, ln_b):
    n = x2.shape[0] // LANE_BLOCKS
    nt = n // TM_COMB
    rows = TOP_K * TM_COMB
    idx3 = (dest * LANE_BLOCKS).reshape(nt, TM_COMB, TOP_K).transpose(0, 2, 1).reshape(nt, 1, rows)
    smem_blk = lambda f: pl.BlockSpec((1, 1, rows), f, memory_space=pltpu.SMEM)
    row = lambda width: pl.BlockSpec((TM_COMB, width), lambda i: (i, 0))
    full = lambda r, c: pl.BlockSpec((r, c), lambda i: (0, 0))
    return pl.pallas_call(
        _combine_kernel,
        grid=(nt,),
        in_specs=[smem_blk(lambda i: (i, 0, 0)),
                  smem_blk(lambda i: (jnp.minimum(i + 1, nt - 1), 0, 0)),
                  pl.BlockSpec(memory_space=pl.ANY),
                  pl.BlockSpec((TM_COMB * LANE_BLOCKS, LANES), lambda i: (i, 0)), row(LANES),
                  full(1, D_MODEL), full(1, D_MODEL)],
        out_specs=row(D_MODEL),
        out_shape=jax.ShapeDtypeStruct((n, D_MODEL), F32),
        scratch_shapes=[pltpu.VMEM((2, rows * LANE_BLOCKS, LANES), F32), pltpu.SemaphoreType.DMA((2,))],
        compiler_params=_params(1),
        name="combine",
    )(idx3, idx3, y_rows, x2, gates, ln_g, ln_b)


def _rank_kernel(idx_ref, dest_ref, counts_ref, running):
    p = pl.program_id(0)
    j = pl.program_id(1)
    t = TM_RANK
    idx = idx_ref[...]
    e_iota = lax.broadcasted_iota(jnp.int32, (t, LANES), 1)
    hot = [idx[:, k:k + 1] == e_iota for k in range(TOP_K)]
    chosen = sum(jnp.where(h, 1.0, 0.0) for h in hot)
    tile_counts = jnp.sum(chosen, axis=0, keepdims=True)

    @pl.when((p == 0) & (j == 0))
    def _():
        running[...] = jnp.zeros_like(running)

    @pl.when(p == 0)
    def _():
        running[...] += tile_counts
        dest_ref[...] = jnp.zeros_like(dest_ref)

    @pl.when((p == 1) & (j == 0))
    def _():
        counts = running[...]
        counts_ref[...] = counts
        padded = jnp.ceil(counts * (1.0 / TM_MOE)) * TM_MOE
        lane = lax.broadcasted_iota(jnp.int32, (1, LANES), 1)
        scan = padded
        shift = 1
        while shift < N_EXPERTS:
            scan = scan + jnp.where(lane >= shift, pltpu.roll(scan, shift, 1), 0.0)
            shift *= 2
        running[...] = scan - padded

    @pl.when(p == 1)
    def _():
        r_iota = lax.broadcasted_iota(jnp.int32, (t, t), 0)
        c_iota = lax.broadcasted_iota(jnp.int32, (t, t), 1)
        earlier = jnp.where(c_iota < r_iota, 1.0, 0.0).astype(BF16)
        base = jnp.dot(earlier, chosen.astype(BF16), preferred_element_type=F32) + running[...]
        out = jnp.zeros((t, LANES), jnp.int32)
        for k in range(TOP_K):
            row = jnp.sum(jnp.where(hot[k], base, 0.0), axis=-1, keepdims=True).astype(jnp.int32)
            out = jnp.where(e_iota == k, row, out)
        dest_ref[...] = out
        running[...] += tile_counts


def _dispatch_plan(top_idx, n):
    n_assign = n * TOP_K
    nb = n_assign // TM_MOE + N_EXPERTS
    dest, counts = pl.pallas_call(
        _rank_kernel,
        grid=(2, n // TM_RANK),
        in_specs=[pl.BlockSpec((TM_RANK, LANES), lambda p, j: (j, 0))],
        out_specs=[pl.BlockSpec((TM_RANK, LANES), lambda p, j: (p * j, 0)),
                   pl.BlockSpec((1, LANES), lambda p, j: (0, 0))],
        out_shape=[jax.ShapeDtypeStruct((n, LANES), jnp.int32), jax.ShapeDtypeStruct((1, LANES), F32)],
        scratch_shapes=[pltpu.VMEM((1, LANES), F32)],
        compiler_params=_params(2),
        name="expert_rank",
    )(top_idx)
    dest = dest[:, :TOP_K].reshape(-1)
    counts = counts[0, :N_EXPERTS].astype(jnp.int32)
    padded = (counts + TM_MOE - 1) // TM_MOE * TM_MOE
    pad_end = jnp.cumsum(padded)
    row_token = jnp.zeros((nb * TM_MOE,), jnp.int32).at[dest].set(
        jnp.arange(n_assign, dtype=jnp.int32) // TOP_K, unique_indices=True, mode='promise_in_bounds')
    block_first_row = jnp.arange(nb, dtype=jnp.int32) * TM_MOE
    block_expert = jnp.minimum(jnp.sum(pad_end[None, :] <= block_first_row[:, None], axis=1),
                               N_EXPERTS - 1).astype(jnp.int32)
    n_used = (pad_end[-1:] // TM_MOE).astype(jnp.int32)
    return dest, row_token, block_expert, n_used


def kernel(x, mem, w_in, rel_bias, g_group_a, g_group_b, w_out, w_q_mem, w_kv_mem, w_o_mem, w_router, b_router, w_gate_up, b_gate_up, w_down, b_down, ln_g, ln_b):
    b, s, d = x.shape
    n = b * s
    depth = w_in.shape[0]
    xf = x.reshape(n, d)
    for l in range(depth):
        q_scale = jnp.ones((3 * MIX_WIDTH,), F32).at[:WIDTH_A].set(HEAD_DIM ** -0.5)
        q_scale = q_scale.at[3 * WIDTH_A:3 * WIDTH_A + WIDTH_B].set(HEAD_DIM ** -0.5)
        proj = _matmul(xf, (w_in[l] * q_scale).astype(BF16), BF16, TM_TOK, 1024).reshape(b, s, 3 * MIX_WIDTH)
        out_a = _chunk_attention(proj, _chunk_bias(rel_bias[l]), b, s).reshape(n, WIDTH_A)
        out_b = _sb_attention(proj, b, s).reshape(n, WIDTH_B)
        x1, qm = _mixer_out(out_a, out_b, xf, g_group_a[l][None], g_group_b[l][None],
                            w_out[l].astype(BF16), ln_g[l, 0][None], ln_b[l, 0][None],
                            w_q_mem[l].astype(BF16))
        mem_len = mem.shape[1]
        kv = _matmul(mem.reshape(b * mem_len, d), w_kv_mem[l].astype(BF16), BF16, b * mem_len, 1024)
        x2, top_idx, gates = _mem_attn(qm, kv.reshape(b, mem_len, 2 * d), x1, w_o_mem[l].astype(BF16),
                                       ln_g[l, 1][None], ln_b[l, 1][None],
                                       w_router[l].astype(BF16), b_router[l][None], b, s)
        dest, row_token, block_expert, n_used = _dispatch_plan(top_idx, n)
        y_rows = _moe_experts(block_expert, n_used, row_token, x2,
                              w_gate_up[l].astype(BF16), b_gate_up[l][:, None, :],
                              w_down[l].astype(BF16), b_down[l][:, None, :])
        xf = _combine(dest, y_rows, x2, gates, ln_g[l, 2][None], ln_b[l, 2][None])
    return xf.reshape(b, s, d)
```

```python
import functools

import jax
import jax.numpy as jnp
from jax import lax
from jax.experimental import pallas as pl
from jax.experimental.pallas import tpu as pltpu

D_MODEL = 1024
CHUNK = 64
LEFT_CHUNKS = 8
LEFT = LEFT_CHUNKS * CHUNK
HEAD_DIM = 64
N_HEADS_A = 8
N_HEADS_B = 8
WIDTH_A = N_HEADS_A * HEAD_DIM
WIDTH_B = N_HEADS_B * HEAD_DIM
MIX_WIDTH = WIDTH_A + WIDTH_B
REL_CLIP = 128
N_HEADS_MEM = 4
HEAD_DIM_MEM = D_MODEL // N_HEADS_MEM
N_EXPERTS = 32
TOP_K = 4
D_FF = D_MODEL
SWIGLU_LIMIT = 7.0
SWIGLU_ALPHA = 1.702
LN_EPS = 1e-5
RMS_EPS = 1e-6
DEEPNORM_ALPHA = 2.0 ** 0.25
NEG_INF = -1e30
LOG2E = 1.4426950408889634

LANES = 128
LANE_BLOCKS = D_MODEL // LANES
VMEM_LIMIT = 48 * 1024 * 1024

TQ_A = 512
TB_SB = 256
SB_UNDERFLOW = 110.0
TM_TOK = 512
TM_MOE = 256
TM_COMB = 256
TM_RANK = 512

F32 = jnp.float32
BF16 = jnp.bfloat16
_NT = (((1,), (1,)), ((), ()))


def _params(n_axes):
    return pltpu.CompilerParams(dimension_semantics=("arbitrary",) * n_axes,
                                vmem_limit_bytes=VMEM_LIMIT)


def _store_token_major(ref, value):
    rows = value.shape[0]
    for c in range(LANE_BLOCKS):
        ref[pl.ds(c, rows, stride=LANE_BLOCKS), :] = value[:, c * LANES:(c + 1) * LANES]


def _load_token_major(ref, first_row, rows):
    return jnp.concatenate(
        [ref[pl.ds(first_row * LANE_BLOCKS + c, rows, stride=LANE_BLOCKS), :] for c in range(LANE_BLOCKS)],
        axis=1)


def _layer_norm(r, g, b):
    mu = jnp.mean(r, axis=-1, keepdims=True)
    d = r - mu
    var = jnp.mean(d * d, axis=-1, keepdims=True)
    return d * lax.rsqrt(var + LN_EPS) * g + b


def _matmul_kernel(x_ref, w_ref, o_ref):
    o_ref[...] = jnp.dot(x_ref[...].astype(BF16), w_ref[...],
                         preferred_element_type=F32).astype(o_ref.dtype)


def _matmul(x, w, out_dtype, tm, tn):
    m, k = x.shape
    n = w.shape[1]
    return pl.pallas_call(
        _matmul_kernel,
        grid=(m // tm, n // tn),
        in_specs=[pl.BlockSpec((tm, k), lambda i, j: (i, 0)),
                  pl.BlockSpec((k, tn), lambda i, j: (0, j))],
        out_specs=pl.BlockSpec((tm, tn), lambda i, j: (i, j)),
        out_shape=jax.ShapeDtypeStruct((m, n), out_dtype),
        compiler_params=_params(2),
        name="matmul",
    )(x, w)


def _chunk_attn_kernel(q_ref, kp_ref, kc_ref, vp_ref, vc_ref, bias_ref, o_ref):
    i = pl.program_id(2)
    w = TQ_A + LEFT
    col = lax.broadcasted_iota(jnp.int32, (TQ_A, w), 1)
    first_valid = jnp.where(i == 0, LEFT, 0)
    for h in range(2):
        sl = slice(h * HEAD_DIM, (h + 1) * HEAD_DIM)
        q = q_ref[:, sl]
        k = jnp.concatenate([kp_ref[:, sl], kc_ref[:, sl]], axis=0)
        v = jnp.concatenate([vp_ref[:, sl], vc_ref[:, sl]], axis=0)
        s = lax.dot_general(q, k, _NT, preferred_element_type=F32) + bias_ref[h]
        s = jnp.where(col >= first_valid, s, NEG_INF)
        m = jnp.max(s, axis=-1, keepdims=True)
        p = jnp.exp(s - m)
        l = jnp.sum(p, axis=-1, keepdims=True)
        o = jnp.dot(p.astype(BF16), v, preferred_element_type=F32) / l
        o_ref[:, sl] = o.astype(o_ref.dtype)


def _chunk_bias(rel_bias):
    w = TQ_A + LEFT
    heads = rel_bias.shape[0]
    m = jnp.arange(2 * w)
    rel = jnp.where(m < w, LEFT - m, LEFT + 2 * w - m)
    diag = rel_bias[:, jnp.clip(rel, -REL_CLIP, REL_CLIP) + REL_CLIP].astype(F32)[:, None, :]
    return pl.pallas_call(
        _chunk_bias_kernel,
        grid=(heads,),
        in_specs=[pl.BlockSpec((1, 1, 2 * w), lambda h: (h, 0, 0))],
        out_specs=pl.BlockSpec((1, TQ_A, w), lambda h: (h, 0, 0)),
        out_shape=jax.ShapeDtypeStruct((heads, TQ_A, w), F32),
        compiler_params=_params(1),
        name="chunk_bias",
    )(diag)


def _chunk_bias_kernel(diag_ref, o_ref):
    w = TQ_A + LEFT
    rolled = pltpu.roll(jnp.broadcast_to(diag_ref[0], (TQ_A, 2 * w)), 0, 1, stride=1, stride_axis=0)
    qc = lax.broadcasted_iota(jnp.int32, (TQ_A, w), 0) // CHUNK
    kc = lax.broadcasted_iota(jnp.int32, (TQ_A, w), 1) // CHUNK
    band = (kc >= qc) & (kc <= qc + LEFT_CHUNKS)
    o_ref[0] = jnp.where(band, rolled[:, :w], NEG_INF)


def _chunk_attention(proj, bias, b, s):
    nq = s // TQ_A
    pairs = WIDTH_A // LANES
    blk = lambda off, prev: pl.BlockSpec(
        (None, TQ_A, LANES),
        (lambda bi, p, i: (bi, jnp.maximum(i - 1, 0), off + p)) if prev
        else (lambda bi, p, i: (bi, i, off + p)))
    return pl.pallas_call(
        _chunk_attn_kernel,
        grid=(b, pairs, nq),
        in_specs=[blk(0, False),
                  blk(pairs, True), blk(pairs, False),
                  blk(2 * pairs, True), blk(2 * pairs, False),
                  pl.BlockSpec((2, TQ_A, TQ_A + LEFT), lambda bi, p, i: (p, 0, 0))],
        out_specs=pl.BlockSpec((None, TQ_A, LANES), lambda bi, p, i: (bi, i, p)),
        out_shape=jax.ShapeDtypeStruct((b, s, WIDTH_A), BF16),
        compiler_params=_params(3),
        name="chunk_attn",
    )(proj, proj, proj, proj, proj, bias)


def _sb_block(q, k_ref, v_ref, sl, j, later, tri, causal):
    t = TB_SB
    start = pl.multiple_of(j * t, t)
    k = k_ref[pl.ds(start, t), sl]
    v = v_ref[pl.ds(start, t), sl]
    z = lax.dot_general(q, k, _NT, preferred_element_type=F32)
    sp = jnp.maximum(z, 0.0) + jnp.log(1.0 + jnp.exp2(jnp.abs(z) * (-LOG2E)))
    if causal is not None:
        sp = jnp.where(causal, sp, 0.0)
    csum = jnp.dot(sp.astype(BF16), tri, preferred_element_type=F32) + later
    wgt = jnp.exp(z - csum)
    if causal is not None:
        wgt = jnp.where(causal, wgt, 0.0)
    pv = jnp.dot(wgt.astype(BF16), v, preferred_element_type=F32)
    return pv, csum[:, 0:1]


def _sb_attn_kernel(q_ref, k_ref, v_ref, o_ref):
    i = pl.program_id(2)
    t = TB_SB
    row = lax.broadcasted_iota(jnp.int32, (t, t), 0)
    col = lax.broadcasted_iota(jnp.int32, (t, t), 1)
    tri = jnp.where(row >= col, 1.0, 0.0).astype(BF16)
    causal = col < row
    sls = [slice(h * HEAD_DIM, (h + 1) * HEAD_DIM) for h in range(2)]
    qs = [q_ref[:, sl] for sl in sls]
    block = functools.partial(_sb_block, tri=tri)

    carry = []
    for h in range(2):
        pv, later = block(qs[h], k_ref, v_ref, sls[h], i, jnp.zeros((t, 1), F32), causal=causal)
        carry += [later, pv]

    def pending(c):
        return (c[0] >= 0) & (jnp.minimum(jnp.min(c[1]), jnp.min(c[3])) < SB_UNDERFLOW)

    def step(c):
        out = [c[0] - 1]
        for h in range(2):
            pv, later = block(qs[h], k_ref, v_ref, sls[h], c[0], c[1 + 2 * h], causal=None)
            out += [later, c[2 + 2 * h] + pv]
        return tuple(out)

    carry = lax.while_loop(pending, step, (i - 1, *carry))
    for h in range(2):
        o_ref[:, sls[h]] = carry[2 + 2 * h].astype(o_ref.dtype)


def _sb_attention(proj, b, s):
    nq = s // TB_SB
    pairs = WIDTH_B // LANES
    base = 3 * WIDTH_A // LANES
    return pl.pallas_call(
        _sb_attn_kernel,
        grid=(b, pairs, nq),
        in_specs=[pl.BlockSpec((None, TB_SB, LANES), lambda bi, p, i: (bi, i, base + p)),
                  pl.BlockSpec((None, s, LANES), lambda bi, p, i: (bi, 0, base + pairs + p)),
                  pl.BlockSpec((None, s, LANES), lambda bi, p, i: (bi, 0, base + 2 * pairs + p))],
        out_specs=pl.BlockSpec((None, TB_SB, LANES), lambda bi, p, i: (bi, i, p)),
        out_shape=jax.ShapeDtypeStruct((b, s, WIDTH_B), BF16),
        compiler_params=_params(3),
        name="sb_attn",
    )(proj, proj, proj)


def _mixer_out_kernel(oa_ref, ob_ref, x_ref, ga_ref, gb_ref, wout_ref, lng_ref, lnb_ref, wq_ref,
                      x1_ref, qm_ref):
    def rms(ref, g_ref):
        a = ref[...].astype(F32)
        return (a * lax.rsqrt(jnp.mean(a * a, axis=-1, keepdims=True) + RMS_EPS) * g_ref[...]).astype(BF16)

    y = jnp.dot(rms(oa_ref, ga_ref), wout_ref[:WIDTH_A, :], preferred_element_type=F32)
    y = y + jnp.dot(rms(ob_ref, gb_ref), wout_ref[WIDTH_A:, :], preferred_element_type=F32)
    x1 = _layer_norm(DEEPNORM_ALPHA * x_ref[...] + y, lng_ref[...], lnb_ref[...])
    x1_ref[...] = x1
    qm_ref[...] = jnp.dot(x1.astype(BF16), wq_ref[...], preferred_element_type=F32).astype(BF16)


def _mixer_out(out_a, out_b, x, g_a, g_b, w_out, ln_g, ln_b, w_q):
    n = x.shape[0]
    tm = TM_TOK
    row = lambda width: pl.BlockSpec((tm, width), lambda i: (i, 0))
    full = lambda r, c: pl.BlockSpec((r, c), lambda i: (0, 0))
    return pl.pallas_call(
        _mixer_out_kernel,
        grid=(n // tm,),
        in_specs=[row(WIDTH_A), row(WIDTH_B), row(D_MODEL), full(1, WIDTH_A), full(1, WIDTH_B),
                  full(MIX_WIDTH, D_MODEL), full(1, D_MODEL), full(1, D_MODEL), full(D_MODEL, D_MODEL)],
        out_specs=[row(D_MODEL), row(D_MODEL)],
        out_shape=[jax.ShapeDtypeStruct((n, D_MODEL), F32), jax.ShapeDtypeStruct((n, D_MODEL), BF16)],
        compiler_params=_params(1),
        name="mixer_out",
    )(out_a, out_b, x, g_a, g_b, w_out, ln_g, ln_b, w_q)


def _mem_attn_kernel(q_ref, kv_ref, x1_ref, wo_ref, lng_ref, lnb_ref, wr_ref, br_ref,
                     x2_ref, idx_ref, gate_ref):
    heads = []
    for h in range(N_HEADS_MEM):
        sl = slice(h * HEAD_DIM_MEM, (h + 1) * HEAD_DIM_MEM)
        q = q_ref[:, sl] * (HEAD_DIM_MEM ** -0.5)
        k = kv_ref[:, sl]
        v = kv_ref[:, D_MODEL + h * HEAD_DIM_MEM:D_MODEL + (h + 1) * HEAD_DIM_MEM]
        s = lax.dot_general(q, k, _NT, preferred_element_type=F32)
        m = jnp.max(s, axis=-1, keepdims=True)
        p = jnp.exp(s - m)
        l = jnp.sum(p, axis=-1, keepdims=True)
        heads.append((jnp.dot(p.astype(BF16), v, preferred_element_type=F32) / l).astype(BF16))
    o = jnp.concatenate(heads, axis=-1)
    y = jnp.dot(o, wo_ref[...], preferred_element_type=F32)
    x2 = _layer_norm(DEEPNORM_ALPHA * x1_ref[...] + y, lng_ref[...], lnb_ref[...])
    _store_token_major(x2_ref, x2)

    logits = jnp.dot(x2.astype(BF16), wr_ref[...], preferred_element_type=F32) + br_ref[...]
    tm = logits.shape[0]
    e_iota = lax.broadcasted_iota(jnp.int32, (tm, N_EXPERTS), 1)
    lane = lax.broadcasted_iota(jnp.int32, (tm, LANES), 1)
    idx_out = jnp.zeros((tm, LANES), jnp.int32)
    val_out = jnp.zeros((tm, LANES), F32)
    top = None
    denom = jnp.zeros((tm, 1), F32)
    for kk in range(TOP_K):
        m = jnp.max(logits, axis=-1, keepdims=True)
        sel = jnp.min(jnp.where(logits == m, e_iota, N_EXPERTS), axis=-1, keepdims=True)
        if top is None:
            top = m
        e = jnp.exp(m - top)
        denom = denom + e
        idx_out = jnp.where(lane == kk, sel, idx_out)
        val_out = jnp.where(lane == kk, e, val_out)
        logits = jnp.where(e_iota == sel, -jnp.inf, logits)
    idx_ref[...] = idx_out
    gate_ref[...] = val_out / denom


def _mem_attn(qm, kv, x1, w_o, ln_g, ln_b, w_r, b_r, b, s):
    tm = TM_TOK
    nt = s // tm
    mem_len = kv.shape[1]
    row = lambda width: pl.BlockSpec((tm, width), lambda bi, i: (bi * nt + i, 0))
    full = lambda r, c: pl.BlockSpec((r, c), lambda bi, i: (0, 0))
    n = b * s
    return pl.pallas_call(
        _mem_attn_kernel,
        grid=(b, nt),
        in_specs=[row(D_MODEL), pl.BlockSpec((None, mem_len, 2 * D_MODEL), lambda bi, i: (bi, 0, 0)),
                  row(D_MODEL), full(D_MODEL, D_MODEL), full(1, D_MODEL), full(1, D_MODEL),
                  full(D_MODEL, N_EXPERTS), full(1, N_EXPERTS)],
        out_specs=[pl.BlockSpec((tm * LANE_BLOCKS, LANES), lambda bi, i: (bi * nt + i, 0)), row(LANES), row(LANES)],
        out_shape=[jax.ShapeDtypeStruct((n * LANE_BLOCKS, LANES), F32),
                   jax.ShapeDtypeStruct((n, LANES), jnp.int32),
                   jax.ShapeDtypeStruct((n, LANES), F32)],
        compiler_params=_params(2),
        name="mem_attn",
    )(qm, kv, x1, w_o, ln_g, ln_b, w_r, b_r)


def _gather_rows(idx_ref, first, count, src_hbm, dst_ref, sem):
    for r in range(count):
        src = pl.ds(pl.multiple_of(idx_ref[first + r], LANE_BLOCKS), LANE_BLOCKS)
        pltpu.make_async_copy(src_hbm.at[src], dst_ref.at[pl.ds(r * LANE_BLOCKS, LANE_BLOCKS)], sem).start()


def _wait_rows(count, src_hbm, dst_ref, sem):
    pltpu.make_async_copy(src_hbm.at[pl.ds(0, count * LANE_BLOCKS)], dst_ref, sem).wait()


def _moe_kernel(bexp_ref, nused_ref, tok_ref, x_hbm, wgu_ref, bgu_ref, wd_ref, bd_ref,
                y_ref, xbuf, sem):
    i = pl.program_id(0)
    slot = i % 2
    n_used = nused_ref[0]

    @pl.when(i == 0)
    def _():
        _gather_rows(tok_ref, 0, TM_MOE, x_hbm, xbuf.at[0], sem.at[0])

    @pl.when(i <= n_used)
    def _():
        _wait_rows(TM_MOE, x_hbm, xbuf.at[slot], sem.at[slot])

    @pl.when(i < n_used)
    def _():
        x = _load_token_major(xbuf.at[slot], 0, TM_MOE).astype(BF16)
        _gather_rows(tok_ref, (i + 1) * TM_MOE, TM_MOE, x_hbm, xbuf.at[1 - slot], sem.at[1 - slot])
        gu = jnp.dot(x, wgu_ref[0], preferred_element_type=F32) + bgu_ref[0]
        gate = jnp.minimum(gu[:, :D_FF], SWIGLU_LIMIT)
        up = jnp.clip(gu[:, D_FF:], -SWIGLU_LIMIT, SWIGLU_LIMIT)
        glu = gate * jax.nn.sigmoid(gate * SWIGLU_ALPHA)
        hid = ((up + 1.0) * glu).astype(BF16)
        _store_token_major(y_ref, jnp.dot(hid, wd_ref[0], preferred_element_type=F32) + bd_ref[0])

    @pl.when(i >= n_used)
    def _():
        y_ref[...] = jnp.zeros_like(y_ref)


def _moe_experts(block_expert, n_used, row_token, x2, w_gu, b_gu, w_d, b_d):
    nb = block_expert.shape[0]
    grid_spec = pltpu.PrefetchScalarGridSpec(
        num_scalar_prefetch=3,
        grid=(nb,),
        in_specs=[pl.BlockSpec(memory_space=pl.ANY),
                  pl.BlockSpec((1, D_MODEL, 2 * D_FF), lambda i, be, nu, tok: (be[i], 0, 0)),
                  pl.BlockSpec((1, 1, 2 * D_FF), lambda i, be, nu, tok: (be[i], 0, 0)),
                  pl.BlockSpec((1, D_FF, D_MODEL), lambda i, be, nu, tok: (be[i], 0, 0)),
                  pl.BlockSpec((1, 1, D_MODEL), lambda i, be, nu, tok: (be[i], 0, 0))],
        out_specs=pl.BlockSpec((TM_MOE * LANE_BLOCKS, LANES), lambda i, be, nu, tok: (i, 0)),
        scratch_shapes=[pltpu.VMEM((2, TM_MOE * LANE_BLOCKS, LANES), F32), pltpu.SemaphoreType.DMA((2,))],
    )
    return pl.pallas_call(
        _moe_kernel,
        grid_spec=grid_spec,
        out_shape=jax.ShapeDtypeStruct((nb * TM_MOE * LANE_BLOCKS, LANES), F32),
        compiler_params=_params(1),
        name="moe_experts",
    )(block_expert, n_used, row_token * LANE_BLOCKS, x2, w_gu, b_gu, w_d, b_d)


def _combine_kernel(idx_ref, y_hbm, x2_ref, gate_ref, lng_ref, lnb_ref, o_ref, ybuf, sem):
    i = pl.program_id(0)
    nt = pl.num_programs(0)
    slot = i % 2
    rows = TOP_K * TM_COMB

    @pl.when(i == 0)
    def _():
        _gather_rows(idx_ref, 0, rows, y_hbm, ybuf.at[0], sem.at[0])

    @pl.when(i + 1 < nt)
    def _():
        _gather_rows(idx_ref, (i + 1) * rows, rows, y_hbm, ybuf.at[1 - slot], sem.at[1 - slot])

    _wait_rows(rows, y_hbm, ybuf.at[slot], sem.at[slot])

    g = gate_ref[...]
    ff = jnp.zeros((TM_COMB, D_MODEL), F32)
    for kk in range(TOP_K):
        ff = ff + g[:, kk:kk + 1] * _load_token_major(ybuf.at[slot], kk * TM_COMB, TM_COMB)
    x2 = _load_token_major(x2_ref, 0, TM_COMB)
    o_ref[...] = _layer_norm(DEEPNORM_ALPHA * x2 + ff, lng_ref[...], lnb_ref[...])


def _combine(dest, y_rows, x2, gates, ln_g, ln_b):
    n = x2.shape[0] // LANE_BLOCKS
    nt = n // TM_COMB
    rows = TOP_K * TM_COMB
    idx = (dest * LANE_BLOCKS).reshape(nt, TM_COMB, TOP_K).transpose(0, 2, 1).reshape(nt * rows)
    row = lambda width: pl.BlockSpec((TM_COMB, width), lambda i, idx: (i, 0))
    full = lambda r, c: pl.BlockSpec((r, c), lambda i, idx: (0, 0))
    grid_spec = pltpu.PrefetchScalarGridSpec(
        num_scalar_prefetch=1,
        grid=(nt,),
        in_specs=[pl.BlockSpec(memory_space=pl.ANY),
                  pl.BlockSpec((TM_COMB * LANE_BLOCKS, LANES), lambda i, idx: (i, 0)), row(LANES),
                  full(1, D_MODEL), full(1, D_MODEL)],
        out_specs=row(D_MODEL),
        scratch_shapes=[pltpu.VMEM((2, rows * LANE_BLOCKS, LANES), F32), pltpu.SemaphoreType.DMA((2,))],
    )
    return pl.pallas_call(
        _combine_kernel,
        grid_spec=grid_spec,
        out_shape=jax.ShapeDtypeStruct((n, D_MODEL), F32),
        compiler_params=_params(1),
        name="combine",
    )(idx, y_rows, x2, gates, ln_g, ln_b)


def _rank_kernel(idx_ref, dest_ref, counts_ref, running):
    p = pl.program_id(0)
    j = pl.program_id(1)
    t = TM_RANK
    idx = idx_ref[...]
    e_iota = lax.broadcasted_iota(jnp.int32, (t, LANES), 1)
    hot = [idx[:, k:k + 1] == e_iota for k in range(TOP_K)]
    chosen = sum(jnp.where(h, 1.0, 0.0) for h in hot)
    tile_counts = jnp.sum(chosen, axis=0, keepdims=True)

    @pl.when((p == 0) & (j == 0))
    def _():
        running[...] = jnp.zeros_like(running)

    @pl.when(p == 0)
    def _():
        running[...] += tile_counts
        dest_ref[...] = jnp.zeros_like(dest_ref)

    @pl.when((p == 1) & (j == 0))
    def _():
        counts = running[...]
        counts_ref[...] = counts
        padded = jnp.ceil(counts * (1.0 / TM_MOE)) * TM_MOE
        lane = lax.broadcasted_iota(jnp.int32, (1, LANES), 1)
        scan = padded
        shift = 1
        while shift < N_EXPERTS:
            scan = scan + jnp.where(lane >= shift, pltpu.roll(scan, shift, 1), 0.0)
            shift *= 2
        running[...] = scan - padded

    @pl.when(p == 1)
    def _():
        r_iota = lax.broadcasted_iota(jnp.int32, (t, t), 0)
        c_iota = lax.broadcasted_iota(jnp.int32, (t, t), 1)
        earlier = jnp.where(c_iota < r_iota, 1.0, 0.0).astype(BF16)
        base = jnp.dot(earlier, chosen.astype(BF16), preferred_element_type=F32) + running[...]
        out = jnp.zeros((t, LANES), jnp.int32)
        for k in range(TOP_K):
            row = jnp.sum(jnp.where(hot[k], base, 0.0), axis=-1, keepdims=True).astype(jnp.int32)
            out = jnp.where(e_iota == k, row, out)
        dest_ref[...] = out
        running[...] += tile_counts


def _dispatch_plan(top_idx, n):
    n_assign = n * TOP_K
    nb = n_assign // TM_MOE + N_EXPERTS
    dest, counts = pl.pallas_call(
        _rank_kernel,
        grid=(2, n // TM_RANK),
        in_specs=[pl.BlockSpec((TM_RANK, LANES), lambda p, j: (j, 0))],
        out_specs=[pl.BlockSpec((TM_RANK, LANES), lambda p, j: (p * j, 0)),
                   pl.BlockSpec((1, LANES), lambda p, j: (0, 0))],
        out_shape=[jax.ShapeDtypeStruct((n, LANES), jnp.int32), jax.ShapeDtypeStruct((1, LANES), F32)],
        scratch_shapes=[pltpu.VMEM((1, LANES), F32)],
        compiler_params=_params(2),
        name="expert_rank",
    )(top_idx)
    dest = dest[:, :TOP_K].reshape(-1)
    counts = counts[0, :N_EXPERTS].astype(jnp.int32)
    padded = (counts + TM_MOE - 1) // TM_MOE * TM_MOE
    pad_end = jnp.cumsum(padded)
    row_token = jnp.zeros((nb * TM_MOE,), jnp.int32).at[dest].set(
        jnp.arange(n_assign, dtype=jnp.int32) // TOP_K, unique_indices=True, mode='promise_in_bounds')
    block_first_row = jnp.arange(nb, dtype=jnp.int32) * TM_MOE
    block_expert = jnp.minimum(jnp.sum(pad_end[None, :] <= block_first_row[:, None], axis=1),
                               N_EXPERTS - 1).astype(jnp.int32)
    n_used = (pad_end[-1:] // TM_MOE).astype(jnp.int32)
    return dest, row_token, block_expert, n_used


def kernel(x, mem, w_in, rel_bias, g_group_a, g_group_b, w_out, w_q_mem, w_kv_mem, w_o_mem, w_router, b_router, w_gate_up, b_gate_up, w_down, b_down, ln_g, ln_b):
    b, s, d = x.shape
    n = b * s
    depth = w_in.shape[0]
    xf = x.reshape(n, d)
    for l in range(depth):
        q_scale = jnp.ones((3 * MIX_WIDTH,), F32).at[:WIDTH_A].set(HEAD_DIM ** -0.5)
        q_scale = q_scale.at[3 * WIDTH_A:3 * WIDTH_A + WIDTH_B].set(HEAD_DIM ** -0.5)
        proj = _matmul(xf, (w_in[l] * q_scale).astype(BF16), BF16, TM_TOK, 1024).reshape(b, s, 3 * MIX_WIDTH)
        out_a = _chunk_attention(proj, _chunk_bias(rel_bias[l]), b, s).reshape(n, WIDTH_A)
        out_b = _sb_attention(proj, b, s).reshape(n, WIDTH_B)
        x1, qm = _mixer_out(out_a, out_b, xf, g_group_a[l][None], g_group_b[l][None],
                            w_out[l].astype(BF16), ln_g[l, 0][None], ln_b[l, 0][None],
                            w_q_mem[l].astype(BF16))
        mem_len = mem.shape[1]
        kv = _matmul(mem.reshape(b * mem_len, d), w_kv_mem[l].astype(BF16), BF16, b * mem_len, 1024)
        x2, top_idx, gates = _mem_attn(qm, kv.reshape(b, mem_len, 2 * d), x1, w_o_mem[l].astype(BF16),
                                       ln_g[l, 1][None], ln_b[l, 1][None],
                                       w_router[l].astype(BF16), b_router[l][None], b, s)
        dest, row_token, block_expert, n_used = _dispatch_plan(top_idx, n)
        y_rows = _moe_experts(block_expert, n_used, row_token, x2,
                              w_gate_up[l].astype(BF16), b_gate_up[l][:, None, :],
                              w_down[l].astype(BF16), b_down[l][:, None, :])
        xf = _combine(dest, y_rows, x2, gates, ln_g[l, 2][None], ln_b[l, 2][None])
    return xf.reshape(b, s, d)
```

```python
import functools

import jax
import jax.numpy as jnp
from jax import lax
from jax.experimental import pallas as pl
from jax.experimental.pallas import tpu as pltpu

D_MODEL = 1024
CHUNK = 64
LEFT_CHUNKS = 8
LEFT = LEFT_CHUNKS * CHUNK
HEAD_DIM = 64
N_HEADS_A = 8
N_HEADS_B = 8
WIDTH_A = N_HEADS_A * HEAD_DIM
WIDTH_B = N_HEADS_B * HEAD_DIM
MIX_WIDTH = WIDTH_A + WIDTH_B
REL_CLIP = 128
N_HEADS_MEM = 4
HEAD_DIM_MEM = D_MODEL // N_HEADS_MEM
N_EXPERTS = 32
TOP_K = 4
D_FF = D_MODEL
SWIGLU_LIMIT = 7.0
SWIGLU_ALPHA = 1.702
LN_EPS = 1e-5
RMS_EPS = 1e-6
DEEPNORM_ALPHA = 2.0 ** 0.25
NEG_INF = -1e30
LOG2E = 1.4426950408889634

LANES = 128
LANE_BLOCKS = D_MODEL // LANES
VMEM_LIMIT = 48 * 1024 * 1024

TQ_A = 512
TB_SB = 256
SB_UNDERFLOW = 110.0
TM_TOK = 512
TM_MOE = 256
MOE_SLOTS = 3
TM_COMB = 256
TM_RANK = 512

F32 = jnp.float32
BF16 = jnp.bfloat16
_NT = (((1,), (1,)), ((), ()))


def _params(n_axes):
    return pltpu.CompilerParams(dimension_semantics=("arbitrary",) * n_axes,
                                vmem_limit_bytes=VMEM_LIMIT)


def _store_token_major(ref, value):
    rows = value.shape[0]
    for c in range(LANE_BLOCKS):
        ref[pl.ds(c, rows, stride=LANE_BLOCKS), :] = value[:, c * LANES:(c + 1) * LANES]


def _load_token_major(ref, first_row, rows):
    return jnp.concatenate(
        [ref[pl.ds(first_row * LANE_BLOCKS + c, rows, stride=LANE_BLOCKS), :] for c in range(LANE_BLOCKS)],
        axis=1)


def _layer_norm(r, g, b):
    mu = jnp.mean(r, axis=-1, keepdims=True)
    d = r - mu
    var = jnp.mean(d * d, axis=-1, keepdims=True)
    return d * lax.rsqrt(var + LN_EPS) * g + b


def _matmul_kernel(x_ref, w_ref, o_ref):
    o_ref[...] = jnp.dot(x_ref[...].astype(BF16), w_ref[...],
                         preferred_element_type=F32).astype(o_ref.dtype)


def _matmul(x, w, out_dtype, tm, tn):
    m, k = x.shape
    n = w.shape[1]
    return pl.pallas_call(
        _matmul_kernel,
        grid=(m // tm, n // tn),
        in_specs=[pl.BlockSpec((tm, k), lambda i, j: (i, 0)),
                  pl.BlockSpec((k, tn), lambda i, j: (0, j))],
        out_specs=pl.BlockSpec((tm, tn), lambda i, j: (i, j)),
        out_shape=jax.ShapeDtypeStruct((m, n), out_dtype),
        compiler_params=_params(2),
        name="matmul",
    )(x, w)


def _chunk_attn_kernel(q_ref, kp_ref, kc_ref, vp_ref, vc_ref, bias_ref, o_ref):
    i = pl.program_id(2)
    w = TQ_A + LEFT
    col = lax.broadcasted_iota(jnp.int32, (TQ_A, w), 1)
    first_valid = jnp.where(i == 0, LEFT, 0)
    for h in range(2):
        sl = slice(h * HEAD_DIM, (h + 1) * HEAD_DIM)
        q = q_ref[:, sl]
        k = jnp.concatenate([kp_ref[:, sl], kc_ref[:, sl]], axis=0)
        v = jnp.concatenate([vp_ref[:, sl], vc_ref[:, sl]], axis=0)
        s = lax.dot_general(q, k, _NT, preferred_element_type=F32) + bias_ref[h]
        s = jnp.where(col >= first_valid, s, NEG_INF)
        m = jnp.max(s, axis=-1, keepdims=True)
        p = jnp.exp(s - m)
        l = jnp.sum(p, axis=-1, keepdims=True)
        o = jnp.dot(p.astype(BF16), v, preferred_element_type=F32) / l
        o_ref[:, sl] = o.astype(o_ref.dtype)


def _chunk_bias(rel_bias):
    w = TQ_A + LEFT
    heads = rel_bias.shape[0]
    m = jnp.arange(2 * w)
    rel = jnp.where(m < w, LEFT - m, LEFT + 2 * w - m)
    diag = rel_bias[:, jnp.clip(rel, -REL_CLIP, REL_CLIP) + REL_CLIP].astype(F32)[:, None, :]
    return pl.pallas_call(
        _chunk_bias_kernel,
        grid=(heads,),
        in_specs=[pl.BlockSpec((1, 1, 2 * w), lambda h: (h, 0, 0))],
        out_specs=pl.BlockSpec((1, TQ_A, w), lambda h: (h, 0, 0)),
        out_shape=jax.ShapeDtypeStruct((heads, TQ_A, w), F32),
        compiler_params=_params(1),
        name="chunk_bias",
    )(diag)


def _chunk_bias_kernel(diag_ref, o_ref):
    w = TQ_A + LEFT
    rolled = pltpu.roll(jnp.broadcast_to(diag_ref[0], (TQ_A, 2 * w)), 0, 1, stride=1, stride_axis=0)
    qc = lax.broadcasted_iota(jnp.int32, (TQ_A, w), 0) // CHUNK
    kc = lax.broadcasted_iota(jnp.int32, (TQ_A, w), 1) // CHUNK
    band = (kc >= qc) & (kc <= qc + LEFT_CHUNKS)
    o_ref[0] = jnp.where(band, rolled[:, :w], NEG_INF)


def _chunk_attention(proj, bias, b, s):
    nq = s // TQ_A
    pairs = WIDTH_A // LANES
    blk = lambda off, prev: pl.BlockSpec(
        (None, TQ_A, LANES),
        (lambda bi, p, i: (bi, jnp.maximum(i - 1, 0), off + p)) if prev
        else (lambda bi, p, i: (bi, i, off + p)))
    return pl.pallas_call(
        _chunk_attn_kernel,
        grid=(b, pairs, nq),
        in_specs=[blk(0, False),
                  blk(pairs, True), blk(pairs, False),
                  blk(2 * pairs, True), blk(2 * pairs, False),
                  pl.BlockSpec((2, TQ_A, TQ_A + LEFT), lambda bi, p, i: (p, 0, 0))],
        out_specs=pl.BlockSpec((None, TQ_A, LANES), lambda bi, p, i: (bi, i, p)),
        out_shape=jax.ShapeDtypeStruct((b, s, WIDTH_A), BF16),
        compiler_params=_params(3),
        name="chunk_attn",
    )(proj, proj, proj, proj, proj, bias)


def _sb_block(q, k_ref, v_ref, sl, j, later, tri, causal):
    t = TB_SB
    start = pl.multiple_of(j * t, t)
    k = k_ref[pl.ds(start, t), sl]
    v = v_ref[pl.ds(start, t), sl]
    z = lax.dot_general(q, k, _NT, preferred_element_type=F32)
    sp = jnp.maximum(z, 0.0) + jnp.log(1.0 + jnp.exp2(jnp.abs(z) * (-LOG2E)))
    if causal is not None:
        sp = jnp.where(causal, sp, 0.0)
    csum = jnp.dot(sp.astype(BF16), tri, preferred_element_type=F32) + later
    wgt = jnp.exp(z - csum)
    if causal is not None:
        wgt = jnp.where(causal, wgt, 0.0)
    pv = jnp.dot(wgt.astype(BF16), v, preferred_element_type=F32)
    return pv, csum[:, 0:1]


def _sb_attn_kernel(q_ref, k_ref, v_ref, o_ref):
    i = pl.program_id(2)
    t = TB_SB
    row = lax.broadcasted_iota(jnp.int32, (t, t), 0)
    col = lax.broadcasted_iota(jnp.int32, (t, t), 1)
    tri = jnp.where(row >= col, 1.0, 0.0).astype(BF16)
    causal = col < row
    sls = [slice(h * HEAD_DIM, (h + 1) * HEAD_DIM) for h in range(2)]
    qs = [q_ref[:, sl] for sl in sls]
    block = functools.partial(_sb_block, tri=tri)

    carry = []
    for h in range(2):
        pv, later = block(qs[h], k_ref, v_ref, sls[h], i, jnp.zeros((t, 1), F32), causal=causal)
        carry += [later, pv]

    def pending(c):
        return (c[0] >= 0) & (jnp.minimum(jnp.min(c[1]), jnp.min(c[3])) < SB_UNDERFLOW)

    def step(c):
        out = [c[0] - 1]
        for h in range(2):
            pv, later = block(qs[h], k_ref, v_ref, sls[h], c[0], c[1 + 2 * h], causal=None)
            out += [later, c[2 + 2 * h] + pv]
        return tuple(out)

    carry = lax.while_loop(pending, step, (i - 1, *carry))
    for h in range(2):
        o_ref[:, sls[h]] = carry[2 + 2 * h].astype(o_ref.dtype)


def _sb_attention(proj, b, s):
    nq = s // TB_SB
    pairs = WIDTH_B // LANES
    base = 3 * WIDTH_A // LANES
    return pl.pallas_call(
        _sb_attn_kernel,
        grid=(b, pairs, nq),
        in_specs=[pl.BlockSpec((None, TB_SB, LANES), lambda bi, p, i: (bi, i, base + p)),
                  pl.BlockSpec((None, s, LANES), lambda bi, p, i: (bi, 0, base + pairs + p)),
                  pl.BlockSpec((None, s, LANES), lambda bi, p, i: (bi, 0, base + 2 * pairs + p))],
        out_specs=pl.BlockSpec((None, TB_SB, LANES), lambda bi, p, i: (bi, i, p)),
        out_shape=jax.ShapeDtypeStruct((b, s, WIDTH_B), BF16),
        compiler_params=_params(3),
        name="sb_attn",
    )(proj, proj, proj)


def _mixer_out_kernel(oa_ref, ob_ref, x_ref, ga_ref, gb_ref, wout_ref, lng_ref, lnb_ref, wq_ref,
                      x1_ref, qm_ref):
    def rms(ref, g_ref):
        a = ref[...].astype(F32)
        return (a * lax.rsqrt(jnp.mean(a * a, axis=-1, keepdims=True) + RMS_EPS) * g_ref[...]).astype(BF16)

    y = jnp.dot(rms(oa_ref, ga_ref), wout_ref[:WIDTH_A, :], preferred_element_type=F32)
    y = y + jnp.dot(rms(ob_ref, gb_ref), wout_ref[WIDTH_A:, :], preferred_element_type=F32)
    x1 = _layer_norm(DEEPNORM_ALPHA * x_ref[...] + y, lng_ref[...], lnb_ref[...])
    x1_ref[...] = x1
    qm_ref[...] = jnp.dot(x1.astype(BF16), wq_ref[...], preferred_element_type=F32).astype(BF16)


def _mixer_out(out_a, out_b, x, g_a, g_b, w_out, ln_g, ln_b, w_q):
    n = x.shape[0]
    tm = TM_TOK
    row = lambda width: pl.BlockSpec((tm, width), lambda i: (i, 0))
    full = lambda r, c: pl.BlockSpec((r, c), lambda i: (0, 0))
    return pl.pallas_call(
        _mixer_out_kernel,
        grid=(n // tm,),
        in_specs=[row(WIDTH_A), row(WIDTH_B), row(D_MODEL), full(1, WIDTH_A), full(1, WIDTH_B),
                  full(MIX_WIDTH, D_MODEL), full(1, D_MODEL), full(1, D_MODEL), full(D_MODEL, D_MODEL)],
        out_specs=[row(D_MODEL), row(D_MODEL)],
        out_shape=[jax.ShapeDtypeStruct((n, D_MODEL), F32), jax.ShapeDtypeStruct((n, D_MODEL), BF16)],
        compiler_params=_params(1),
        name="mixer_out",
    )(out_a, out_b, x, g_a, g_b, w_out, ln_g, ln_b, w_q)


def _mem_attn_kernel(q_ref, kv_ref, x1_ref, wo_ref, lng_ref, lnb_ref, wr_ref, br_ref,
                     x2_ref, idx_ref, gate_ref):
    heads = []
    for h in range(N_HEADS_MEM):
        sl = slice(h * HEAD_DIM_MEM, (h + 1) * HEAD_DIM_MEM)
        q = q_ref[:, sl] * (HEAD_DIM_MEM ** -0.5)
        k = kv_ref[:, sl]
        v = kv_ref[:, D_MODEL + h * HEAD_DIM_MEM:D_MODEL + (h + 1) * HEAD_DIM_MEM]
        s = lax.dot_general(q, k, _NT, preferred_element_type=F32)
        m = jnp.max(s, axis=-1, keepdims=True)
        p = jnp.exp(s - m)
        l = jnp.sum(p, axis=-1, keepdims=True)
        heads.append((jnp.dot(p.astype(BF16), v, preferred_element_type=F32) / l).astype(BF16))
    o = jnp.concatenate(heads, axis=-1)
    y = jnp.dot(o, wo_ref[...], preferred_element_type=F32)
    x2 = _layer_norm(DEEPNORM_ALPHA * x1_ref[...] + y, lng_ref[...], lnb_ref[...])
    _store_token_major(x2_ref, x2)

    logits = jnp.dot(x2.astype(BF16), wr_ref[...], preferred_element_type=F32) + br_ref[...]
    tm = logits.shape[0]
    e_iota = lax.broadcasted_iota(jnp.int32, (tm, N_EXPERTS), 1)
    lane = lax.broadcasted_iota(jnp.int32, (tm, LANES), 1)
    idx_out = jnp.zeros((tm, LANES), jnp.int32)
    val_out = jnp.zeros((tm, LANES), F32)
    top = None
    denom = jnp.zeros((tm, 1), F32)
    for kk in range(TOP_K):
        m = jnp.max(logits, axis=-1, keepdims=True)
        sel = jnp.min(jnp.where(logits == m, e_iota, N_EXPERTS), axis=-1, keepdims=True)
        if top is None:
            top = m
        e = jnp.exp(m - top)
        denom = denom + e
        idx_out = jnp.where(lane == kk, sel, idx_out)
        val_out = jnp.where(lane == kk, e, val_out)
        logits = jnp.where(e_iota == sel, -jnp.inf, logits)
    idx_ref[...] = idx_out
    gate_ref[...] = val_out / denom


def _mem_attn(qm, kv, x1, w_o, ln_g, ln_b, w_r, b_r, b, s):
    tm = TM_TOK
    nt = s // tm
    mem_len = kv.shape[1]
    row = lambda width: pl.BlockSpec((tm, width), lambda bi, i: (bi * nt + i, 0))
    full = lambda r, c: pl.BlockSpec((r, c), lambda bi, i: (0, 0))
    n = b * s
    return pl.pallas_call(
        _mem_attn_kernel,
        grid=(b, nt),
        in_specs=[row(D_MODEL), pl.BlockSpec((None, mem_len, 2 * D_MODEL), lambda bi, i: (bi, 0, 0)),
                  row(D_MODEL), full(D_MODEL, D_MODEL), full(1, D_MODEL), full(1, D_MODEL),
                  full(D_MODEL, N_EXPERTS), full(1, N_EXPERTS)],
        out_specs=[pl.BlockSpec((tm * LANE_BLOCKS, LANES), lambda bi, i: (bi * nt + i, 0)), row(LANES), row(LANES)],
        out_shape=[jax.ShapeDtypeStruct((n * LANE_BLOCKS, LANES), F32),
                   jax.ShapeDtypeStruct((n, LANES), jnp.int32),
                   jax.ShapeDtypeStruct((n, LANES), F32)],
        compiler_params=_params(2),
        name="mem_attn",
    )(qm, kv, x1, w_o, ln_g, ln_b, w_r, b_r)


def _gather_rows(idx_ref, first, count, src_hbm, dst_ref, sem):
    for r in range(count):
        src = pl.ds(pl.multiple_of(idx_ref[first + r], LANE_BLOCKS), LANE_BLOCKS)
        pltpu.make_async_copy(src_hbm.at[src], dst_ref.at[pl.ds(r * LANE_BLOCKS, LANE_BLOCKS)], sem).start()


def _wait_rows(count, src_hbm, dst_ref, sem):
    pltpu.make_async_copy(src_hbm.at[pl.ds(0, count * LANE_BLOCKS)], dst_ref, sem).wait()


def _moe_kernel(bexp_ref, nused_ref, tok_ref, x_hbm, wgu_ref, bgu_ref, wd_ref, bd_ref,
                y_ref, xbuf, sem):
    i = pl.program_id(0)
    slot = i % MOE_SLOTS
    ahead = MOE_SLOTS - 1
    n_used = nused_ref[0]

    @pl.when(i == 0)
    def _():
        for b in range(ahead):
            _gather_rows(tok_ref, b * TM_MOE, TM_MOE, x_hbm, xbuf.at[b], sem.at[b])

    @pl.when(i < n_used + ahead)
    def _():
        _wait_rows(TM_MOE, x_hbm, xbuf.at[slot], sem.at[slot])

    @pl.when(i < n_used)
    def _():
        x = _load_token_major(xbuf.at[slot], 0, TM_MOE).astype(BF16)
        nxt = (i + ahead) % MOE_SLOTS
        _gather_rows(tok_ref, (i + ahead) * TM_MOE, TM_MOE, x_hbm, xbuf.at[nxt], sem.at[nxt])
        gu = jnp.dot(x, wgu_ref[0], preferred_element_type=F32) + bgu_ref[0]
        gate = jnp.minimum(gu[:, :D_FF], SWIGLU_LIMIT)
        up = jnp.clip(gu[:, D_FF:], -SWIGLU_LIMIT, SWIGLU_LIMIT)
        glu = gate * jax.nn.sigmoid(gate * SWIGLU_ALPHA)
        hid = ((up + 1.0) * glu).astype(BF16)
        _store_token_major(y_ref, jnp.dot(hid, wd_ref[0], preferred_element_type=F32) + bd_ref[0])

    @pl.when(i >= n_used)
    def _():
        y_ref[...] = jnp.zeros_like(y_ref)


def _moe_experts(block_expert, n_used, row_token, x2, w_gu, b_gu, w_d, b_d):
    nb = block_expert.shape[0]
    grid_spec = pltpu.PrefetchScalarGridSpec(
        num_scalar_prefetch=3,
        grid=(nb,),
        in_specs=[pl.BlockSpec(memory_space=pl.ANY),
                  pl.BlockSpec((1, D_MODEL, 2 * D_FF), lambda i, be, nu, tok: (be[i], 0, 0)),
                  pl.BlockSpec((1, 1, 2 * D_FF), lambda i, be, nu, tok: (be[i], 0, 0)),
                  pl.BlockSpec((1, D_FF, D_MODEL), lambda i, be, nu, tok: (be[i], 0, 0)),
                  pl.BlockSpec((1, 1, D_MODEL), lambda i, be, nu, tok: (be[i], 0, 0))],
        out_specs=pl.BlockSpec((TM_MOE * LANE_BLOCKS, LANES), lambda i, be, nu, tok: (i, 0)),
        scratch_shapes=[pltpu.VMEM((MOE_SLOTS, TM_MOE * LANE_BLOCKS, LANES), F32),
                        pltpu.SemaphoreType.DMA((MOE_SLOTS,))],
    )
    return pl.pallas_call(
        _moe_kernel,
        grid_spec=grid_spec,
        out_shape=jax.ShapeDtypeStruct((nb * TM_MOE * LANE_BLOCKS, LANES), F32),
        compiler_params=_params(1),
        name="moe_experts",
    )(block_expert, n_used, row_token * LANE_BLOCKS, x2, w_gu, b_gu, w_d, b_d)


def _combine_kernel(idx_ref, y_hbm, x2_ref, gate_ref, lng_ref, lnb_ref, o_ref, ybuf, sem):
    i = pl.program_id(0)
    nt = pl.num_programs(0)
    slot = i % 2
    rows = TOP_K * TM_COMB

    @pl.when(i == 0)
    def _():
        _gather_rows(idx_ref, 0, rows, y_hbm, ybuf.at[0], sem.at[0])

    @pl.when(i + 1 < nt)
    def _():
        _gather_rows(idx_ref, (i + 1) * rows, rows, y_hbm, ybuf.at[1 - slot], sem.at[1 - slot])

    _wait_rows(rows, y_hbm, ybuf.at[slot], sem.at[slot])

    g = gate_ref[...]
    ff = jnp.zeros((TM_COMB, D_MODEL), F32)
    for kk in range(TOP_K):
        ff = ff + g[:, kk:kk + 1] * _load_token_major(ybuf.at[slot], kk * TM_COMB, TM_COMB)
    x2 = _load_token_major(x2_ref, 0, TM_COMB)
    o_ref[...] = _layer_norm(DEEPNORM_ALPHA * x2 + ff, lng_ref[...], lnb_ref[...])


def _combine(dest, y_rows, x2, gates, ln_g, ln_b):
    n = x2.shape[0] // LANE_BLOCKS
    nt = n // TM_COMB
    rows = TOP_K * TM_COMB
    idx = (dest * LANE_BLOCKS).reshape(nt, TM_COMB, TOP_K).transpose(0, 2, 1).reshape(nt * rows)
    row = lambda width: pl.BlockSpec((TM_COMB, width), lambda i, idx: (i, 0))
    full = lambda r, c: pl.BlockSpec((r, c), lambda i, idx: (0, 0))
    grid_spec = pltpu.PrefetchScalarGridSpec(
        num_scalar_prefetch=1,
        grid=(nt,),
        in_specs=[pl.BlockSpec(memory_space=pl.ANY),
                  pl.BlockSpec((TM_COMB * LANE_BLOCKS, LANES), lambda i, idx: (i, 0)), row(LANES),
                  full(1, D_MODEL), full(1, D_MODEL)],
        out_specs=row(D_MODEL),
        scratch_shapes=[pltpu.VMEM((2, rows * LANE_BLOCKS, LANES), F32), pltpu.SemaphoreType.DMA((2,))],
    )
    return pl.pallas_call(
        _combine_kernel,
        grid_spec=grid_spec,
        out_shape=jax.ShapeDtypeStruct((n, D_MODEL), F32),
        compiler_params=_params(1),
        name="combine",
    )(idx, y_rows, x2, gates, ln_g, ln_b)


def _rank_kernel(idx_ref, dest_ref, counts_ref, running):
    p = pl.program_id(0)
    j = pl.program_id(1)
    t = TM_RANK
    idx = idx_ref[...]
    e_iota = lax.broadcasted_iota(jnp.int32, (t, LANES), 1)
    hot = [idx[:, k:k + 1] == e_iota for k in range(TOP_K)]
    chosen = sum(jnp.where(h, 1.0, 0.0) for h in hot)
    tile_counts = jnp.sum(chosen, axis=0, keepdims=True)

    @pl.when((p == 0) & (j == 0))
    def _():
        running[...] = jnp.zeros_like(running)

    @pl.when(p == 0)
    def _():
        running[...] += tile_counts
        dest_ref[...] = jnp.zeros_like(dest_ref)

    @pl.when((p == 1) & (j == 0))
    def _():
        counts = running[...]
        counts_ref[...] = counts
        padded = jnp.ceil(counts * (1.0 / TM_MOE)) * TM_MOE
        lane = lax.broadcasted_iota(jnp.int32, (1, LANES), 1)
        scan = padded
        shift = 1
        while shift < N_EXPERTS:
            scan = scan + jnp.where(lane >= shift, pltpu.roll(scan, shift, 1), 0.0)
            shift *= 2
        running[...] = scan - padded

    @pl.when(p == 1)
    def _():
        r_iota = lax.broadcasted_iota(jnp.int32, (t, t), 0)
        c_iota = lax.broadcasted_iota(jnp.int32, (t, t), 1)
        earlier = jnp.where(c_iota < r_iota, 1.0, 0.0).astype(BF16)
        base = jnp.dot(earlier, chosen.astype(BF16), preferred_element_type=F32) + running[...]
        out = jnp.zeros((t, LANES), jnp.int32)
        for k in range(TOP_K):
            row = jnp.sum(jnp.where(hot[k], base, 0.0), axis=-1, keepdims=True).astype(jnp.int32)
            out = jnp.where(e_iota == k, row, out)
        dest_ref[...] = out
        running[...] += tile_counts


def _dispatch_plan(top_idx, n):
    n_assign = n * TOP_K
    nb = n_assign // TM_MOE + N_EXPERTS + MOE_SLOTS - 2
    dest, counts = pl.pallas_call(
        _rank_kernel,
        grid=(2, n // TM_RANK),
        in_specs=[pl.BlockSpec((TM_RANK, LANES), lambda p, j: (j, 0))],
        out_specs=[pl.BlockSpec((TM_RANK, LANES), lambda p, j: (p * j, 0)),
                   pl.BlockSpec((1, LANES), lambda p, j: (0, 0))],
        out_shape=[jax.ShapeDtypeStruct((n, LANES), jnp.int32), jax.ShapeDtypeStruct((1, LANES), F32)],
        scratch_shapes=[pltpu.VMEM((1, LANES), F32)],
        compiler_params=_params(2),
        name="expert_rank",
    )(top_idx)
    dest = dest[:, :TOP_K].reshape(-1)
    counts = counts[0, :N_EXPERTS].astype(jnp.int32)
    padded = (counts + TM_MOE - 1) // TM_MOE * TM_MOE
    pad_end = jnp.cumsum(padded)
    row_token = jnp.zeros((nb * TM_MOE,), jnp.int32).at[dest].set(
        jnp.arange(n_assign, dtype=jnp.int32) // TOP_K, unique_indices=True, mode='promise_in_bounds')
    block_first_row = jnp.arange(nb, dtype=jnp.int32) * TM_MOE
    block_expert = jnp.minimum(jnp.sum(pad_end[None, :] <= block_first_row[:, None], axis=1),
                               N_EXPERTS - 1).astype(jnp.int32)
    n_used = (pad_end[-1:] // TM_MOE).astype(jnp.int32)
    return dest, row_token, block_expert, n_used


def kernel(x, mem, w_in, rel_bias, g_group_a, g_group_b, w_out, w_q_mem, w_kv_mem, w_o_mem, w_router, b_router, w_gate_up, b_gate_up, w_down, b_down, ln_g, ln_b):
    b, s, d = x.shape
    n = b * s
    depth = w_in.shape[0]
    xf = x.reshape(n, d)
    for l in range(depth):
        q_scale = jnp.ones((3 * MIX_WIDTH,), F32).at[:WIDTH_A].set(HEAD_DIM ** -0.5)
        q_scale = q_scale.at[3 * WIDTH_A:3 * WIDTH_A + WIDTH_B].set(HEAD_DIM ** -0.5)
        proj = _matmul(xf, (w_in[l] * q_scale).astype(BF16), BF16, TM_TOK, 1024).reshape(b, s, 3 * MIX_WIDTH)
        out_a = _chunk_attention(proj, _chunk_bias(rel_bias[l]), b, s).reshape(n, WIDTH_A)
        out_b = _sb_attention(proj, b, s).reshape(n, WIDTH_B)
        x1, qm = _mixer_out(out_a, out_b, xf, g_group_a[l][None], g_group_b[l][None],
                            w_out[l].astype(BF16), ln_g[l, 0][None], ln_b[l, 0][None],
                            w_q_mem[l].astype(BF16))
        mem_len = mem.shape[1]
        kv = _matmul(mem.reshape(b * mem_len, d), w_kv_mem[l].astype(BF16), BF16, b * mem_len, 1024)
        x2, top_idx, gates = _mem_attn(qm, kv.reshape(b, mem_len, 2 * d), x1, w_o_mem[l].astype(BF16),
                                       ln_g[l, 1][None], ln_b[l, 1][None],
                                       w_router[l].astype(BF16), b_router[l][None], b, s)
        dest, row_token, block_expert, n_used = _dispatch_plan(top_idx, n)
        y_rows = _moe_experts(block_expert, n_used, row_token, x2,
                              w_gate_up[l].astype(BF16), b_gate_up[l][:, None, :],
                              w_down[l].astype(BF16), b_down[l][:, None, :])
        xf = _combine(dest, y_rows, x2, gates, ln_g[l, 2][None], ln_b[l, 2][None])
    return xf.reshape(b, s, d)
```

```python
import functools

import jax
import jax.numpy as jnp
from jax import lax
from jax.experimental import pallas as pl
from jax.experimental.pallas import tpu as pltpu

D_MODEL = 1024
CHUNK = 64
LEFT_CHUNKS = 8
LEFT = LEFT_CHUNKS * CHUNK
HEAD_DIM = 64
N_HEADS_A = 8
N_HEADS_B = 8
WIDTH_A = N_HEADS_A * HEAD_DIM
WIDTH_B = N_HEADS_B * HEAD_DIM
MIX_WIDTH = WIDTH_A + WIDTH_B
REL_CLIP = 128
N_HEADS_MEM = 4
HEAD_DIM_MEM = D_MODEL // N_HEADS_MEM
N_EXPERTS = 32
TOP_K = 4
D_FF = D_MODEL
SWIGLU_LIMIT = 7.0
SWIGLU_ALPHA = 1.702
LN_EPS = 1e-5
RMS_EPS = 1e-6
DEEPNORM_ALPHA = 2.0 ** 0.25
NEG_INF = -1e30
LOG2E = 1.4426950408889634

LANES = 128
LANE_BLOCKS = D_MODEL // LANES
VMEM_LIMIT = 48 * 1024 * 1024

TQ_A = 512
SUB_A = 256
TB_SB = 256
SB_UNDERFLOW = 110.0
TM_TOK = 512
TM_MOE = 256
MOE_SLOTS = 3
TM_COMB = 256
TM_RANK = 512

F32 = jnp.float32
BF16 = jnp.bfloat16
_NT = (((1,), (1,)), ((), ()))


def _params(n_axes):
    return pltpu.CompilerParams(dimension_semantics=("arbitrary",) * n_axes,
                                vmem_limit_bytes=VMEM_LIMIT)


def _store_token_major(ref, value):
    rows = value.shape[0]
    for c in range(LANE_BLOCKS):
        ref[pl.ds(c, rows, stride=LANE_BLOCKS), :] = value[:, c * LANES:(c + 1) * LANES]


def _load_token_major(ref, first_row, rows):
    return jnp.concatenate(
        [ref[pl.ds(first_row * LANE_BLOCKS + c, rows, stride=LANE_BLOCKS), :] for c in range(LANE_BLOCKS)],
        axis=1)


def _layer_norm(r, g, b):
    mu = jnp.mean(r, axis=-1, keepdims=True)
    d = r - mu
    var = jnp.mean(d * d, axis=-1, keepdims=True)
    return d * lax.rsqrt(var + LN_EPS) * g + b


def _matmul_kernel(x_ref, w_ref, o_ref):
    o_ref[...] = jnp.dot(x_ref[...].astype(BF16), w_ref[...],
                         preferred_element_type=F32).astype(o_ref.dtype)


def _matmul(x, w, out_dtype, tm, tn):
    m, k = x.shape
    n = w.shape[1]
    return pl.pallas_call(
        _matmul_kernel,
        grid=(m // tm, n // tn),
        in_specs=[pl.BlockSpec((tm, k), lambda i, j: (i, 0)),
                  pl.BlockSpec((k, tn), lambda i, j: (0, j))],
        out_specs=pl.BlockSpec((tm, tn), lambda i, j: (i, j)),
        out_shape=jax.ShapeDtypeStruct((m, n), out_dtype),
        compiler_params=_params(2),
        name="matmul",
    )(x, w)


def _chunk_attn_kernel(q_ref, kp_ref, kc_ref, vp_ref, vc_ref, bias0_ref, bias1_ref, o_ref):
    w = SUB_A + LEFT
    for h in range(2):
        sl = slice(h * HEAD_DIM, (h + 1) * HEAD_DIM)
        k = jnp.concatenate([kp_ref[:, sl], kc_ref[:, sl]], axis=0)
        v = jnp.concatenate([vp_ref[:, sl], vc_ref[:, sl]], axis=0)
        for sub, bias_ref in enumerate((bias0_ref, bias1_ref)):
            rows = slice(sub * SUB_A, (sub + 1) * SUB_A)
            keys = slice(sub * SUB_A, sub * SUB_A + w)
            s = lax.dot_general(q_ref[rows, sl], k[keys], _NT, preferred_element_type=F32) + bias_ref[h]
            m = jnp.max(s, axis=-1, keepdims=True)
            p = jnp.exp(s - m)
            l = jnp.sum(p, axis=-1, keepdims=True)
            o = jnp.dot(p.astype(BF16), v[keys], preferred_element_type=F32) / l
            o_ref[rows, sl] = o.astype(o_ref.dtype)


def _chunk_bias(rel_bias):
    w = SUB_A + LEFT
    heads = rel_bias.shape[0]
    m = jnp.arange(2 * w)
    rel = jnp.where(m < w, LEFT - m, LEFT + 2 * w - m)
    diag = rel_bias[:, jnp.clip(rel, -REL_CLIP, REL_CLIP) + REL_CLIP].astype(F32)[:, None, :]
    n_tables = 1 + TQ_A // SUB_A
    return pl.pallas_call(
        _chunk_bias_kernel,
        grid=(heads,),
        in_specs=[pl.BlockSpec((1, 1, 2 * w), lambda h: (h, 0, 0))],
        out_specs=pl.BlockSpec((n_tables, 1, SUB_A, w), lambda h: (0, h, 0, 0)),
        out_shape=jax.ShapeDtypeStruct((n_tables, heads, SUB_A, w), F32),
        compiler_params=_params(1),
        name="chunk_bias",
    )(diag)


def _chunk_bias_kernel(diag_ref, o_ref):
    w = SUB_A + LEFT
    rolled = pltpu.roll(jnp.broadcast_to(diag_ref[0], (SUB_A, 2 * w)), 0, 1, stride=1, stride_axis=0)
    col = lax.broadcasted_iota(jnp.int32, (SUB_A, w), 1)
    qc = lax.broadcasted_iota(jnp.int32, (SUB_A, w), 0) // CHUNK
    kc = col // CHUNK
    band = (kc >= qc) & (kc <= qc + LEFT_CHUNKS)
    table = jnp.where(band, rolled[:, :w], NEG_INF)
    o_ref[0, 0] = table
    for sub in range(TQ_A // SUB_A):
        o_ref[1 + sub, 0] = jnp.where(col >= LEFT - sub * SUB_A, table, NEG_INF)


def _chunk_attention(proj, bias, b, s):
    nq = s // TQ_A
    pairs = WIDTH_A // LANES
    blk = lambda off, prev: pl.BlockSpec(
        (None, TQ_A, LANES),
        (lambda bi, p, i: (bi, jnp.maximum(i - 1, 0), off + p)) if prev
        else (lambda bi, p, i: (bi, i, off + p)))
    table = lambda sub: pl.BlockSpec((None, 2, SUB_A, SUB_A + LEFT),
                                     lambda bi, p, i: (jnp.where(i == 0, 1 + sub, 0), p, 0, 0))
    return pl.pallas_call(
        _chunk_attn_kernel,
        grid=(b, pairs, nq),
        in_specs=[blk(0, False),
                  blk(pairs, True), blk(pairs, False),
                  blk(2 * pairs, True), blk(2 * pairs, False),
                  table(0), table(1)],
        out_specs=pl.BlockSpec((None, TQ_A, LANES), lambda bi, p, i: (bi, i, p)),
        out_shape=jax.ShapeDtypeStruct((b, s, WIDTH_A), BF16),
        compiler_params=_params(3),
        name="chunk_attn",
    )(proj, proj, proj, proj, proj, bias, bias)


def _sb_block(q, k_ref, v_ref, sl, j, later, tri, causal):
    t = TB_SB
    start = pl.multiple_of(j * t, t)
    k = k_ref[pl.ds(start, t), sl]
    v = v_ref[pl.ds(start, t), sl]
    z = lax.dot_general(q, k, _NT, preferred_element_type=F32)
    sp = jnp.maximum(z, 0.0) + jnp.log(1.0 + jnp.exp2(jnp.abs(z) * (-LOG2E)))
    if causal is not None:
        sp = jnp.where(causal, sp, 0.0)
    csum = jnp.dot(sp.astype(BF16), tri, preferred_element_type=F32) + later
    wgt = jnp.exp(z - csum)
    if causal is not None:
        wgt = jnp.where(causal, wgt, 0.0)
    pv = jnp.dot(wgt.astype(BF16), v, preferred_element_type=F32)
    return pv, csum[:, 0:1]


def _sb_attn_kernel(q_ref, k_ref, v_ref, o_ref):
    i = pl.program_id(2)
    t = TB_SB
    row = lax.broadcasted_iota(jnp.int32, (t, t), 0)
    col = lax.broadcasted_iota(jnp.int32, (t, t), 1)
    tri = jnp.where(row >= col, 1.0, 0.0).astype(BF16)
    causal = col < row
    sls = [slice(h * HEAD_DIM, (h + 1) * HEAD_DIM) for h in range(2)]
    qs = [q_ref[:, sl] for sl in sls]
    block = functools.partial(_sb_block, tri=tri)

    carry = []
    for h in range(2):
        pv, later = block(qs[h], k_ref, v_ref, sls[h], i, jnp.zeros((t, 1), F32), causal=causal)
        carry += [later, pv]

    def pending(c):
        return (c[0] >= 0) & (jnp.minimum(jnp.min(c[1]), jnp.min(c[3])) < SB_UNDERFLOW)

    def step(c):
        out = [c[0] - 1]
        for h in range(2):
            pv, later = block(qs[h], k_ref, v_ref, sls[h], c[0], c[1 + 2 * h], causal=None)
            out += [later, c[2 + 2 * h] + pv]
        return tuple(out)

    carry = lax.while_loop(pending, step, (i - 1, *carry))
    for h in range(2):
        o_ref[:, sls[h]] = carry[2 + 2 * h].astype(o_ref.dtype)


def _sb_attention(proj, b, s):
    nq = s // TB_SB
    pairs = WIDTH_B // LANES
    base = 3 * WIDTH_A // LANES
    return pl.pallas_call(
        _sb_attn_kernel,
        grid=(b, pairs, nq),
        in_specs=[pl.BlockSpec((None, TB_SB, LANES), lambda bi, p, i: (bi, i, base + p)),
                  pl.BlockSpec((None, s, LANES), lambda bi, p, i: (bi, 0, base + pairs + p)),
                  pl.BlockSpec((None, s, LANES), lambda bi, p, i: (bi, 0, base + 2 * pairs + p))],
        out_specs=pl.BlockSpec((None, TB_SB, LANES), lambda bi, p, i: (bi, i, p)),
        out_shape=jax.ShapeDtypeStruct((b, s, WIDTH_B), BF16),
        compiler_params=_params(3),
        name="sb_attn",
    )(proj, proj, proj)


def _mixer_out_kernel(oa_ref, ob_ref, x_ref, ga_ref, gb_ref, wout_ref, lng_ref, lnb_ref, wq_ref,
                      x1_ref, qm_ref):
    def rms(ref, g_ref):
        a = ref[...].astype(F32)
        return (a * lax.rsqrt(jnp.mean(a * a, axis=-1, keepdims=True) + RMS_EPS) * g_ref[...]).astype(BF16)

    y = jnp.dot(rms(oa_ref, ga_ref), wout_ref[:WIDTH_A, :], preferred_element_type=F32)
    y = y + jnp.dot(rms(ob_ref, gb_ref), wout_ref[WIDTH_A:, :], preferred_element_type=F32)
    x1 = _layer_norm(DEEPNORM_ALPHA * x_ref[...] + y, lng_ref[...], lnb_ref[...])
    x1_ref[...] = x1
    qm_ref[...] = jnp.dot(x1.astype(BF16), wq_ref[...], preferred_element_type=F32).astype(BF16)


def _mixer_out(out_a, out_b, x, g_a, g_b, w_out, ln_g, ln_b, w_q):
    n = x.shape[0]
    tm = TM_TOK
    row = lambda width: pl.BlockSpec((tm, width), lambda i: (i, 0))
    full = lambda r, c: pl.BlockSpec((r, c), lambda i: (0, 0))
    return pl.pallas_call(
        _mixer_out_kernel,
        grid=(n // tm,),
        in_specs=[row(WIDTH_A), row(WIDTH_B), row(D_MODEL), full(1, WIDTH_A), full(1, WIDTH_B),
                  full(MIX_WIDTH, D_MODEL), full(1, D_MODEL), full(1, D_MODEL), full(D_MODEL, D_MODEL)],
        out_specs=[row(D_MODEL), row(D_MODEL)],
        out_shape=[jax.ShapeDtypeStruct((n, D_MODEL), F32), jax.ShapeDtypeStruct((n, D_MODEL), BF16)],
        compiler_params=_params(1),
        name="mixer_out",
    )(out_a, out_b, x, g_a, g_b, w_out, ln_g, ln_b, w_q)


def _mem_attn_kernel(q_ref, kv_ref, x1_ref, wo_ref, lng_ref, lnb_ref, wr_ref, br_ref,
                     x2_ref, idx_ref, gate_ref):
    heads = []
    for h in range(N_HEADS_MEM):
        sl = slice(h * HEAD_DIM_MEM, (h + 1) * HEAD_DIM_MEM)
        q = q_ref[:, sl] * (HEAD_DIM_MEM ** -0.5)
        k = kv_ref[:, sl]
        v = kv_ref[:, D_MODEL + h * HEAD_DIM_MEM:D_MODEL + (h + 1) * HEAD_DIM_MEM]
        s = lax.dot_general(q, k, _NT, preferred_element_type=F32)
        m = jnp.max(s, axis=-1, keepdims=True)
        p = jnp.exp(s - m)
        l = jnp.sum(p, axis=-1, keepdims=True)
        heads.append((jnp.dot(p.astype(BF16), v, preferred_element_type=F32) / l).astype(BF16))
    o = jnp.concatenate(heads, axis=-1)
    y = jnp.dot(o, wo_ref[...], preferred_element_type=F32)
    x2 = _layer_norm(DEEPNORM_ALPHA * x1_ref[...] + y, lng_ref[...], lnb_ref[...])
    _store_token_major(x2_ref, x2)

    logits = jnp.dot(x2.astype(BF16), wr_ref[...], preferred_element_type=F32) + br_ref[...]
    tm = logits.shape[0]
    e_iota = lax.broadcasted_iota(jnp.int32, (tm, N_EXPERTS), 1)
    lane = lax.broadcasted_iota(jnp.int32, (tm, LANES), 1)
    idx_out = jnp.zeros((tm, LANES), jnp.int32)
    val_out = jnp.zeros((tm, LANES), F32)
    top = None
    denom = jnp.zeros((tm, 1), F32)
    for kk in range(TOP_K):
        m = jnp.max(logits, axis=-1, keepdims=True)
        sel = jnp.min(jnp.where(logits == m, e_iota, N_EXPERTS), axis=-1, keepdims=True)
        if top is None:
            top = m
        e = jnp.exp(m - top)
        denom = denom + e
        idx_out = jnp.where(lane == kk, sel, idx_out)
        val_out = jnp.where(lane == kk, e, val_out)
        logits = jnp.where(e_iota == sel, -jnp.inf, logits)
    idx_ref[...] = idx_out
    gate_ref[...] = val_out / denom


def _mem_attn(qm, kv, x1, w_o, ln_g, ln_b, w_r, b_r, b, s):
    tm = TM_TOK
    nt = s // tm
    mem_len = kv.shape[1]
    row = lambda width: pl.BlockSpec((tm, width), lambda bi, i: (bi * nt + i, 0))
    full = lambda r, c: pl.BlockSpec((r, c), lambda bi, i: (0, 0))
    n = b * s
    return pl.pallas_call(
        _mem_attn_kernel,
        grid=(b, nt),
        in_specs=[row(D_MODEL), pl.BlockSpec((None, mem_len, 2 * D_MODEL), lambda bi, i: (bi, 0, 0)),
                  row(D_MODEL), full(D_MODEL, D_MODEL), full(1, D_MODEL), full(1, D_MODEL),
                  full(D_MODEL, N_EXPERTS), full(1, N_EXPERTS)],
        out_specs=[pl.BlockSpec((tm * LANE_BLOCKS, LANES), lambda bi, i: (bi * nt + i, 0)), row(LANES), row(LANES)],
        out_shape=[jax.ShapeDtypeStruct((n * LANE_BLOCKS, LANES), F32),
                   jax.ShapeDtypeStruct((n, LANES), jnp.int32),
                   jax.ShapeDtypeStruct((n, LANES), F32)],
        compiler_params=_params(2),
        name="mem_attn",
    )(qm, kv, x1, w_o, ln_g, ln_b, w_r, b_r)


def _gather_rows(idx_ref, first, count, src_hbm, dst_ref, sem):
    for r in range(count):
        src = pl.ds(pl.multiple_of(idx_ref[first + r], LANE_BLOCKS), LANE_BLOCKS)
        pltpu.make_async_copy(src_hbm.at[src], dst_ref.at[pl.ds(r * LANE_BLOCKS, LANE_BLOCKS)], sem).start()


def _wait_rows(count, src_hbm, dst_ref, sem):
    pltpu.make_async_copy(src_hbm.at[pl.ds(0, count * LANE_BLOCKS)], dst_ref, sem).wait()


def _moe_kernel(bexp_ref, nused_ref, tok_ref, x_hbm, wgu_ref, bgu_ref, wd_ref, bd_ref,
                y_ref, xbuf, sem):
    i = pl.program_id(0)
    slot = i % MOE_SLOTS
    ahead = MOE_SLOTS - 1
    n_used = nused_ref[0]

    @pl.when(i == 0)
    def _():
        for b in range(ahead):
            _gather_rows(tok_ref, b * TM_MOE, TM_MOE, x_hbm, xbuf.at[b], sem.at[b])

    @pl.when(i < n_used + ahead)
    def _():
        _wait_rows(TM_MOE, x_hbm, xbuf.at[slot], sem.at[slot])

    @pl.when(i < n_used)
    def _():
        x = _load_token_major(xbuf.at[slot], 0, TM_MOE).astype(BF16)
        nxt = (i + ahead) % MOE_SLOTS
        _gather_rows(tok_ref, (i + ahead) * TM_MOE, TM_MOE, x_hbm, xbuf.at[nxt], sem.at[nxt])
        gu = jnp.dot(x, wgu_ref[0], preferred_element_type=F32) + bgu_ref[0]
        gate = jnp.minimum(gu[:, :D_FF], SWIGLU_LIMIT)
        up = jnp.clip(gu[:, D_FF:], -SWIGLU_LIMIT, SWIGLU_LIMIT)
        glu = gate * jax.nn.sigmoid(gate * SWIGLU_ALPHA)
        hid = ((up + 1.0) * glu).astype(BF16)
        _store_token_major(y_ref, jnp.dot(hid, wd_ref[0], preferred_element_type=F32) + bd_ref[0])

    @pl.when(i >= n_used)
    def _():
        y_ref[...] = jnp.zeros_like(y_ref)


def _moe_experts(block_expert, n_used, row_token, x2, w_gu, b_gu, w_d, b_d):
    nb = block_expert.shape[0]
    grid_spec = pltpu.PrefetchScalarGridSpec(
        num_scalar_prefetch=3,
        grid=(nb,),
        in_specs=[pl.BlockSpec(memory_space=pl.ANY),
                  pl.BlockSpec((1, D_MODEL, 2 * D_FF), lambda i, be, nu, tok: (be[i], 0, 0)),
                  pl.BlockSpec((1, 1, 2 * D_FF), lambda i, be, nu, tok: (be[i], 0, 0)),
                  pl.BlockSpec((1, D_FF, D_MODEL), lambda i, be, nu, tok: (be[i], 0, 0)),
                  pl.BlockSpec((1, 1, D_MODEL), lambda i, be, nu, tok: (be[i], 0, 0))],
        out_specs=pl.BlockSpec((TM_MOE * LANE_BLOCKS, LANES), lambda i, be, nu, tok: (i, 0)),
        scratch_shapes=[pltpu.VMEM((MOE_SLOTS, TM_MOE * LANE_BLOCKS, LANES), F32),
                        pltpu.SemaphoreType.DMA((MOE_SLOTS,))],
    )
    return pl.pallas_call(
        _moe_kernel,
        grid_spec=grid_spec,
        out_shape=jax.ShapeDtypeStruct((nb * TM_MOE * LANE_BLOCKS, LANES), F32),
        compiler_params=_params(1),
        name="moe_experts",
    )(block_expert, n_used, row_token * LANE_BLOCKS, x2, w_gu, b_gu, w_d, b_d)


def _combine_kernel(idx_ref, y_hbm, x2_ref, gate_ref, lng_ref, lnb_ref, o_ref, ybuf, sem):
    i = pl.program_id(0)
    nt = pl.num_programs(0)
    slot = i % 2
    rows = TOP_K * TM_COMB

    @pl.when(i == 0)
    def _():
        _gather_rows(idx_ref, 0, rows, y_hbm, ybuf.at[0], sem.at[0])

    @pl.when(i + 1 < nt)
    def _():
        _gather_rows(idx_ref, (i + 1) * rows, rows, y_hbm, ybuf.at[1 - slot], sem.at[1 - slot])

    _wait_rows(rows, y_hbm, ybuf.at[slot], sem.at[slot])

    g = gate_ref[...]
    ff = jnp.zeros((TM_COMB, D_MODEL), F32)
    for kk in range(TOP_K):
        ff = ff + g[:, kk:kk + 1] * _load_token_major(ybuf.at[slot], kk * TM_COMB, TM_COMB)
    x2 = _load_token_major(x2_ref, 0, TM_COMB)
    o_ref[...] = _layer_norm(DEEPNORM_ALPHA * x2 + ff, lng_ref[...], lnb_ref[...])


def _combine(dest, y_rows, x2, gates, ln_g, ln_b):
    n = x2.shape[0] // LANE_BLOCKS
    nt = n // TM_COMB
    rows = TOP_K * TM_COMB
    idx = (dest * LANE_BLOCKS).reshape(nt, TM_COMB, TOP_K).transpose(0, 2, 1).reshape(nt * rows)
    row = lambda width: pl.BlockSpec((TM_COMB, width), lambda i, idx: (i, 0))
    full = lambda r, c: pl.BlockSpec((r, c), lambda i, idx: (0, 0))
    grid_spec = pltpu.PrefetchScalarGridSpec(
        num_scalar_prefetch=1,
        grid=(nt,),
        in_specs=[pl.BlockSpec(memory_space=pl.ANY),
                  pl.BlockSpec((TM_COMB * LANE_BLOCKS, LANES), lambda i, idx: (i, 0)), row(LANES),
                  full(1, D_MODEL), full(1, D_MODEL)],
        out_specs=row(D_MODEL),
        scratch_shapes=[pltpu.VMEM((2, rows * LANE_BLOCKS, LANES), F32), pltpu.SemaphoreType.DMA((2,))],
    )
    return pl.pallas_call(
        _combine_kernel,
        grid_spec=grid_spec,
        out_shape=jax.ShapeDtypeStruct((n, D_MODEL), F32),
        compiler_params=_params(1),
        name="combine",
    )(idx, y_rows, x2, gates, ln_g, ln_b)


def _rank_kernel(idx_ref, dest_ref, counts_ref, running):
    p = pl.program_id(0)
    j = pl.program_id(1)
    t = TM_RANK
    idx = idx_ref[...]
    e_iota = lax.broadcasted_iota(jnp.int32, (t, LANES), 1)
    hot = [idx[:, k:k + 1] == e_iota for k in range(TOP_K)]
    chosen = sum(jnp.where(h, 1.0, 0.0) for h in hot)
    tile_counts = jnp.sum(chosen, axis=0, keepdims=True)

    @pl.when((p == 0) & (j == 0))
    def _():
        running[...] = jnp.zeros_like(running)

    @pl.when(p == 0)
    def _():
        running[...] += tile_counts
        dest_ref[...] = jnp.zeros_like(dest_ref)

    @pl.when((p == 1) & (j == 0))
    def _():
        counts = running[...]
        counts_ref[...] = counts
        padded = jnp.ceil(counts * (1.0 / TM_MOE)) * TM_MOE
        lane = lax.broadcasted_iota(jnp.int32, (1, LANES), 1)
        scan = padded
        shift = 1
        while shift < N_EXPERTS:
            scan = scan + jnp.where(lane >= shift, pltpu.roll(scan, shift, 1), 0.0)
            shift *= 2
        running[...] = scan - padded

    @pl.when(p == 1)
    def _():
        r_iota = lax.broadcasted_iota(jnp.int32, (t, t), 0)
        c_iota = lax.broadcasted_iota(jnp.int32, (t, t), 1)
        earlier = jnp.where(c_iota < r_iota, 1.0, 0.0).astype(BF16)
        base = jnp.dot(earlier, chosen.astype(BF16), preferred_element_type=F32) + running[...]
        out = jnp.zeros((t, LANES), jnp.int32)
        for k in range(TOP_K):
            row = jnp.sum(jnp.where(hot[k], base, 0.0), axis=-1, keepdims=True).astype(jnp.int32)
            out = jnp.where(e_iota == k, row, out)
        dest_ref[...] = out
        running[...] += tile_counts


def _dispatch_plan(top_idx, n):
    n_assign = n * TOP_K
    nb = n_assign // TM_MOE + N_EXPERTS + MOE_SLOTS - 2
    dest, counts = pl.pallas_call(
        _rank_kernel,
        grid=(2, n // TM_RANK),
        in_specs=[pl.BlockSpec((TM_RANK, LANES), lambda p, j: (j, 0))],
        out_specs=[pl.BlockSpec((TM_RANK, LANES), lambda p, j: (p * j, 0)),
                   pl.BlockSpec((1, LANES), lambda p, j: (0, 0))],
        out_shape=[jax.ShapeDtypeStruct((n, LANES), jnp.int32), jax.ShapeDtypeStruct((1, LANES), F32)],
        scratch_shapes=[pltpu.VMEM((1, LANES), F32)],
        compiler_params=_params(2),
        name="expert_rank",
    )(top_idx)
    dest = dest[:, :TOP_K].reshape(-1)
    counts = counts[0, :N_EXPERTS].astype(jnp.int32)
    padded = (counts + TM_MOE - 1) // TM_MOE * TM_MOE
    pad_end = jnp.cumsum(padded)
    row_token = jnp.zeros((nb * TM_MOE,), jnp.int32).at[dest].set(
        jnp.arange(n_assign, dtype=jnp.int32) // TOP_K, unique_indices=True, mode='promise_in_bounds')
    block_first_row = jnp.arange(nb, dtype=jnp.int32) * TM_MOE
    block_expert = jnp.minimum(jnp.sum(pad_end[None, :] <= block_first_row[:, None], axis=1),
                               N_EXPERTS - 1).astype(jnp.int32)
    n_used = (pad_end[-1:] // TM_MOE).astype(jnp.int32)
    return dest, row_token, block_expert, n_used


def kernel(x, mem, w_in, rel_bias, g_group_a, g_group_b, w_out, w_q_mem, w_kv_mem, w_o_mem, w_router, b_router, w_gate_up, b_gate_up, w_down, b_down, ln_g, ln_b):
    b, s, d = x.shape
    n = b * s
    depth = w_in.shape[0]
    xf = x.reshape(n, d)
    for l in range(depth):
        q_scale = jnp.ones((3 * MIX_WIDTH,), F32).at[:WIDTH_A].set(HEAD_DIM ** -0.5)
        q_scale = q_scale.at[3 * WIDTH_A:3 * WIDTH_A + WIDTH_B].set(HEAD_DIM ** -0.5)
        proj = _matmul(xf, (w_in[l] * q_scale).astype(BF16), BF16, TM_TOK, 1024).reshape(b, s, 3 * MIX_WIDTH)
        out_a = _chunk_attention(proj, _chunk_bias(rel_bias[l]), b, s).reshape(n, WIDTH_A)
        out_b = _sb_attention(proj, b, s).reshape(n, WIDTH_B)
        x1, qm = _mixer_out(out_a, out_b, xf, g_group_a[l][None], g_group_b[l][None],
                            w_out[l].astype(BF16), ln_g[l, 0][None], ln_b[l, 0][None],
                            w_q_mem[l].astype(BF16))
        mem_len = mem.shape[1]
        kv = _matmul(mem.reshape(b * mem_len, d), w_kv_mem[l].astype(BF16), BF16, b * mem_len, 1024)
        x2, top_idx, gates = _mem_attn(qm, kv.reshape(b, mem_len, 2 * d), x1, w_o_mem[l].astype(BF16),
                                       ln_g[l, 1][None], ln_b[l, 1][None],
                                       w_router[l].astype(BF16), b_router[l][None], b, s)
        dest, row_token, block_expert, n_used = _dispatch_plan(top_idx, n)
        y_rows = _moe_experts(block_expert, n_used, row_token, x2,
                              w_gate_up[l].astype(BF16), b_gate_up[l][:, None, :],
                              w_down[l].astype(BF16), b_down[l][:, None, :])
        xf = _combine(dest, y_rows, x2, gates, ln_g[l, 2][None], ln_b[l, 2][None])
    return xf.reshape(b, s, d)
```

```python
import functools

import jax
import jax.numpy as jnp
from jax import lax
from jax.experimental import pallas as pl
from jax.experimental.pallas import tpu as pltpu

D_MODEL = 1024
CHUNK = 64
LEFT_CHUNKS = 8
LEFT = LEFT_CHUNKS * CHUNK
HEAD_DIM = 64
N_HEADS_A = 8
N_HEADS_B = 8
WIDTH_A = N_HEADS_A * HEAD_DIM
WIDTH_B = N_HEADS_B * HEAD_DIM
MIX_WIDTH = WIDTH_A + WIDTH_B
REL_CLIP = 128
N_HEADS_MEM = 4
HEAD_DIM_MEM = D_MODEL // N_HEADS_MEM
N_EXPERTS = 32
TOP_K = 4
D_FF = D_MODEL
SWIGLU_LIMIT = 7.0
SWIGLU_ALPHA = 1.702
LN_EPS = 1e-5
RMS_EPS = 1e-6
DEEPNORM_ALPHA = 2.0 ** 0.25
NEG_INF = -1e30
LOG2E = 1.4426950408889634

LANES = 128
LANE_BLOCKS = D_MODEL // LANES
VMEM_LIMIT = 48 * 1024 * 1024

TQ_A = 512
SUB_A = 256
TB_SB = 256
SB_UNDERFLOW = 110.0
TM_TOK = 512
TM_MOE = 256
MOE_SLOTS = 3
TM_COMB = 256
TM_RANK = 512

F32 = jnp.float32
BF16 = jnp.bfloat16
_NT = (((1,), (1,)), ((), ()))


def _params(n_axes):
    return pltpu.CompilerParams(dimension_semantics=("arbitrary",) * n_axes,
                                vmem_limit_bytes=VMEM_LIMIT)


def _store_token_major(ref, value):
    rows = value.shape[0]
    for c in range(LANE_BLOCKS):
        ref[pl.ds(c, rows, stride=LANE_BLOCKS), :] = value[:, c * LANES:(c + 1) * LANES]


def _load_token_major(ref, first_row, rows):
    return jnp.concatenate(
        [ref[pl.ds(first_row * LANE_BLOCKS + c, rows, stride=LANE_BLOCKS), :] for c in range(LANE_BLOCKS)],
        axis=1)


def _layer_norm(r, g, b):
    mu = jnp.mean(r, axis=-1, keepdims=True)
    d = r - mu
    var = jnp.mean(d * d, axis=-1, keepdims=True)
    return d * lax.rsqrt(var + LN_EPS) * g + b


def _matmul_kernel(x_ref, w_ref, o_ref):
    o_ref[...] = jnp.dot(x_ref[...].astype(BF16), w_ref[...],
                         preferred_element_type=F32).astype(o_ref.dtype)


def _matmul(x, w, out_dtype, tm, tn):
    m, k = x.shape
    n = w.shape[1]
    return pl.pallas_call(
        _matmul_kernel,
        grid=(m // tm, n // tn),
        in_specs=[pl.BlockSpec((tm, k), lambda i, j: (i, 0)),
                  pl.BlockSpec((k, tn), lambda i, j: (0, j))],
        out_specs=pl.BlockSpec((tm, tn), lambda i, j: (i, j)),
        out_shape=jax.ShapeDtypeStruct((m, n), out_dtype),
        compiler_params=_params(2),
        name="matmul",
    )(x, w)


def _chunk_attn_kernel(q_ref, kp_ref, kc_ref, vp_ref, vc_ref, bias0_ref, bias1_ref, o_ref):
    w = SUB_A + LEFT
    for h in range(2):
        sl = slice(h * HEAD_DIM, (h + 1) * HEAD_DIM)
        k = jnp.concatenate([kp_ref[:, sl], kc_ref[:, sl]], axis=0)
        v = jnp.concatenate([vp_ref[:, sl], vc_ref[:, sl]], axis=0)
        for sub, bias_ref in enumerate((bias0_ref, bias1_ref)):
            rows = slice(sub * SUB_A, (sub + 1) * SUB_A)
            keys = slice(sub * SUB_A, sub * SUB_A + w)
            s = lax.dot_general(q_ref[rows, sl], k[keys], _NT, preferred_element_type=F32) + bias_ref[h]
            m = jnp.max(s, axis=-1, keepdims=True)
            p = jnp.exp(s - m)
            l = jnp.sum(p, axis=-1, keepdims=True)
            o = jnp.dot(p.astype(BF16), v[keys], preferred_element_type=F32) / l
            o_ref[rows, sl] = o.astype(o_ref.dtype)


def _chunk_bias(rel_bias):
    w = SUB_A + LEFT
    heads = rel_bias.shape[0]
    m = jnp.arange(2 * w)
    rel = jnp.where(m < w, LEFT - m, LEFT + 2 * w - m)
    diag = rel_bias[:, jnp.clip(rel, -REL_CLIP, REL_CLIP) + REL_CLIP].astype(F32)[:, None, :]
    n_tables = 1 + TQ_A // SUB_A
    return pl.pallas_call(
        _chunk_bias_kernel,
        grid=(heads,),
        in_specs=[pl.BlockSpec((1, 1, 2 * w), lambda h: (h, 0, 0))],
        out_specs=pl.BlockSpec((n_tables, 1, SUB_A, w), lambda h: (0, h, 0, 0)),
        out_shape=jax.ShapeDtypeStruct((n_tables, heads, SUB_A, w), F32),
        compiler_params=_params(1),
        name="chunk_bias",
    )(diag)


def _chunk_bias_kernel(diag_ref, o_ref):
    w = SUB_A + LEFT
    rolled = pltpu.roll(jnp.broadcast_to(diag_ref[0], (SUB_A, 2 * w)), 0, 1, stride=1, stride_axis=0)
    col = lax.broadcasted_iota(jnp.int32, (SUB_A, w), 1)
    qc = lax.broadcasted_iota(jnp.int32, (SUB_A, w), 0) // CHUNK
    kc = col // CHUNK
    band = (kc >= qc) & (kc <= qc + LEFT_CHUNKS)
    table = jnp.where(band, rolled[:, :w], NEG_INF)
    o_ref[0, 0] = table
    for sub in range(TQ_A // SUB_A):
        o_ref[1 + sub, 0] = jnp.where(col >= LEFT - sub * SUB_A, table, NEG_INF)


def _chunk_attention(proj, bias, b, s):
    nq = s // TQ_A
    pairs = WIDTH_A // LANES
    blk = lambda off, prev: pl.BlockSpec(
        (None, TQ_A, LANES),
        (lambda bi, p, i: (bi, jnp.maximum(i - 1, 0), off + p)) if prev
        else (lambda bi, p, i: (bi, i, off + p)))
    table = lambda sub: pl.BlockSpec((None, 2, SUB_A, SUB_A + LEFT),
                                     lambda bi, p, i: (jnp.where(i == 0, 1 + sub, 0), p, 0, 0))
    return pl.pallas_call(
        _chunk_attn_kernel,
        grid=(b, pairs, nq),
        in_specs=[blk(0, False),
                  blk(pairs, True), blk(pairs, False),
                  blk(2 * pairs, True), blk(2 * pairs, False),
                  table(0), table(1)],
        out_specs=pl.BlockSpec((None, TQ_A, LANES), lambda bi, p, i: (bi, i, p)),
        out_shape=jax.ShapeDtypeStruct((b, s, WIDTH_A), BF16),
        compiler_params=_params(3),
        name="chunk_attn",
    )(proj, proj, proj, proj, proj, bias, bias)


def _sb_block(q, k_ref, v_ref, sl, j, later, tri, causal):
    t = TB_SB
    start = pl.multiple_of(j * t, t)
    k = k_ref[pl.ds(start, t), sl]
    v = v_ref[pl.ds(start, t), sl]
    z = lax.dot_general(q, k, _NT, preferred_element_type=F32)
    sp = jnp.maximum(z, 0.0) + jnp.log(1.0 + jnp.exp2(jnp.abs(z) * (-LOG2E)))
    if causal is not None:
        sp = jnp.where(causal, sp, 0.0)
    csum = jnp.dot(sp.astype(BF16), tri, preferred_element_type=F32) + later
    wgt = jnp.exp(z - csum)
    if causal is not None:
        wgt = jnp.where(causal, wgt, 0.0)
    pv = jnp.dot(wgt.astype(BF16), v, preferred_element_type=F32)
    return pv, csum[:, 0:1]


def _sb_attn_kernel(q_ref, k_ref, v_ref, o_ref):
    i = pl.program_id(2)
    t = TB_SB
    row = lax.broadcasted_iota(jnp.int32, (t, t), 0)
    col = lax.broadcasted_iota(jnp.int32, (t, t), 1)
    tri = jnp.where(row >= col, 1.0, 0.0).astype(BF16)
    causal = col < row
    sls = [slice(h * HEAD_DIM, (h + 1) * HEAD_DIM) for h in range(2)]
    qs = [q_ref[:, sl] for sl in sls]
    block = functools.partial(_sb_block, tri=tri)

    has_prev = i >= 1
    carry = []
    for h in range(2):
        pv, later = block(qs[h], k_ref, v_ref, sls[h], i, jnp.zeros((t, 1), F32), causal=causal)
        pv_prev, later_prev = block(qs[h], k_ref, v_ref, sls[h], jnp.maximum(i - 1, 0), later, causal=None)
        carry += [jnp.where(has_prev, later_prev, later), pv + jnp.where(has_prev, pv_prev, 0.0)]

    def pending(c):
        return (c[0] >= 0) & (jnp.minimum(jnp.min(c[1]), jnp.min(c[3])) < SB_UNDERFLOW)

    def step(c):
        out = [c[0] - 1]
        for h in range(2):
            pv, later = block(qs[h], k_ref, v_ref, sls[h], c[0], c[1 + 2 * h], causal=None)
            out += [later, c[2 + 2 * h] + pv]
        return tuple(out)

    carry = lax.while_loop(pending, step, (i - 2, *carry))
    for h in range(2):
        o_ref[:, sls[h]] = carry[2 + 2 * h].astype(o_ref.dtype)


def _sb_attention(proj, b, s):
    nq = s // TB_SB
    pairs = WIDTH_B // LANES
    base = 3 * WIDTH_A // LANES
    return pl.pallas_call(
        _sb_attn_kernel,
        grid=(b, pairs, nq),
        in_specs=[pl.BlockSpec((None, TB_SB, LANES), lambda bi, p, i: (bi, i, base + p)),
                  pl.BlockSpec((None, s, LANES), lambda bi, p, i: (bi, 0, base + pairs + p)),
                  pl.BlockSpec((None, s, LANES), lambda bi, p, i: (bi, 0, base + 2 * pairs + p))],
        out_specs=pl.BlockSpec((None, TB_SB, LANES), lambda bi, p, i: (bi, i, p)),
        out_shape=jax.ShapeDtypeStruct((b, s, WIDTH_B), BF16),
        compiler_params=_params(3),
        name="sb_attn",
    )(proj, proj, proj)


def _mixer_out_kernel(oa_ref, ob_ref, x_ref, ga_ref, gb_ref, wout_ref, lng_ref, lnb_ref, wq_ref,
                      x1_ref, qm_ref):
    def rms(ref, g_ref):
        a = ref[...].astype(F32)
        return (a * lax.rsqrt(jnp.mean(a * a, axis=-1, keepdims=True) + RMS_EPS) * g_ref[...]).astype(BF16)

    y = jnp.dot(rms(oa_ref, ga_ref), wout_ref[:WIDTH_A, :], preferred_element_type=F32)
    y = y + jnp.dot(rms(ob_ref, gb_ref), wout_ref[WIDTH_A:, :], preferred_element_type=F32)
    x1 = _layer_norm(DEEPNORM_ALPHA * x_ref[...] + y, lng_ref[...], lnb_ref[...])
    x1_ref[...] = x1
    qm_ref[...] = jnp.dot(x1.astype(BF16), wq_ref[...], preferred_element_type=F32).astype(BF16)


def _mixer_out(out_a, out_b, x, g_a, g_b, w_out, ln_g, ln_b, w_q):
    n = x.shape[0]
    tm = TM_TOK
    row = lambda width: pl.BlockSpec((tm, width), lambda i: (i, 0))
    full = lambda r, c: pl.BlockSpec((r, c), lambda i: (0, 0))
    return pl.pallas_call(
        _mixer_out_kernel,
        grid=(n // tm,),
        in_specs=[row(WIDTH_A), row(WIDTH_B), row(D_MODEL), full(1, WIDTH_A), full(1, WIDTH_B),
                  full(MIX_WIDTH, D_MODEL), full(1, D_MODEL), full(1, D_MODEL), full(D_MODEL, D_MODEL)],
        out_specs=[row(D_MODEL), row(D_MODEL)],
        out_shape=[jax.ShapeDtypeStruct((n, D_MODEL), F32), jax.ShapeDtypeStruct((n, D_MODEL), BF16)],
        compiler_params=_params(1),
        name="mixer_out",
    )(out_a, out_b, x, g_a, g_b, w_out, ln_g, ln_b, w_q)


def _mem_attn_kernel(q_ref, kv_ref, x1_ref, wo_ref, lng_ref, lnb_ref, wr_ref, br_ref,
                     x2_ref, idx_ref, gate_ref):
    heads = []
    for h in range(N_HEADS_MEM):
        sl = slice(h * HEAD_DIM_MEM, (h + 1) * HEAD_DIM_MEM)
        q = q_ref[:, sl] * (HEAD_DIM_MEM ** -0.5)
        k = kv_ref[:, sl]
        v = kv_ref[:, D_MODEL + h * HEAD_DIM_MEM:D_MODEL + (h + 1) * HEAD_DIM_MEM]
        s = lax.dot_general(q, k, _NT, preferred_element_type=F32)
        m = jnp.max(s, axis=-1, keepdims=True)
        p = jnp.exp(s - m)
        l = jnp.sum(p, axis=-1, keepdims=True)
        heads.append((jnp.dot(p.astype(BF16), v, preferred_element_type=F32) / l).astype(BF16))
    o = jnp.concatenate(heads, axis=-1)
    y = jnp.dot(o, wo_ref[...], preferred_element_type=F32)
    x2 = _layer_norm(DEEPNORM_ALPHA * x1_ref[...] + y, lng_ref[...], lnb_ref[...])
    _store_token_major(x2_ref, x2)

    logits = jnp.dot(x2.astype(BF16), wr_ref[...], preferred_element_type=F32) + br_ref[...]
    tm = logits.shape[0]
    e_iota = lax.broadcasted_iota(jnp.int32, (tm, N_EXPERTS), 1)
    lane = lax.broadcasted_iota(jnp.int32, (tm, LANES), 1)
    idx_out = jnp.zeros((tm, LANES), jnp.int32)
    val_out = jnp.zeros((tm, LANES), F32)
    top = None
    denom = jnp.zeros((tm, 1), F32)
    for kk in range(TOP_K):
        m = jnp.max(logits, axis=-1, keepdims=True)
        sel = jnp.min(jnp.where(logits == m, e_iota, N_EXPERTS), axis=-1, keepdims=True)
        if top is None:
            top = m
        e = jnp.exp(m - top)
        denom = denom + e
        idx_out = jnp.where(lane == kk, sel, idx_out)
        val_out = jnp.where(lane == kk, e, val_out)
        logits = jnp.where(e_iota == sel, -jnp.inf, logits)
    idx_ref[...] = idx_out
    gate_ref[...] = val_out / denom


def _mem_attn(qm, kv, x1, w_o, ln_g, ln_b, w_r, b_r, b, s):
    tm = TM_TOK
    nt = s // tm
    mem_len = kv.shape[1]
    row = lambda width: pl.BlockSpec((tm, width), lambda bi, i: (bi * nt + i, 0))
    full = lambda r, c: pl.BlockSpec((r, c), lambda bi, i: (0, 0))
    n = b * s
    return pl.pallas_call(
        _mem_attn_kernel,
        grid=(b, nt),
        in_specs=[row(D_MODEL), pl.BlockSpec((None, mem_len, 2 * D_MODEL), lambda bi, i: (bi, 0, 0)),
                  row(D_MODEL), full(D_MODEL, D_MODEL), full(1, D_MODEL), full(1, D_MODEL),
                  full(D_MODEL, N_EXPERTS), full(1, N_EXPERTS)],
        out_specs=[pl.BlockSpec((tm * LANE_BLOCKS, LANES), lambda bi, i: (bi * nt + i, 0)), row(LANES), row(LANES)],
        out_shape=[jax.ShapeDtypeStruct((n * LANE_BLOCKS, LANES), F32),
                   jax.ShapeDtypeStruct((n, LANES), jnp.int32),
                   jax.ShapeDtypeStruct((n, LANES), F32)],
        compiler_params=_params(2),
        name="mem_attn",
    )(qm, kv, x1, w_o, ln_g, ln_b, w_r, b_r)


def _gather_rows(idx_ref, first, count, src_hbm, dst_ref, sem):
    for r in range(count):
        src = pl.ds(pl.multiple_of(idx_ref[first + r], LANE_BLOCKS), LANE_BLOCKS)
        pltpu.make_async_copy(src_hbm.at[src], dst_ref.at[pl.ds(r * LANE_BLOCKS, LANE_BLOCKS)], sem).start()


def _wait_rows(count, src_hbm, dst_ref, sem):
    pltpu.make_async_copy(src_hbm.at[pl.ds(0, count * LANE_BLOCKS)], dst_ref, sem).wait()


def _moe_kernel(bexp_ref, nused_ref, tok_ref, x_hbm, wgu_ref, bgu_ref, wd_ref, bd_ref,
                y_ref, xbuf, sem):
    i = pl.program_id(0)
    slot = i % MOE_SLOTS
    ahead = MOE_SLOTS - 1
    n_used = nused_ref[0]

    @pl.when(i == 0)
    def _():
        for b in range(ahead):
            _gather_rows(tok_ref, b * TM_MOE, TM_MOE, x_hbm, xbuf.at[b], sem.at[b])

    @pl.when(i < n_used + ahead)
    def _():
        _wait_rows(TM_MOE, x_hbm, xbuf.at[slot], sem.at[slot])

    @pl.when(i < n_used)
    def _():
        x = _load_token_major(xbuf.at[slot], 0, TM_MOE).astype(BF16)
        nxt = (i + ahead) % MOE_SLOTS
        _gather_rows(tok_ref, (i + ahead) * TM_MOE, TM_MOE, x_hbm, xbuf.at[nxt], sem.at[nxt])
        gu = jnp.dot(x, wgu_ref[0], preferred_element_type=F32) + bgu_ref[0]
        gate = jnp.minimum(gu[:, :D_FF], SWIGLU_LIMIT)
        up = jnp.clip(gu[:, D_FF:], -SWIGLU_LIMIT, SWIGLU_LIMIT)
        glu = gate * jax.nn.sigmoid(gate * SWIGLU_ALPHA)
        hid = ((up + 1.0) * glu).astype(BF16)
        _store_token_major(y_ref, jnp.dot(hid, wd_ref[0], preferred_element_type=F32) + bd_ref[0])

    @pl.when(i >= n_used)
    def _():
        y_ref[...] = jnp.zeros_like(y_ref)


def _moe_experts(block_expert, n_used, row_token, x2, w_gu, b_gu, w_d, b_d):
    nb = block_expert.shape[0]
    grid_spec = pltpu.PrefetchScalarGridSpec(
        num_scalar_prefetch=3,
        grid=(nb,),
        in_specs=[pl.BlockSpec(memory_space=pl.ANY),
                  pl.BlockSpec((1, D_MODEL, 2 * D_FF), lambda i, be, nu, tok: (be[i], 0, 0)),
                  pl.BlockSpec((1, 1, 2 * D_FF), lambda i, be, nu, tok: (be[i], 0, 0)),
                  pl.BlockSpec((1, D_FF, D_MODEL), lambda i, be, nu, tok: (be[i], 0, 0)),
                  pl.BlockSpec((1, 1, D_MODEL), lambda i, be, nu, tok: (be[i], 0, 0))],
        out_specs=pl.BlockSpec((TM_MOE * LANE_BLOCKS, LANES), lambda i, be, nu, tok: (i, 0)),
        scratch_shapes=[pltpu.VMEM((MOE_SLOTS, TM_MOE * LANE_BLOCKS, LANES), F32),
                        pltpu.SemaphoreType.DMA((MOE_SLOTS,))],
    )
    return pl.pallas_call(
        _moe_kernel,
        grid_spec=grid_spec,
        out_shape=jax.ShapeDtypeStruct((nb * TM_MOE * LANE_BLOCKS, LANES), F32),
        compiler_params=_params(1),
        name="moe_experts",
    )(block_expert, n_used, row_token * LANE_BLOCKS, x2, w_gu, b_gu, w_d, b_d)


def _combine_kernel(idx_ref, y_hbm, x2_ref, gate_ref, lng_ref, lnb_ref, o_ref, ybuf, sem):
    i = pl.program_id(0)
    nt = pl.num_programs(0)
    slot = i % 2
    rows = TOP_K * TM_COMB

    @pl.when(i == 0)
    def _():
        _gather_rows(idx_ref, 0, rows, y_hbm, ybuf.at[0], sem.at[0])

    @pl.when(i + 1 < nt)
    def _():
        _gather_rows(idx_ref, (i + 1) * rows, rows, y_hbm, ybuf.at[1 - slot], sem.at[1 - slot])

    _wait_rows(rows, y_hbm, ybuf.at[slot], sem.at[slot])

    g = gate_ref[...]
    ff = jnp.zeros((TM_COMB, D_MODEL), F32)
    for kk in range(TOP_K):
        ff = ff + g[:, kk:kk + 1] * _load_token_major(ybuf.at[slot], kk * TM_COMB, TM_COMB)
    x2 = _load_token_major(x2_ref, 0, TM_COMB)
    o_ref[...] = _layer_norm(DEEPNORM_ALPHA * x2 + ff, lng_ref[...], lnb_ref[...])


def _combine(dest, y_rows, x2, gates, ln_g, ln_b):
    n = x2.shape[0] // LANE_BLOCKS
    nt = n // TM_COMB
    rows = TOP_K * TM_COMB
    idx = (dest * LANE_BLOCKS).reshape(nt, TM_COMB, TOP_K).transpose(0, 2, 1).reshape(nt * rows)
    row = lambda width: pl.BlockSpec((TM_COMB, width), lambda i, idx: (i, 0))
    full = lambda r, c: pl.BlockSpec((r, c), lambda i, idx: (0, 0))
    grid_spec = pltpu.PrefetchScalarGridSpec(
        num_scalar_prefetch=1,
        grid=(nt,),
        in_specs=[pl.BlockSpec(memory_space=pl.ANY),
                  pl.BlockSpec((TM_COMB * LANE_BLOCKS, LANES), lambda i, idx: (i, 0)), row(LANES),
                  full(1, D_MODEL), full(1, D_MODEL)],
        out_specs=row(D_MODEL),
        scratch_shapes=[pltpu.VMEM((2, rows * LANE_BLOCKS, LANES), F32), pltpu.SemaphoreType.DMA((2,))],
    )
    return pl.pallas_call(
        _combine_kernel,
        grid_spec=grid_spec,
        out_shape=jax.ShapeDtypeStruct((n, D_MODEL), F32),
        compiler_params=_params(1),
        name="combine",
    )(idx, y_rows, x2, gates, ln_g, ln_b)


def _rank_kernel(idx_ref, dest_ref, counts_ref, running):
    p = pl.program_id(0)
    j = pl.program_id(1)
    t = TM_RANK
    idx = idx_ref[...]
    e_iota = lax.broadcasted_iota(jnp.int32, (t, LANES), 1)
    hot = [idx[:, k:k + 1] == e_iota for k in range(TOP_K)]
    chosen = sum(jnp.where(h, 1.0, 0.0) for h in hot)
    tile_counts = jnp.sum(chosen, axis=0, keepdims=True)

    @pl.when((p == 0) & (j == 0))
    def _():
        running[...] = jnp.zeros_like(running)

    @pl.when(p == 0)
    def _():
        running[...] += tile_counts
        dest_ref[...] = jnp.zeros_like(dest_ref)

    @pl.when((p == 1) & (j == 0))
    def _():
        counts = running[...]
        counts_ref[...] = counts
        padded = jnp.ceil(counts * (1.0 / TM_MOE)) * TM_MOE
        lane = lax.broadcasted_iota(jnp.int32, (1, LANES), 1)
        scan = padded
        shift = 1
        while shift < N_EXPERTS:
            scan = scan + jnp.where(lane >= shift, pltpu.roll(scan, shift, 1), 0.0)
            shift *= 2
        running[...] = scan - padded

    @pl.when(p == 1)
    def _():
        r_iota = lax.broadcasted_iota(jnp.int32, (t, t), 0)
        c_iota = lax.broadcasted_iota(jnp.int32, (t, t), 1)
        earlier = jnp.where(c_iota < r_iota, 1.0, 0.0).astype(BF16)
        base = jnp.dot(earlier, chosen.astype(BF16), preferred_element_type=F32) + running[...]
        out = jnp.zeros((t, LANES), jnp.int32)
        for k in range(TOP_K):
            row = jnp.sum(jnp.where(hot[k], base, 0.0), axis=-1, keepdims=True).astype(jnp.int32)
            out = jnp.where(e_iota == k, row, out)
        dest_ref[...] = out
        running[...] += tile_counts


def _dispatch_plan(top_idx, n):
    n_assign = n * TOP_K
    nb = n_assign // TM_MOE + N_EXPERTS + MOE_SLOTS - 2
    dest, counts = pl.pallas_call(
        _rank_kernel,
        grid=(2, n // TM_RANK),
        in_specs=[pl.BlockSpec((TM_RANK, LANES), lambda p, j: (j, 0))],
        out_specs=[pl.BlockSpec((TM_RANK, LANES), lambda p, j: (p * j, 0)),
                   pl.BlockSpec((1, LANES), lambda p, j: (0, 0))],
        out_shape=[jax.ShapeDtypeStruct((n, LANES), jnp.int32), jax.ShapeDtypeStruct((1, LANES), F32)],
        scratch_shapes=[pltpu.VMEM((1, LANES), F32)],
        compiler_params=_params(2),
        name="expert_rank",
    )(top_idx)
    dest = dest[:, :TOP_K].reshape(-1)
    counts = counts[0, :N_EXPERTS].astype(jnp.int32)
    padded = (counts + TM_MOE - 1) // TM_MOE * TM_MOE
    pad_end = jnp.cumsum(padded)
    row_token = jnp.zeros((nb * TM_MOE,), jnp.int32).at[dest].set(
        jnp.arange(n_assign, dtype=jnp.int32) // TOP_K, unique_indices=True, mode='promise_in_bounds')
    block_first_row = jnp.arange(nb, dtype=jnp.int32) * TM_MOE
    block_expert = jnp.minimum(jnp.sum(pad_end[None, :] <= block_first_row[:, None], axis=1),
                               N_EXPERTS - 1).astype(jnp.int32)
    n_used = (pad_end[-1:] // TM_MOE).astype(jnp.int32)
    return dest, row_token, block_expert, n_used


def kernel(x, mem, w_in, rel_bias, g_group_a, g_group_b, w_out, w_q_mem, w_kv_mem, w_o_mem, w_router, b_router, w_gate_up, b_gate_up, w_down, b_down, ln_g, ln_b):
    b, s, d = x.shape
    n = b * s
    depth = w_in.shape[0]
    xf = x.reshape(n, d)
    for l in range(depth):
        q_scale = jnp.ones((3 * MIX_WIDTH,), F32).at[:WIDTH_A].set(HEAD_DIM ** -0.5)
        q_scale = q_scale.at[3 * WIDTH_A:3 * WIDTH_A + WIDTH_B].set(HEAD_DIM ** -0.5)
        proj = _matmul(xf, (w_in[l] * q_scale).astype(BF16), BF16, TM_TOK, 1024).reshape(b, s, 3 * MIX_WIDTH)
        out_a = _chunk_attention(proj, _chunk_bias(rel_bias[l]), b, s).reshape(n, WIDTH_A)
        out_b = _sb_attention(proj, b, s).reshape(n, WIDTH_B)
        x1, qm = _mixer_out(out_a, out_b, xf, g_group_a[l][None], g_group_b[l][None],
                            w_out[l].astype(BF16), ln_g[l, 0][None], ln_b[l, 0][None],
                            w_q_mem[l].astype(BF16))
        mem_len = mem.shape[1]
        kv = _matmul(mem.reshape(b * mem_len, d), w_kv_mem[l].astype(BF16), BF16, b * mem_len, 1024)
        x2, top_idx, gates = _mem_attn(qm, kv.reshape(b, mem_len, 2 * d), x1, w_o_mem[l].astype(BF16),
                                       ln_g[l, 1][None], ln_b[l, 1][None],
                                       w_router[l].astype(BF16), b_router[l][None], b, s)
        dest, row_token, block_expert, n_used = _dispatch_plan(top_idx, n)
        y_rows = _moe_experts(block_expert, n_used, row_token, x2,
                              w_gate_up[l].astype(BF16), b_gate_up[l][:, None, :],
                              w_down[l].astype(BF16), b_down[l][:, None, :])
        xf = _combine(dest, y_rows, x2, gates, ln_g[l, 2][None], ln_b[l, 2][None])
    return xf.reshape(b, s, d)
```

```python
import functools

import jax
import jax.numpy as jnp
from jax import lax
from jax.experimental import pallas as pl
from jax.experimental.pallas import tpu as pltpu

D_MODEL = 1024
CHUNK = 64
LEFT_CHUNKS = 8
LEFT = LEFT_CHUNKS * CHUNK
HEAD_DIM = 64
N_HEADS_A = 8
N_HEADS_B = 8
WIDTH_A = N_HEADS_A * HEAD_DIM
WIDTH_B = N_HEADS_B * HEAD_DIM
MIX_WIDTH = WIDTH_A + WIDTH_B
REL_CLIP = 128
N_HEADS_MEM = 4
HEAD_DIM_MEM = D_MODEL // N_HEADS_MEM
N_EXPERTS = 32
TOP_K = 4
D_FF = D_MODEL
SWIGLU_LIMIT = 7.0
SWIGLU_ALPHA = 1.702
LN_EPS = 1e-5
RMS_EPS = 1e-6
DEEPNORM_ALPHA = 2.0 ** 0.25
NEG_INF = -1e30
LOG2E = 1.4426950408889634

LANES = 128
LANE_BLOCKS = D_MODEL // LANES
VMEM_LIMIT = 48 * 1024 * 1024

TQ_A = 512
SUB_A = 256
TB_SB = 256
SB_UNDERFLOW = 110.0
TM_TOK = 512
TM_MOE = 256
MOE_SLOTS = 3
TM_COMB = 256
TM_RANK = 512

F32 = jnp.float32
BF16 = jnp.bfloat16
_NT = (((1,), (1,)), ((), ()))


def _params(n_axes):
    return pltpu.CompilerParams(dimension_semantics=("arbitrary",) * n_axes,
                                vmem_limit_bytes=VMEM_LIMIT)


def _store_token_major(ref, value):
    rows = value.shape[0]
    for c in range(LANE_BLOCKS):
        ref[pl.ds(c, rows, stride=LANE_BLOCKS), :] = value[:, c * LANES:(c + 1) * LANES]


def _load_token_major(ref, first_row, rows):
    return jnp.concatenate(
        [ref[pl.ds(first_row * LANE_BLOCKS + c, rows, stride=LANE_BLOCKS), :] for c in range(LANE_BLOCKS)],
        axis=1)


def _layer_norm(r, g, b):
    mu = jnp.mean(r, axis=-1, keepdims=True)
    d = r - mu
    var = jnp.mean(d * d, axis=-1, keepdims=True)
    return d * lax.rsqrt(var + LN_EPS) * g + b


def _matmul_kernel(x_ref, w_ref, o_ref):
    o_ref[...] = jnp.dot(x_ref[...].astype(BF16), w_ref[...],
                         preferred_element_type=F32).astype(o_ref.dtype)


def _matmul(x, w, out_dtype, tm, tn):
    m, k = x.shape
    n = w.shape[1]
    return pl.pallas_call(
        _matmul_kernel,
        grid=(m // tm, n // tn),
        in_specs=[pl.BlockSpec((tm, k), lambda i, j: (i, 0)),
                  pl.BlockSpec((k, tn), lambda i, j: (0, j))],
        out_specs=pl.BlockSpec((tm, tn), lambda i, j: (i, j)),
        out_shape=jax.ShapeDtypeStruct((m, n), out_dtype),
        compiler_params=_params(2),
        name="matmul",
    )(x, w)


def _chunk_attn_kernel(q_ref, kp_ref, kc_ref, vp_ref, vc_ref, bias0_ref, bias1_ref, o_ref):
    w = SUB_A + LEFT
    for h in range(2):
        sl = slice(h * HEAD_DIM, (h + 1) * HEAD_DIM)
        k = jnp.concatenate([kp_ref[:, sl], kc_ref[:, sl]], axis=0)
        v = jnp.concatenate([vp_ref[:, sl], vc_ref[:, sl]], axis=0)
        for sub, bias_ref in enumerate((bias0_ref, bias1_ref)):
            rows = slice(sub * SUB_A, (sub + 1) * SUB_A)
            keys = slice(sub * SUB_A, sub * SUB_A + w)
            s = lax.dot_general(q_ref[rows, sl], k[keys], _NT, preferred_element_type=F32) + bias_ref[h]
            m = jnp.max(s, axis=-1, keepdims=True)
            p = jnp.exp(s - m)
            l = jnp.sum(p, axis=-1, keepdims=True)
            o = jnp.dot(p.astype(BF16), v[keys], preferred_element_type=F32) / l
            o_ref[rows, sl] = o.astype(o_ref.dtype)


def _chunk_bias(rel_bias):
    w = SUB_A + LEFT
    heads = rel_bias.shape[0]
    m = jnp.arange(2 * w)
    rel = jnp.where(m < w, LEFT - m, LEFT + 2 * w - m)
    diag = rel_bias[:, jnp.clip(rel, -REL_CLIP, REL_CLIP) + REL_CLIP].astype(F32)[:, None, :]
    n_tables = 1 + TQ_A // SUB_A
    return pl.pallas_call(
        _chunk_bias_kernel,
        grid=(heads,),
        in_specs=[pl.BlockSpec((1, 1, 2 * w), lambda h: (h, 0, 0))],
        out_specs=pl.BlockSpec((n_tables, 1, SUB_A, w), lambda h: (0, h, 0, 0)),
        out_shape=jax.ShapeDtypeStruct((n_tables, heads, SUB_A, w), F32),
        compiler_params=_params(1),
        name="chunk_bias",
    )(diag)


def _chunk_bias_kernel(diag_ref, o_ref):
    w = SUB_A + LEFT
    rolled = pltpu.roll(jnp.broadcast_to(diag_ref[0], (SUB_A, 2 * w)), 0, 1, stride=1, stride_axis=0)
    col = lax.broadcasted_iota(jnp.int32, (SUB_A, w), 1)
    qc = lax.broadcasted_iota(jnp.int32, (SUB_A, w), 0) // CHUNK
    kc = col // CHUNK
    band = (kc >= qc) & (kc <= qc + LEFT_CHUNKS)
    table = jnp.where(band, rolled[:, :w], NEG_INF)
    o_ref[0, 0] = table
    for sub in range(TQ_A // SUB_A):
        o_ref[1 + sub, 0] = jnp.where(col >= LEFT - sub * SUB_A, table, NEG_INF)


def _chunk_attention(proj, bias, b, s):
    nq = s // TQ_A
    pairs = WIDTH_A // LANES
    blk = lambda off, prev: pl.BlockSpec(
        (None, TQ_A, LANES),
        (lambda bi, p, i: (bi, jnp.maximum(i - 1, 0), off + p)) if prev
        else (lambda bi, p, i: (bi, i, off + p)))
    table = lambda sub: pl.BlockSpec((None, 2, SUB_A, SUB_A + LEFT),
                                     lambda bi, p, i: (jnp.where(i == 0, 1 + sub, 0), p, 0, 0))
    return pl.pallas_call(
        _chunk_attn_kernel,
        grid=(b, pairs, nq),
        in_specs=[blk(0, False),
                  blk(pairs, True), blk(pairs, False),
                  blk(2 * pairs, True), blk(2 * pairs, False),
                  table(0), table(1)],
        out_specs=pl.BlockSpec((None, TQ_A, LANES), lambda bi, p, i: (bi, i, p)),
        out_shape=jax.ShapeDtypeStruct((b, s, WIDTH_A), BF16),
        compiler_params=_params(3),
        name="chunk_attn",
    )(proj, proj, proj, proj, proj, bias, bias)


def _sb_block(q, k_ref, v_ref, sl, j, later, tri, causal):
    t = TB_SB
    start = pl.multiple_of(j * t, t)
    k = k_ref[pl.ds(start, t), sl]
    v = v_ref[pl.ds(start, t), sl]
    z = lax.dot_general(q, k, _NT, preferred_element_type=F32)
    sp = jnp.maximum(z, 0.0) + jnp.log(1.0 + jnp.exp2(jnp.abs(z) * (-LOG2E)))
    if causal is not None:
        sp = jnp.where(causal, sp, 0.0)
    csum = jnp.dot(sp.astype(BF16), tri, preferred_element_type=F32) + later
    wgt = jnp.exp(z - csum)
    if causal is not None:
        wgt = jnp.where(causal, wgt, 0.0)
    pv = jnp.dot(wgt.astype(BF16), v, preferred_element_type=F32)
    return pv, csum[:, 0:1]


def _sb_attn_kernel(q_ref, k_ref, v_ref, o_ref):
    i = pl.program_id(2)
    t = TB_SB
    row = lax.broadcasted_iota(jnp.int32, (t, t), 0)
    col = lax.broadcasted_iota(jnp.int32, (t, t), 1)
    tri = jnp.where(row >= col, 1.0, 0.0).astype(BF16)
    causal = col < row
    sls = [slice(h * HEAD_DIM, (h + 1) * HEAD_DIM) for h in range(2)]
    qs = [q_ref[:, sl] for sl in sls]
    block = functools.partial(_sb_block, tri=tri)

    has_prev = i >= 1
    carry = []
    for h in range(2):
        pv, later = block(qs[h], k_ref, v_ref, sls[h], i, jnp.zeros((t, 1), F32), causal=causal)
        pv_prev, later_prev = block(qs[h], k_ref, v_ref, sls[h], jnp.maximum(i - 1, 0), later, causal=None)
        carry += [jnp.where(has_prev, later_prev, later), pv + jnp.where(has_prev, pv_prev, 0.0)]

    def pending(c):
        return (c[0] >= 0) & (jnp.minimum(jnp.min(c[1]), jnp.min(c[3])) < SB_UNDERFLOW)

    def step(c):
        out = [c[0] - 1]
        for h in range(2):
            pv, later = block(qs[h], k_ref, v_ref, sls[h], c[0], c[1 + 2 * h], causal=None)
            out += [later, c[2 + 2 * h] + pv]
        return tuple(out)

    carry = lax.while_loop(pending, step, (i - 2, *carry))
    for h in range(2):
        o_ref[:, sls[h]] = carry[2 + 2 * h].astype(o_ref.dtype)


def _sb_attention(proj, b, s):
    nq = s // TB_SB
    pairs = WIDTH_B // LANES
    base = 3 * WIDTH_A // LANES
    return pl.pallas_call(
        _sb_attn_kernel,
        grid=(b, pairs, nq),
        in_specs=[pl.BlockSpec((None, TB_SB, LANES), lambda bi, p, i: (bi, i, base + p)),
                  pl.BlockSpec((None, s, LANES), lambda bi, p, i: (bi, 0, base + pairs + p)),
                  pl.BlockSpec((None, s, LANES), lambda bi, p, i: (bi, 0, base + 2 * pairs + p))],
        out_specs=pl.BlockSpec((None, TB_SB, LANES), lambda bi, p, i: (bi, i, p)),
        out_shape=jax.ShapeDtypeStruct((b, s, WIDTH_B), BF16),
        compiler_params=_params(3),
        name="sb_attn",
    )(proj, proj, proj)


def _mixer_out_kernel(oa_ref, ob_ref, x_ref, ga_ref, gb_ref, wout_ref, lng_ref, lnb_ref, wq_ref,
                      x1_ref, qm_ref):
    def rms(ref, g_ref):
        a = ref[...].astype(F32)
        return (a * lax.rsqrt(jnp.mean(a * a, axis=-1, keepdims=True) + RMS_EPS) * g_ref[...]).astype(BF16)

    y = jnp.dot(rms(oa_ref, ga_ref), wout_ref[:WIDTH_A, :], preferred_element_type=F32)
    y = y + jnp.dot(rms(ob_ref, gb_ref), wout_ref[WIDTH_A:, :], preferred_element_type=F32)
    x1 = _layer_norm(DEEPNORM_ALPHA * x_ref[...] + y, lng_ref[...], lnb_ref[...])
    x1_ref[...] = x1
    qm_ref[...] = jnp.dot(x1.astype(BF16), wq_ref[...], preferred_element_type=F32).astype(BF16)


def _mixer_out(out_a, out_b, x, g_a, g_b, w_out, ln_g, ln_b, w_q):
    n = x.shape[0]
    tm = TM_TOK
    row = lambda width: pl.BlockSpec((tm, width), lambda i: (i, 0))
    full = lambda r, c: pl.BlockSpec((r, c), lambda i: (0, 0))
    return pl.pallas_call(
        _mixer_out_kernel,
        grid=(n // tm,),
        in_specs=[row(WIDTH_A), row(WIDTH_B), row(D_MODEL), full(1, WIDTH_A), full(1, WIDTH_B),
                  full(MIX_WIDTH, D_MODEL), full(1, D_MODEL), full(1, D_MODEL), full(D_MODEL, D_MODEL)],
        out_specs=[row(D_MODEL), row(D_MODEL)],
        out_shape=[jax.ShapeDtypeStruct((n, D_MODEL), F32), jax.ShapeDtypeStruct((n, D_MODEL), BF16)],
        compiler_params=_params(1),
        name="mixer_out",
    )(out_a, out_b, x, g_a, g_b, w_out, ln_g, ln_b, w_q)


def _mem_attn_kernel(q_ref, kv_ref, x1_ref, wo_ref, lng_ref, lnb_ref, wr_ref, br_ref,
                     x2_ref, idx_ref, gate_ref):
    heads = []
    for h in range(N_HEADS_MEM):
        sl = slice(h * HEAD_DIM_MEM, (h + 1) * HEAD_DIM_MEM)
        q = q_ref[:, sl] * (HEAD_DIM_MEM ** -0.5)
        k = kv_ref[:, sl]
        v = kv_ref[:, D_MODEL + h * HEAD_DIM_MEM:D_MODEL + (h + 1) * HEAD_DIM_MEM]
        s = lax.dot_general(q, k, _NT, preferred_element_type=F32)
        m = jnp.max(s, axis=-1, keepdims=True)
        p = jnp.exp(s - m)
        l = jnp.sum(p, axis=-1, keepdims=True)
        heads.append((jnp.dot(p.astype(BF16), v, preferred_element_type=F32) / l).astype(BF16))
    o = jnp.concatenate(heads, axis=-1)
    y = jnp.dot(o, wo_ref[...], preferred_element_type=F32)
    x2 = _layer_norm(DEEPNORM_ALPHA * x1_ref[...] + y, lng_ref[...], lnb_ref[...])
    _store_token_major(x2_ref, x2)

    logits = jnp.dot(x2.astype(BF16), wr_ref[...], preferred_element_type=F32) + br_ref[...]
    tm = logits.shape[0]
    e_iota = lax.broadcasted_iota(jnp.int32, (tm, N_EXPERTS), 1)
    lane = lax.broadcasted_iota(jnp.int32, (tm, LANES), 1)
    idx_out = jnp.zeros((tm, LANES), jnp.int32)
    val_out = jnp.zeros((tm, LANES), F32)
    top = None
    denom = jnp.zeros((tm, 1), F32)
    for kk in range(TOP_K):
        m = jnp.max(logits, axis=-1, keepdims=True)
        sel = jnp.min(jnp.where(logits == m, e_iota, N_EXPERTS), axis=-1, keepdims=True)
        if top is None:
            top = m
        e = jnp.exp(m - top)
        denom = denom + e
        idx_out = jnp.where(lane == kk, sel, idx_out)
        val_out = jnp.where(lane == kk, e, val_out)
        logits = jnp.where(e_iota == sel, -jnp.inf, logits)
    idx_ref[...] = idx_out
    gate_ref[...] = val_out / denom


def _mem_attn(qm, kv, x1, w_o, ln_g, ln_b, w_r, b_r, b, s):
    tm = TM_TOK
    nt = s // tm
    mem_len = kv.shape[1]
    row = lambda width: pl.BlockSpec((tm, width), lambda bi, i: (bi * nt + i, 0))
    full = lambda r, c: pl.BlockSpec((r, c), lambda bi, i: (0, 0))
    n = b * s
    return pl.pallas_call(
        _mem_attn_kernel,
        grid=(b, nt),
        in_specs=[row(D_MODEL), pl.BlockSpec((None, mem_len, 2 * D_MODEL), lambda bi, i: (bi, 0, 0)),
                  row(D_MODEL), full(D_MODEL, D_MODEL), full(1, D_MODEL), full(1, D_MODEL),
                  full(D_MODEL, N_EXPERTS), full(1, N_EXPERTS)],
        out_specs=[pl.BlockSpec((tm * LANE_BLOCKS, LANES), lambda bi, i: (bi * nt + i, 0)), row(LANES), row(LANES)],
        out_shape=[jax.ShapeDtypeStruct((n * LANE_BLOCKS, LANES), F32),
                   jax.ShapeDtypeStruct((n, LANES), jnp.int32),
                   jax.ShapeDtypeStruct((n, LANES), F32)],
        compiler_params=_params(2),
        name="mem_attn",
    )(qm, kv, x1, w_o, ln_g, ln_b, w_r, b_r)


def _gather_rows(idx_ref, first, count, src_hbm, dst_ref, sem):
    for r in range(count):
        src = pl.ds(pl.multiple_of(idx_ref[first + r], LANE_BLOCKS), LANE_BLOCKS)
        pltpu.make_async_copy(src_hbm.at[src], dst_ref.at[pl.ds(r * LANE_BLOCKS, LANE_BLOCKS)], sem).start()


def _wait_rows(count, src_hbm, dst_ref, sem):
    pltpu.make_async_copy(src_hbm.at[pl.ds(0, count * LANE_BLOCKS)], dst_ref, sem).wait()


def _moe_kernel(bexp_ref, nused_ref, tok_ref, x_hbm, wgu_ref, bgu_ref, wd_ref, bd_ref,
                y_ref, xbuf, sem):
    i = pl.program_id(0)
    slot = i % MOE_SLOTS
    ahead = MOE_SLOTS - 1
    n_used = nused_ref[0]

    @pl.when(i == 0)
    def _():
        for b in range(ahead):
            _gather_rows(tok_ref, b * TM_MOE, TM_MOE, x_hbm, xbuf.at[b], sem.at[b])

    @pl.when(i < n_used + ahead)
    def _():
        _wait_rows(TM_MOE, x_hbm, xbuf.at[slot], sem.at[slot])

    @pl.when(i < n_used)
    def _():
        x = _load_token_major(xbuf.at[slot], 0, TM_MOE).astype(BF16)
        nxt = (i + ahead) % MOE_SLOTS
        _gather_rows(tok_ref, (i + ahead) * TM_MOE, TM_MOE, x_hbm, xbuf.at[nxt], sem.at[nxt])
        gu = jnp.dot(x, wgu_ref[0], preferred_element_type=F32) + bgu_ref[0]
        gate = jnp.minimum(gu[:, :D_FF], SWIGLU_LIMIT)
        up = jnp.clip(gu[:, D_FF:], -SWIGLU_LIMIT, SWIGLU_LIMIT)
        glu = gate * jax.nn.sigmoid(gate * SWIGLU_ALPHA)
        hid = ((up + 1.0) * glu).astype(BF16)
        _store_token_major(y_ref, jnp.dot(hid, wd_ref[0], preferred_element_type=F32) + bd_ref[0])

    @pl.when(i >= n_used)
    def _():
        y_ref[...] = jnp.zeros_like(y_ref)


def _moe_experts(block_expert, n_used, row_token, x2, w_gu, b_gu, w_d, b_d):
    nb = block_expert.shape[0]
    grid_spec = pltpu.PrefetchScalarGridSpec(
        num_scalar_prefetch=3,
        grid=(nb,),
        in_specs=[pl.BlockSpec(memory_space=pl.ANY),
                  pl.BlockSpec((1, D_MODEL, 2 * D_FF), lambda i, be, nu, tok: (be[i], 0, 0)),
                  pl.BlockSpec((1, 1, 2 * D_FF), lambda i, be, nu, tok: (be[i], 0, 0)),
                  pl.BlockSpec((1, D_FF, D_MODEL), lambda i, be, nu, tok: (be[i], 0, 0)),
                  pl.BlockSpec((1, 1, D_MODEL), lambda i, be, nu, tok: (be[i], 0, 0))],
        out_specs=pl.BlockSpec((TM_MOE * LANE_BLOCKS, LANES), lambda i, be, nu, tok: (i, 0)),
        scratch_shapes=[pltpu.VMEM((MOE_SLOTS, TM_MOE * LANE_BLOCKS, LANES), F32),
                        pltpu.SemaphoreType.DMA((MOE_SLOTS,))],
    )
    return pl.pallas_call(
        _moe_kernel,
        grid_spec=grid_spec,
        out_shape=jax.ShapeDtypeStruct((nb * TM_MOE * LANE_BLOCKS, LANES), F32),
        compiler_params=_params(1),
        name="moe_experts",
    )(block_expert, n_used, row_token * LANE_BLOCKS, x2, w_gu, b_gu, w_d, b_d)


def _combine_kernel(idx_ref, y_hbm, x2_ref, gate_ref, lng_ref, lnb_ref, o_ref, ybuf, sem):
    i = pl.program_id(0)
    nt = pl.num_programs(0)
    slot = i % 2
    rows = TOP_K * TM_COMB

    @pl.when(i == 0)
    def _():
        _gather_rows(idx_ref, 0, rows, y_hbm, ybuf.at[0], sem.at[0])

    @pl.when(i + 1 < nt)
    def _():
        _gather_rows(idx_ref, (i + 1) * rows, rows, y_hbm, ybuf.at[1 - slot], sem.at[1 - slot])

    _wait_rows(rows, y_hbm, ybuf.at[slot], sem.at[slot])

    g = gate_ref[...]
    ff = jnp.zeros((TM_COMB, D_MODEL), F32)
    for kk in range(TOP_K):
        ff = ff + g[:, kk:kk + 1] * _load_token_major(ybuf.at[slot], kk * TM_COMB, TM_COMB)
    x2 = _load_token_major(x2_ref, 0, TM_COMB)
    o_ref[...] = _layer_norm(DEEPNORM_ALPHA * x2 + ff, lng_ref[...], lnb_ref[...])


def _combine(dest, y_rows, x2, gates, ln_g, ln_b):
    n = x2.shape[0] // LANE_BLOCKS
    nt = n // TM_COMB
    rows = TOP_K * TM_COMB
    idx = (dest * LANE_BLOCKS).reshape(nt, TM_COMB, TOP_K).transpose(0, 2, 1).reshape(nt * rows)
    row = lambda width: pl.BlockSpec((TM_COMB, width), lambda i, idx: (i, 0))
    full = lambda r, c: pl.BlockSpec((r, c), lambda i, idx: (0, 0))
    grid_spec = pltpu.PrefetchScalarGridSpec(
        num_scalar_prefetch=1,
        grid=(nt,),
        in_specs=[pl.BlockSpec(memory_space=pl.ANY),
                  pl.BlockSpec((TM_COMB * LANE_BLOCKS, LANES), lambda i, idx: (i, 0)), row(LANES),
                  full(1, D_MODEL), full(1, D_MODEL)],
        out_specs=row(D_MODEL),
        scratch_shapes=[pltpu.VMEM((2, rows * LANE_BLOCKS, LANES), F32), pltpu.SemaphoreType.DMA((2,))],
    )
    return pl.pallas_call(
        _combine_kernel,
        grid_spec=grid_spec,
        out_shape=jax.ShapeDtypeStruct((n, D_MODEL), F32),
        compiler_params=_params(1),
        name="combine",
    )(idx, y_rows, x2, gates, ln_g, ln_b)


def _rank_kernel(idx_ref, dest_ref, counts_ref, running):
    p = pl.program_id(0)
    j = pl.program_id(1)
    t = TM_RANK
    idx = idx_ref[...]
    e_iota = lax.broadcasted_iota(jnp.int32, (t, LANES), 1)
    hot = [idx[:, k:k + 1] == e_iota for k in range(TOP_K)]
    chosen = sum(jnp.where(h, 1.0, 0.0) for h in hot)
    tile_counts = jnp.sum(chosen, axis=0, keepdims=True)

    @pl.when((p == 0) & (j == 0))
    def _():
        running[...] = jnp.zeros_like(running)

    @pl.when(p == 0)
    def _():
        running[...] += tile_counts
        dest_ref[...] = jnp.zeros_like(dest_ref)

    @pl.when((p == 1) & (j == 0))
    def _():
        counts = running[...]
        counts_ref[...] = counts
        padded = jnp.ceil(counts * (1.0 / TM_MOE)) * TM_MOE
        lane = lax.broadcasted_iota(jnp.int32, (1, LANES), 1)
        scan = padded
        shift = 1
        while shift < N_EXPERTS:
            scan = scan + jnp.where(lane >= shift, pltpu.roll(scan, shift, 1), 0.0)
            shift *= 2
        running[...] = scan - padded

    @pl.when(p == 1)
    def _():
        r_iota = lax.broadcasted_iota(jnp.int32, (t, t), 0)
        c_iota = lax.broadcasted_iota(jnp.int32, (t, t), 1)
        earlier = jnp.where(c_iota < r_iota, 1.0, 0.0).astype(BF16)
        base = jnp.dot(earlier, chosen.astype(BF16), preferred_element_type=F32) + running[...]
        out = jnp.zeros((t, LANES), jnp.int32)
        for k in range(TOP_K):
            row = jnp.sum(jnp.where(hot[k], base, 0.0), axis=-1, keepdims=True).astype(jnp.int32)
            out = jnp.where(e_iota == k, row, out)
        dest_ref[...] = out
        running[...] += tile_counts


def _row_token_kernel(dest_ref, gap_ref, tok_ref):
    unroll = 8

    def clear(row, carry):
        tok_ref[row] = 0
        return carry

    def place(c, carry):
        rows = [dest_ref[c * unroll + u] for u in range(unroll)]
        for u in range(unroll):
            tok_ref[rows[u]] = c * (unroll // TOP_K) + u // TOP_K
        return carry

    for g in range(gap_ref.shape[1]):
        lax.fori_loop(gap_ref[0, g], gap_ref[1, g], clear, 0)
    lax.fori_loop(0, dest_ref.shape[0] // unroll, place, 0)


def _dispatch_plan(top_idx, n):
    n_assign = n * TOP_K
    nb = n_assign // TM_MOE + N_EXPERTS + MOE_SLOTS - 2
    dest, counts = pl.pallas_call(
        _rank_kernel,
        grid=(2, n // TM_RANK),
        in_specs=[pl.BlockSpec((TM_RANK, LANES), lambda p, j: (j, 0))],
        out_specs=[pl.BlockSpec((TM_RANK, LANES), lambda p, j: (p * j, 0)),
                   pl.BlockSpec((1, LANES), lambda p, j: (0, 0))],
        out_shape=[jax.ShapeDtypeStruct((n, LANES), jnp.int32), jax.ShapeDtypeStruct((1, LANES), F32)],
        scratch_shapes=[pltpu.VMEM((1, LANES), F32)],
        compiler_params=_params(2),
        name="expert_rank",
    )(top_idx)
    dest = dest[:, :TOP_K].reshape(-1)
    counts = counts[0, :N_EXPERTS].astype(jnp.int32)
    padded = (counts + TM_MOE - 1) // TM_MOE * TM_MOE
    pad_end = jnp.cumsum(padded)
    tail = jnp.stack([pad_end[-1], jnp.int32(nb * TM_MOE)])[:, None]
    gaps = jnp.concatenate([jnp.stack([pad_end - padded + counts, pad_end]), tail], axis=1)
    row_token = pl.pallas_call(
        _row_token_kernel,
        in_specs=[pl.BlockSpec(memory_space=pltpu.SMEM), pl.BlockSpec(memory_space=pltpu.SMEM)],
        out_specs=pl.BlockSpec(memory_space=pltpu.SMEM),
        out_shape=jax.ShapeDtypeStruct((nb * TM_MOE,), jnp.int32),
        name="row_token",
    )(dest, gaps)
    block_first_row = jnp.arange(nb, dtype=jnp.int32) * TM_MOE
    block_expert = jnp.minimum(jnp.sum(pad_end[None, :] <= block_first_row[:, None], axis=1),
                               N_EXPERTS - 1).astype(jnp.int32)
    n_used = (pad_end[-1:] // TM_MOE).astype(jnp.int32)
    return dest, row_token, block_expert, n_used


def kernel(x, mem, w_in, rel_bias, g_group_a, g_group_b, w_out, w_q_mem, w_kv_mem, w_o_mem, w_router, b_router, w_gate_up, b_gate_up, w_down, b_down, ln_g, ln_b):
    b, s, d = x.shape
    n = b * s
    depth = w_in.shape[0]
    xf = x.reshape(n, d)
    for l in range(depth):
        q_scale = jnp.ones((3 * MIX_WIDTH,), F32).at[:WIDTH_A].set(HEAD_DIM ** -0.5)
        q_scale = q_scale.at[3 * WIDTH_A:3 * WIDTH_A + WIDTH_B].set(HEAD_DIM ** -0.5)
        proj = _matmul(xf, (w_in[l] * q_scale).astype(BF16), BF16, TM_TOK, 1024).reshape(b, s, 3 * MIX_WIDTH)
        out_a = _chunk_attention(proj, _chunk_bias(rel_bias[l]), b, s).reshape(n, WIDTH_A)
        out_b = _sb_attention(proj, b, s).reshape(n, WIDTH_B)
        x1, qm = _mixer_out(out_a, out_b, xf, g_group_a[l][None], g_group_b[l][None],
                            w_out[l].astype(BF16), ln_g[l, 0][None], ln_b[l, 0][None],
                            w_q_mem[l].astype(BF16))
        mem_len = mem.shape[1]
        kv = _matmul(mem.reshape(b * mem_len, d), w_kv_mem[l].astype(BF16), BF16, b * mem_len, 1024)
        x2, top_idx, gates = _mem_attn(qm, kv.reshape(b, mem_len, 2 * d), x1, w_o_mem[l].astype(BF16),
                                       ln_g[l, 1][None], ln_b[l, 1][None],
                                       w_router[l].astype(BF16), b_router[l][None], b, s)
        dest, row_token, block_expert, n_used = _dispatch_plan(top_idx, n)
        y_rows = _moe_experts(block_expert, n_used, row_token, x2,
                              w_gate_up[l].astype(BF16), b_gate_up[l][:, None, :],
                              w_down[l].astype(BF16), b_down[l][:, None, :])
        xf = _combine(dest, y_rows, x2, gates, ln_g[l, 2][None], ln_b[l, 2][None])
    return xf.reshape(b, s, d)
```

```python
import functools

import jax
import jax.numpy as jnp
from jax import lax
from jax.experimental import pallas as pl
from jax.experimental.pallas import tpu as pltpu

D_MODEL = 1024
CHUNK = 64
LEFT_CHUNKS = 8
LEFT = LEFT_CHUNKS * CHUNK
HEAD_DIM = 64
N_HEADS_A = 8
N_HEADS_B = 8
WIDTH_A = N_HEADS_A * HEAD_DIM
WIDTH_B = N_HEADS_B * HEAD_DIM
MIX_WIDTH = WIDTH_A + WIDTH_B
REL_CLIP = 128
N_HEADS_MEM = 4
HEAD_DIM_MEM = D_MODEL // N_HEADS_MEM
N_EXPERTS = 32
TOP_K = 4
D_FF = D_MODEL
SWIGLU_LIMIT = 7.0
SWIGLU_ALPHA = 1.702
LN_EPS = 1e-5
RMS_EPS = 1e-6
DEEPNORM_ALPHA = 2.0 ** 0.25
NEG_INF = -1e30
LOG2E = 1.4426950408889634

LANES = 128
LANE_BLOCKS = D_MODEL // LANES
VMEM_LIMIT = 48 * 1024 * 1024

TQ_A = 512
SUB_A = 256
TB_SB = 256
SB_UNDERFLOW = 110.0
TM_TOK = 512
TM_MOE = 256
MOE_SLOTS = 3
TM_COMB = 256
TM_RANK = 512

F32 = jnp.float32
BF16 = jnp.bfloat16
_NT = (((1,), (1,)), ((), ()))


def _params(n_axes):
    return pltpu.CompilerParams(dimension_semantics=("arbitrary",) * n_axes,
                                vmem_limit_bytes=VMEM_LIMIT)


def _store_token_major(ref, value):
    rows = value.shape[0]
    for c in range(LANE_BLOCKS):
        ref[pl.ds(c, rows, stride=LANE_BLOCKS), :] = value[:, c * LANES:(c + 1) * LANES]


def _load_token_major(ref, first_row, rows):
    return jnp.concatenate(
        [ref[pl.ds(first_row * LANE_BLOCKS + c, rows, stride=LANE_BLOCKS), :] for c in range(LANE_BLOCKS)],
        axis=1)


def _layer_norm(r, g, b):
    mu = jnp.mean(r, axis=-1, keepdims=True)
    d = r - mu
    var = jnp.mean(d * d, axis=-1, keepdims=True)
    return d * lax.rsqrt(var + LN_EPS) * g + b


def _matmul_kernel(x_ref, w_ref, o_ref):
    o_ref[...] = jnp.dot(x_ref[...].astype(BF16), w_ref[...],
                         preferred_element_type=F32).astype(o_ref.dtype)


def _matmul(x, w, out_dtype, tm, tn):
    m, k = x.shape
    n = w.shape[1]
    return pl.pallas_call(
        _matmul_kernel,
        grid=(m // tm, n // tn),
        in_specs=[pl.BlockSpec((tm, k), lambda i, j: (i, 0)),
                  pl.BlockSpec((k, tn), lambda i, j: (0, j))],
        out_specs=pl.BlockSpec((tm, tn), lambda i, j: (i, j)),
        out_shape=jax.ShapeDtypeStruct((m, n), out_dtype),
        compiler_params=_params(2),
        name="matmul",
    )(x, w)


def _chunk_attn_kernel(q_ref, kp_ref, kc_ref, vp_ref, vc_ref, bias0_ref, bias1_ref, o_ref):
    w = SUB_A + LEFT
    for h in range(2):
        sl = slice(h * HEAD_DIM, (h + 1) * HEAD_DIM)
        k = jnp.concatenate([kp_ref[:, sl], kc_ref[:, sl]], axis=0)
        v = jnp.concatenate([vp_ref[:, sl], vc_ref[:, sl]], axis=0)
        for sub, bias_ref in enumerate((bias0_ref, bias1_ref)):
            rows = slice(sub * SUB_A, (sub + 1) * SUB_A)
            keys = slice(sub * SUB_A, sub * SUB_A + w)
            s = lax.dot_general(q_ref[rows, sl], k[keys], _NT, preferred_element_type=F32) + bias_ref[h]
            m = jnp.max(s, axis=-1, keepdims=True)
            p = jnp.exp(s - m)
            l = jnp.sum(p, axis=-1, keepdims=True)
            o = jnp.dot(p.astype(BF16), v[keys], preferred_element_type=F32) / l
            o_ref[rows, sl] = o.astype(o_ref.dtype)


def _chunk_bias(rel_bias):
    w = SUB_A + LEFT
    heads = rel_bias.shape[0]
    m = jnp.arange(2 * w)
    rel = jnp.where(m < w, LEFT - m, LEFT + 2 * w - m)
    diag = rel_bias[:, jnp.clip(rel, -REL_CLIP, REL_CLIP) + REL_CLIP].astype(F32)[:, None, :]
    n_tables = 1 + TQ_A // SUB_A
    return pl.pallas_call(
        _chunk_bias_kernel,
        grid=(heads,),
        in_specs=[pl.BlockSpec((1, 1, 2 * w), lambda h: (h, 0, 0))],
        out_specs=pl.BlockSpec((n_tables, 1, SUB_A, w), lambda h: (0, h, 0, 0)),
        out_shape=jax.ShapeDtypeStruct((n_tables, heads, SUB_A, w), F32),
        compiler_params=_params(1),
        name="chunk_bias",
    )(diag)


def _chunk_bias_kernel(diag_ref, o_ref):
    w = SUB_A + LEFT
    rolled = pltpu.roll(jnp.broadcast_to(diag_ref[0], (SUB_A, 2 * w)), 0, 1, stride=1, stride_axis=0)
    col = lax.broadcasted_iota(jnp.int32, (SUB_A, w), 1)
    qc = lax.broadcasted_iota(jnp.int32, (SUB_A, w), 0) // CHUNK
    kc = col // CHUNK
    band = (kc >= qc) & (kc <= qc + LEFT_CHUNKS)
    table = jnp.where(band, rolled[:, :w], NEG_INF)
    o_ref[0, 0] = table
    for sub in range(TQ_A // SUB_A):
        o_ref[1 + sub, 0] = jnp.where(col >= LEFT - sub * SUB_A, table, NEG_INF)


def _chunk_attention(proj, bias, b, s):
    nq = s // TQ_A
    pairs = WIDTH_A // LANES
    blk = lambda off, prev: pl.BlockSpec(
        (None, TQ_A, LANES),
        (lambda bi, p, i: (bi, jnp.maximum(i - 1, 0), off + p)) if prev
        else (lambda bi, p, i: (bi, i, off + p)))
    table = lambda sub: pl.BlockSpec((None, 2, SUB_A, SUB_A + LEFT),
                                     lambda bi, p, i: (jnp.where(i == 0, 1 + sub, 0), p, 0, 0))
    return pl.pallas_call(
        _chunk_attn_kernel,
        grid=(b, pairs, nq),
        in_specs=[blk(0, False),
                  blk(pairs, True), blk(pairs, False),
                  blk(2 * pairs, True), blk(2 * pairs, False),
                  table(0), table(1)],
        out_specs=pl.BlockSpec((None, TQ_A, LANES), lambda bi, p, i: (bi, i, p)),
        out_shape=jax.ShapeDtypeStruct((b, s, WIDTH_A), BF16),
        compiler_params=_params(3),
        name="chunk_attn",
    )(proj, proj, proj, proj, proj, bias, bias)


def _sb_block(q, k_ref, v_ref, sl, j, later, tri, causal):
    t = TB_SB
    start = pl.multiple_of(j * t, t)
    k = k_ref[pl.ds(start, t), sl]
    v = v_ref[pl.ds(start, t), sl]
    z = lax.dot_general(q, k, _NT, preferred_element_type=F32)
    sp = jnp.maximum(z, 0.0) + jnp.log(1.0 + jnp.exp2(jnp.abs(z) * (-LOG2E)))
    if causal is not None:
        sp = jnp.where(causal, sp, 0.0)
    csum = jnp.dot(sp.astype(BF16), tri, preferred_element_type=F32) + later
    wgt = jnp.exp(z - csum)
    if causal is not None:
        wgt = jnp.where(causal, wgt, 0.0)
    pv = jnp.dot(wgt.astype(BF16), v, preferred_element_type=F32)
    return pv, csum[:, 0:1]


def _sb_attn_kernel(q_ref, k_ref, v_ref, o_ref):
    i = pl.program_id(2)
    t = TB_SB
    row = lax.broadcasted_iota(jnp.int32, (t, t), 0)
    col = lax.broadcasted_iota(jnp.int32, (t, t), 1)
    tri = jnp.where(row >= col, 1.0, 0.0).astype(BF16)
    causal = col < row
    sls = [slice(h * HEAD_DIM, (h + 1) * HEAD_DIM) for h in range(2)]
    qs = [q_ref[:, sl] for sl in sls]
    block = functools.partial(_sb_block, tri=tri)

    has_prev = i >= 1
    carry = []
    for h in range(2):
        pv, later = block(qs[h], k_ref, v_ref, sls[h], i, jnp.zeros((t, 1), F32), causal=causal)
        pv_prev, later_prev = block(qs[h], k_ref, v_ref, sls[h], jnp.maximum(i - 1, 0), later, causal=None)
        carry += [jnp.where(has_prev, later_prev, later), pv + jnp.where(has_prev, pv_prev, 0.0)]

    def pending(c):
        return (c[0] >= 0) & (jnp.minimum(jnp.min(c[1]), jnp.min(c[3])) < SB_UNDERFLOW)

    def step(c):
        out = [c[0] - 1]
        for h in range(2):
            pv, later = block(qs[h], k_ref, v_ref, sls[h], c[0], c[1 + 2 * h], causal=None)
            out += [later, c[2 + 2 * h] + pv]
        return tuple(out)

    carry = lax.while_loop(pending, step, (i - 2, *carry))
    for h in range(2):
        o_ref[:, sls[h]] = carry[2 + 2 * h].astype(o_ref.dtype)


def _sb_attention(proj, b, s):
    nq = s // TB_SB
    pairs = WIDTH_B // LANES
    base = 3 * WIDTH_A // LANES
    return pl.pallas_call(
        _sb_attn_kernel,
        grid=(b, pairs, nq),
        in_specs=[pl.BlockSpec((None, TB_SB, LANES), lambda bi, p, i: (bi, i, base + p)),
                  pl.BlockSpec((None, s, LANES), lambda bi, p, i: (bi, 0, base + pairs + p)),
                  pl.BlockSpec((None, s, LANES), lambda bi, p, i: (bi, 0, base + 2 * pairs + p))],
        out_specs=pl.BlockSpec((None, TB_SB, LANES), lambda bi, p, i: (bi, i, p)),
        out_shape=jax.ShapeDtypeStruct((b, s, WIDTH_B), BF16),
        compiler_params=_params(3),
        name="sb_attn",
    )(proj, proj, proj)


def _mixer_out_kernel(oa_ref, ob_ref, x_ref, ga_ref, gb_ref, wout_ref, lng_ref, lnb_ref, wq_ref,
                      x1_ref, qm_ref):
    def rms(ref, g_ref):
        a = ref[...].astype(F32)
        return (a * lax.rsqrt(jnp.mean(a * a, axis=-1, keepdims=True) + RMS_EPS) * g_ref[...]).astype(BF16)

    y = jnp.dot(rms(oa_ref, ga_ref), wout_ref[:WIDTH_A, :], preferred_element_type=F32)
    y = y + jnp.dot(rms(ob_ref, gb_ref), wout_ref[WIDTH_A:, :], preferred_element_type=F32)
    x1 = _layer_norm(DEEPNORM_ALPHA * x_ref[...] + y, lng_ref[...], lnb_ref[...])
    x1_ref[...] = x1
    qm_ref[...] = jnp.dot(x1.astype(BF16), wq_ref[...], preferred_element_type=F32).astype(BF16)


def _mixer_out(out_a, out_b, x, g_a, g_b, w_out, ln_g, ln_b, w_q):
    n = x.shape[0]
    tm = TM_TOK
    row = lambda width: pl.BlockSpec((tm, width), lambda i: (i, 0))
    full = lambda r, c: pl.BlockSpec((r, c), lambda i: (0, 0))
    return pl.pallas_call(
        _mixer_out_kernel,
        grid=(n // tm,),
        in_specs=[row(WIDTH_A), row(WIDTH_B), row(D_MODEL), full(1, WIDTH_A), full(1, WIDTH_B),
                  full(MIX_WIDTH, D_MODEL), full(1, D_MODEL), full(1, D_MODEL), full(D_MODEL, D_MODEL)],
        out_specs=[row(D_MODEL), row(D_MODEL)],
        out_shape=[jax.ShapeDtypeStruct((n, D_MODEL), F32), jax.ShapeDtypeStruct((n, D_MODEL), BF16)],
        compiler_params=_params(1),
        name="mixer_out",
    )(out_a, out_b, x, g_a, g_b, w_out, ln_g, ln_b, w_q)


def _mem_attn_kernel(q_ref, kv_ref, x1_ref, wo_ref, lng_ref, lnb_ref, wr_ref, br_ref,
                     x2_ref, idx_ref, gate_ref):
    heads = []
    for h in range(N_HEADS_MEM):
        sl = slice(h * HEAD_DIM_MEM, (h + 1) * HEAD_DIM_MEM)
        q = q_ref[:, sl] * (HEAD_DIM_MEM ** -0.5)
        k = kv_ref[:, sl]
        v = kv_ref[:, D_MODEL + h * HEAD_DIM_MEM:D_MODEL + (h + 1) * HEAD_DIM_MEM]
        s = lax.dot_general(q, k, _NT, preferred_element_type=F32)
        m = jnp.max(s, axis=-1, keepdims=True)
        p = jnp.exp(s - m)
        l = jnp.sum(p, axis=-1, keepdims=True)
        heads.append((jnp.dot(p.astype(BF16), v, preferred_element_type=F32) / l).astype(BF16))
    o = jnp.concatenate(heads, axis=-1)
    y = jnp.dot(o, wo_ref[...], preferred_element_type=F32)
    x2 = _layer_norm(DEEPNORM_ALPHA * x1_ref[...] + y, lng_ref[...], lnb_ref[...])
    _store_token_major(x2_ref, x2)

    logits = jnp.dot(x2.astype(BF16), wr_ref[...], preferred_element_type=F32) + br_ref[...]
    tm = logits.shape[0]
    e_iota = lax.broadcasted_iota(jnp.int32, (tm, N_EXPERTS), 1)
    lane = lax.broadcasted_iota(jnp.int32, (tm, LANES), 1)
    idx_out = jnp.zeros((tm, LANES), jnp.int32)
    val_out = jnp.zeros((tm, LANES), F32)
    top = None
    denom = jnp.zeros((tm, 1), F32)
    for kk in range(TOP_K):
        m = jnp.max(logits, axis=-1, keepdims=True)
        sel = jnp.min(jnp.where(logits == m, e_iota, N_EXPERTS), axis=-1, keepdims=True)
        if top is None:
            top = m
        e = jnp.exp(m - top)
        denom = denom + e
        idx_out = jnp.where(lane == kk, sel, idx_out)
        val_out = jnp.where(lane == kk, e, val_out)
        logits = jnp.where(e_iota == sel, -jnp.inf, logits)
    idx_ref[...] = idx_out
    gate_ref[...] = val_out / denom


def _mem_attn(qm, kv, x1, w_o, ln_g, ln_b, w_r, b_r, b, s):
    tm = TM_TOK
    nt = s // tm
    mem_len = kv.shape[1]
    row = lambda width: pl.BlockSpec((tm, width), lambda bi, i: (bi * nt + i, 0))
    full = lambda r, c: pl.BlockSpec((r, c), lambda bi, i: (0, 0))
    n = b * s
    return pl.pallas_call(
        _mem_attn_kernel,
        grid=(b, nt),
        in_specs=[row(D_MODEL), pl.BlockSpec((None, mem_len, 2 * D_MODEL), lambda bi, i: (bi, 0, 0)),
                  row(D_MODEL), full(D_MODEL, D_MODEL), full(1, D_MODEL), full(1, D_MODEL),
                  full(D_MODEL, N_EXPERTS), full(1, N_EXPERTS)],
        out_specs=[pl.BlockSpec((tm * LANE_BLOCKS, LANES), lambda bi, i: (bi * nt + i, 0)), row(LANES), row(LANES)],
        out_shape=[jax.ShapeDtypeStruct((n * LANE_BLOCKS, LANES), F32),
                   jax.ShapeDtypeStruct((n, LANES), jnp.int32),
                   jax.ShapeDtypeStruct((n, LANES), F32)],
        compiler_params=_params(2),
        name="mem_attn",
    )(qm, kv, x1, w_o, ln_g, ln_b, w_r, b_r)


def _gather_rows(idx_ref, first, count, src_hbm, dst_ref, sem):
    for r in range(count):
        src = pl.ds(pl.multiple_of(idx_ref[first + r], LANE_BLOCKS), LANE_BLOCKS)
        pltpu.make_async_copy(src_hbm.at[src], dst_ref.at[pl.ds(r * LANE_BLOCKS, LANE_BLOCKS)], sem).start(
            priority=r % 2)


def _wait_rows(count, src_hbm, dst_ref, sem):
    pltpu.make_async_copy(src_hbm.at[pl.ds(0, count * LANE_BLOCKS)], dst_ref, sem).wait()


def _moe_kernel(bexp_ref, nused_ref, tok_ref, x_hbm, wgu_ref, bgu_ref, wd_ref, bd_ref,
                y_ref, xbuf, sem):
    i = pl.program_id(0)
    slot = i % MOE_SLOTS
    ahead = MOE_SLOTS - 1
    n_used = nused_ref[0]

    @pl.when(i == 0)
    def _():
        for b in range(ahead):
            _gather_rows(tok_ref, b * TM_MOE, TM_MOE, x_hbm, xbuf.at[b], sem.at[b])

    @pl.when(i < n_used + ahead)
    def _():
        _wait_rows(TM_MOE, x_hbm, xbuf.at[slot], sem.at[slot])

    @pl.when(i < n_used)
    def _():
        x = _load_token_major(xbuf.at[slot], 0, TM_MOE).astype(BF16)
        nxt = (i + ahead) % MOE_SLOTS
        _gather_rows(tok_ref, (i + ahead) * TM_MOE, TM_MOE, x_hbm, xbuf.at[nxt], sem.at[nxt])
        gu = jnp.dot(x, wgu_ref[0], preferred_element_type=F32) + bgu_ref[0]
        gate = jnp.minimum(gu[:, :D_FF], SWIGLU_LIMIT)
        up = jnp.clip(gu[:, D_FF:], -SWIGLU_LIMIT, SWIGLU_LIMIT)
        glu = gate * jax.nn.sigmoid(gate * SWIGLU_ALPHA)
        hid = ((up + 1.0) * glu).astype(BF16)
        _store_token_major(y_ref, jnp.dot(hid, wd_ref[0], preferred_element_type=F32) + bd_ref[0])

    @pl.when(i >= n_used)
    def _():
        y_ref[...] = jnp.zeros_like(y_ref)


def _moe_experts(block_expert, n_used, row_token, x2, w_gu, b_gu, w_d, b_d):
    nb = block_expert.shape[0]
    grid_spec = pltpu.PrefetchScalarGridSpec(
        num_scalar_prefetch=3,
        grid=(nb,),
        in_specs=[pl.BlockSpec(memory_space=pl.ANY),
                  pl.BlockSpec((1, D_MODEL, 2 * D_FF), lambda i, be, nu, tok: (be[i], 0, 0)),
                  pl.BlockSpec((1, 1, 2 * D_FF), lambda i, be, nu, tok: (be[i], 0, 0)),
                  pl.BlockSpec((1, D_FF, D_MODEL), lambda i, be, nu, tok: (be[i], 0, 0)),
                  pl.BlockSpec((1, 1, D_MODEL), lambda i, be, nu, tok: (be[i], 0, 0))],
        out_specs=pl.BlockSpec((TM_MOE * LANE_BLOCKS, LANES), lambda i, be, nu, tok: (i, 0)),
        scratch_shapes=[pltpu.VMEM((MOE_SLOTS, TM_MOE * LANE_BLOCKS, LANES), F32),
                        pltpu.SemaphoreType.DMA((MOE_SLOTS,))],
    )
    return pl.pallas_call(
        _moe_kernel,
        grid_spec=grid_spec,
        out_shape=jax.ShapeDtypeStruct((nb * TM_MOE * LANE_BLOCKS, LANES), F32),
        compiler_params=_params(1),
        name="moe_experts",
    )(block_expert, n_used, row_token * LANE_BLOCKS, x2, w_gu, b_gu, w_d, b_d)


def _combine_kernel(idx_ref, y_hbm, x2_ref, gate_ref, lng_ref, lnb_ref, o_ref, ybuf, sem):
    i = pl.program_id(0)
    nt = pl.num_programs(0)
    slot = i % 2
    rows = TOP_K * TM_COMB

    @pl.when(i == 0)
    def _():
        _gather_rows(idx_ref, 0, rows, y_hbm, ybuf.at[0], sem.at[0])

    @pl.when(i + 1 < nt)
    def _():
        _gather_rows(idx_ref, (i + 1) * rows, rows, y_hbm, ybuf.at[1 - slot], sem.at[1 - slot])

    _wait_rows(rows, y_hbm, ybuf.at[slot], sem.at[slot])

    g = gate_ref[...]
    ff = jnp.zeros((TM_COMB, D_MODEL), F32)
    for kk in range(TOP_K):
        ff = ff + g[:, kk:kk + 1] * _load_token_major(ybuf.at[slot], kk * TM_COMB, TM_COMB)
    x2 = _load_token_major(x2_ref, 0, TM_COMB)
    o_ref[...] = _layer_norm(DEEPNORM_ALPHA * x2 + ff, lng_ref[...], lnb_ref[...])


def _combine(dest, y_rows, x2, gates, ln_g, ln_b):
    n = x2.shape[0] // LANE_BLOCKS
    nt = n // TM_COMB
    rows = TOP_K * TM_COMB
    idx = (dest * LANE_BLOCKS).reshape(nt, TM_COMB, TOP_K).transpose(0, 2, 1).reshape(nt * rows)
    row = lambda width: pl.BlockSpec((TM_COMB, width), lambda i, idx: (i, 0))
    full = lambda r, c: pl.BlockSpec((r, c), lambda i, idx: (0, 0))
    grid_spec = pltpu.PrefetchScalarGridSpec(
        num_scalar_prefetch=1,
        grid=(nt,),
        in_specs=[pl.BlockSpec(memory_space=pl.ANY),
                  pl.BlockSpec((TM_COMB * LANE_BLOCKS, LANES), lambda i, idx: (i, 0)), row(LANES),
                  full(1, D_MODEL), full(1, D_MODEL)],
        out_specs=row(D_MODEL),
        scratch_shapes=[pltpu.VMEM((2, rows * LANE_BLOCKS, LANES), F32), pltpu.SemaphoreType.DMA((2,))],
    )
    return pl.pallas_call(
        _combine_kernel,
        grid_spec=grid_spec,
        out_shape=jax.ShapeDtypeStruct((n, D_MODEL), F32),
        compiler_params=_params(1),
        name="combine",
    )(idx, y_rows, x2, gates, ln_g, ln_b)


def _rank_kernel(idx_ref, dest_ref, counts_ref, running):
    p = pl.program_id(0)
    j = pl.program_id(1)
    t = TM_RANK
    idx = idx_ref[...]
    e_iota = lax.broadcasted_iota(jnp.int32, (t, LANES), 1)
    hot = [idx[:, k:k + 1] == e_iota for k in range(TOP_K)]
    chosen = sum(jnp.where(h, 1.0, 0.0) for h in hot)
    tile_counts = jnp.sum(chosen, axis=0, keepdims=True)

    @pl.when((p == 0) & (j == 0))
    def _():
        running[...] = jnp.zeros_like(running)

    @pl.when(p == 0)
    def _():
        running[...] += tile_counts
        dest_ref[...] = jnp.zeros_like(dest_ref)

    @pl.when((p == 1) & (j == 0))
    def _():
        counts = running[...]
        counts_ref[...] = counts
        padded = jnp.ceil(counts * (1.0 / TM_MOE)) * TM_MOE
        lane = lax.broadcasted_iota(jnp.int32, (1, LANES), 1)
        scan = padded
        shift = 1
        while shift < N_EXPERTS:
            scan = scan + jnp.where(lane >= shift, pltpu.roll(scan, shift, 1), 0.0)
            shift *= 2
        running[...] = scan - padded

    @pl.when(p == 1)
    def _():
        r_iota = lax.broadcasted_iota(jnp.int32, (t, t), 0)
        c_iota = lax.broadcasted_iota(jnp.int32, (t, t), 1)
        earlier = jnp.where(c_iota < r_iota, 1.0, 0.0).astype(BF16)
        base = jnp.dot(earlier, chosen.astype(BF16), preferred_element_type=F32) + running[...]
        out = jnp.zeros((t, LANES), jnp.int32)
        for k in range(TOP_K):
            row = jnp.sum(jnp.where(hot[k], base, 0.0), axis=-1, keepdims=True).astype(jnp.int32)
            out = jnp.where(e_iota == k, row, out)
        dest_ref[...] = out
        running[...] += tile_counts


def _row_token_kernel(dest_ref, gap_ref, tok_ref):
    unroll = 8

    def clear(row, carry):
        tok_ref[row] = 0
        return carry

    def place(c, carry):
        rows = [dest_ref[c * unroll + u] for u in range(unroll)]
        for u in range(unroll):
            tok_ref[rows[u]] = c * (unroll // TOP_K) + u // TOP_K
        return carry

    for g in range(gap_ref.shape[1]):
        lax.fori_loop(gap_ref[0, g], gap_ref[1, g], clear, 0)
    lax.fori_loop(0, dest_ref.shape[0] // unroll, place, 0)


def _dispatch_plan(top_idx, n):
    n_assign = n * TOP_K
    nb = n_assign // TM_MOE + N_EXPERTS + MOE_SLOTS - 2
    dest, counts = pl.pallas_call(
        _rank_kernel,
        grid=(2, n // TM_RANK),
        in_specs=[pl.BlockSpec((TM_RANK, LANES), lambda p, j: (j, 0))],
        out_specs=[pl.BlockSpec((TM_RANK, LANES), lambda p, j: (p * j, 0)),
                   pl.BlockSpec((1, LANES), lambda p, j: (0, 0))],
        out_shape=[jax.ShapeDtypeStruct((n, LANES), jnp.int32), jax.ShapeDtypeStruct((1, LANES), F32)],
        scratch_shapes=[pltpu.VMEM((1, LANES), F32)],
        compiler_params=_params(2),
        name="expert_rank",
    )(top_idx)
    dest = dest[:, :TOP_K].reshape(-1)
    counts = counts[0, :N_EXPERTS].astype(jnp.int32)
    padded = (counts + TM_MOE - 1) // TM_MOE * TM_MOE
    pad_end = jnp.cumsum(padded)
    tail = jnp.stack([pad_end[-1], jnp.int32(nb * TM_MOE)])[:, None]
    gaps = jnp.concatenate([jnp.stack([pad_end - padded + counts, pad_end]), tail], axis=1)
    row_token = pl.pallas_call(
        _row_token_kernel,
        in_specs=[pl.BlockSpec(memory_space=pltpu.SMEM), pl.BlockSpec(memory_space=pltpu.SMEM)],
        out_specs=pl.BlockSpec(memory_space=pltpu.SMEM),
        out_shape=jax.ShapeDtypeStruct((nb * TM_MOE,), jnp.int32),
        name="row_token",
    )(dest, gaps)
    block_first_row = jnp.arange(nb, dtype=jnp.int32) * TM_MOE
    block_expert = jnp.minimum(jnp.sum(pad_end[None, :] <= block_first_row[:, None], axis=1),
                               N_EXPERTS - 1).astype(jnp.int32)
    n_used = (pad_end[-1:] // TM_MOE).astype(jnp.int32)
    return dest, row_token, block_expert, n_used


def kernel(x, mem, w_in, rel_bias, g_group_a, g_group_b, w_out, w_q_mem, w_kv_mem, w_o_mem, w_router, b_router, w_gate_up, b_gate_up, w_down, b_down, ln_g, ln_b):
    b, s, d = x.shape
    n = b * s
    depth = w_in.shape[0]
    xf = x.reshape(n, d)
    for l in range(depth):
        q_scale = jnp.ones((3 * MIX_WIDTH,), F32).at[:WIDTH_A].set(HEAD_DIM ** -0.5)
        q_scale = q_scale.at[3 * WIDTH_A:3 * WIDTH_A + WIDTH_B].set(HEAD_DIM ** -0.5)
        proj = _matmul(xf, (w_in[l] * q_scale).astype(BF16), BF16, TM_TOK, 1024).reshape(b, s, 3 * MIX_WIDTH)
        out_a = _chunk_attention(proj, _chunk_bias(rel_bias[l]), b, s).reshape(n, WIDTH_A)
        out_b = _sb_attention(proj, b, s).reshape(n, WIDTH_B)
        x1, qm = _mixer_out(out_a, out_b, xf, g_group_a[l][None], g_group_b[l][None],
                            w_out[l].astype(BF16), ln_g[l, 0][None], ln_b[l, 0][None],
                            w_q_mem[l].astype(BF16))
        mem_len = mem.shape[1]
        kv = _matmul(mem.reshape(b * mem_len, d), w_kv_mem[l].astype(BF16), BF16, b * mem_len, 1024)
        x2, top_idx, gates = _mem_attn(qm, kv.reshape(b, mem_len, 2 * d), x1, w_o_mem[l].astype(BF16),
                                       ln_g[l, 1][None], ln_b[l, 1][None],
                                       w_router[l].astype(BF16), b_router[l][None], b, s)
        dest, row_token, block_expert, n_used = _dispatch_plan(top_idx, n)
        y_rows = _moe_experts(block_expert, n_used, row_token, x2,
                              w_gate_up[l].astype(BF16), b_gate_up[l][:, None, :],
                              w_down[l].astype(BF16), b_down[l][:, None, :])
        xf = _combine(dest, y_rows, x2, gates, ln_g[l, 2][None], ln_b[l, 2][None])
    return xf.reshape(b, s, d)
```

```python
import functools

import jax
import jax.numpy as jnp
from jax import lax
from jax.experimental import pallas as pl
from jax.experimental.pallas import tpu as pltpu

D_MODEL = 1024
CHUNK = 64
LEFT_CHUNKS = 8
LEFT = LEFT_CHUNKS * CHUNK
HEAD_DIM = 64
N_HEADS_A = 8
N_HEADS_B = 8
WIDTH_A = N_HEADS_A * HEAD_DIM
WIDTH_B = N_HEADS_B * HEAD_DIM
MIX_WIDTH = WIDTH_A + WIDTH_B
REL_CLIP = 128
N_HEADS_MEM = 4
HEAD_DIM_MEM = D_MODEL // N_HEADS_MEM
N_EXPERTS = 32
TOP_K = 4
D_FF = D_MODEL
SWIGLU_LIMIT = 7.0
SWIGLU_ALPHA = 1.702
LN_EPS = 1e-5
RMS_EPS = 1e-6
DEEPNORM_ALPHA = 2.0 ** 0.25
NEG_INF = -1e30
LOG2E = 1.4426950408889634

LANES = 128
LANE_BLOCKS = D_MODEL // LANES
VMEM_LIMIT = 48 * 1024 * 1024

TQ_A = 512
SUB_A = 256
TB_SB = 256
SB_UNDERFLOW = 110.0
TM_TOK = 512
TM_MOE = 256
MOE_SLOTS = 3
TM_COMB = 256
COMB_SLOTS = 3
TM_RANK = 512

F32 = jnp.float32
BF16 = jnp.bfloat16
_NT = (((1,), (1,)), ((), ()))


def _params(n_axes):
    return pltpu.CompilerParams(dimension_semantics=("arbitrary",) * n_axes,
                                vmem_limit_bytes=VMEM_LIMIT)


def _store_token_major(ref, value):
    rows = value.shape[0]
    for c in range(LANE_BLOCKS):
        ref[pl.ds(c, rows, stride=LANE_BLOCKS), :] = value[:, c * LANES:(c + 1) * LANES]


def _load_token_major(ref, first_row, rows):
    return jnp.concatenate(
        [ref[pl.ds(first_row * LANE_BLOCKS + c, rows, stride=LANE_BLOCKS), :] for c in range(LANE_BLOCKS)],
        axis=1)


def _layer_norm(r, g, b):
    mu = jnp.mean(r, axis=-1, keepdims=True)
    d = r - mu
    var = jnp.mean(d * d, axis=-1, keepdims=True)
    return d * lax.rsqrt(var + LN_EPS) * g + b


def _matmul_kernel(x_ref, w_ref, o_ref):
    o_ref[...] = jnp.dot(x_ref[...].astype(BF16), w_ref[...],
                         preferred_element_type=F32).astype(o_ref.dtype)


def _matmul(x, w, out_dtype, tm, tn):
    m, k = x.shape
    n = w.shape[1]
    return pl.pallas_call(
        _matmul_kernel,
        grid=(m // tm, n // tn),
        in_specs=[pl.BlockSpec((tm, k), lambda i, j: (i, 0)),
                  pl.BlockSpec((k, tn), lambda i, j: (0, j))],
        out_specs=pl.BlockSpec((tm, tn), lambda i, j: (i, j)),
        out_shape=jax.ShapeDtypeStruct((m, n), out_dtype),
        compiler_params=_params(2),
        name="matmul",
    )(x, w)


def _chunk_attn_kernel(q_ref, kp_ref, kc_ref, vp_ref, vc_ref, bias0_ref, bias1_ref, o_ref):
    w = SUB_A + LEFT
    for h in range(2):
        sl = slice(h * HEAD_DIM, (h + 1) * HEAD_DIM)
        k = jnp.concatenate([kp_ref[:, sl], kc_ref[:, sl]], axis=0)
        v = jnp.concatenate([vp_ref[:, sl], vc_ref[:, sl]], axis=0)
        for sub, bias_ref in enumerate((bias0_ref, bias1_ref)):
            rows = slice(sub * SUB_A, (sub + 1) * SUB_A)
            keys = slice(sub * SUB_A, sub * SUB_A + w)
            s = lax.dot_general(q_ref[rows, sl], k[keys], _NT, preferred_element_type=F32) + bias_ref[h]
            m = jnp.max(s, axis=-1, keepdims=True)
            p = jnp.exp(s - m)
            l = jnp.sum(p, axis=-1, keepdims=True)
            o = jnp.dot(p.astype(BF16), v[keys], preferred_element_type=F32) / l
            o_ref[rows, sl] = o.astype(o_ref.dtype)


def _chunk_bias(rel_bias):
    w = SUB_A + LEFT
    heads = rel_bias.shape[0]
    m = jnp.arange(2 * w)
    rel = jnp.where(m < w, LEFT - m, LEFT + 2 * w - m)
    diag = rel_bias[:, jnp.clip(rel, -REL_CLIP, REL_CLIP) + REL_CLIP].astype(F32)[:, None, :]
    n_tables = 1 + TQ_A // SUB_A
    return pl.pallas_call(
        _chunk_bias_kernel,
        grid=(heads,),
        in_specs=[pl.BlockSpec((1, 1, 2 * w), lambda h: (h, 0, 0))],
        out_specs=pl.BlockSpec((n_tables, 1, SUB_A, w), lambda h: (0, h, 0, 0)),
        out_shape=jax.ShapeDtypeStruct((n_tables, heads, SUB_A, w), F32),
        compiler_params=_params(1),
        name="chunk_bias",
    )(diag)


def _chunk_bias_kernel(diag_ref, o_ref):
    w = SUB_A + LEFT
    rolled = pltpu.roll(jnp.broadcast_to(diag_ref[0], (SUB_A, 2 * w)), 0, 1, stride=1, stride_axis=0)
    col = lax.broadcasted_iota(jnp.int32, (SUB_A, w), 1)
    qc = lax.broadcasted_iota(jnp.int32, (SUB_A, w), 0) // CHUNK
    kc = col // CHUNK
    band = (kc >= qc) & (kc <= qc + LEFT_CHUNKS)
    table = jnp.where(band, rolled[:, :w], NEG_INF)
    o_ref[0, 0] = table
    for sub in range(TQ_A // SUB_A):
        o_ref[1 + sub, 0] = jnp.where(col >= LEFT - sub * SUB_A, table, NEG_INF)


def _chunk_attention(proj, bias, b, s):
    nq = s // TQ_A
    pairs = WIDTH_A // LANES
    blk = lambda off, prev: pl.BlockSpec(
        (None, TQ_A, LANES),
        (lambda bi, p, i: (bi, jnp.maximum(i - 1, 0), off + p)) if prev
        else (lambda bi, p, i: (bi, i, off + p)))
    table = lambda sub: pl.BlockSpec((None, 2, SUB_A, SUB_A + LEFT),
                                     lambda bi, p, i: (jnp.where(i == 0, 1 + sub, 0), p, 0, 0))
    return pl.pallas_call(
        _chunk_attn_kernel,
        grid=(b, pairs, nq),
        in_specs=[blk(0, False),
                  blk(pairs, True), blk(pairs, False),
                  blk(2 * pairs, True), blk(2 * pairs, False),
                  table(0), table(1)],
        out_specs=pl.BlockSpec((None, TQ_A, LANES), lambda bi, p, i: (bi, i, p)),
        out_shape=jax.ShapeDtypeStruct((b, s, WIDTH_A), BF16),
        compiler_params=_params(3),
        name="chunk_attn",
    )(proj, proj, proj, proj, proj, bias, bias)


def _sb_block(q, k_ref, v_ref, sl, j, later, tri, causal):
    t = TB_SB
    start = pl.multiple_of(j * t, t)
    k = k_ref[pl.ds(start, t), sl]
    v = v_ref[pl.ds(start, t), sl]
    z = lax.dot_general(q, k, _NT, preferred_element_type=F32)
    sp = jnp.maximum(z, 0.0) + jnp.log(1.0 + jnp.exp2(jnp.abs(z) * (-LOG2E)))
    if causal is not None:
        sp = jnp.where(causal, sp, 0.0)
    csum = jnp.dot(sp.astype(BF16), tri, preferred_element_type=F32) + later
    wgt = jnp.exp(z - csum)
    if causal is not None:
        wgt = jnp.where(causal, wgt, 0.0)
    pv = jnp.dot(wgt.astype(BF16), v, preferred_element_type=F32)
    return pv, csum[:, 0:1]


def _sb_attn_kernel(q_ref, k_ref, v_ref, o_ref):
    i = pl.program_id(2)
    t = TB_SB
    row = lax.broadcasted_iota(jnp.int32, (t, t), 0)
    col = lax.broadcasted_iota(jnp.int32, (t, t), 1)
    tri = jnp.where(row >= col, 1.0, 0.0).astype(BF16)
    causal = col < row
    sls = [slice(h * HEAD_DIM, (h + 1) * HEAD_DIM) for h in range(2)]
    qs = [q_ref[:, sl] for sl in sls]
    block = functools.partial(_sb_block, tri=tri)

    has_prev = i >= 1
    carry = []
    for h in range(2):
        pv, later = block(qs[h], k_ref, v_ref, sls[h], i, jnp.zeros((t, 1), F32), causal=causal)
        pv_prev, later_prev = block(qs[h], k_ref, v_ref, sls[h], jnp.maximum(i - 1, 0), later, causal=None)
        carry += [jnp.where(has_prev, later_prev, later), pv + jnp.where(has_prev, pv_prev, 0.0)]

    def pending(c):
        return (c[0] >= 0) & (jnp.minimum(jnp.min(c[1]), jnp.min(c[3])) < SB_UNDERFLOW)

    def step(c):
        out = [c[0] - 1]
        for h in range(2):
            pv, later = block(qs[h], k_ref, v_ref, sls[h], c[0], c[1 + 2 * h], causal=None)
            out += [later, c[2 + 2 * h] + pv]
        return tuple(out)

    carry = lax.while_loop(pending, step, (i - 2, *carry))
    for h in range(2):
        o_ref[:, sls[h]] = carry[2 + 2 * h].astype(o_ref.dtype)


def _sb_attention(proj, b, s):
    nq = s // TB_SB
    pairs = WIDTH_B // LANES
    base = 3 * WIDTH_A // LANES
    return pl.pallas_call(
        _sb_attn_kernel,
        grid=(b, pairs, nq),
        in_specs=[pl.BlockSpec((None, TB_SB, LANES), lambda bi, p, i: (bi, i, base + p)),
                  pl.BlockSpec((None, s, LANES), lambda bi, p, i: (bi, 0, base + pairs + p)),
                  pl.BlockSpec((None, s, LANES), lambda bi, p, i: (bi, 0, base + 2 * pairs + p))],
        out_specs=pl.BlockSpec((None, TB_SB, LANES), lambda bi, p, i: (bi, i, p)),
        out_shape=jax.ShapeDtypeStruct((b, s, WIDTH_B), BF16),
        compiler_params=_params(3),
        name="sb_attn",
    )(proj, proj, proj)


def _mixer_out_kernel(oa_ref, ob_ref, x_ref, ga_ref, gb_ref, wout_ref, lng_ref, lnb_ref, wq_ref,
                      x1_ref, qm_ref):
    def rms(ref, g_ref):
        a = ref[...].astype(F32)
        return (a * lax.rsqrt(jnp.mean(a * a, axis=-1, keepdims=True) + RMS_EPS) * g_ref[...]).astype(BF16)

    y = jnp.dot(rms(oa_ref, ga_ref), wout_ref[:WIDTH_A, :], preferred_element_type=F32)
    y = y + jnp.dot(rms(ob_ref, gb_ref), wout_ref[WIDTH_A:, :], preferred_element_type=F32)
    x1 = _layer_norm(DEEPNORM_ALPHA * x_ref[...] + y, lng_ref[...], lnb_ref[...])
    x1_ref[...] = x1
    qm_ref[...] = jnp.dot(x1.astype(BF16), wq_ref[...], preferred_element_type=F32).astype(BF16)


def _mixer_out(out_a, out_b, x, g_a, g_b, w_out, ln_g, ln_b, w_q):
    n = x.shape[0]
    tm = TM_TOK
    row = lambda width: pl.BlockSpec((tm, width), lambda i: (i, 0))
    full = lambda r, c: pl.BlockSpec((r, c), lambda i: (0, 0))
    return pl.pallas_call(
        _mixer_out_kernel,
        grid=(n // tm,),
        in_specs=[row(WIDTH_A), row(WIDTH_B), row(D_MODEL), full(1, WIDTH_A), full(1, WIDTH_B),
                  full(MIX_WIDTH, D_MODEL), full(1, D_MODEL), full(1, D_MODEL), full(D_MODEL, D_MODEL)],
        out_specs=[row(D_MODEL), row(D_MODEL)],
        out_shape=[jax.ShapeDtypeStruct((n, D_MODEL), F32), jax.ShapeDtypeStruct((n, D_MODEL), BF16)],
        compiler_params=_params(1),
        name="mixer_out",
    )(out_a, out_b, x, g_a, g_b, w_out, ln_g, ln_b, w_q)


def _mem_attn_kernel(q_ref, kv_ref, x1_ref, wo_ref, lng_ref, lnb_ref, wr_ref, br_ref,
                     x2_ref, idx_ref, gate_ref):
    heads = []
    for h in range(N_HEADS_MEM):
        sl = slice(h * HEAD_DIM_MEM, (h + 1) * HEAD_DIM_MEM)
        q = q_ref[:, sl] * (HEAD_DIM_MEM ** -0.5)
        k = kv_ref[:, sl]
        v = kv_ref[:, D_MODEL + h * HEAD_DIM_MEM:D_MODEL + (h + 1) * HEAD_DIM_MEM]
        s = lax.dot_general(q, k, _NT, preferred_element_type=F32)
        m = jnp.max(s, axis=-1, keepdims=True)
        p = jnp.exp(s - m)
        l = jnp.sum(p, axis=-1, keepdims=True)
        heads.append((jnp.dot(p.astype(BF16), v, preferred_element_type=F32) / l).astype(BF16))
    o = jnp.concatenate(heads, axis=-1)
    y = jnp.dot(o, wo_ref[...], preferred_element_type=F32)
    x2 = _layer_norm(DEEPNORM_ALPHA * x1_ref[...] + y, lng_ref[...], lnb_ref[...])
    _store_token_major(x2_ref, x2)

    logits = jnp.dot(x2.astype(BF16), wr_ref[...], preferred_element_type=F32) + br_ref[...]
    tm = logits.shape[0]
    e_iota = lax.broadcasted_iota(jnp.int32, (tm, N_EXPERTS), 1)
    lane = lax.broadcasted_iota(jnp.int32, (tm, LANES), 1)
    idx_out = jnp.zeros((tm, LANES), jnp.int32)
    val_out = jnp.zeros((tm, LANES), F32)
    top = None
    denom = jnp.zeros((tm, 1), F32)
    for kk in range(TOP_K):
        m = jnp.max(logits, axis=-1, keepdims=True)
        sel = jnp.min(jnp.where(logits == m, e_iota, N_EXPERTS), axis=-1, keepdims=True)
        if top is None:
            top = m
        e = jnp.exp(m - top)
        denom = denom + e
        idx_out = jnp.where(lane == kk, sel, idx_out)
        val_out = jnp.where(lane == kk, e, val_out)
        logits = jnp.where(e_iota == sel, -jnp.inf, logits)
    idx_ref[...] = idx_out
    gate_ref[...] = val_out / denom


def _mem_attn(qm, kv, x1, w_o, ln_g, ln_b, w_r, b_r, b, s):
    tm = TM_TOK
    nt = s // tm
    mem_len = kv.shape[1]
    row = lambda width: pl.BlockSpec((tm, width), lambda bi, i: (bi * nt + i, 0))
    full = lambda r, c: pl.BlockSpec((r, c), lambda bi, i: (0, 0))
    n = b * s
    return pl.pallas_call(
        _mem_attn_kernel,
        grid=(b, nt),
        in_specs=[row(D_MODEL), pl.BlockSpec((None, mem_len, 2 * D_MODEL), lambda bi, i: (bi, 0, 0)),
                  row(D_MODEL), full(D_MODEL, D_MODEL), full(1, D_MODEL), full(1, D_MODEL),
                  full(D_MODEL, N_EXPERTS), full(1, N_EXPERTS)],
        out_specs=[pl.BlockSpec((tm * LANE_BLOCKS, LANES), lambda bi, i: (bi * nt + i, 0)), row(LANES), row(LANES)],
        out_shape=[jax.ShapeDtypeStruct((n * LANE_BLOCKS, LANES), F32),
                   jax.ShapeDtypeStruct((n, LANES), jnp.int32),
                   jax.ShapeDtypeStruct((n, LANES), F32)],
        compiler_params=_params(2),
        name="mem_attn",
    )(qm, kv, x1, w_o, ln_g, ln_b, w_r, b_r)


def _gather_rows(idx_ref, first, count, src_hbm, dst_ref, sem):
    for r in range(count):
        src = pl.ds(pl.multiple_of(idx_ref[first + r], LANE_BLOCKS), LANE_BLOCKS)
        pltpu.make_async_copy(src_hbm.at[src], dst_ref.at[pl.ds(r * LANE_BLOCKS, LANE_BLOCKS)], sem).start(
            priority=r % 2)


def _wait_rows(count, src_hbm, dst_ref, sem):
    pltpu.make_async_copy(src_hbm.at[pl.ds(0, count * LANE_BLOCKS)], dst_ref, sem).wait()


def _moe_kernel(bexp_ref, nused_ref, tok_ref, x_hbm, wgu_ref, bgu_ref, wd_ref, bd_ref,
                y_ref, xbuf, sem):
    i = pl.program_id(0)
    slot = i % MOE_SLOTS
    ahead = MOE_SLOTS - 1
    n_used = nused_ref[0]

    @pl.when(i == 0)
    def _():
        for b in range(ahead):
            _gather_rows(tok_ref, b * TM_MOE, TM_MOE, x_hbm, xbuf.at[b], sem.at[b])

    @pl.when(i < n_used + ahead)
    def _():
        _wait_rows(TM_MOE, x_hbm, xbuf.at[slot], sem.at[slot])

    @pl.when(i < n_used)
    def _():
        x = _load_token_major(xbuf.at[slot], 0, TM_MOE).astype(BF16)
        nxt = (i + ahead) % MOE_SLOTS
        _gather_rows(tok_ref, (i + ahead) * TM_MOE, TM_MOE, x_hbm, xbuf.at[nxt], sem.at[nxt])
        gu = jnp.dot(x, wgu_ref[0], preferred_element_type=F32) + bgu_ref[0]
        gate = jnp.minimum(gu[:, :D_FF], SWIGLU_LIMIT)
        up = jnp.clip(gu[:, D_FF:], -SWIGLU_LIMIT, SWIGLU_LIMIT)
        glu = gate * jax.nn.sigmoid(gate * SWIGLU_ALPHA)
        hid = ((up + 1.0) * glu).astype(BF16)
        _store_token_major(y_ref, jnp.dot(hid, wd_ref[0], preferred_element_type=F32) + bd_ref[0])

    @pl.when(i >= n_used)
    def _():
        y_ref[...] = jnp.zeros_like(y_ref)


def _moe_experts(block_expert, n_used, row_token, x2, w_gu, b_gu, w_d, b_d):
    nb = block_expert.shape[0]
    grid_spec = pltpu.PrefetchScalarGridSpec(
        num_scalar_prefetch=3,
        grid=(nb,),
        in_specs=[pl.BlockSpec(memory_space=pl.ANY),
                  pl.BlockSpec((1, D_MODEL, 2 * D_FF), lambda i, be, nu, tok: (be[i], 0, 0)),
                  pl.BlockSpec((1, 1, 2 * D_FF), lambda i, be, nu, tok: (be[i], 0, 0)),
                  pl.BlockSpec((1, D_FF, D_MODEL), lambda i, be, nu, tok: (be[i], 0, 0)),
                  pl.BlockSpec((1, 1, D_MODEL), lambda i, be, nu, tok: (be[i], 0, 0))],
        out_specs=pl.BlockSpec((TM_MOE * LANE_BLOCKS, LANES), lambda i, be, nu, tok: (i, 0)),
        scratch_shapes=[pltpu.VMEM((MOE_SLOTS, TM_MOE * LANE_BLOCKS, LANES), F32),
                        pltpu.SemaphoreType.DMA((MOE_SLOTS,))],
    )
    return pl.pallas_call(
        _moe_kernel,
        grid_spec=grid_spec,
        out_shape=jax.ShapeDtypeStruct((nb * TM_MOE * LANE_BLOCKS, LANES), F32),
        compiler_params=_params(1),
        name="moe_experts",
    )(block_expert, n_used, row_token * LANE_BLOCKS, x2, w_gu, b_gu, w_d, b_d)


def _combine_kernel(idx_ref, y_hbm, x2_ref, gate_ref, lng_ref, lnb_ref, o_ref, ybuf, sem):
    i = pl.program_id(0)
    nt = pl.num_programs(0)
    slot = i % COMB_SLOTS
    ahead = COMB_SLOTS - 1
    rows = TOP_K * TM_COMB

    @pl.when(i == 0)
    def _():
        for t in range(ahead):
            _gather_rows(idx_ref, t * rows, rows, y_hbm, ybuf.at[t], sem.at[t])

    _wait_rows(rows, y_hbm, ybuf.at[slot], sem.at[slot])

    def compute():
        g = gate_ref[...]
        ff = jnp.zeros((TM_COMB, D_MODEL), F32)
        for kk in range(TOP_K):
            ff = ff + g[:, kk:kk + 1] * _load_token_major(ybuf.at[slot], kk * TM_COMB, TM_COMB)
        x2 = _load_token_major(x2_ref, 0, TM_COMB)
        o_ref[...] = _layer_norm(DEEPNORM_ALPHA * x2 + ff, lng_ref[...], lnb_ref[...])

    @pl.when(i + ahead < nt)
    def _():
        nxt = (i + ahead) % COMB_SLOTS
        _gather_rows(idx_ref, (i + ahead) * rows, rows, y_hbm, ybuf.at[nxt], sem.at[nxt])
        compute()

    @pl.when(i + ahead >= nt)
    def _():
        compute()


def _combine(dest, y_rows, x2, gates, ln_g, ln_b):
    n = x2.shape[0] // LANE_BLOCKS
    nt = n // TM_COMB
    rows = TOP_K * TM_COMB
    idx = (dest * LANE_BLOCKS).reshape(nt, TM_COMB, TOP_K).transpose(0, 2, 1).reshape(nt * rows)
    row = lambda width: pl.BlockSpec((TM_COMB, width), lambda i, idx: (i, 0))
    full = lambda r, c: pl.BlockSpec((r, c), lambda i, idx: (0, 0))
    grid_spec = pltpu.PrefetchScalarGridSpec(
        num_scalar_prefetch=1,
        grid=(nt,),
        in_specs=[pl.BlockSpec(memory_space=pl.ANY),
                  pl.BlockSpec((TM_COMB * LANE_BLOCKS, LANES), lambda i, idx: (i, 0)), row(LANES),
                  full(1, D_MODEL), full(1, D_MODEL)],
        out_specs=row(D_MODEL),
        scratch_shapes=[pltpu.VMEM((COMB_SLOTS, rows * LANE_BLOCKS, LANES), F32),
                        pltpu.SemaphoreType.DMA((COMB_SLOTS,))],
    )
    return pl.pallas_call(
        _combine_kernel,
        grid_spec=grid_spec,
        out_shape=jax.ShapeDtypeStruct((n, D_MODEL), F32),
        compiler_params=_params(1),
        name="combine",
    )(idx, y_rows, x2, gates, ln_g, ln_b)


def _rank_kernel(idx_ref, dest_ref, counts_ref, running):
    p = pl.program_id(0)
    j = pl.program_id(1)
    t = TM_RANK
    idx = idx_ref[...]
    e_iota = lax.broadcasted_iota(jnp.int32, (t, LANES), 1)
    hot = [idx[:, k:k + 1] == e_iota for k in range(TOP_K)]
    chosen = sum(jnp.where(h, 1.0, 0.0) for h in hot)
    tile_counts = jnp.sum(chosen, axis=0, keepdims=True)

    @pl.when((p == 0) & (j == 0))
    def _():
        running[...] = jnp.zeros_like(running)

    @pl.when(p == 0)
    def _():
        running[...] += tile_counts
        dest_ref[...] = jnp.zeros_like(dest_ref)

    @pl.when((p == 1) & (j == 0))
    def _():
        counts = running[...]
        counts_ref[...] = counts
        padded = jnp.ceil(counts * (1.0 / TM_MOE)) * TM_MOE
        lane = lax.broadcasted_iota(jnp.int32, (1, LANES), 1)
        scan = padded
        shift = 1
        while shift < N_EXPERTS:
            scan = scan + jnp.where(lane >= shift, pltpu.roll(scan, shift, 1), 0.0)
            shift *= 2
        running[...] = scan - padded

    @pl.when(p == 1)
    def _():
        r_iota = lax.broadcasted_iota(jnp.int32, (t, t), 0)
        c_iota = lax.broadcasted_iota(jnp.int32, (t, t), 1)
        earlier = jnp.where(c_iota < r_iota, 1.0, 0.0).astype(BF16)
        base = jnp.dot(earlier, chosen.astype(BF16), preferred_element_type=F32) + running[...]
        out = jnp.zeros((t, LANES), jnp.int32)
        for k in range(TOP_K):
            row = jnp.sum(jnp.where(hot[k], base, 0.0), axis=-1, keepdims=True).astype(jnp.int32)
            out = jnp.where(e_iota == k, row, out)
        dest_ref[...] = out
        running[...] += tile_counts


def _row_token_kernel(dest_ref, gap_ref, tok_ref):
    unroll = 8

    def clear(row, carry):
        tok_ref[row] = 0
        return carry

    def place(c, carry):
        rows = [dest_ref[c * unroll + u] for u in range(unroll)]
        for u in range(unroll):
            tok_ref[rows[u]] = c * (unroll // TOP_K) + u // TOP_K
        return carry

    for g in range(gap_ref.shape[1]):
        lax.fori_loop(gap_ref[0, g], gap_ref[1, g], clear, 0)
    lax.fori_loop(0, dest_ref.shape[0] // unroll, place, 0)


def _dispatch_plan(top_idx, n):
    n_assign = n * TOP_K
    nb = n_assign // TM_MOE + N_EXPERTS + MOE_SLOTS - 2
    dest, counts = pl.pallas_call(
        _rank_kernel,
        grid=(2, n // TM_RANK),
        in_specs=[pl.BlockSpec((TM_RANK, LANES), lambda p, j: (j, 0))],
        out_specs=[pl.BlockSpec((TM_RANK, LANES), lambda p, j: (p * j, 0)),
                   pl.BlockSpec((1, LANES), lambda p, j: (0, 0))],
        out_shape=[jax.ShapeDtypeStruct((n, LANES), jnp.int32), jax.ShapeDtypeStruct((1, LANES), F32)],
        scratch_shapes=[pltpu.VMEM((1, LANES), F32)],
        compiler_params=_params(2),
        name="expert_rank",
    )(top_idx)
    dest = dest[:, :TOP_K].reshape(-1)
    counts = counts[0, :N_EXPERTS].astype(jnp.int32)
    padded = (counts + TM_MOE - 1) // TM_MOE * TM_MOE
    pad_end = jnp.cumsum(padded)
    tail = jnp.stack([pad_end[-1], jnp.int32(nb * TM_MOE)])[:, None]
    gaps = jnp.concatenate([jnp.stack([pad_end - padded + counts, pad_end]), tail], axis=1)
    row_token = pl.pallas_call(
        _row_token_kernel,
        in_specs=[pl.BlockSpec(memory_space=pltpu.SMEM), pl.BlockSpec(memory_space=pltpu.SMEM)],
        out_specs=pl.BlockSpec(memory_space=pltpu.SMEM),
        out_shape=jax.ShapeDtypeStruct((nb * TM_MOE,), jnp.int32),
        name="row_token",
    )(dest, gaps)
    block_first_row = jnp.arange(nb, dtype=jnp.int32) * TM_MOE
    block_expert = jnp.minimum(jnp.sum(pad_end[None, :] <= block_first_row[:, None], axis=1),
                               N_EXPERTS - 1).astype(jnp.int32)
    n_used = (pad_end[-1:] // TM_MOE).astype(jnp.int32)
    return dest, row_token, block_expert, n_used


def kernel(x, mem, w_in, rel_bias, g_group_a, g_group_b, w_out, w_q_mem, w_kv_mem, w_o_mem, w_router, b_router, w_gate_up, b_gate_up, w_down, b_down, ln_g, ln_b):
    b, s, d = x.shape
    n = b * s
    depth = w_in.shape[0]
    xf = x.reshape(n, d)
    for l in range(depth):
        q_scale = jnp.ones((3 * MIX_WIDTH,), F32).at[:WIDTH_A].set(HEAD_DIM ** -0.5)
        q_scale = q_scale.at[3 * WIDTH_A:3 * WIDTH_A + WIDTH_B].set(HEAD_DIM ** -0.5)
        proj = _matmul(xf, (w_in[l] * q_scale).astype(BF16), BF16, TM_TOK, 3 * MIX_WIDTH).reshape(b, s, 3 * MIX_WIDTH)
        out_a = _chunk_attention(proj, _chunk_bias(rel_bias[l]), b, s).reshape(n, WIDTH_A)
        out_b = _sb_attention(proj, b, s).reshape(n, WIDTH_B)
        x1, qm = _mixer_out(out_a, out_b, xf, g_group_a[l][None], g_group_b[l][None],
                            w_out[l].astype(BF16), ln_g[l, 0][None], ln_b[l, 0][None],
                            w_q_mem[l].astype(BF16))
        mem_len = mem.shape[1]
        kv = _matmul(mem.reshape(b * mem_len, d), w_kv_mem[l].astype(BF16), BF16, b * mem_len, 1024)
        x2, top_idx, gates = _mem_attn(qm, kv.reshape(b, mem_len, 2 * d), x1, w_o_mem[l].astype(BF16),
                                       ln_g[l, 1][None], ln_b[l, 1][None],
                                       w_router[l].astype(BF16), b_router[l][None], b, s)
        dest, row_token, block_expert, n_used = _dispatch_plan(top_idx, n)
        y_rows = _moe_experts(block_expert, n_used, row_token, x2,
                              w_gate_up[l].astype(BF16), b_gate_up[l][:, None, :],
                              w_down[l].astype(BF16), b_down[l][:, None, :])
        xf = _combine(dest, y_rows, x2, gates, ln_g[l, 2][None], ln_b[l, 2][None])
    return xf.reshape(b, s, d)
```

```python
import functools

import jax
import jax.numpy as jnp
from jax import lax
from jax.experimental import pallas as pl
from jax.experimental.pallas import tpu as pltpu

D_MODEL = 1024
CHUNK = 64
LEFT_CHUNKS = 8
LEFT = LEFT_CHUNKS * CHUNK
HEAD_DIM = 64
N_HEADS_A = 8
N_HEADS_B = 8
WIDTH_A = N_HEADS_A * HEAD_DIM
WIDTH_B = N_HEADS_B * HEAD_DIM
MIX_WIDTH = WIDTH_A + WIDTH_B
REL_CLIP = 128
N_HEADS_MEM = 4
HEAD_DIM_MEM = D_MODEL // N_HEADS_MEM
N_EXPERTS = 32
TOP_K = 4
D_FF = D_MODEL
SWIGLU_LIMIT = 7.0
SWIGLU_ALPHA = 1.702
LN_EPS = 1e-5
RMS_EPS = 1e-6
DEEPNORM_ALPHA = 2.0 ** 0.25
NEG_INF = -1e30
LOG2E = 1.4426950408889634

LANES = 128
LANE_BLOCKS = D_MODEL // LANES
VMEM_LIMIT = 48 * 1024 * 1024

TQ_A = 512
SUB_A = 256
TB_SB = 256
SB_UNDERFLOW = 110.0
TM_TOK = 512
TM_MOE = 512
MOE_SLOTS = 3
TM_COMB = 256
COMB_SLOTS = 3
TM_RANK = 512

F32 = jnp.float32
BF16 = jnp.bfloat16
_NT = (((1,), (1,)), ((), ()))


def _params(n_axes):
    return pltpu.CompilerParams(dimension_semantics=("arbitrary",) * n_axes,
                                vmem_limit_bytes=VMEM_LIMIT)


def _store_token_major(ref, value):
    rows = value.shape[0]
    for c in range(LANE_BLOCKS):
        ref[pl.ds(c, rows, stride=LANE_BLOCKS), :] = value[:, c * LANES:(c + 1) * LANES]


def _load_token_major(ref, first_row, rows):
    return jnp.concatenate(
        [ref[pl.ds(first_row * LANE_BLOCKS + c, rows, stride=LANE_BLOCKS), :] for c in range(LANE_BLOCKS)],
        axis=1)


def _layer_norm(r, g, b):
    mu = jnp.mean(r, axis=-1, keepdims=True)
    d = r - mu
    var = jnp.mean(d * d, axis=-1, keepdims=True)
    return d * lax.rsqrt(var + LN_EPS) * g + b


def _matmul_kernel(x_ref, w_ref, o_ref):
    o_ref[...] = jnp.dot(x_ref[...].astype(BF16), w_ref[...],
                         preferred_element_type=F32).astype(o_ref.dtype)


def _matmul(x, w, out_dtype, tm, tn):
    m, k = x.shape
    n = w.shape[1]
    return pl.pallas_call(
        _matmul_kernel,
        grid=(m // tm, n // tn),
        in_specs=[pl.BlockSpec((tm, k), lambda i, j: (i, 0)),
                  pl.BlockSpec((k, tn), lambda i, j: (0, j))],
        out_specs=pl.BlockSpec((tm, tn), lambda i, j: (i, j)),
        out_shape=jax.ShapeDtypeStruct((m, n), out_dtype),
        compiler_params=_params(2),
        name="matmul",
    )(x, w)


def _chunk_attn_kernel(q_ref, kp_ref, kc_ref, vp_ref, vc_ref, bias0_ref, bias1_ref, o_ref):
    w = SUB_A + LEFT
    for h in range(2):
        sl = slice(h * HEAD_DIM, (h + 1) * HEAD_DIM)
        k = jnp.concatenate([kp_ref[:, sl], kc_ref[:, sl]], axis=0)
        v = jnp.concatenate([vp_ref[:, sl], vc_ref[:, sl]], axis=0)
        for sub, bias_ref in enumerate((bias0_ref, bias1_ref)):
            rows = slice(sub * SUB_A, (sub + 1) * SUB_A)
            keys = slice(sub * SUB_A, sub * SUB_A + w)
            s = lax.dot_general(q_ref[rows, sl], k[keys], _NT, preferred_element_type=F32) + bias_ref[h]
            m = jnp.max(s, axis=-1, keepdims=True)
            p = jnp.exp(s - m)
            l = jnp.sum(p, axis=-1, keepdims=True)
            o = jnp.dot(p.astype(BF16), v[keys], preferred_element_type=F32) / l
            o_ref[rows, sl] = o.astype(o_ref.dtype)


def _chunk_bias(rel_bias):
    w = SUB_A + LEFT
    heads = rel_bias.shape[0]
    m = jnp.arange(2 * w)
    rel = jnp.where(m < w, LEFT - m, LEFT + 2 * w - m)
    diag = rel_bias[:, jnp.clip(rel, -REL_CLIP, REL_CLIP) + REL_CLIP].astype(F32)[:, None, :]
    n_tables = 1 + TQ_A // SUB_A
    return pl.pallas_call(
        _chunk_bias_kernel,
        grid=(heads,),
        in_specs=[pl.BlockSpec((1, 1, 2 * w), lambda h: (h, 0, 0))],
        out_specs=pl.BlockSpec((n_tables, 1, SUB_A, w), lambda h: (0, h, 0, 0)),
        out_shape=jax.ShapeDtypeStruct((n_tables, heads, SUB_A, w), F32),
        compiler_params=_params(1),
        name="chunk_bias",
    )(diag)


def _chunk_bias_kernel(diag_ref, o_ref):
    w = SUB_A + LEFT
    rolled = pltpu.roll(jnp.broadcast_to(diag_ref[0], (SUB_A, 2 * w)), 0, 1, stride=1, stride_axis=0)
    col = lax.broadcasted_iota(jnp.int32, (SUB_A, w), 1)
    qc = lax.broadcasted_iota(jnp.int32, (SUB_A, w), 0) // CHUNK
    kc = col // CHUNK
    band = (kc >= qc) & (kc <= qc + LEFT_CHUNKS)
    table = jnp.where(band, rolled[:, :w], NEG_INF)
    o_ref[0, 0] = table
    for sub in range(TQ_A // SUB_A):
        o_ref[1 + sub, 0] = jnp.where(col >= LEFT - sub * SUB_A, table, NEG_INF)


def _chunk_attention(proj, bias, b, s):
    nq = s // TQ_A
    pairs = WIDTH_A // LANES
    blk = lambda off, prev: pl.BlockSpec(
        (None, TQ_A, LANES),
        (lambda bi, p, i: (bi, jnp.maximum(i - 1, 0), off + p)) if prev
        else (lambda bi, p, i: (bi, i, off + p)))
    table = lambda sub: pl.BlockSpec((None, 2, SUB_A, SUB_A + LEFT),
                                     lambda bi, p, i: (jnp.where(i == 0, 1 + sub, 0), p, 0, 0))
    return pl.pallas_call(
        _chunk_attn_kernel,
        grid=(b, pairs, nq),
        in_specs=[blk(0, False),
                  blk(pairs, True), blk(pairs, False),
                  blk(2 * pairs, True), blk(2 * pairs, False),
                  table(0), table(1)],
        out_specs=pl.BlockSpec((None, TQ_A, LANES), lambda bi, p, i: (bi, i, p)),
        out_shape=jax.ShapeDtypeStruct((b, s, WIDTH_A), BF16),
        compiler_params=_params(3),
        name="chunk_attn",
    )(proj, proj, proj, proj, proj, bias, bias)


def _sb_block(q, k_ref, v_ref, sl, j, later, tri, causal):
    t = TB_SB
    start = pl.multiple_of(j * t, t)
    k = k_ref[pl.ds(start, t), sl]
    v = v_ref[pl.ds(start, t), sl]
    z = lax.dot_general(q, k, _NT, preferred_element_type=F32)
    sp = jnp.maximum(z, 0.0) + jnp.log(1.0 + jnp.exp2(jnp.abs(z) * (-LOG2E)))
    if causal is not None:
        sp = jnp.where(causal, sp, 0.0)
    csum = jnp.dot(sp.astype(BF16), tri, preferred_element_type=F32) + later
    wgt = jnp.exp(z - csum)
    if causal is not None:
        wgt = jnp.where(causal, wgt, 0.0)
    pv = jnp.dot(wgt.astype(BF16), v, preferred_element_type=F32)
    return pv, csum[:, 0:1]


def _sb_attn_kernel(q_ref, k_ref, v_ref, o_ref):
    i = pl.program_id(2)
    t = TB_SB
    row = lax.broadcasted_iota(jnp.int32, (t, t), 0)
    col = lax.broadcasted_iota(jnp.int32, (t, t), 1)
    tri = jnp.where(row >= col, 1.0, 0.0).astype(BF16)
    causal = col < row
    sls = [slice(h * HEAD_DIM, (h + 1) * HEAD_DIM) for h in range(2)]
    qs = [q_ref[:, sl] for sl in sls]
    block = functools.partial(_sb_block, tri=tri)

    has_prev = i >= 1
    carry = []
    for h in range(2):
        pv, later = block(qs[h], k_ref, v_ref, sls[h], i, jnp.zeros((t, 1), F32), causal=causal)
        pv_prev, later_prev = block(qs[h], k_ref, v_ref, sls[h], jnp.maximum(i - 1, 0), later, causal=None)
        carry += [jnp.where(has_prev, later_prev, later), pv + jnp.where(has_prev, pv_prev, 0.0)]

    def pending(c):
        return (c[0] >= 0) & (jnp.minimum(jnp.min(c[1]), jnp.min(c[3])) < SB_UNDERFLOW)

    def step(c):
        out = [c[0] - 1]
        for h in range(2):
            pv, later = block(qs[h], k_ref, v_ref, sls[h], c[0], c[1 + 2 * h], causal=None)
            out += [later, c[2 + 2 * h] + pv]
        return tuple(out)

    carry = lax.while_loop(pending, step, (i - 2, *carry))
    for h in range(2):
        o_ref[:, sls[h]] = carry[2 + 2 * h].astype(o_ref.dtype)


def _sb_attention(proj, b, s):
    nq = s // TB_SB
    pairs = WIDTH_B // LANES
    base = 3 * WIDTH_A // LANES
    return pl.pallas_call(
        _sb_attn_kernel,
        grid=(b, pairs, nq),
        in_specs=[pl.BlockSpec((None, TB_SB, LANES), lambda bi, p, i: (bi, i, base + p)),
                  pl.BlockSpec((None, s, LANES), lambda bi, p, i: (bi, 0, base + pairs + p)),
                  pl.BlockSpec((None, s, LANES), lambda bi, p, i: (bi, 0, base + 2 * pairs + p))],
        out_specs=pl.BlockSpec((None, TB_SB, LANES), lambda bi, p, i: (bi, i, p)),
        out_shape=jax.ShapeDtypeStruct((b, s, WIDTH_B), BF16),
        compiler_params=_params(3),
        name="sb_attn",
    )(proj, proj, proj)


def _mixer_out_kernel(oa_ref, ob_ref, x_ref, ga_ref, gb_ref, wout_ref, lng_ref, lnb_ref, wq_ref,
                      x1_ref, qm_ref):
    def rms(ref, g_ref):
        a = ref[...].astype(F32)
        return (a * lax.rsqrt(jnp.mean(a * a, axis=-1, keepdims=True) + RMS_EPS) * g_ref[...]).astype(BF16)

    y = jnp.dot(rms(oa_ref, ga_ref), wout_ref[:WIDTH_A, :], preferred_element_type=F32)
    y = y + jnp.dot(rms(ob_ref, gb_ref), wout_ref[WIDTH_A:, :], preferred_element_type=F32)
    x1 = _layer_norm(DEEPNORM_ALPHA * x_ref[...] + y, lng_ref[...], lnb_ref[...])
    x1_ref[...] = x1
    qm_ref[...] = jnp.dot(x1.astype(BF16), wq_ref[...], preferred_element_type=F32).astype(BF16)


def _mixer_out(out_a, out_b, x, g_a, g_b, w_out, ln_g, ln_b, w_q):
    n = x.shape[0]
    tm = TM_TOK
    row = lambda width: pl.BlockSpec((tm, width), lambda i: (i, 0))
    full = lambda r, c: pl.BlockSpec((r, c), lambda i: (0, 0))
    return pl.pallas_call(
        _mixer_out_kernel,
        grid=(n // tm,),
        in_specs=[row(WIDTH_A), row(WIDTH_B), row(D_MODEL), full(1, WIDTH_A), full(1, WIDTH_B),
                  full(MIX_WIDTH, D_MODEL), full(1, D_MODEL), full(1, D_MODEL), full(D_MODEL, D_MODEL)],
        out_specs=[row(D_MODEL), row(D_MODEL)],
        out_shape=[jax.ShapeDtypeStruct((n, D_MODEL), F32), jax.ShapeDtypeStruct((n, D_MODEL), BF16)],
        compiler_params=_params(1),
        name="mixer_out",
    )(out_a, out_b, x, g_a, g_b, w_out, ln_g, ln_b, w_q)


def _mem_attn_kernel(q_ref, kv_ref, x1_ref, wo_ref, lng_ref, lnb_ref, wr_ref, br_ref,
                     x2_ref, idx_ref, gate_ref):
    heads = []
    for h in range(N_HEADS_MEM):
        sl = slice(h * HEAD_DIM_MEM, (h + 1) * HEAD_DIM_MEM)
        q = q_ref[:, sl] * (HEAD_DIM_MEM ** -0.5)
        k = kv_ref[:, sl]
        v = kv_ref[:, D_MODEL + h * HEAD_DIM_MEM:D_MODEL + (h + 1) * HEAD_DIM_MEM]
        s = lax.dot_general(q, k, _NT, preferred_element_type=F32)
        m = jnp.max(s, axis=-1, keepdims=True)
        p = jnp.exp(s - m)
        l = jnp.sum(p, axis=-1, keepdims=True)
        heads.append((jnp.dot(p.astype(BF16), v, preferred_element_type=F32) / l).astype(BF16))
    o = jnp.concatenate(heads, axis=-1)
    y = jnp.dot(o, wo_ref[...], preferred_element_type=F32)
    x2 = _layer_norm(DEEPNORM_ALPHA * x1_ref[...] + y, lng_ref[...], lnb_ref[...])
    _store_token_major(x2_ref, x2)

    logits = jnp.dot(x2.astype(BF16), wr_ref[...], preferred_element_type=F32) + br_ref[...]
    tm = logits.shape[0]
    e_iota = lax.broadcasted_iota(jnp.int32, (tm, N_EXPERTS), 1)
    lane = lax.broadcasted_iota(jnp.int32, (tm, LANES), 1)
    idx_out = jnp.zeros((tm, LANES), jnp.int32)
    val_out = jnp.zeros((tm, LANES), F32)
    top = None
    denom = jnp.zeros((tm, 1), F32)
    for kk in range(TOP_K):
        m = jnp.max(logits, axis=-1, keepdims=True)
        sel = jnp.min(jnp.where(logits == m, e_iota, N_EXPERTS), axis=-1, keepdims=True)
        if top is None:
            top = m
        e = jnp.exp(m - top)
        denom = denom + e
        idx_out = jnp.where(lane == kk, sel, idx_out)
        val_out = jnp.where(lane == kk, e, val_out)
        logits = jnp.where(e_iota == sel, -jnp.inf, logits)
    idx_ref[...] = idx_out
    gate_ref[...] = val_out / denom


def _mem_attn(qm, kv, x1, w_o, ln_g, ln_b, w_r, b_r, b, s):
    tm = TM_TOK
    nt = s // tm
    mem_len = kv.shape[1]
    row = lambda width: pl.BlockSpec((tm, width), lambda bi, i: (bi * nt + i, 0))
    full = lambda r, c: pl.BlockSpec((r, c), lambda bi, i: (0, 0))
    n = b * s
    return pl.pallas_call(
        _mem_attn_kernel,
        grid=(b, nt),
        in_specs=[row(D_MODEL), pl.BlockSpec((None, mem_len, 2 * D_MODEL), lambda bi, i: (bi, 0, 0)),
                  row(D_MODEL), full(D_MODEL, D_MODEL), full(1, D_MODEL), full(1, D_MODEL),
                  full(D_MODEL, N_EXPERTS), full(1, N_EXPERTS)],
        out_specs=[pl.BlockSpec((tm * LANE_BLOCKS, LANES), lambda bi, i: (bi * nt + i, 0)), row(LANES), row(LANES)],
        out_shape=[jax.ShapeDtypeStruct((n * LANE_BLOCKS, LANES), F32),
                   jax.ShapeDtypeStruct((n, LANES), jnp.int32),
                   jax.ShapeDtypeStruct((n, LANES), F32)],
        compiler_params=_params(2),
        name="mem_attn",
    )(qm, kv, x1, w_o, ln_g, ln_b, w_r, b_r)


def _gather_rows(idx_ref, first, count, src_hbm, dst_ref, sem):
    for r in range(count):
        src = pl.ds(pl.multiple_of(idx_ref[first + r], LANE_BLOCKS), LANE_BLOCKS)
        pltpu.make_async_copy(src_hbm.at[src], dst_ref.at[pl.ds(r * LANE_BLOCKS, LANE_BLOCKS)], sem).start(
            priority=r % 2)


def _wait_rows(count, src_hbm, dst_ref, sem):
    pltpu.make_async_copy(src_hbm.at[pl.ds(0, count * LANE_BLOCKS)], dst_ref, sem).wait()


def _moe_kernel(bexp_ref, nused_ref, tok_ref, x_hbm, wgu_ref, bgu_ref, wd_ref, bd_ref,
                y_ref, xbuf, sem):
    i = pl.program_id(0)
    slot = i % MOE_SLOTS
    ahead = MOE_SLOTS - 1
    n_used = nused_ref[0]

    @pl.when(i == 0)
    def _():
        for b in range(ahead):
            _gather_rows(tok_ref, b * TM_MOE, TM_MOE, x_hbm, xbuf.at[b], sem.at[b])

    @pl.when(i < n_used + ahead)
    def _():
        _wait_rows(TM_MOE, x_hbm, xbuf.at[slot], sem.at[slot])

    @pl.when(i < n_used)
    def _():
        x = _load_token_major(xbuf.at[slot], 0, TM_MOE).astype(BF16)
        nxt = (i + ahead) % MOE_SLOTS
        _gather_rows(tok_ref, (i + ahead) * TM_MOE, TM_MOE, x_hbm, xbuf.at[nxt], sem.at[nxt])
        gu = jnp.dot(x, wgu_ref[0], preferred_element_type=F32) + bgu_ref[0]
        gate = jnp.minimum(gu[:, :D_FF], SWIGLU_LIMIT)
        up = jnp.clip(gu[:, D_FF:], -SWIGLU_LIMIT, SWIGLU_LIMIT)
        glu = gate * jax.nn.sigmoid(gate * SWIGLU_ALPHA)
        hid = ((up + 1.0) * glu).astype(BF16)
        _store_token_major(y_ref, jnp.dot(hid, wd_ref[0], preferred_element_type=F32) + bd_ref[0])

    @pl.when(i >= n_used)
    def _():
        y_ref[...] = jnp.zeros_like(y_ref)


def _moe_experts(block_expert, n_used, row_token, x2, w_gu, b_gu, w_d, b_d):
    nb = block_expert.shape[0]
    grid_spec = pltpu.PrefetchScalarGridSpec(
        num_scalar_prefetch=3,
        grid=(nb,),
        in_specs=[pl.BlockSpec(memory_space=pl.ANY),
                  pl.BlockSpec((1, D_MODEL, 2 * D_FF), lambda i, be, nu, tok: (be[i], 0, 0)),
                  pl.BlockSpec((1, 1, 2 * D_FF), lambda i, be, nu, tok: (be[i], 0, 0)),
                  pl.BlockSpec((1, D_FF, D_MODEL), lambda i, be, nu, tok: (be[i], 0, 0)),
                  pl.BlockSpec((1, 1, D_MODEL), lambda i, be, nu, tok: (be[i], 0, 0))],
        out_specs=pl.BlockSpec((TM_MOE * LANE_BLOCKS, LANES), lambda i, be, nu, tok: (i, 0)),
        scratch_shapes=[pltpu.VMEM((MOE_SLOTS, TM_MOE * LANE_BLOCKS, LANES), F32),
                        pltpu.SemaphoreType.DMA((MOE_SLOTS,))],
    )
    return pl.pallas_call(
        _moe_kernel,
        grid_spec=grid_spec,
        out_shape=jax.ShapeDtypeStruct((nb * TM_MOE * LANE_BLOCKS, LANES), F32),
        compiler_params=_params(1),
        name="moe_experts",
    )(block_expert, n_used, row_token * LANE_BLOCKS, x2, w_gu, b_gu, w_d, b_d)


def _combine_kernel(idx_ref, y_hbm, x2_ref, gate_ref, lng_ref, lnb_ref, o_ref, ybuf, sem):
    i = pl.program_id(0)
    nt = pl.num_programs(0)
    slot = i % COMB_SLOTS
    ahead = COMB_SLOTS - 1
    rows = TOP_K * TM_COMB

    @pl.when(i == 0)
    def _():
        for t in range(ahead):
            _gather_rows(idx_ref, t * rows, rows, y_hbm, ybuf.at[t], sem.at[t])

    _wait_rows(rows, y_hbm, ybuf.at[slot], sem.at[slot])

    def compute():
        g = gate_ref[...]
        ff = jnp.zeros((TM_COMB, D_MODEL), F32)
        for kk in range(TOP_K):
            ff = ff + g[:, kk:kk + 1] * _load_token_major(ybuf.at[slot], kk * TM_COMB, TM_COMB)
        x2 = _load_token_major(x2_ref, 0, TM_COMB)
        o_ref[...] = _layer_norm(DEEPNORM_ALPHA * x2 + ff, lng_ref[...], lnb_ref[...])

    @pl.when(i + ahead < nt)
    def _():
        nxt = (i + ahead) % COMB_SLOTS
        _gather_rows(idx_ref, (i + ahead) * rows, rows, y_hbm, ybuf.at[nxt], sem.at[nxt])
        compute()

    @pl.when(i + ahead >= nt)
    def _():
        compute()


def _combine(dest, y_rows, x2, gates, ln_g, ln_b):
    n = x2.shape[0] // LANE_BLOCKS
    nt = n // TM_COMB
    rows = TOP_K * TM_COMB
    idx = (dest * LANE_BLOCKS).reshape(nt, TM_COMB, TOP_K).transpose(0, 2, 1).reshape(nt * rows)
    row = lambda width: pl.BlockSpec((TM_COMB, width), lambda i, idx: (i, 0))
    full = lambda r, c: pl.BlockSpec((r, c), lambda i, idx: (0, 0))
    grid_spec = pltpu.PrefetchScalarGridSpec(
        num_scalar_prefetch=1,
        grid=(nt,),
        in_specs=[pl.BlockSpec(memory_space=pl.ANY),
                  pl.BlockSpec((TM_COMB * LANE_BLOCKS, LANES), lambda i, idx: (i, 0)), row(LANES),
                  full(1, D_MODEL), full(1, D_MODEL)],
        out_specs=row(D_MODEL),
        scratch_shapes=[pltpu.VMEM((COMB_SLOTS, rows * LANE_BLOCKS, LANES), F32),
                        pltpu.SemaphoreType.DMA((COMB_SLOTS,))],
    )
    return pl.pallas_call(
        _combine_kernel,
        grid_spec=grid_spec,
        out_shape=jax.ShapeDtypeStruct((n, D_MODEL), F32),
        compiler_params=_params(1),
        name="combine",
    )(idx, y_rows, x2, gates, ln_g, ln_b)


def _rank_kernel(idx_ref, dest_ref, counts_ref, running):
    p = pl.program_id(0)
    j = pl.program_id(1)
    t = TM_RANK
    idx = idx_ref[...]
    e_iota = lax.broadcasted_iota(jnp.int32, (t, LANES), 1)
    hot = [idx[:, k:k + 1] == e_iota for k in range(TOP_K)]
    chosen = sum(jnp.where(h, 1.0, 0.0) for h in hot)
    tile_counts = jnp.sum(chosen, axis=0, keepdims=True)

    @pl.when((p == 0) & (j == 0))
    def _():
        running[...] = jnp.zeros_like(running)

    @pl.when(p == 0)
    def _():
        running[...] += tile_counts
        dest_ref[...] = jnp.zeros_like(dest_ref)

    @pl.when((p == 1) & (j == 0))
    def _():
        counts = running[...]
        counts_ref[...] = counts
        padded = jnp.ceil(counts * (1.0 / TM_MOE)) * TM_MOE
        lane = lax.broadcasted_iota(jnp.int32, (1, LANES), 1)
        scan = padded
        shift = 1
        while shift < N_EXPERTS:
            scan = scan + jnp.where(lane >= shift, pltpu.roll(scan, shift, 1), 0.0)
            shift *= 2
        running[...] = scan - padded

    @pl.when(p == 1)
    def _():
        r_iota = lax.broadcasted_iota(jnp.int32, (t, t), 0)
        c_iota = lax.broadcasted_iota(jnp.int32, (t, t), 1)
        earlier = jnp.where(c_iota < r_iota, 1.0, 0.0).astype(BF16)
        base = jnp.dot(earlier, chosen.astype(BF16), preferred_element_type=F32) + running[...]
        out = jnp.zeros((t, LANES), jnp.int32)
        for k in range(TOP_K):
            row = jnp.sum(jnp.where(hot[k], base, 0.0), axis=-1, keepdims=True).astype(jnp.int32)
            out = jnp.where(e_iota == k, row, out)
        dest_ref[...] = out
        running[...] += tile_counts


def _row_token_kernel(dest_ref, gap_ref, tok_ref):
    unroll = 8

    def clear(row, carry):
        tok_ref[row] = 0
        return carry

    def place(c, carry):
        rows = [dest_ref[c * unroll + u] for u in range(unroll)]
        for u in range(unroll):
            tok_ref[rows[u]] = c * (unroll // TOP_K) + u // TOP_K
        return carry

    for g in range(gap_ref.shape[1]):
        lax.fori_loop(gap_ref[0, g], gap_ref[1, g], clear, 0)
    lax.fori_loop(0, dest_ref.shape[0] // unroll, place, 0)


def _dispatch_plan(top_idx, n):
    n_assign = n * TOP_K
    nb = n_assign // TM_MOE + N_EXPERTS + MOE_SLOTS - 2
    dest, counts = pl.pallas_call(
        _rank_kernel,
        grid=(2, n // TM_RANK),
        in_specs=[pl.BlockSpec((TM_RANK, LANES), lambda p, j: (j, 0))],
        out_specs=[pl.BlockSpec((TM_RANK, LANES), lambda p, j: (p * j, 0)),
                   pl.BlockSpec((1, LANES), lambda p, j: (0, 0))],
        out_shape=[jax.ShapeDtypeStruct((n, LANES), jnp.int32), jax.ShapeDtypeStruct((1, LANES), F32)],
        scratch_shapes=[pltpu.VMEM((1, LANES), F32)],
        compiler_params=_params(2),
        name="expert_rank",
    )(top_idx)
    dest = dest[:, :TOP_K].reshape(-1)
    counts = counts[0, :N_EXPERTS].astype(jnp.int32)
    padded = (counts + TM_MOE - 1) // TM_MOE * TM_MOE
    pad_end = jnp.cumsum(padded)
    tail = jnp.stack([pad_end[-1], jnp.int32(nb * TM_MOE)])[:, None]
    gaps = jnp.concatenate([jnp.stack([pad_end - padded + counts, pad_end]), tail], axis=1)
    row_token = pl.pallas_call(
        _row_token_kernel,
        in_specs=[pl.BlockSpec(memory_space=pltpu.SMEM), pl.BlockSpec(memory_space=pltpu.SMEM)],
        out_specs=pl.BlockSpec(memory_space=pltpu.SMEM),
        out_shape=jax.ShapeDtypeStruct((nb * TM_MOE,), jnp.int32),
        name="row_token",
    )(dest, gaps)
    block_first_row = jnp.arange(nb, dtype=jnp.int32) * TM_MOE
    block_expert = jnp.minimum(jnp.sum(pad_end[None, :] <= block_first_row[:, None], axis=1),
                               N_EXPERTS - 1).astype(jnp.int32)
    n_used = (pad_end[-1:] // TM_MOE).astype(jnp.int32)
    return dest, row_token, block_expert, n_used


def kernel(x, mem, w_in, rel_bias, g_group_a, g_group_b, w_out, w_q_mem, w_kv_mem, w_o_mem, w_router, b_router, w_gate_up, b_gate_up, w_down, b_down, ln_g, ln_b):
    b, s, d = x.shape
    n = b * s
    depth = w_in.shape[0]
    xf = x.reshape(n, d)
    for l in range(depth):
        q_scale = jnp.ones((3 * MIX_WIDTH,), F32).at[:WIDTH_A].set(HEAD_DIM ** -0.5)
        q_scale = q_scale.at[3 * WIDTH_A:3 * WIDTH_A + WIDTH_B].set(HEAD_DIM ** -0.5)
        proj = _matmul(xf, (w_in[l] * q_scale).astype(BF16), BF16, TM_TOK, 3 * MIX_WIDTH).reshape(b, s, 3 * MIX_WIDTH)
        out_a = _chunk_attention(proj, _chunk_bias(rel_bias[l]), b, s).reshape(n, WIDTH_A)
        out_b = _sb_attention(proj, b, s).reshape(n, WIDTH_B)
        x1, qm = _mixer_out(out_a, out_b, xf, g_group_a[l][None], g_group_b[l][None],
                            w_out[l].astype(BF16), ln_g[l, 0][None], ln_b[l, 0][None],
                            w_q_mem[l].astype(BF16))
        mem_len = mem.shape[1]
        kv = _matmul(mem.reshape(b * mem_len, d), w_kv_mem[l].astype(BF16), BF16, b * mem_len, 1024)
        x2, top_idx, gates = _mem_attn(qm, kv.reshape(b, mem_len, 2 * d), x1, w_o_mem[l].astype(BF16),
                                       ln_g[l, 1][None], ln_b[l, 1][None],
                                       w_router[l].astype(BF16), b_router[l][None], b, s)
        dest, row_token, block_expert, n_used = _dispatch_plan(top_idx, n)
        y_rows = _moe_experts(block_expert, n_used, row_token, x2,
                              w_gate_up[l].astype(BF16), b_gate_up[l][:, None, :],
                              w_down[l].astype(BF16), b_down[l][:, None, :])
        xf = _combine(dest, y_rows, x2, gates, ln_g[l, 2][None], ln_b[l, 2][None])
    return xf.reshape(b, s, d)
```

```python
import functools

import jax
import jax.numpy as jnp
from jax import lax
from jax.experimental import pallas as pl
from jax.experimental.pallas import tpu as pltpu

D_MODEL = 1024
CHUNK = 64
LEFT_CHUNKS = 8
LEFT = LEFT_CHUNKS * CHUNK
HEAD_DIM = 64
N_HEADS_A = 8
N_HEADS_B = 8
WIDTH_A = N_HEADS_A * HEAD_DIM
WIDTH_B = N_HEADS_B * HEAD_DIM
MIX_WIDTH = WIDTH_A + WIDTH_B
REL_CLIP = 128
N_HEADS_MEM = 4
HEAD_DIM_MEM = D_MODEL // N_HEADS_MEM
N_EXPERTS = 32
TOP_K = 4
D_FF = D_MODEL
SWIGLU_LIMIT = 7.0
SWIGLU_ALPHA = 1.702
LN_EPS = 1e-5
RMS_EPS = 1e-6
DEEPNORM_ALPHA = 2.0 ** 0.25
NEG_INF = -1e30
LOG2E = 1.4426950408889634

LANES = 128
LANE_BLOCKS = D_MODEL // LANES
VMEM_LIMIT = 48 * 1024 * 1024
VMEM_LIMIT_MOE = 56 * 1024 * 1024

TQ_A = 512
SUB_A = 256
TB_SB = 256
SB_UNDERFLOW = 110.0
TM_TOK = 512
TM_MOE = 256
MOE_SLOTS = 3
TM_COMB = 256
COMB_SLOTS = 3
TM_RANK = 512

F32 = jnp.float32
BF16 = jnp.bfloat16
_NT = (((1,), (1,)), ((), ()))


def _params(n_axes):
    return pltpu.CompilerParams(dimension_semantics=("arbitrary",) * n_axes,
                                vmem_limit_bytes=VMEM_LIMIT)


def _store_token_major(ref, value):
    rows = value.shape[0]
    for c in range(LANE_BLOCKS):
        ref[pl.ds(c, rows, stride=LANE_BLOCKS), :] = value[:, c * LANES:(c + 1) * LANES]


def _load_token_major(ref, first_row, rows):
    return jnp.concatenate(
        [ref[pl.ds(first_row * LANE_BLOCKS + c, rows, stride=LANE_BLOCKS), :] for c in range(LANE_BLOCKS)],
        axis=1)


def _layer_norm(r, g, b):
    mu = jnp.mean(r, axis=-1, keepdims=True)
    d = r - mu
    var = jnp.mean(d * d, axis=-1, keepdims=True)
    return d * lax.rsqrt(var + LN_EPS) * g + b


def _matmul_kernel(x_ref, w_ref, o_ref):
    o_ref[...] = jnp.dot(x_ref[...].astype(BF16), w_ref[...],
                         preferred_element_type=F32).astype(o_ref.dtype)


def _matmul(x, w, out_dtype, tm, tn):
    m, k = x.shape
    n = w.shape[1]
    return pl.pallas_call(
        _matmul_kernel,
        grid=(m // tm, n // tn),
        in_specs=[pl.BlockSpec((tm, k), lambda i, j: (i, 0)),
                  pl.BlockSpec((k, tn), lambda i, j: (0, j))],
        out_specs=pl.BlockSpec((tm, tn), lambda i, j: (i, j)),
        out_shape=jax.ShapeDtypeStruct((m, n), out_dtype),
        compiler_params=_params(2),
        name="matmul",
    )(x, w)


def _chunk_attn_kernel(q_ref, kp_ref, kc_ref, vp_ref, vc_ref, bias0_ref, bias1_ref, o_ref):
    w = SUB_A + LEFT
    for h in range(2):
        sl = slice(h * HEAD_DIM, (h + 1) * HEAD_DIM)
        k = jnp.concatenate([kp_ref[:, sl], kc_ref[:, sl]], axis=0)
        v = jnp.concatenate([vp_ref[:, sl], vc_ref[:, sl]], axis=0)
        for sub, bias_ref in enumerate((bias0_ref, bias1_ref)):
            rows = slice(sub * SUB_A, (sub + 1) * SUB_A)
            keys = slice(sub * SUB_A, sub * SUB_A + w)
            s = lax.dot_general(q_ref[rows, sl], k[keys], _NT, preferred_element_type=F32) + bias_ref[h]
            m = jnp.max(s, axis=-1, keepdims=True)
            p = jnp.exp(s - m)
            l = jnp.sum(p, axis=-1, keepdims=True)
            o = jnp.dot(p.astype(BF16), v[keys], preferred_element_type=F32) / l
            o_ref[rows, sl] = o.astype(o_ref.dtype)


def _chunk_bias(rel_bias):
    w = SUB_A + LEFT
    heads = rel_bias.shape[0]
    m = jnp.arange(2 * w)
    rel = jnp.where(m < w, LEFT - m, LEFT + 2 * w - m)
    diag = rel_bias[:, jnp.clip(rel, -REL_CLIP, REL_CLIP) + REL_CLIP].astype(F32)[:, None, :]
    n_tables = 1 + TQ_A // SUB_A
    return pl.pallas_call(
        _chunk_bias_kernel,
        grid=(heads,),
        in_specs=[pl.BlockSpec((1, 1, 2 * w), lambda h: (h, 0, 0))],
        out_specs=pl.BlockSpec((n_tables, 1, SUB_A, w), lambda h: (0, h, 0, 0)),
        out_shape=jax.ShapeDtypeStruct((n_tables, heads, SUB_A, w), F32),
        compiler_params=_params(1),
        name="chunk_bias",
    )(diag)


def _chunk_bias_kernel(diag_ref, o_ref):
    w = SUB_A + LEFT
    rolled = pltpu.roll(jnp.broadcast_to(diag_ref[0], (SUB_A, 2 * w)), 0, 1, stride=1, stride_axis=0)
    col = lax.broadcasted_iota(jnp.int32, (SUB_A, w), 1)
    qc = lax.broadcasted_iota(jnp.int32, (SUB_A, w), 0) // CHUNK
    kc = col // CHUNK
    band = (kc >= qc) & (kc <= qc + LEFT_CHUNKS)
    table = jnp.where(band, rolled[:, :w], NEG_INF)
    o_ref[0, 0] = table
    for sub in range(TQ_A // SUB_A):
        o_ref[1 + sub, 0] = jnp.where(col >= LEFT - sub * SUB_A, table, NEG_INF)


def _chunk_attention(proj, bias, b, s):
    nq = s // TQ_A
    pairs = WIDTH_A // LANES
    blk = lambda off, prev: pl.BlockSpec(
        (None, TQ_A, LANES),
        (lambda bi, p, i: (bi, jnp.maximum(i - 1, 0), off + p)) if prev
        else (lambda bi, p, i: (bi, i, off + p)))
    table = lambda sub: pl.BlockSpec((None, 2, SUB_A, SUB_A + LEFT),
                                     lambda bi, p, i: (jnp.where(i == 0, 1 + sub, 0), p, 0, 0))
    return pl.pallas_call(
        _chunk_attn_kernel,
        grid=(b, pairs, nq),
        in_specs=[blk(0, False),
                  blk(pairs, True), blk(pairs, False),
                  blk(2 * pairs, True), blk(2 * pairs, False),
                  table(0), table(1)],
        out_specs=pl.BlockSpec((None, TQ_A, LANES), lambda bi, p, i: (bi, i, p)),
        out_shape=jax.ShapeDtypeStruct((b, s, WIDTH_A), BF16),
        compiler_params=_params(3),
        name="chunk_attn",
    )(proj, proj, proj, proj, proj, bias, bias)


def _sb_block(q, k_ref, v_ref, sl, j, later, tri, causal):
    t = TB_SB
    start = pl.multiple_of(j * t, t)
    k = k_ref[pl.ds(start, t), sl]
    v = v_ref[pl.ds(start, t), sl]
    z = lax.dot_general(q, k, _NT, preferred_element_type=F32)
    sp = jnp.maximum(z, 0.0) + jnp.log(1.0 + jnp.exp2(jnp.abs(z) * (-LOG2E)))
    if causal is not None:
        sp = jnp.where(causal, sp, 0.0)
    csum = jnp.dot(sp.astype(BF16), tri, preferred_element_type=F32) + later
    wgt = jnp.exp(z - csum)
    if causal is not None:
        wgt = jnp.where(causal, wgt, 0.0)
    pv = jnp.dot(wgt.astype(BF16), v, preferred_element_type=F32)
    return pv, csum[:, 0:1]


def _sb_attn_kernel(q_ref, k_ref, v_ref, o_ref):
    i = pl.program_id(2)
    t = TB_SB
    row = lax.broadcasted_iota(jnp.int32, (t, t), 0)
    col = lax.broadcasted_iota(jnp.int32, (t, t), 1)
    tri = jnp.where(row >= col, 1.0, 0.0).astype(BF16)
    causal = col < row
    sls = [slice(h * HEAD_DIM, (h + 1) * HEAD_DIM) for h in range(2)]
    qs = [q_ref[:, sl] for sl in sls]
    block = functools.partial(_sb_block, tri=tri)

    has_prev = i >= 1
    carry = []
    for h in range(2):
        pv, later = block(qs[h], k_ref, v_ref, sls[h], i, jnp.zeros((t, 1), F32), causal=causal)
        pv_prev, later_prev = block(qs[h], k_ref, v_ref, sls[h], jnp.maximum(i - 1, 0), later, causal=None)
        carry += [jnp.where(has_prev, later_prev, later), pv + jnp.where(has_prev, pv_prev, 0.0)]

    def pending(c):
        return (c[0] >= 0) & (jnp.minimum(jnp.min(c[1]), jnp.min(c[3])) < SB_UNDERFLOW)

    def step(c):
        out = [c[0] - 1]
        for h in range(2):
            pv, later = block(qs[h], k_ref, v_ref, sls[h], c[0], c[1 + 2 * h], causal=None)
            out += [later, c[2 + 2 * h] + pv]
        return tuple(out)

    carry = lax.while_loop(pending, step, (i - 2, *carry))
    for h in range(2):
        o_ref[:, sls[h]] = carry[2 + 2 * h].astype(o_ref.dtype)


def _sb_attention(proj, b, s):
    nq = s // TB_SB
    pairs = WIDTH_B // LANES
    base = 3 * WIDTH_A // LANES
    return pl.pallas_call(
        _sb_attn_kernel,
        grid=(b, pairs, nq),
        in_specs=[pl.BlockSpec((None, TB_SB, LANES), lambda bi, p, i: (bi, i, base + p)),
                  pl.BlockSpec((None, s, LANES), lambda bi, p, i: (bi, 0, base + pairs + p)),
                  pl.BlockSpec((None, s, LANES), lambda bi, p, i: (bi, 0, base + 2 * pairs + p))],
        out_specs=pl.BlockSpec((None, TB_SB, LANES), lambda bi, p, i: (bi, i, p)),
        out_shape=jax.ShapeDtypeStruct((b, s, WIDTH_B), BF16),
        compiler_params=_params(3),
        name="sb_attn",
    )(proj, proj, proj)


def _mixer_out_kernel(oa_ref, ob_ref, x_ref, ga_ref, gb_ref, wout_ref, lng_ref, lnb_ref, wq_ref,
                      x1_ref, qm_ref):
    def rms(ref, g_ref):
        a = ref[...].astype(F32)
        return (a * lax.rsqrt(jnp.mean(a * a, axis=-1, keepdims=True) + RMS_EPS) * g_ref[...]).astype(BF16)

    y = jnp.dot(rms(oa_ref, ga_ref), wout_ref[:WIDTH_A, :], preferred_element_type=F32)
    y = y + jnp.dot(rms(ob_ref, gb_ref), wout_ref[WIDTH_A:, :], preferred_element_type=F32)
    x1 = _layer_norm(DEEPNORM_ALPHA * x_ref[...] + y, lng_ref[...], lnb_ref[...])
    x1_ref[...] = x1
    qm_ref[...] = jnp.dot(x1.astype(BF16), wq_ref[...], preferred_element_type=F32).astype(BF16)


def _mixer_out(out_a, out_b, x, g_a, g_b, w_out, ln_g, ln_b, w_q):
    n = x.shape[0]
    tm = TM_TOK
    row = lambda width: pl.BlockSpec((tm, width), lambda i: (i, 0))
    full = lambda r, c: pl.BlockSpec((r, c), lambda i: (0, 0))
    return pl.pallas_call(
        _mixer_out_kernel,
        grid=(n // tm,),
        in_specs=[row(WIDTH_A), row(WIDTH_B), row(D_MODEL), full(1, WIDTH_A), full(1, WIDTH_B),
                  full(MIX_WIDTH, D_MODEL), full(1, D_MODEL), full(1, D_MODEL), full(D_MODEL, D_MODEL)],
        out_specs=[row(D_MODEL), row(D_MODEL)],
        out_shape=[jax.ShapeDtypeStruct((n, D_MODEL), F32), jax.ShapeDtypeStruct((n, D_MODEL), BF16)],
        compiler_params=_params(1),
        name="mixer_out",
    )(out_a, out_b, x, g_a, g_b, w_out, ln_g, ln_b, w_q)


def _mem_attn_kernel(q_ref, kv_ref, x1_ref, wo_ref, lng_ref, lnb_ref, wr_ref, br_ref,
                     x2_ref, idx_ref, gate_ref):
    heads = []
    for h in range(N_HEADS_MEM):
        sl = slice(h * HEAD_DIM_MEM, (h + 1) * HEAD_DIM_MEM)
        q = q_ref[:, sl] * (HEAD_DIM_MEM ** -0.5)
        k = kv_ref[:, sl]
        v = kv_ref[:, D_MODEL + h * HEAD_DIM_MEM:D_MODEL + (h + 1) * HEAD_DIM_MEM]
        s = lax.dot_general(q, k, _NT, preferred_element_type=F32)
        m = jnp.max(s, axis=-1, keepdims=True)
        p = jnp.exp(s - m)
        l = jnp.sum(p, axis=-1, keepdims=True)
        heads.append((jnp.dot(p.astype(BF16), v, preferred_element_type=F32) / l).astype(BF16))
    o = jnp.concatenate(heads, axis=-1)
    y = jnp.dot(o, wo_ref[...], preferred_element_type=F32)
    x2 = _layer_norm(DEEPNORM_ALPHA * x1_ref[...] + y, lng_ref[...], lnb_ref[...])
    _store_token_major(x2_ref, x2)

    logits = jnp.dot(x2.astype(BF16), wr_ref[...], preferred_element_type=F32) + br_ref[...]
    tm = logits.shape[0]
    e_iota = lax.broadcasted_iota(jnp.int32, (tm, N_EXPERTS), 1)
    lane = lax.broadcasted_iota(jnp.int32, (tm, LANES), 1)
    idx_out = jnp.zeros((tm, LANES), jnp.int32)
    val_out = jnp.zeros((tm, LANES), F32)
    top = None
    denom = jnp.zeros((tm, 1), F32)
    for kk in range(TOP_K):
        m = jnp.max(logits, axis=-1, keepdims=True)
        sel = jnp.min(jnp.where(logits == m, e_iota, N_EXPERTS), axis=-1, keepdims=True)
        if top is None:
            top = m
        e = jnp.exp(m - top)
        denom = denom + e
        idx_out = jnp.where(lane == kk, sel, idx_out)
        val_out = jnp.where(lane == kk, e, val_out)
        logits = jnp.where(e_iota == sel, -jnp.inf, logits)
    idx_ref[...] = idx_out
    gate_ref[...] = val_out / denom


def _mem_attn(qm, kv, x1, w_o, ln_g, ln_b, w_r, b_r, b, s):
    tm = TM_TOK
    nt = s // tm
    mem_len = kv.shape[1]
    row = lambda width: pl.BlockSpec((tm, width), lambda bi, i: (bi * nt + i, 0))
    full = lambda r, c: pl.BlockSpec((r, c), lambda bi, i: (0, 0))
    n = b * s
    return pl.pallas_call(
        _mem_attn_kernel,
        grid=(b, nt),
        in_specs=[row(D_MODEL), pl.BlockSpec((None, mem_len, 2 * D_MODEL), lambda bi, i: (bi, 0, 0)),
                  row(D_MODEL), full(D_MODEL, D_MODEL), full(1, D_MODEL), full(1, D_MODEL),
                  full(D_MODEL, N_EXPERTS), full(1, N_EXPERTS)],
        out_specs=[pl.BlockSpec((tm * LANE_BLOCKS, LANES), lambda bi, i: (bi * nt + i, 0)), row(LANES), row(LANES)],
        out_shape=[jax.ShapeDtypeStruct((n * LANE_BLOCKS, LANES), F32),
                   jax.ShapeDtypeStruct((n, LANES), jnp.int32),
                   jax.ShapeDtypeStruct((n, LANES), F32)],
        compiler_params=_params(2),
        name="mem_attn",
    )(qm, kv, x1, w_o, ln_g, ln_b, w_r, b_r)


def _gather_rows(idx_ref, first, count, src_hbm, dst_ref, sem):
    for r in range(count):
        src = pl.ds(pl.multiple_of(idx_ref[first + r], LANE_BLOCKS), LANE_BLOCKS)
        pltpu.make_async_copy(src_hbm.at[src], dst_ref.at[pl.ds(r * LANE_BLOCKS, LANE_BLOCKS)], sem).start(
            priority=r % 2)


def _wait_rows(count, src_hbm, dst_ref, sem):
    pltpu.make_async_copy(src_hbm.at[pl.ds(0, count * LANE_BLOCKS)], dst_ref, sem).wait()


def _moe_kernel(bexp_ref, nused_ref, tok_ref, x_hbm, wgu_ref, bgu_ref, wd_ref, bd_ref,
                y_ref, xbuf, wgu_bf, wd_bf, sem):
    i = pl.program_id(0)
    slot = i % MOE_SLOTS
    ahead = MOE_SLOTS - 1
    n_used = nused_ref[0]

    @pl.when(i == 0)
    def _():
        for b in range(ahead):
            _gather_rows(tok_ref, b * TM_MOE, TM_MOE, x_hbm, xbuf.at[b], sem.at[b])

    @pl.when(i < n_used + ahead)
    def _():
        _wait_rows(TM_MOE, x_hbm, xbuf.at[slot], sem.at[slot])

    @pl.when((i < n_used) & ((i == 0) | (bexp_ref[i] != bexp_ref[jnp.maximum(i - 1, 0)])))
    def _():
        wgu_bf[...] = wgu_ref[0].astype(BF16)
        wd_bf[...] = wd_ref[0].astype(BF16)

    @pl.when(i < n_used)
    def _():
        x = _load_token_major(xbuf.at[slot], 0, TM_MOE).astype(BF16)
        nxt = (i + ahead) % MOE_SLOTS
        _gather_rows(tok_ref, (i + ahead) * TM_MOE, TM_MOE, x_hbm, xbuf.at[nxt], sem.at[nxt])
        gu = jnp.dot(x, wgu_bf[...], preferred_element_type=F32) + bgu_ref[0]
        gate = jnp.minimum(gu[:, :D_FF], SWIGLU_LIMIT)
        up = jnp.clip(gu[:, D_FF:], -SWIGLU_LIMIT, SWIGLU_LIMIT)
        glu = gate * jax.nn.sigmoid(gate * SWIGLU_ALPHA)
        hid = ((up + 1.0) * glu).astype(BF16)
        _store_token_major(y_ref, jnp.dot(hid, wd_bf[...], preferred_element_type=F32) + bd_ref[0])

    @pl.when(i >= n_used)
    def _():
        y_ref[...] = jnp.zeros_like(y_ref)


def _moe_experts(block_expert, n_used, row_token, x2, w_gu, b_gu, w_d, b_d):
    nb = block_expert.shape[0]
    grid_spec = pltpu.PrefetchScalarGridSpec(
        num_scalar_prefetch=3,
        grid=(nb,),
        in_specs=[pl.BlockSpec(memory_space=pl.ANY),
                  pl.BlockSpec((1, D_MODEL, 2 * D_FF), lambda i, be, nu, tok: (be[i], 0, 0)),
                  pl.BlockSpec((1, 1, 2 * D_FF), lambda i, be, nu, tok: (be[i], 0, 0)),
                  pl.BlockSpec((1, D_FF, D_MODEL), lambda i, be, nu, tok: (be[i], 0, 0)),
                  pl.BlockSpec((1, 1, D_MODEL), lambda i, be, nu, tok: (be[i], 0, 0))],
        out_specs=pl.BlockSpec((TM_MOE * LANE_BLOCKS, LANES), lambda i, be, nu, tok: (i, 0)),
        scratch_shapes=[pltpu.VMEM((MOE_SLOTS, TM_MOE * LANE_BLOCKS, LANES), F32),
                        pltpu.VMEM((D_MODEL, 2 * D_FF), BF16), pltpu.VMEM((D_FF, D_MODEL), BF16),
                        pltpu.SemaphoreType.DMA((MOE_SLOTS,))],
    )
    return pl.pallas_call(
        _moe_kernel,
        grid_spec=grid_spec,
        out_shape=jax.ShapeDtypeStruct((nb * TM_MOE * LANE_BLOCKS, LANES), F32),
        compiler_params=pltpu.CompilerParams(dimension_semantics=("arbitrary",),
                                             vmem_limit_bytes=VMEM_LIMIT_MOE),
        name="moe_experts",
    )(block_expert, n_used, row_token * LANE_BLOCKS, x2, w_gu, b_gu, w_d, b_d)


def _combine_kernel(idx_ref, y_hbm, x2_ref, gate_ref, lng_ref, lnb_ref, o_ref, ybuf, sem):
    i = pl.program_id(0)
    nt = pl.num_programs(0)
    slot = i % COMB_SLOTS
    ahead = COMB_SLOTS - 1
    rows = TOP_K * TM_COMB

    @pl.when(i == 0)
    def _():
        for t in range(ahead):
            _gather_rows(idx_ref, t * rows, rows, y_hbm, ybuf.at[t], sem.at[t])

    _wait_rows(rows, y_hbm, ybuf.at[slot], sem.at[slot])

    def compute():
        g = gate_ref[...]
        ff = jnp.zeros((TM_COMB, D_MODEL), F32)
        for kk in range(TOP_K):
            ff = ff + g[:, kk:kk + 1] * _load_token_major(ybuf.at[slot], kk * TM_COMB, TM_COMB)
        x2 = _load_token_major(x2_ref, 0, TM_COMB)
        o_ref[...] = _layer_norm(DEEPNORM_ALPHA * x2 + ff, lng_ref[...], lnb_ref[...])

    @pl.when(i + ahead < nt)
    def _():
        nxt = (i + ahead) % COMB_SLOTS
        _gather_rows(idx_ref, (i + ahead) * rows, rows, y_hbm, ybuf.at[nxt], sem.at[nxt])
        compute()

    @pl.when(i + ahead >= nt)
    def _():
        compute()


def _combine(dest, y_rows, x2, gates, ln_g, ln_b):
    n = x2.shape[0] // LANE_BLOCKS
    nt = n // TM_COMB
    rows = TOP_K * TM_COMB
    idx = (dest * LANE_BLOCKS).reshape(nt, TM_COMB, TOP_K).transpose(0, 2, 1).reshape(nt * rows)
    row = lambda width: pl.BlockSpec((TM_COMB, width), lambda i, idx: (i, 0))
    full = lambda r, c: pl.BlockSpec((r, c), lambda i, idx: (0, 0))
    grid_spec = pltpu.PrefetchScalarGridSpec(
        num_scalar_prefetch=1,
        grid=(nt,),
        in_specs=[pl.BlockSpec(memory_space=pl.ANY),
                  pl.BlockSpec((TM_COMB * LANE_BLOCKS, LANES), lambda i, idx: (i, 0)), row(LANES),
                  full(1, D_MODEL), full(1, D_MODEL)],
        out_specs=row(D_MODEL),
        scratch_shapes=[pltpu.VMEM((COMB_SLOTS, rows * LANE_BLOCKS, LANES), F32),
                        pltpu.SemaphoreType.DMA((COMB_SLOTS,))],
    )
    return pl.pallas_call(
        _combine_kernel,
        grid_spec=grid_spec,
        out_shape=jax.ShapeDtypeStruct((n, D_MODEL), F32),
        compiler_params=_params(1),
        name="combine",
    )(idx, y_rows, x2, gates, ln_g, ln_b)


def _rank_kernel(idx_ref, dest_ref, counts_ref, running):
    p = pl.program_id(0)
    j = pl.program_id(1)
    t = TM_RANK
    idx = idx_ref[...]
    e_iota = lax.broadcasted_iota(jnp.int32, (t, LANES), 1)
    hot = [idx[:, k:k + 1] == e_iota for k in range(TOP_K)]
    chosen = sum(jnp.where(h, 1.0, 0.0) for h in hot)
    tile_counts = jnp.sum(chosen, axis=0, keepdims=True)

    @pl.when((p == 0) & (j == 0))
    def _():
        running[...] = jnp.zeros_like(running)

    @pl.when(p == 0)
    def _():
        running[...] += tile_counts
        dest_ref[...] = jnp.zeros_like(dest_ref)

    @pl.when((p == 1) & (j == 0))
    def _():
        counts = running[...]
        counts_ref[...] = counts
        padded = jnp.ceil(counts * (1.0 / TM_MOE)) * TM_MOE
        lane = lax.broadcasted_iota(jnp.int32, (1, LANES), 1)
        scan = padded
        shift = 1
        while shift < N_EXPERTS:
            scan = scan + jnp.where(lane >= shift, pltpu.roll(scan, shift, 1), 0.0)
            shift *= 2
        running[...] = scan - padded

    @pl.when(p == 1)
    def _():
        r_iota = lax.broadcasted_iota(jnp.int32, (t, t), 0)
        c_iota = lax.broadcasted_iota(jnp.int32, (t, t), 1)
        earlier = jnp.where(c_iota < r_iota, 1.0, 0.0).astype(BF16)
        base = jnp.dot(earlier, chosen.astype(BF16), preferred_element_type=F32) + running[...]
        out = jnp.zeros((t, LANES), jnp.int32)
        for k in range(TOP_K):
            row = jnp.sum(jnp.where(hot[k], base, 0.0), axis=-1, keepdims=True).astype(jnp.int32)
            out = jnp.where(e_iota == k, row, out)
        dest_ref[...] = out
        running[...] += tile_counts


def _row_token_kernel(dest_ref, gap_ref, tok_ref):
    unroll = 8

    def clear(row, carry):
        tok_ref[row] = 0
        return carry

    def place(c, carry):
        rows = [dest_ref[c * unroll + u] for u in range(unroll)]
        for u in range(unroll):
            tok_ref[rows[u]] = c * (unroll // TOP_K) + u // TOP_K
        return carry

    for g in range(gap_ref.shape[1]):
        lax.fori_loop(gap_ref[0, g], gap_ref[1, g], clear, 0)
    lax.fori_loop(0, dest_ref.shape[0] // unroll, place, 0)


def _dispatch_plan(top_idx, n):
    n_assign = n * TOP_K
    nb = n_assign // TM_MOE + N_EXPERTS + MOE_SLOTS - 2
    dest, counts = pl.pallas_call(
        _rank_kernel,
        grid=(2, n // TM_RANK),
        in_specs=[pl.BlockSpec((TM_RANK, LANES), lambda p, j: (j, 0))],
        out_specs=[pl.BlockSpec((TM_RANK, LANES), lambda p, j: (p * j, 0)),
                   pl.BlockSpec((1, LANES), lambda p, j: (0, 0))],
        out_shape=[jax.ShapeDtypeStruct((n, LANES), jnp.int32), jax.ShapeDtypeStruct((1, LANES), F32)],
        scratch_shapes=[pltpu.VMEM((1, LANES), F32)],
        compiler_params=_params(2),
        name="expert_rank",
    )(top_idx)
    dest = dest[:, :TOP_K].reshape(-1)
    counts = counts[0, :N_EXPERTS].astype(jnp.int32)
    padded = (counts + TM_MOE - 1) // TM_MOE * TM_MOE
    pad_end = jnp.cumsum(padded)
    tail = jnp.stack([pad_end[-1], jnp.int32(nb * TM_MOE)])[:, None]
    gaps = jnp.concatenate([jnp.stack([pad_end - padded + counts, pad_end]), tail], axis=1)
    row_token = pl.pallas_call(
        _row_token_kernel,
        in_specs=[pl.BlockSpec(memory_space=pltpu.SMEM), pl.BlockSpec(memory_space=pltpu.SMEM)],
        out_specs=pl.BlockSpec(memory_space=pltpu.SMEM),
        out_shape=jax.ShapeDtypeStruct((nb * TM_MOE,), jnp.int32),
        name="row_token",
    )(dest, gaps)
    block_first_row = jnp.arange(nb, dtype=jnp.int32) * TM_MOE
    block_expert = jnp.minimum(jnp.sum(pad_end[None, :] <= block_first_row[:, None], axis=1),
                               N_EXPERTS - 1).astype(jnp.int32)
    n_used = (pad_end[-1:] // TM_MOE).astype(jnp.int32)
    return dest, row_token, block_expert, n_used


def kernel(x, mem, w_in, rel_bias, g_group_a, g_group_b, w_out, w_q_mem, w_kv_mem, w_o_mem, w_router, b_router, w_gate_up, b_gate_up, w_down, b_down, ln_g, ln_b):
    b, s, d = x.shape
    n = b * s
    depth = w_in.shape[0]
    xf = x.reshape(n, d)
    for l in range(depth):
        q_scale = jnp.ones((3 * MIX_WIDTH,), F32).at[:WIDTH_A].set(HEAD_DIM ** -0.5)
        q_scale = q_scale.at[3 * WIDTH_A:3 * WIDTH_A + WIDTH_B].set(HEAD_DIM ** -0.5)
        proj = _matmul(xf, (w_in[l] * q_scale).astype(BF16), BF16, TM_TOK, 3 * MIX_WIDTH).reshape(b, s, 3 * MIX_WIDTH)
        out_a = _chunk_attention(proj, _chunk_bias(rel_bias[l]), b, s).reshape(n, WIDTH_A)
        out_b = _sb_attention(proj, b, s).reshape(n, WIDTH_B)
        x1, qm = _mixer_out(out_a, out_b, xf, g_group_a[l][None], g_group_b[l][None],
                            w_out[l].astype(BF16), ln_g[l, 0][None], ln_b[l, 0][None],
                            w_q_mem[l].astype(BF16))
        mem_len = mem.shape[1]
        kv = _matmul(mem.reshape(b * mem_len, d), w_kv_mem[l].astype(BF16), BF16, b * mem_len, 1024)
        x2, top_idx, gates = _mem_attn(qm, kv.reshape(b, mem_len, 2 * d), x1, w_o_mem[l].astype(BF16),
                                       ln_g[l, 1][None], ln_b[l, 1][None],
                                       w_router[l].astype(BF16), b_router[l][None], b, s)
        dest, row_token, block_expert, n_used = _dispatch_plan(top_idx, n)
        y_rows = _moe_experts(block_expert, n_used, row_token, x2,
                              w_gate_up[l], b_gate_up[l][:, None, :], w_down[l], b_down[l][:, None, :])
        xf = _combine(dest, y_rows, x2, gates, ln_g[l, 2][None], ln_b[l, 2][None])
    return xf.reshape(b, s, d)
```

```python
import functools

import jax
import jax.numpy as jnp
from jax import lax
from jax.experimental import pallas as pl
from jax.experimental.pallas import tpu as pltpu

D_MODEL = 1024
CHUNK = 64
LEFT_CHUNKS = 8
LEFT = LEFT_CHUNKS * CHUNK
HEAD_DIM = 64
N_HEADS_A = 8
N_HEADS_B = 8
WIDTH_A = N_HEADS_A * HEAD_DIM
WIDTH_B = N_HEADS_B * HEAD_DIM
MIX_WIDTH = WIDTH_A + WIDTH_B
REL_CLIP = 128
N_HEADS_MEM = 4
HEAD_DIM_MEM = D_MODEL // N_HEADS_MEM
N_EXPERTS = 32
TOP_K = 4
D_FF = D_MODEL
SWIGLU_LIMIT = 7.0
SWIGLU_ALPHA = 1.702
LN_EPS = 1e-5
RMS_EPS = 1e-6
DEEPNORM_ALPHA = 2.0 ** 0.25
NEG_INF = -1e30
LOG2E = 1.4426950408889634

LANES = 128
LANE_BLOCKS = D_MODEL // LANES
VMEM_LIMIT = 48 * 1024 * 1024
VMEM_LIMIT_MOE = 56 * 1024 * 1024

TQ_A = 512
SUB_A = 256
TB_SB = 256
SB_TILES = 2
SB_UNDERFLOW = 110.0
TM_TOK = 512
TM_MOE = 256
MOE_SLOTS = 3
TM_COMB = 256
COMB_SLOTS = 3
TM_RANK = 512

F32 = jnp.float32
BF16 = jnp.bfloat16
_NT = (((1,), (1,)), ((), ()))


def _params(n_axes):
    return pltpu.CompilerParams(dimension_semantics=("arbitrary",) * n_axes,
                                vmem_limit_bytes=VMEM_LIMIT)


def _store_token_major(ref, value):
    rows = value.shape[0]
    for c in range(LANE_BLOCKS):
        ref[pl.ds(c, rows, stride=LANE_BLOCKS), :] = value[:, c * LANES:(c + 1) * LANES]


def _load_token_major(ref, first_row, rows):
    return jnp.concatenate(
        [ref[pl.ds(first_row * LANE_BLOCKS + c, rows, stride=LANE_BLOCKS), :] for c in range(LANE_BLOCKS)],
        axis=1)


def _layer_norm(r, g, b):
    mu = jnp.mean(r, axis=-1, keepdims=True)
    d = r - mu
    var = jnp.mean(d * d, axis=-1, keepdims=True)
    return d * lax.rsqrt(var + LN_EPS) * g + b


def _matmul_kernel(x_ref, w_ref, o_ref):
    o_ref[...] = jnp.dot(x_ref[...].astype(BF16), w_ref[...],
                         preferred_element_type=F32).astype(o_ref.dtype)


def _matmul(x, w, out_dtype, tm, tn):
    m, k = x.shape
    n = w.shape[1]
    return pl.pallas_call(
        _matmul_kernel,
        grid=(m // tm, n // tn),
        in_specs=[pl.BlockSpec((tm, k), lambda i, j: (i, 0)),
                  pl.BlockSpec((k, tn), lambda i, j: (0, j))],
        out_specs=pl.BlockSpec((tm, tn), lambda i, j: (i, j)),
        out_shape=jax.ShapeDtypeStruct((m, n), out_dtype),
        compiler_params=_params(2),
        name="matmul",
    )(x, w)


def _chunk_attn_kernel(q_ref, kp_ref, kc_ref, vp_ref, vc_ref, bias0_ref, bias1_ref, o_ref):
    w = SUB_A + LEFT
    for h in range(2):
        sl = slice(h * HEAD_DIM, (h + 1) * HEAD_DIM)
        k = jnp.concatenate([kp_ref[:, sl], kc_ref[:, sl]], axis=0)
        v = jnp.concatenate([vp_ref[:, sl], vc_ref[:, sl]], axis=0)
        for sub, bias_ref in enumerate((bias0_ref, bias1_ref)):
            rows = slice(sub * SUB_A, (sub + 1) * SUB_A)
            keys = slice(sub * SUB_A, sub * SUB_A + w)
            s = lax.dot_general(q_ref[rows, sl], k[keys], _NT, preferred_element_type=F32) + bias_ref[h]
            m = jnp.max(s, axis=-1, keepdims=True)
            p = jnp.exp(s - m)
            l = jnp.sum(p, axis=-1, keepdims=True)
            o = jnp.dot(p.astype(BF16), v[keys], preferred_element_type=F32) / l
            o_ref[rows, sl] = o.astype(o_ref.dtype)


def _chunk_bias(rel_bias):
    w = SUB_A + LEFT
    heads = rel_bias.shape[0]
    m = jnp.arange(2 * w)
    rel = jnp.where(m < w, LEFT - m, LEFT + 2 * w - m)
    diag = rel_bias[:, jnp.clip(rel, -REL_CLIP, REL_CLIP) + REL_CLIP].astype(F32)[:, None, :]
    n_tables = 1 + TQ_A // SUB_A
    return pl.pallas_call(
        _chunk_bias_kernel,
        grid=(heads,),
        in_specs=[pl.BlockSpec((1, 1, 2 * w), lambda h: (h, 0, 0))],
        out_specs=pl.BlockSpec((n_tables, 1, SUB_A, w), lambda h: (0, h, 0, 0)),
        out_shape=jax.ShapeDtypeStruct((n_tables, heads, SUB_A, w), F32),
        compiler_params=_params(1),
        name="chunk_bias",
    )(diag)


def _chunk_bias_kernel(diag_ref, o_ref):
    w = SUB_A + LEFT
    rolled = pltpu.roll(jnp.broadcast_to(diag_ref[0], (SUB_A, 2 * w)), 0, 1, stride=1, stride_axis=0)
    col = lax.broadcasted_iota(jnp.int32, (SUB_A, w), 1)
    qc = lax.broadcasted_iota(jnp.int32, (SUB_A, w), 0) // CHUNK
    kc = col // CHUNK
    band = (kc >= qc) & (kc <= qc + LEFT_CHUNKS)
    table = jnp.where(band, rolled[:, :w], NEG_INF)
    o_ref[0, 0] = table
    for sub in range(TQ_A // SUB_A):
        o_ref[1 + sub, 0] = jnp.where(col >= LEFT - sub * SUB_A, table, NEG_INF)


def _chunk_attention(proj, bias, b, s):
    nq = s // TQ_A
    pairs = WIDTH_A // LANES
    blk = lambda off, prev: pl.BlockSpec(
        (None, TQ_A, LANES),
        (lambda bi, p, i: (bi, jnp.maximum(i - 1, 0), off + p)) if prev
        else (lambda bi, p, i: (bi, i, off + p)))
    table = lambda sub: pl.BlockSpec((None, 2, SUB_A, SUB_A + LEFT),
                                     lambda bi, p, i: (jnp.where(i == 0, 1 + sub, 0), p, 0, 0))
    return pl.pallas_call(
        _chunk_attn_kernel,
        grid=(b, pairs, nq),
        in_specs=[blk(0, False),
                  blk(pairs, True), blk(pairs, False),
                  blk(2 * pairs, True), blk(2 * pairs, False),
                  table(0), table(1)],
        out_specs=pl.BlockSpec((None, TQ_A, LANES), lambda bi, p, i: (bi, i, p)),
        out_shape=jax.ShapeDtypeStruct((b, s, WIDTH_A), BF16),
        compiler_params=_params(3),
        name="chunk_attn",
    )(proj, proj, proj, proj, proj, bias, bias)


def _sb_block(q, k_ref, v_ref, sl, j, later, tri, causal):
    t = TB_SB
    start = pl.multiple_of(j * t, t)
    k = k_ref[pl.ds(start, t), sl]
    v = v_ref[pl.ds(start, t), sl]
    z = lax.dot_general(q, k, _NT, preferred_element_type=F32)
    sp = jnp.maximum(z, 0.0) + jnp.log(1.0 + jnp.exp2(jnp.abs(z) * (-LOG2E)))
    if causal is not None:
        sp = jnp.where(causal, sp, 0.0)
    csum = jnp.dot(sp.astype(BF16), tri, preferred_element_type=F32) + later
    wgt = jnp.exp(z - csum)
    if causal is not None:
        wgt = jnp.where(causal, wgt, 0.0)
    pv = jnp.dot(wgt.astype(BF16), v, preferred_element_type=F32)
    return pv, csum[:, 0:1]


def _sb_attn_kernel(q_ref, k_ref, v_ref, o_ref):
    first = pl.program_id(2) * SB_TILES
    t = TB_SB
    row = lax.broadcasted_iota(jnp.int32, (t, t), 0)
    col = lax.broadcasted_iota(jnp.int32, (t, t), 1)
    tri = jnp.where(row >= col, 1.0, 0.0).astype(BF16)
    causal = col < row
    chains = [(u, h) for u in range(SB_TILES) for h in range(2)]
    rows = lambda u: slice(u * t, (u + 1) * t)
    lanes = lambda h: slice(h * HEAD_DIM, (h + 1) * HEAD_DIM)
    qs = [q_ref[rows(u), lanes(h)] for u, h in chains]

    def block(c, j, later, masked):
        return _sb_block(qs[c], k_ref, v_ref, lanes(chains[c][1]), j, later, tri, causal if masked else None)

    carry = []
    for c, (u, h) in enumerate(chains):
        tile = first + u
        pv, later = block(c, tile, jnp.zeros((t, 1), F32), True)
        pv_prev, later_prev = block(c, jnp.maximum(tile - 1, 0), later, False)
        has_prev = tile >= 1
        carry += [jnp.where(has_prev, later_prev, later), pv + jnp.where(has_prev, pv_prev, 0.0)]

    def pending(c):
        todo = False
        for ci, (u, h) in enumerate(chains):
            todo = todo | ((first + u - 2 - c[0] >= 0) & (jnp.min(c[1 + 2 * ci]) < SB_UNDERFLOW))
        return todo

    def step(c):
        out = [c[0] + 1]
        for ci, (u, h) in enumerate(chains):
            j = first + u - 2 - c[0]
            pv, later = block(ci, jnp.maximum(j, 0), c[1 + 2 * ci], False)
            out += [jnp.where(j >= 0, later, c[1 + 2 * ci]), c[2 + 2 * ci] + jnp.where(j >= 0, pv, 0.0)]
        return tuple(out)

    carry = lax.while_loop(pending, step, (jnp.int32(0), *carry))
    for ci, (u, h) in enumerate(chains):
        o_ref[rows(u), lanes(h)] = carry[2 + 2 * ci].astype(o_ref.dtype)


def _sb_attention(proj, b, s):
    tq = SB_TILES * TB_SB
    nq = s // tq
    pairs = WIDTH_B // LANES
    base = 3 * WIDTH_A // LANES
    return pl.pallas_call(
        _sb_attn_kernel,
        grid=(b, pairs, nq),
        in_specs=[pl.BlockSpec((None, tq, LANES), lambda bi, p, i: (bi, i, base + p)),
                  pl.BlockSpec((None, s, LANES), lambda bi, p, i: (bi, 0, base + pairs + p)),
                  pl.BlockSpec((None, s, LANES), lambda bi, p, i: (bi, 0, base + 2 * pairs + p))],
        out_specs=pl.BlockSpec((None, tq, LANES), lambda bi, p, i: (bi, i, p)),
        out_shape=jax.ShapeDtypeStruct((b, s, WIDTH_B), BF16),
        compiler_params=_params(3),
        name="sb_attn",
    )(proj, proj, proj)


def _mixer_out_kernel(oa_ref, ob_ref, x_ref, ga_ref, gb_ref, wout_ref, lng_ref, lnb_ref, wq_ref,
                      x1_ref, qm_ref):
    def rms(ref, g_ref):
        a = ref[...].astype(F32)
        return (a * lax.rsqrt(jnp.mean(a * a, axis=-1, keepdims=True) + RMS_EPS) * g_ref[...]).astype(BF16)

    y = jnp.dot(rms(oa_ref, ga_ref), wout_ref[:WIDTH_A, :], preferred_element_type=F32)
    y = y + jnp.dot(rms(ob_ref, gb_ref), wout_ref[WIDTH_A:, :], preferred_element_type=F32)
    x1 = _layer_norm(DEEPNORM_ALPHA * x_ref[...] + y, lng_ref[...], lnb_ref[...])
    x1_ref[...] = x1
    qm_ref[...] = jnp.dot(x1.astype(BF16), wq_ref[...], preferred_element_type=F32).astype(BF16)


def _mixer_out(out_a, out_b, x, g_a, g_b, w_out, ln_g, ln_b, w_q):
    n = x.shape[0]
    tm = TM_TOK
    row = lambda width: pl.BlockSpec((tm, width), lambda i: (i, 0))
    full = lambda r, c: pl.BlockSpec((r, c), lambda i: (0, 0))
    return pl.pallas_call(
        _mixer_out_kernel,
        grid=(n // tm,),
        in_specs=[row(WIDTH_A), row(WIDTH_B), row(D_MODEL), full(1, WIDTH_A), full(1, WIDTH_B),
                  full(MIX_WIDTH, D_MODEL), full(1, D_MODEL), full(1, D_MODEL), full(D_MODEL, D_MODEL)],
        out_specs=[row(D_MODEL), row(D_MODEL)],
        out_shape=[jax.ShapeDtypeStruct((n, D_MODEL), F32), jax.ShapeDtypeStruct((n, D_MODEL), BF16)],
        compiler_params=_params(1),
        name="mixer_out",
    )(out_a, out_b, x, g_a, g_b, w_out, ln_g, ln_b, w_q)


def _mem_attn_kernel(q_ref, kv_ref, x1_ref, wo_ref, lng_ref, lnb_ref, wr_ref, br_ref,
                     x2_ref, idx_ref, gate_ref):
    heads = []
    for h in range(N_HEADS_MEM):
        sl = slice(h * HEAD_DIM_MEM, (h + 1) * HEAD_DIM_MEM)
        q = q_ref[:, sl] * (HEAD_DIM_MEM ** -0.5)
        k = kv_ref[:, sl]
        v = kv_ref[:, D_MODEL + h * HEAD_DIM_MEM:D_MODEL + (h + 1) * HEAD_DIM_MEM]
        s = lax.dot_general(q, k, _NT, preferred_element_type=F32)
        m = jnp.max(s, axis=-1, keepdims=True)
        p = jnp.exp(s - m)
        l = jnp.sum(p, axis=-1, keepdims=True)
        heads.append((jnp.dot(p.astype(BF16), v, preferred_element_type=F32) / l).astype(BF16))
    o = jnp.concatenate(heads, axis=-1)
    y = jnp.dot(o, wo_ref[...], preferred_element_type=F32)
    x2 = _layer_norm(DEEPNORM_ALPHA * x1_ref[...] + y, lng_ref[...], lnb_ref[...])
    _store_token_major(x2_ref, x2)

    logits = jnp.dot(x2.astype(BF16), wr_ref[...], preferred_element_type=F32) + br_ref[...]
    tm = logits.shape[0]
    e_iota = lax.broadcasted_iota(jnp.int32, (tm, N_EXPERTS), 1)
    lane = lax.broadcasted_iota(jnp.int32, (tm, LANES), 1)
    idx_out = jnp.zeros((tm, LANES), jnp.int32)
    val_out = jnp.zeros((tm, LANES), F32)
    top = None
    denom = jnp.zeros((tm, 1), F32)
    for kk in range(TOP_K):
        m = jnp.max(logits, axis=-1, keepdims=True)
        sel = jnp.min(jnp.where(logits == m, e_iota, N_EXPERTS), axis=-1, keepdims=True)
        if top is None:
            top = m
        e = jnp.exp(m - top)
        denom = denom + e
        idx_out = jnp.where(lane == kk, sel, idx_out)
        val_out = jnp.where(lane == kk, e, val_out)
        logits = jnp.where(e_iota == sel, -jnp.inf, logits)
    idx_ref[...] = idx_out
    gate_ref[...] = val_out / denom


def _mem_attn(qm, kv, x1, w_o, ln_g, ln_b, w_r, b_r, b, s):
    tm = TM_TOK
    nt = s // tm
    mem_len = kv.shape[1]
    row = lambda width: pl.BlockSpec((tm, width), lambda bi, i: (bi * nt + i, 0))
    full = lambda r, c: pl.BlockSpec((r, c), lambda bi, i: (0, 0))
    n = b * s
    return pl.pallas_call(
        _mem_attn_kernel,
        grid=(b, nt),
        in_specs=[row(D_MODEL), pl.BlockSpec((None, mem_len, 2 * D_MODEL), lambda bi, i: (bi, 0, 0)),
                  row(D_MODEL), full(D_MODEL, D_MODEL), full(1, D_MODEL), full(1, D_MODEL),
                  full(D_MODEL, N_EXPERTS), full(1, N_EXPERTS)],
        out_specs=[pl.BlockSpec((tm * LANE_BLOCKS, LANES), lambda bi, i: (bi * nt + i, 0)), row(LANES), row(LANES)],
        out_shape=[jax.ShapeDtypeStruct((n * LANE_BLOCKS, LANES), F32),
                   jax.ShapeDtypeStruct((n, LANES), jnp.int32),
                   jax.ShapeDtypeStruct((n, LANES), F32)],
        compiler_params=_params(2),
        name="mem_attn",
    )(qm, kv, x1, w_o, ln_g, ln_b, w_r, b_r)


def _gather_rows(idx_ref, first, count, src_hbm, dst_ref, sem):
    for r in range(count):
        src = pl.ds(pl.multiple_of(idx_ref[first + r], LANE_BLOCKS), LANE_BLOCKS)
        pltpu.make_async_copy(src_hbm.at[src], dst_ref.at[pl.ds(r * LANE_BLOCKS, LANE_BLOCKS)], sem).start(
            priority=r % 2)


def _wait_rows(count, src_hbm, dst_ref, sem):
    pltpu.make_async_copy(src_hbm.at[pl.ds(0, count * LANE_BLOCKS)], dst_ref, sem).wait()


def _moe_kernel(bexp_ref, nused_ref, tok_ref, x_hbm, wgu_ref, bgu_ref, wd_ref, bd_ref,
                y_ref, xbuf, wgu_bf, wd_bf, sem):
    i = pl.program_id(0)
    slot = i % MOE_SLOTS
    ahead = MOE_SLOTS - 1
    n_used = nused_ref[0]

    @pl.when(i == 0)
    def _():
        for b in range(ahead):
            _gather_rows(tok_ref, b * TM_MOE, TM_MOE, x_hbm, xbuf.at[b], sem.at[b])

    @pl.when(i < n_used + ahead)
    def _():
        _wait_rows(TM_MOE, x_hbm, xbuf.at[slot], sem.at[slot])

    @pl.when((i < n_used) & ((i == 0) | (bexp_ref[i] != bexp_ref[jnp.maximum(i - 1, 0)])))
    def _():
        wgu_bf[...] = wgu_ref[0].astype(BF16)
        wd_bf[...] = wd_ref[0].astype(BF16)

    @pl.when(i < n_used)
    def _():
        x = _load_token_major(xbuf.at[slot], 0, TM_MOE).astype(BF16)
        nxt = (i + ahead) % MOE_SLOTS
        _gather_rows(tok_ref, (i + ahead) * TM_MOE, TM_MOE, x_hbm, xbuf.at[nxt], sem.at[nxt])
        gu = jnp.dot(x, wgu_bf[...], preferred_element_type=F32) + bgu_ref[0]
        gate = jnp.minimum(gu[:, :D_FF], SWIGLU_LIMIT)
        up = jnp.clip(gu[:, D_FF:], -SWIGLU_LIMIT, SWIGLU_LIMIT)
        glu = gate * jax.nn.sigmoid(gate * SWIGLU_ALPHA)
        hid = ((up + 1.0) * glu).astype(BF16)
        _store_token_major(y_ref, jnp.dot(hid, wd_bf[...], preferred_element_type=F32) + bd_ref[0])

    @pl.when(i >= n_used)
    def _():
        y_ref[...] = jnp.zeros_like(y_ref)


def _moe_experts(block_expert, n_used, row_token, x2, w_gu, b_gu, w_d, b_d):
    nb = block_expert.shape[0]
    grid_spec = pltpu.PrefetchScalarGridSpec(
        num_scalar_prefetch=3,
        grid=(nb,),
        in_specs=[pl.BlockSpec(memory_space=pl.ANY),
                  pl.BlockSpec((1, D_MODEL, 2 * D_FF), lambda i, be, nu, tok: (be[i], 0, 0)),
                  pl.BlockSpec((1, 1, 2 * D_FF), lambda i, be, nu, tok: (be[i], 0, 0)),
                  pl.BlockSpec((1, D_FF, D_MODEL), lambda i, be, nu, tok: (be[i], 0, 0)),
                  pl.BlockSpec((1, 1, D_MODEL), lambda i, be, nu, tok: (be[i], 0, 0))],
        out_specs=pl.BlockSpec((TM_MOE * LANE_BLOCKS, LANES), lambda i, be, nu, tok: (i, 0)),
        scratch_shapes=[pltpu.VMEM((MOE_SLOTS, TM_MOE * LANE_BLOCKS, LANES), F32),
                        pltpu.VMEM((D_MODEL, 2 * D_FF), BF16), pltpu.VMEM((D_FF, D_MODEL), BF16),
                        pltpu.SemaphoreType.DMA((MOE_SLOTS,))],
    )
    return pl.pallas_call(
        _moe_kernel,
        grid_spec=grid_spec,
        out_shape=jax.ShapeDtypeStruct((nb * TM_MOE * LANE_BLOCKS, LANES), F32),
        compiler_params=pltpu.CompilerParams(dimension_semantics=("arbitrary",),
                                             vmem_limit_bytes=VMEM_LIMIT_MOE),
        name="moe_experts",
    )(block_expert, n_used, row_token * LANE_BLOCKS, x2, w_gu, b_gu, w_d, b_d)


def _combine_kernel(idx_ref, y_hbm, x2_ref, gate_ref, lng_ref, lnb_ref, o_ref, ybuf, sem):
    i = pl.program_id(0)
    nt = pl.num_programs(0)
    slot = i % COMB_SLOTS
    ahead = COMB_SLOTS - 1
    rows = TOP_K * TM_COMB

    @pl.when(i == 0)
    def _():
        for t in range(ahead):
            _gather_rows(idx_ref, t * rows, rows, y_hbm, ybuf.at[t], sem.at[t])

    _wait_rows(rows, y_hbm, ybuf.at[slot], sem.at[slot])

    def compute():
        g = gate_ref[...]
        ff = jnp.zeros((TM_COMB, D_MODEL), F32)
        for kk in range(TOP_K):
            ff = ff + g[:, kk:kk + 1] * _load_token_major(ybuf.at[slot], kk * TM_COMB, TM_COMB)
        x2 = _load_token_major(x2_ref, 0, TM_COMB)
        o_ref[...] = _layer_norm(DEEPNORM_ALPHA * x2 + ff, lng_ref[...], lnb_ref[...])

    @pl.when(i + ahead < nt)
    def _():
        nxt = (i + ahead) % COMB_SLOTS
        _gather_rows(idx_ref, (i + ahead) * rows, rows, y_hbm, ybuf.at[nxt], sem.at[nxt])
        compute()

    @pl.when(i + ahead >= nt)
    def _():
        compute()


def _combine(dest, y_rows, x2, gates, ln_g, ln_b):
    n = x2.shape[0] // LANE_BLOCKS
    nt = n // TM_COMB
    rows = TOP_K * TM_COMB
    idx = (dest * LANE_BLOCKS).reshape(nt, TM_COMB, TOP_K).transpose(0, 2, 1).reshape(nt * rows)
    row = lambda width: pl.BlockSpec((TM_COMB, width), lambda i, idx: (i, 0))
    full = lambda r, c: pl.BlockSpec((r, c), lambda i, idx: (0, 0))
    grid_spec = pltpu.PrefetchScalarGridSpec(
        num_scalar_prefetch=1,
        grid=(nt,),
        in_specs=[pl.BlockSpec(memory_space=pl.ANY),
                  pl.BlockSpec((TM_COMB * LANE_BLOCKS, LANES), lambda i, idx: (i, 0)), row(LANES),
                  full(1, D_MODEL), full(1, D_MODEL)],
        out_specs=row(D_MODEL),
        scratch_shapes=[pltpu.VMEM((COMB_SLOTS, rows * LANE_BLOCKS, LANES), F32),
                        pltpu.SemaphoreType.DMA((COMB_SLOTS,))],
    )
    return pl.pallas_call(
        _combine_kernel,
        grid_spec=grid_spec,
        out_shape=jax.ShapeDtypeStruct((n, D_MODEL), F32),
        compiler_params=_params(1),
        name="combine",
    )(idx, y_rows, x2, gates, ln_g, ln_b)


def _rank_kernel(idx_ref, dest_ref, counts_ref, running):
    p = pl.program_id(0)
    j = pl.program_id(1)
    t = TM_RANK
    idx = idx_ref[...]
    e_iota = lax.broadcasted_iota(jnp.int32, (t, LANES), 1)
    hot = [idx[:, k:k + 1] == e_iota for k in range(TOP_K)]
    chosen = sum(jnp.where(h, 1.0, 0.0) for h in hot)
    tile_counts = jnp.sum(chosen, axis=0, keepdims=True)

    @pl.when((p == 0) & (j == 0))
    def _():
        running[...] = jnp.zeros_like(running)

    @pl.when(p == 0)
    def _():
        running[...] += tile_counts
        dest_ref[...] = jnp.zeros_like(dest_ref)

    @pl.when((p == 1) & (j == 0))
    def _():
        counts = running[...]
        counts_ref[...] = counts
        padded = jnp.ceil(counts * (1.0 / TM_MOE)) * TM_MOE
        lane = lax.broadcasted_iota(jnp.int32, (1, LANES), 1)
        scan = padded
        shift = 1
        while shift < N_EXPERTS:
            scan = scan + jnp.where(lane >= shift, pltpu.roll(scan, shift, 1), 0.0)
            shift *= 2
        running[...] = scan - padded

    @pl.when(p == 1)
    def _():
        r_iota = lax.broadcasted_iota(jnp.int32, (t, t), 0)
        c_iota = lax.broadcasted_iota(jnp.int32, (t, t), 1)
        earlier = jnp.where(c_iota < r_iota, 1.0, 0.0).astype(BF16)
        base = jnp.dot(earlier, chosen.astype(BF16), preferred_element_type=F32) + running[...]
        out = jnp.zeros((t, LANES), jnp.int32)
        for k in range(TOP_K):
            row = jnp.sum(jnp.where(hot[k], base, 0.0), axis=-1, keepdims=True).astype(jnp.int32)
            out = jnp.where(e_iota == k, row, out)
        dest_ref[...] = out
        running[...] += tile_counts


def _row_token_kernel(dest_ref, gap_ref, tok_ref):
    unroll = 8

    def clear(row, carry):
        tok_ref[row] = 0
        return carry

    def place(c, carry):
        rows = [dest_ref[c * unroll + u] for u in range(unroll)]
        for u in range(unroll):
            tok_ref[rows[u]] = c * (unroll // TOP_K) + u // TOP_K
        return carry

    for g in range(gap_ref.shape[1]):
        lax.fori_loop(gap_ref[0, g], gap_ref[1, g], clear, 0)
    lax.fori_loop(0, dest_ref.shape[0] // unroll, place, 0)


def _dispatch_plan(top_idx, n):
    n_assign = n * TOP_K
    nb = n_assign // TM_MOE + N_EXPERTS + MOE_SLOTS - 2
    dest, counts = pl.pallas_call(
        _rank_kernel,
        grid=(2, n // TM_RANK),
        in_specs=[pl.BlockSpec((TM_RANK, LANES), lambda p, j: (j, 0))],
        out_specs=[pl.BlockSpec((TM_RANK, LANES), lambda p, j: (p * j, 0)),
                   pl.BlockSpec((1, LANES), lambda p, j: (0, 0))],
        out_shape=[jax.ShapeDtypeStruct((n, LANES), jnp.int32), jax.ShapeDtypeStruct((1, LANES), F32)],
        scratch_shapes=[pltpu.VMEM((1, LANES), F32)],
        compiler_params=_params(2),
        name="expert_rank",
    )(top_idx)
    dest = dest[:, :TOP_K].reshape(-1)
    counts = counts[0, :N_EXPERTS].astype(jnp.int32)
    padded = (counts + TM_MOE - 1) // TM_MOE * TM_MOE
    pad_end = jnp.cumsum(padded)
    tail = jnp.stack([pad_end[-1], jnp.int32(nb * TM_MOE)])[:, None]
    gaps = jnp.concatenate([jnp.stack([pad_end - padded + counts, pad_end]), tail], axis=1)
    row_token = pl.pallas_call(
        _row_token_kernel,
        in_specs=[pl.BlockSpec(memory_space=pltpu.SMEM), pl.BlockSpec(memory_space=pltpu.SMEM)],
        out_specs=pl.BlockSpec(memory_space=pltpu.SMEM),
        out_shape=jax.ShapeDtypeStruct((nb * TM_MOE,), jnp.int32),
        name="row_token",
    )(dest, gaps)
    block_first_row = jnp.arange(nb, dtype=jnp.int32) * TM_MOE
    block_expert = jnp.minimum(jnp.sum(pad_end[None, :] <= block_first_row[:, None], axis=1),
                               N_EXPERTS - 1).astype(jnp.int32)
    n_used = (pad_end[-1:] // TM_MOE).astype(jnp.int32)
    return dest, row_token, block_expert, n_used


def kernel(x, mem, w_in, rel_bias, g_group_a, g_group_b, w_out, w_q_mem, w_kv_mem, w_o_mem, w_router, b_router, w_gate_up, b_gate_up, w_down, b_down, ln_g, ln_b):
    b, s, d = x.shape
    n = b * s
    depth = w_in.shape[0]
    xf = x.reshape(n, d)
    for l in range(depth):
        q_scale = jnp.ones((3 * MIX_WIDTH,), F32).at[:WIDTH_A].set(HEAD_DIM ** -0.5)
        q_scale = q_scale.at[3 * WIDTH_A:3 * WIDTH_A + WIDTH_B].set(HEAD_DIM ** -0.5)
        proj = _matmul(xf, (w_in[l] * q_scale).astype(BF16), BF16, TM_TOK, 3 * MIX_WIDTH).reshape(b, s, 3 * MIX_WIDTH)
        out_a = _chunk_attention(proj, _chunk_bias(rel_bias[l]), b, s).reshape(n, WIDTH_A)
        out_b = _sb_attention(proj, b, s).reshape(n, WIDTH_B)
        x1, qm = _mixer_out(out_a, out_b, xf, g_group_a[l][None], g_group_b[l][None],
                            w_out[l].astype(BF16), ln_g[l, 0][None], ln_b[l, 0][None],
                            w_q_mem[l].astype(BF16))
        mem_len = mem.shape[1]
        kv = _matmul(mem.reshape(b * mem_len, d), w_kv_mem[l].astype(BF16), BF16, b * mem_len, 1024)
        x2, top_idx, gates = _mem_attn(qm, kv.reshape(b, mem_len, 2 * d), x1, w_o_mem[l].astype(BF16),
                                       ln_g[l, 1][None], ln_b[l, 1][None],
                                       w_router[l].astype(BF16), b_router[l][None], b, s)
        dest, row_token, block_expert, n_used = _dispatch_plan(top_idx, n)
        y_rows = _moe_experts(block_expert, n_used, row_token, x2,
                              w_gate_up[l], b_gate_up[l][:, None, :], w_down[l], b_down[l][:, None, :])
        xf = _combine(dest, y_rows, x2, gates, ln_g[l, 2][None], ln_b[l, 2][None])
    return xf.reshape(b, s, d)
```

```python
import functools

import jax
import jax.numpy as jnp
from jax import lax
from jax.experimental import pallas as pl
from jax.experimental.pallas import tpu as pltpu

D_MODEL = 1024
CHUNK = 64
LEFT_CHUNKS = 8
LEFT = LEFT_CHUNKS * CHUNK
HEAD_DIM = 64
N_HEADS_A = 8
N_HEADS_B = 8
WIDTH_A = N_HEADS_A * HEAD_DIM
WIDTH_B = N_HEADS_B * HEAD_DIM
MIX_WIDTH = WIDTH_A + WIDTH_B
REL_CLIP = 128
N_HEADS_MEM = 4
HEAD_DIM_MEM = D_MODEL // N_HEADS_MEM
N_EXPERTS = 32
TOP_K = 4
D_FF = D_MODEL
SWIGLU_LIMIT = 7.0
SWIGLU_ALPHA = 1.702
LN_EPS = 1e-5
RMS_EPS = 1e-6
DEEPNORM_ALPHA = 2.0 ** 0.25
NEG_INF = -1e30
LOG2E = 1.4426950408889634

LANES = 128
LANE_BLOCKS = D_MODEL // LANES
VMEM_LIMIT = 48 * 1024 * 1024
VMEM_LIMIT_MOE = 56 * 1024 * 1024

TQ_A = 512
SUB_A = 256
HEADS_A_STEP = 4
TB_SB = 256
SB_TILES = 4
SB_UNDERFLOW = 110.0
TM_TOK = 512
TM_MOE = 256
MOE_SLOTS = 3
TM_COMB = 256
COMB_SLOTS = 3
TM_RANK = 512

F32 = jnp.float32
BF16 = jnp.bfloat16
_NT = (((1,), (1,)), ((), ()))


def _params(n_axes):
    return pltpu.CompilerParams(dimension_semantics=("arbitrary",) * n_axes,
                                vmem_limit_bytes=VMEM_LIMIT)


def _store_token_major(ref, value):
    rows = value.shape[0]
    for c in range(LANE_BLOCKS):
        ref[pl.ds(c, rows, stride=LANE_BLOCKS), :] = value[:, c * LANES:(c + 1) * LANES]


def _load_token_major(ref, first_row, rows):
    return jnp.concatenate(
        [ref[pl.ds(first_row * LANE_BLOCKS + c, rows, stride=LANE_BLOCKS), :] for c in range(LANE_BLOCKS)],
        axis=1)


def _layer_norm(r, g, b):
    mu = jnp.mean(r, axis=-1, keepdims=True)
    d = r - mu
    var = jnp.mean(d * d, axis=-1, keepdims=True)
    return d * lax.rsqrt(var + LN_EPS) * g + b


def _matmul_kernel(x_ref, w_ref, o_ref):
    o_ref[...] = jnp.dot(x_ref[...].astype(BF16), w_ref[...],
                         preferred_element_type=F32).astype(o_ref.dtype)


def _matmul(x, w, out_dtype, tm, tn):
    m, k = x.shape
    n = w.shape[1]
    return pl.pallas_call(
        _matmul_kernel,
        grid=(m // tm, n // tn),
        in_specs=[pl.BlockSpec((tm, k), lambda i, j: (i, 0)),
                  pl.BlockSpec((k, tn), lambda i, j: (0, j))],
        out_specs=pl.BlockSpec((tm, tn), lambda i, j: (i, j)),
        out_shape=jax.ShapeDtypeStruct((m, n), out_dtype),
        compiler_params=_params(2),
        name="matmul",
    )(x, w)


def _chunk_attn_kernel(q_ref, kp_ref, kc_ref, vp_ref, vc_ref, bias0_ref, bias1_ref, o_ref):
    w = SUB_A + LEFT
    for h in range(HEADS_A_STEP):
        sl = slice(h * HEAD_DIM, (h + 1) * HEAD_DIM)
        k = jnp.concatenate([kp_ref[:, sl], kc_ref[:, sl]], axis=0)
        v = jnp.concatenate([vp_ref[:, sl], vc_ref[:, sl]], axis=0)
        for sub, bias_ref in enumerate((bias0_ref, bias1_ref)):
            rows = slice(sub * SUB_A, (sub + 1) * SUB_A)
            keys = slice(sub * SUB_A, sub * SUB_A + w)
            s = lax.dot_general(q_ref[rows, sl], k[keys], _NT, preferred_element_type=F32) + bias_ref[h]
            m = jnp.max(s, axis=-1, keepdims=True)
            p = jnp.exp(s - m)
            l = jnp.sum(p, axis=-1, keepdims=True)
            o = jnp.dot(p.astype(BF16), v[keys], preferred_element_type=F32) / l
            o_ref[rows, sl] = o.astype(o_ref.dtype)


def _chunk_bias(rel_bias):
    w = SUB_A + LEFT
    heads = rel_bias.shape[0]
    m = jnp.arange(2 * w)
    rel = jnp.where(m < w, LEFT - m, LEFT + 2 * w - m)
    diag = rel_bias[:, jnp.clip(rel, -REL_CLIP, REL_CLIP) + REL_CLIP].astype(F32)[:, None, :]
    n_tables = 1 + TQ_A // SUB_A
    return pl.pallas_call(
        _chunk_bias_kernel,
        grid=(heads,),
        in_specs=[pl.BlockSpec((1, 1, 2 * w), lambda h: (h, 0, 0))],
        out_specs=pl.BlockSpec((n_tables, 1, SUB_A, w), lambda h: (0, h, 0, 0)),
        out_shape=jax.ShapeDtypeStruct((n_tables, heads, SUB_A, w), F32),
        compiler_params=_params(1),
        name="chunk_bias",
    )(diag)


def _chunk_bias_kernel(diag_ref, o_ref):
    w = SUB_A + LEFT
    rolled = pltpu.roll(jnp.broadcast_to(diag_ref[0], (SUB_A, 2 * w)), 0, 1, stride=1, stride_axis=0)
    col = lax.broadcasted_iota(jnp.int32, (SUB_A, w), 1)
    qc = lax.broadcasted_iota(jnp.int32, (SUB_A, w), 0) // CHUNK
    kc = col // CHUNK
    band = (kc >= qc) & (kc <= qc + LEFT_CHUNKS)
    table = jnp.where(band, rolled[:, :w], NEG_INF)
    o_ref[0, 0] = table
    for sub in range(TQ_A // SUB_A):
        o_ref[1 + sub, 0] = jnp.where(col >= LEFT - sub * SUB_A, table, NEG_INF)


def _chunk_attention(proj, bias, b, s):
    nq = s // TQ_A
    width = HEADS_A_STEP * HEAD_DIM
    groups = WIDTH_A // width
    blk = lambda off, prev: pl.BlockSpec(
        (None, TQ_A, width),
        (lambda bi, p, i: (bi, jnp.maximum(i - 1, 0), off + p)) if prev
        else (lambda bi, p, i: (bi, i, off + p)))
    table = lambda sub: pl.BlockSpec((None, HEADS_A_STEP, SUB_A, SUB_A + LEFT),
                                     lambda bi, p, i: (jnp.where(i == 0, 1 + sub, 0), p, 0, 0))
    return pl.pallas_call(
        _chunk_attn_kernel,
        grid=(b, groups, nq),
        in_specs=[blk(0, False),
                  blk(groups, True), blk(groups, False),
                  blk(2 * groups, True), blk(2 * groups, False),
                  table(0), table(1)],
        out_specs=pl.BlockSpec((None, TQ_A, width), lambda bi, p, i: (bi, i, p)),
        out_shape=jax.ShapeDtypeStruct((b, s, WIDTH_A), BF16),
        compiler_params=_params(3),
        name="chunk_attn",
    )(proj, proj, proj, proj, proj, bias, bias)


def _sb_block(q, k_ref, v_ref, sl, j, later, tri, causal):
    t = TB_SB
    start = pl.multiple_of(j * t, t)
    k = k_ref[pl.ds(start, t), sl]
    v = v_ref[pl.ds(start, t), sl]
    z = lax.dot_general(q, k, _NT, preferred_element_type=F32)
    sp = jnp.maximum(z, 0.0) + jnp.log(1.0 + jnp.exp2(jnp.abs(z) * (-LOG2E)))
    if causal is not None:
        sp = jnp.where(causal, sp, 0.0)
    csum = jnp.dot(sp.astype(BF16), tri, preferred_element_type=F32) + later
    wgt = jnp.exp(z - csum)
    if causal is not None:
        wgt = jnp.where(causal, wgt, 0.0)
    pv = jnp.dot(wgt.astype(BF16), v, preferred_element_type=F32)
    return pv, csum[:, 0:1]


def _sb_attn_kernel(q_ref, k_ref, v_ref, o_ref):
    first = pl.program_id(2) * SB_TILES
    t = TB_SB
    row = lax.broadcasted_iota(jnp.int32, (t, t), 0)
    col = lax.broadcasted_iota(jnp.int32, (t, t), 1)
    tri = jnp.where(row >= col, 1.0, 0.0).astype(BF16)
    causal = col < row
    chains = [(u, h) for u in range(SB_TILES) for h in range(2)]
    rows = lambda u: slice(u * t, (u + 1) * t)
    lanes = lambda h: slice(h * HEAD_DIM, (h + 1) * HEAD_DIM)
    qs = [q_ref[rows(u), lanes(h)] for u, h in chains]

    def block(c, j, later, masked):
        return _sb_block(qs[c], k_ref, v_ref, lanes(chains[c][1]), j, later, tri, causal if masked else None)

    carry = []
    for c, (u, h) in enumerate(chains):
        tile = first + u
        pv, later = block(c, tile, jnp.zeros((t, 1), F32), True)
        pv_prev, later_prev = block(c, jnp.maximum(tile - 1, 0), later, False)
        has_prev = tile >= 1
        carry += [jnp.where(has_prev, later_prev, later), pv + jnp.where(has_prev, pv_prev, 0.0)]

    def pending(c):
        todo = False
        for ci, (u, h) in enumerate(chains):
            todo = todo | ((first + u - 2 - c[0] >= 0) & (jnp.min(c[1 + 2 * ci]) < SB_UNDERFLOW))
        return todo

    def step(c):
        out = [c[0] + 1]
        for ci, (u, h) in enumerate(chains):
            j = first + u - 2 - c[0]
            pv, later = block(ci, jnp.maximum(j, 0), c[1 + 2 * ci], False)
            out += [jnp.where(j >= 0, later, c[1 + 2 * ci]), c[2 + 2 * ci] + jnp.where(j >= 0, pv, 0.0)]
        return tuple(out)

    carry = lax.while_loop(pending, step, (jnp.int32(0), *carry))
    for ci, (u, h) in enumerate(chains):
        o_ref[rows(u), lanes(h)] = carry[2 + 2 * ci].astype(o_ref.dtype)


def _sb_attention(proj, b, s):
    tq = SB_TILES * TB_SB
    nq = s // tq
    pairs = WIDTH_B // LANES
    base = 3 * WIDTH_A // LANES
    return pl.pallas_call(
        _sb_attn_kernel,
        grid=(b, pairs, nq),
        in_specs=[pl.BlockSpec((None, tq, LANES), lambda bi, p, i: (bi, i, base + p)),
                  pl.BlockSpec((None, s, LANES), lambda bi, p, i: (bi, 0, base + pairs + p)),
                  pl.BlockSpec((None, s, LANES), lambda bi, p, i: (bi, 0, base + 2 * pairs + p))],
        out_specs=pl.BlockSpec((None, tq, LANES), lambda bi, p, i: (bi, i, p)),
        out_shape=jax.ShapeDtypeStruct((b, s, WIDTH_B), BF16),
        compiler_params=_params(3),
        name="sb_attn",
    )(proj, proj, proj)


def _mixer_out_kernel(oa_ref, ob_ref, x_ref, ga_ref, gb_ref, wout_ref, lng_ref, lnb_ref, wq_ref,
                      x1_ref, qm_ref):
    def rms(ref, g_ref):
        a = ref[...].astype(F32)
        return (a * lax.rsqrt(jnp.mean(a * a, axis=-1, keepdims=True) + RMS_EPS) * g_ref[...]).astype(BF16)

    y = jnp.dot(rms(oa_ref, ga_ref), wout_ref[:WIDTH_A, :], preferred_element_type=F32)
    y = y + jnp.dot(rms(ob_ref, gb_ref), wout_ref[WIDTH_A:, :], preferred_element_type=F32)
    x1 = _layer_norm(DEEPNORM_ALPHA * x_ref[...] + y, lng_ref[...], lnb_ref[...])
    x1_ref[...] = x1
    qm_ref[...] = jnp.dot(x1.astype(BF16), wq_ref[...], preferred_element_type=F32).astype(BF16)


def _mixer_out(out_a, out_b, x, g_a, g_b, w_out, ln_g, ln_b, w_q):
    n = x.shape[0]
    tm = TM_TOK
    row = lambda width: pl.BlockSpec((tm, width), lambda i: (i, 0))
    full = lambda r, c: pl.BlockSpec((r, c), lambda i: (0, 0))
    return pl.pallas_call(
        _mixer_out_kernel,
        grid=(n // tm,),
        in_specs=[row(WIDTH_A), row(WIDTH_B), row(D_MODEL), full(1, WIDTH_A), full(1, WIDTH_B),
                  full(MIX_WIDTH, D_MODEL), full(1, D_MODEL), full(1, D_MODEL), full(D_MODEL, D_MODEL)],
        out_specs=[row(D_MODEL), row(D_MODEL)],
        out_shape=[jax.ShapeDtypeStruct((n, D_MODEL), F32), jax.ShapeDtypeStruct((n, D_MODEL), BF16)],
        compiler_params=_params(1),
        name="mixer_out",
    )(out_a, out_b, x, g_a, g_b, w_out, ln_g, ln_b, w_q)


def _mem_attn_kernel(q_ref, kv_ref, x1_ref, wo_ref, lng_ref, lnb_ref, wr_ref, br_ref,
                     x2_ref, idx_ref, gate_ref):
    heads = []
    for h in range(N_HEADS_MEM):
        sl = slice(h * HEAD_DIM_MEM, (h + 1) * HEAD_DIM_MEM)
        q = q_ref[:, sl] * (HEAD_DIM_MEM ** -0.5)
        k = kv_ref[:, sl]
        v = kv_ref[:, D_MODEL + h * HEAD_DIM_MEM:D_MODEL + (h + 1) * HEAD_DIM_MEM]
        s = lax.dot_general(q, k, _NT, preferred_element_type=F32)
        m = jnp.max(s, axis=-1, keepdims=True)
        p = jnp.exp(s - m)
        l = jnp.sum(p, axis=-1, keepdims=True)
        heads.append((jnp.dot(p.astype(BF16), v, preferred_element_type=F32) / l).astype(BF16))
    o = jnp.concatenate(heads, axis=-1)
    y = jnp.dot(o, wo_ref[...], preferred_element_type=F32)
    x2 = _layer_norm(DEEPNORM_ALPHA * x1_ref[...] + y, lng_ref[...], lnb_ref[...])
    _store_token_major(x2_ref, x2)

    logits = jnp.dot(x2.astype(BF16), wr_ref[...], preferred_element_type=F32) + br_ref[...]
    tm = logits.shape[0]
    e_iota = lax.broadcasted_iota(jnp.int32, (tm, N_EXPERTS), 1)
    lane = lax.broadcasted_iota(jnp.int32, (tm, LANES), 1)
    idx_out = jnp.zeros((tm, LANES), jnp.int32)
    val_out = jnp.zeros((tm, LANES), F32)
    top = None
    denom = jnp.zeros((tm, 1), F32)
    for kk in range(TOP_K):
        m = jnp.max(logits, axis=-1, keepdims=True)
        sel = jnp.min(jnp.where(logits == m, e_iota, N_EXPERTS), axis=-1, keepdims=True)
        if top is None:
            top = m
        e = jnp.exp(m - top)
        denom = denom + e
        idx_out = jnp.where(lane == kk, sel, idx_out)
        val_out = jnp.where(lane == kk, e, val_out)
        logits = jnp.where(e_iota == sel, -jnp.inf, logits)
    idx_ref[...] = idx_out
    gate_ref[...] = val_out / denom


def _mem_attn(qm, kv, x1, w_o, ln_g, ln_b, w_r, b_r, b, s):
    tm = TM_TOK
    nt = s // tm
    mem_len = kv.shape[1]
    row = lambda width: pl.BlockSpec((tm, width), lambda bi, i: (bi * nt + i, 0))
    full = lambda r, c: pl.BlockSpec((r, c), lambda bi, i: (0, 0))
    n = b * s
    return pl.pallas_call(
        _mem_attn_kernel,
        grid=(b, nt),
        in_specs=[row(D_MODEL), pl.BlockSpec((None, mem_len, 2 * D_MODEL), lambda bi, i: (bi, 0, 0)),
                  row(D_MODEL), full(D_MODEL, D_MODEL), full(1, D_MODEL), full(1, D_MODEL),
                  full(D_MODEL, N_EXPERTS), full(1, N_EXPERTS)],
        out_specs=[pl.BlockSpec((tm * LANE_BLOCKS, LANES), lambda bi, i: (bi * nt + i, 0)), row(LANES), row(LANES)],
        out_shape=[jax.ShapeDtypeStruct((n * LANE_BLOCKS, LANES), F32),
                   jax.ShapeDtypeStruct((n, LANES), jnp.int32),
                   jax.ShapeDtypeStruct((n, LANES), F32)],
        compiler_params=_params(2),
        name="mem_attn",
    )(qm, kv, x1, w_o, ln_g, ln_b, w_r, b_r)


def _gather_rows(idx_ref, first, count, src_hbm, dst_ref, sem):
    for r in range(count):
        src = pl.ds(pl.multiple_of(idx_ref[first + r], LANE_BLOCKS), LANE_BLOCKS)
        pltpu.make_async_copy(src_hbm.at[src], dst_ref.at[pl.ds(r * LANE_BLOCKS, LANE_BLOCKS)], sem).start(
            priority=r % 2)


def _wait_rows(count, src_hbm, dst_ref, sem):
    pltpu.make_async_copy(src_hbm.at[pl.ds(0, count * LANE_BLOCKS)], dst_ref, sem).wait()


def _moe_kernel(bexp_ref, nused_ref, tok_ref, x_hbm, wgu_ref, bgu_ref, wd_ref, bd_ref,
                y_ref, xbuf, wgu_bf, wd_bf, sem):
    i = pl.program_id(0)
    slot = i % MOE_SLOTS
    ahead = MOE_SLOTS - 1
    n_used = nused_ref[0]

    @pl.when(i == 0)
    def _():
        for b in range(ahead):
            _gather_rows(tok_ref, b * TM_MOE, TM_MOE, x_hbm, xbuf.at[b], sem.at[b])

    @pl.when(i < n_used + ahead)
    def _():
        _wait_rows(TM_MOE, x_hbm, xbuf.at[slot], sem.at[slot])

    @pl.when((i < n_used) & ((i == 0) | (bexp_ref[i] != bexp_ref[jnp.maximum(i - 1, 0)])))
    def _():
        wgu_bf[...] = wgu_ref[0].astype(BF16)
        wd_bf[...] = wd_ref[0].astype(BF16)

    @pl.when(i < n_used)
    def _():
        x = _load_token_major(xbuf.at[slot], 0, TM_MOE).astype(BF16)
        nxt = (i + ahead) % MOE_SLOTS
        _gather_rows(tok_ref, (i + ahead) * TM_MOE, TM_MOE, x_hbm, xbuf.at[nxt], sem.at[nxt])
        gu = jnp.dot(x, wgu_bf[...], preferred_element_type=F32) + bgu_ref[0]
        gate = jnp.minimum(gu[:, :D_FF], SWIGLU_LIMIT)
        up = jnp.clip(gu[:, D_FF:], -SWIGLU_LIMIT, SWIGLU_LIMIT)
        glu = gate * jax.nn.sigmoid(gate * SWIGLU_ALPHA)
        hid = ((up + 1.0) * glu).astype(BF16)
        _store_token_major(y_ref, jnp.dot(hid, wd_bf[...], preferred_element_type=F32) + bd_ref[0])

    @pl.when(i >= n_used)
    def _():
        y_ref[...] = jnp.zeros_like(y_ref)


def _moe_experts(block_expert, n_used, row_token, x2, w_gu, b_gu, w_d, b_d):
    nb = block_expert.shape[0]
    grid_spec = pltpu.PrefetchScalarGridSpec(
        num_scalar_prefetch=3,
        grid=(nb,),
        in_specs=[pl.BlockSpec(memory_space=pl.ANY),
                  pl.BlockSpec((1, D_MODEL, 2 * D_FF), lambda i, be, nu, tok: (be[i], 0, 0)),
                  pl.BlockSpec((1, 1, 2 * D_FF), lambda i, be, nu, tok: (be[i], 0, 0)),
                  pl.BlockSpec((1, D_FF, D_MODEL), lambda i, be, nu, tok: (be[i], 0, 0)),
                  pl.BlockSpec((1, 1, D_MODEL), lambda i, be, nu, tok: (be[i], 0, 0))],
        out_specs=pl.BlockSpec((TM_MOE * LANE_BLOCKS, LANES), lambda i, be, nu, tok: (i, 0)),
        scratch_shapes=[pltpu.VMEM((MOE_SLOTS, TM_MOE * LANE_BLOCKS, LANES), F32),
                        pltpu.VMEM((D_MODEL, 2 * D_FF), BF16), pltpu.VMEM((D_FF, D_MODEL), BF16),
                        pltpu.SemaphoreType.DMA((MOE_SLOTS,))],
    )
    return pl.pallas_call(
        _moe_kernel,
        grid_spec=grid_spec,
        out_shape=jax.ShapeDtypeStruct((nb * TM_MOE * LANE_BLOCKS, LANES), F32),
        compiler_params=pltpu.CompilerParams(dimension_semantics=("arbitrary",),
                                             vmem_limit_bytes=VMEM_LIMIT_MOE),
        name="moe_experts",
    )(block_expert, n_used, row_token * LANE_BLOCKS, x2, w_gu, b_gu, w_d, b_d)


def _combine_kernel(idx_ref, y_hbm, x2_ref, gate_ref, lng_ref, lnb_ref, o_ref, ybuf, sem):
    i = pl.program_id(0)
    nt = pl.num_programs(0)
    slot = i % COMB_SLOTS
    ahead = COMB_SLOTS - 1
    rows = TOP_K * TM_COMB

    @pl.when(i == 0)
    def _():
        for t in range(ahead):
            _gather_rows(idx_ref, t * rows, rows, y_hbm, ybuf.at[t], sem.at[t])

    _wait_rows(rows, y_hbm, ybuf.at[slot], sem.at[slot])

    def compute():
        g = gate_ref[...]
        ff = jnp.zeros((TM_COMB, D_MODEL), F32)
        for kk in range(TOP_K):
            ff = ff + g[:, kk:kk + 1] * _load_token_major(ybuf.at[slot], kk * TM_COMB, TM_COMB)
        x2 = _load_token_major(x2_ref, 0, TM_COMB)
        o_ref[...] = _layer_norm(DEEPNORM_ALPHA * x2 + ff, lng_ref[...], lnb_ref[...])

    @pl.when(i + ahead < nt)
    def _():
        nxt = (i + ahead) % COMB_SLOTS
        _gather_rows(idx_ref, (i + ahead) * rows, rows, y_hbm, ybuf.at[nxt], sem.at[nxt])
        compute()

    @pl.when(i + ahead >= nt)
    def _():
        compute()


def _combine(dest, y_rows, x2, gates, ln_g, ln_b):
    n = x2.shape[0] // LANE_BLOCKS
    nt = n // TM_COMB
    rows = TOP_K * TM_COMB
    idx = (dest * LANE_BLOCKS).reshape(nt, TM_COMB, TOP_K).transpose(0, 2, 1).reshape(nt * rows)
    row = lambda width: pl.BlockSpec((TM_COMB, width), lambda i, idx: (i, 0))
    full = lambda r, c: pl.BlockSpec((r, c), lambda i, idx: (0, 0))
    grid_spec = pltpu.PrefetchScalarGridSpec(
        num_scalar_prefetch=1,
        grid=(nt,),
        in_specs=[pl.BlockSpec(memory_space=pl.ANY),
                  pl.BlockSpec((TM_COMB * LANE_BLOCKS, LANES), lambda i, idx: (i, 0)), row(LANES),
                  full(1, D_MODEL), full(1, D_MODEL)],
        out_specs=row(D_MODEL),
        scratch_shapes=[pltpu.VMEM((COMB_SLOTS, rows * LANE_BLOCKS, LANES), F32),
                        pltpu.SemaphoreType.DMA((COMB_SLOTS,))],
    )
    return pl.pallas_call(
        _combine_kernel,
        grid_spec=grid_spec,
        out_shape=jax.ShapeDtypeStruct((n, D_MODEL), F32),
        compiler_params=_params(1),
        name="combine",
    )(idx, y_rows, x2, gates, ln_g, ln_b)


def _rank_kernel(idx_ref, dest_ref, counts_ref, running):
    p = pl.program_id(0)
    j = pl.program_id(1)
    t = TM_RANK
    idx = idx_ref[...]
    e_iota = lax.broadcasted_iota(jnp.int32, (t, LANES), 1)
    hot = [idx[:, k:k + 1] == e_iota for k in range(TOP_K)]
    chosen = sum(jnp.where(h, 1.0, 0.0) for h in hot)
    tile_counts = jnp.sum(chosen, axis=0, keepdims=True)

    @pl.when((p == 0) & (j == 0))
    def _():
        running[...] = jnp.zeros_like(running)

    @pl.when(p == 0)
    def _():
        running[...] += tile_counts
        dest_ref[...] = jnp.zeros_like(dest_ref)

    @pl.when((p == 1) & (j == 0))
    def _():
        counts = running[...]
        counts_ref[...] = counts
        padded = jnp.ceil(counts * (1.0 / TM_MOE)) * TM_MOE
        lane = lax.broadcasted_iota(jnp.int32, (1, LANES), 1)
        scan = padded
        shift = 1
        while shift < N_EXPERTS:
            scan = scan + jnp.where(lane >= shift, pltpu.roll(scan, shift, 1), 0.0)
            shift *= 2
        running[...] = scan - padded

    @pl.when(p == 1)
    def _():
        r_iota = lax.broadcasted_iota(jnp.int32, (t, t), 0)
        c_iota = lax.broadcasted_iota(jnp.int32, (t, t), 1)
        earlier = jnp.where(c_iota < r_iota, 1.0, 0.0).astype(BF16)
        base = jnp.dot(earlier, chosen.astype(BF16), preferred_element_type=F32) + running[...]
        out = jnp.zeros((t, LANES), jnp.int32)
        for k in range(TOP_K):
            row = jnp.sum(jnp.where(hot[k], base, 0.0), axis=-1, keepdims=True).astype(jnp.int32)
            out = jnp.where(e_iota == k, row, out)
        dest_ref[...] = out
        running[...] += tile_counts


def _row_token_kernel(dest_ref, gap_ref, tok_ref):
    unroll = 8

    def clear(row, carry):
        tok_ref[row] = 0
        return carry

    def place(c, carry):
        rows = [dest_ref[c * unroll + u] for u in range(unroll)]
        for u in range(unroll):
            tok_ref[rows[u]] = c * (unroll // TOP_K) + u // TOP_K
        return carry

    for g in range(gap_ref.shape[1]):
        lax.fori_loop(gap_ref[0, g], gap_ref[1, g], clear, 0)
    lax.fori_loop(0, dest_ref.shape[0] // unroll, place, 0)


def _dispatch_plan(top_idx, n):
    n_assign = n * TOP_K
    nb = n_assign // TM_MOE + N_EXPERTS + MOE_SLOTS - 2
    dest, counts = pl.pallas_call(
        _rank_kernel,
        grid=(2, n // TM_RANK),
        in_specs=[pl.BlockSpec((TM_RANK, LANES), lambda p, j: (j, 0))],
        out_specs=[pl.BlockSpec((TM_RANK, LANES), lambda p, j: (p * j, 0)),
                   pl.BlockSpec((1, LANES), lambda p, j: (0, 0))],
        out_shape=[jax.ShapeDtypeStruct((n, LANES), jnp.int32), jax.ShapeDtypeStruct((1, LANES), F32)],
        scratch_shapes=[pltpu.VMEM((1, LANES), F32)],
        compiler_params=_params(2),
        name="expert_rank",
    )(top_idx)
    dest = dest[:, :TOP_K].reshape(-1)
    counts = counts[0, :N_EXPERTS].astype(jnp.int32)
    padded = (counts + TM_MOE - 1) // TM_MOE * TM_MOE
    pad_end = jnp.cumsum(padded)
    tail = jnp.stack([pad_end[-1], jnp.int32(nb * TM_MOE)])[:, None]
    gaps = jnp.concatenate([jnp.stack([pad_end - padded + counts, pad_end]), tail], axis=1)
    row_token = pl.pallas_call(
        _row_token_kernel,
        in_specs=[pl.BlockSpec(memory_space=pltpu.SMEM), pl.BlockSpec(memory_space=pltpu.SMEM)],
        out_specs=pl.BlockSpec(memory_space=pltpu.SMEM),
        out_shape=jax.ShapeDtypeStruct((nb * TM_MOE,), jnp.int32),
        name="row_token",
    )(dest, gaps)
    block_first_row = jnp.arange(nb, dtype=jnp.int32) * TM_MOE
    block_expert = jnp.minimum(jnp.sum(pad_end[None, :] <= block_first_row[:, None], axis=1),
                               N_EXPERTS - 1).astype(jnp.int32)
    n_used = (pad_end[-1:] // TM_MOE).astype(jnp.int32)
    return dest, row_token, block_expert, n_used


def kernel(x, mem, w_in, rel_bias, g_group_a, g_group_b, w_out, w_q_mem, w_kv_mem, w_o_mem, w_router, b_router, w_gate_up, b_gate_up, w_down, b_down, ln_g, ln_b):
    b, s, d = x.shape
    n = b * s
    depth = w_in.shape[0]
    xf = x.reshape(n, d)
    for l in range(depth):
        q_scale = jnp.ones((3 * MIX_WIDTH,), F32).at[:WIDTH_A].set(HEAD_DIM ** -0.5)
        q_scale = q_scale.at[3 * WIDTH_A:3 * WIDTH_A + WIDTH_B].set(HEAD_DIM ** -0.5)
        proj = _matmul(xf, (w_in[l] * q_scale).astype(BF16), BF16, TM_TOK, 3 * MIX_WIDTH).reshape(b, s, 3 * MIX_WIDTH)
        out_a = _chunk_attention(proj, _chunk_bias(rel_bias[l]), b, s).reshape(n, WIDTH_A)
        out_b = _sb_attention(proj, b, s).reshape(n, WIDTH_B)
        x1, qm = _mixer_out(out_a, out_b, xf, g_group_a[l][None], g_group_b[l][None],
                            w_out[l].astype(BF16), ln_g[l, 0][None], ln_b[l, 0][None],
                            w_q_mem[l].astype(BF16))
        mem_len = mem.shape[1]
        kv = _matmul(mem.reshape(b * mem_len, d), w_kv_mem[l].astype(BF16), BF16, b * mem_len, 1024)
        x2, top_idx, gates = _mem_attn(qm, kv.reshape(b, mem_len, 2 * d), x1, w_o_mem[l].astype(BF16),
                                       ln_g[l, 1][None], ln_b[l, 1][None],
                                       w_router[l].astype(BF16), b_router[l][None], b, s)
        dest, row_token, block_expert, n_used = _dispatch_plan(top_idx, n)
        y_rows = _moe_experts(block_expert, n_used, row_token, x2,
                              w_gate_up[l], b_gate_up[l][:, None, :], w_down[l], b_down[l][:, None, :])
        xf = _combine(dest, y_rows, x2, gates, ln_g[l, 2][None], ln_b[l, 2][None])
    return xf.reshape(b, s, d)
```

```python
import functools

import jax
import jax.numpy as jnp
from jax import lax
from jax.experimental import pallas as pl
from jax.experimental.pallas import tpu as pltpu

D_MODEL = 1024
CHUNK = 64
LEFT_CHUNKS = 8
LEFT = LEFT_CHUNKS * CHUNK
HEAD_DIM = 64
N_HEADS_A = 8
N_HEADS_B = 8
WIDTH_A = N_HEADS_A * HEAD_DIM
WIDTH_B = N_HEADS_B * HEAD_DIM
MIX_WIDTH = WIDTH_A + WIDTH_B
REL_CLIP = 128
N_HEADS_MEM = 4
HEAD_DIM_MEM = D_MODEL // N_HEADS_MEM
N_EXPERTS = 32
TOP_K = 4
D_FF = D_MODEL
SWIGLU_LIMIT = 7.0
SWIGLU_ALPHA = 1.702
LN_EPS = 1e-5
RMS_EPS = 1e-6
DEEPNORM_ALPHA = 2.0 ** 0.25
NEG_INF = -1e30
LOG2E = 1.4426950408889634

LANES = 128
LANE_BLOCKS = D_MODEL // LANES
VMEM_LIMIT = 48 * 1024 * 1024
VMEM_LIMIT_MOE = 56 * 1024 * 1024

TQ_A = 512
SUB_A = 256
HEADS_A_STEP = 4
TB_SB = 256
SB_TILES = 4
SB_UNDERFLOW = 110.0
TM_TOK = 512
TM_MOE = 256
MOE_SLOTS = 3
TM_COMB = 256
COMB_SLOTS = 3
TM_RANK = 512

F32 = jnp.float32
BF16 = jnp.bfloat16
_NT = (((1,), (1,)), ((), ()))


def _params(n_axes):
    return pltpu.CompilerParams(dimension_semantics=("arbitrary",) * n_axes,
                                vmem_limit_bytes=VMEM_LIMIT)


def _store_token_major(ref, value):
    rows = value.shape[0]
    for c in range(LANE_BLOCKS):
        ref[pl.ds(c, rows, stride=LANE_BLOCKS), :] = value[:, c * LANES:(c + 1) * LANES]


def _load_token_major(ref, first_row, rows):
    return jnp.concatenate(
        [ref[pl.ds(first_row * LANE_BLOCKS + c, rows, stride=LANE_BLOCKS), :] for c in range(LANE_BLOCKS)],
        axis=1)


def _layer_norm(r, g, b):
    mu = jnp.mean(r, axis=-1, keepdims=True)
    d = r - mu
    var = jnp.mean(d * d, axis=-1, keepdims=True)
    return d * lax.rsqrt(var + LN_EPS) * g + b


def _matmul_kernel(x_ref, w_ref, o_ref):
    o_ref[...] = jnp.dot(x_ref[...].astype(BF16), w_ref[...],
                         preferred_element_type=F32).astype(o_ref.dtype)


def _matmul(x, w, out_dtype, tm, tn):
    m, k = x.shape
    n = w.shape[1]
    return pl.pallas_call(
        _matmul_kernel,
        grid=(m // tm, n // tn),
        in_specs=[pl.BlockSpec((tm, k), lambda i, j: (i, 0)),
                  pl.BlockSpec((k, tn), lambda i, j: (0, j))],
        out_specs=pl.BlockSpec((tm, tn), lambda i, j: (i, j)),
        out_shape=jax.ShapeDtypeStruct((m, n), out_dtype),
        compiler_params=_params(2),
        name="matmul",
    )(x, w)


def _chunk_attn_kernel(q_ref, kp_ref, kc_ref, vp_ref, vc_ref, bias0_ref, bias1_ref, o_ref):
    w = SUB_A + LEFT
    for h in range(HEADS_A_STEP):
        sl = slice(h * HEAD_DIM, (h + 1) * HEAD_DIM)
        k = jnp.concatenate([kp_ref[:, sl], kc_ref[:, sl]], axis=0)
        v = jnp.concatenate([vp_ref[:, sl], vc_ref[:, sl]], axis=0)
        for sub, bias_ref in enumerate((bias0_ref, bias1_ref)):
            rows = slice(sub * SUB_A, (sub + 1) * SUB_A)
            keys = slice(sub * SUB_A, sub * SUB_A + w)
            s = lax.dot_general(q_ref[rows, sl], k[keys], _NT, preferred_element_type=F32) + bias_ref[h]
            m = jnp.max(s, axis=-1, keepdims=True)
            p = jnp.exp(s - m)
            l = jnp.sum(p, axis=-1, keepdims=True)
            o = jnp.dot(p.astype(BF16), v[keys], preferred_element_type=F32) / l
            o_ref[rows, sl] = o.astype(o_ref.dtype)


def _chunk_bias(rel_bias):
    w = SUB_A + LEFT
    heads = rel_bias.shape[0]
    m = jnp.arange(2 * w)
    rel = jnp.where(m < w, LEFT - m, LEFT + 2 * w - m)
    diag = rel_bias[:, jnp.clip(rel, -REL_CLIP, REL_CLIP) + REL_CLIP].astype(F32)[:, None, :]
    n_tables = 1 + TQ_A // SUB_A
    return pl.pallas_call(
        _chunk_bias_kernel,
        grid=(heads,),
        in_specs=[pl.BlockSpec((1, 1, 2 * w), lambda h: (h, 0, 0))],
        out_specs=pl.BlockSpec((n_tables, 1, SUB_A, w), lambda h: (0, h, 0, 0)),
        out_shape=jax.ShapeDtypeStruct((n_tables, heads, SUB_A, w), F32),
        compiler_params=_params(1),
        name="chunk_bias",
    )(diag)


def _chunk_bias_kernel(diag_ref, o_ref):
    w = SUB_A + LEFT
    rolled = pltpu.roll(jnp.broadcast_to(diag_ref[0], (SUB_A, 2 * w)), 0, 1, stride=1, stride_axis=0)
    col = lax.broadcasted_iota(jnp.int32, (SUB_A, w), 1)
    qc = lax.broadcasted_iota(jnp.int32, (SUB_A, w), 0) // CHUNK
    kc = col // CHUNK
    band = (kc >= qc) & (kc <= qc + LEFT_CHUNKS)
    table = jnp.where(band, rolled[:, :w], NEG_INF)
    o_ref[0, 0] = table
    for sub in range(TQ_A // SUB_A):
        o_ref[1 + sub, 0] = jnp.where(col >= LEFT - sub * SUB_A, table, NEG_INF)


def _chunk_attention(proj, bias, b, s):
    nq = s // TQ_A
    width = HEADS_A_STEP * HEAD_DIM
    groups = WIDTH_A // width
    blk = lambda off, prev: pl.BlockSpec(
        (None, TQ_A, width),
        (lambda bi, p, i: (bi, jnp.maximum(i - 1, 0), off + p)) if prev
        else (lambda bi, p, i: (bi, i, off + p)))
    table = lambda sub: pl.BlockSpec((None, HEADS_A_STEP, SUB_A, SUB_A + LEFT),
                                     lambda bi, p, i: (jnp.where(i == 0, 1 + sub, 0), p, 0, 0))
    return pl.pallas_call(
        _chunk_attn_kernel,
        grid=(b, groups, nq),
        in_specs=[blk(0, False),
                  blk(groups, True), blk(groups, False),
                  blk(2 * groups, True), blk(2 * groups, False),
                  table(0), table(1)],
        out_specs=pl.BlockSpec((None, TQ_A, width), lambda bi, p, i: (bi, i, p)),
        out_shape=jax.ShapeDtypeStruct((b, s, WIDTH_A), BF16),
        compiler_params=_params(3),
        name="chunk_attn",
    )(proj, proj, proj, proj, proj, bias, bias)


def _sb_block(q, k_ref, v_ref, sl, j, later, tri, causal):
    t = TB_SB
    start = pl.multiple_of(j * t, t)
    k = k_ref[pl.ds(start, t), sl]
    v = v_ref[pl.ds(start, t), sl]
    z = lax.dot_general(q, k, _NT, preferred_element_type=F32)
    sp = jnp.maximum(z, 0.0) + jnp.log(1.0 + jnp.exp2(jnp.abs(z) * (-LOG2E)))
    if causal is not None:
        sp = jnp.where(causal, sp, 0.0)
    csum = jnp.dot(sp.astype(BF16), tri, preferred_element_type=F32) + later
    wgt = jnp.exp(z - csum)
    if causal is not None:
        wgt = jnp.where(causal, wgt, 0.0)
    pv = jnp.dot(wgt.astype(BF16), v, preferred_element_type=F32)
    return pv, csum[:, 0:1]


def _sb_attn_kernel(q_ref, k_ref, v_ref, o_ref):
    first = pl.program_id(2) * SB_TILES
    t = TB_SB
    row = lax.broadcasted_iota(jnp.int32, (t, t), 0)
    col = lax.broadcasted_iota(jnp.int32, (t, t), 1)
    tri = jnp.where(row >= col, 1.0, 0.0).astype(BF16)
    causal = col < row
    chains = [(u, h) for u in range(SB_TILES) for h in range(2)]
    rows = lambda u: slice(u * t, (u + 1) * t)
    lanes = lambda h: slice(h * HEAD_DIM, (h + 1) * HEAD_DIM)
    qs = [q_ref[rows(u), lanes(h)] for u, h in chains]

    def block(c, j, later, masked):
        return _sb_block(qs[c], k_ref, v_ref, lanes(chains[c][1]), j, later, tri, causal if masked else None)

    carry = []
    for c, (u, h) in enumerate(chains):
        tile = first + u
        pv, later = block(c, tile, jnp.zeros((t, 1), F32), True)
        pv_prev, later_prev = block(c, jnp.maximum(tile - 1, 0), later, False)
        has_prev = tile >= 1
        carry += [jnp.where(has_prev, later_prev, later), pv + jnp.where(has_prev, pv_prev, 0.0)]

    def pending(c):
        todo = False
        for ci, (u, h) in enumerate(chains):
            todo = todo | ((first + u - 2 - c[0] >= 0) & (jnp.min(c[1 + 2 * ci]) < SB_UNDERFLOW))
        return todo

    def step(c):
        out = [c[0] + 1]
        for ci, (u, h) in enumerate(chains):
            j = first + u - 2 - c[0]
            pv, later = block(ci, jnp.maximum(j, 0), c[1 + 2 * ci], False)
            out += [jnp.where(j >= 0, later, c[1 + 2 * ci]), c[2 + 2 * ci] + jnp.where(j >= 0, pv, 0.0)]
        return tuple(out)

    carry = lax.while_loop(pending, step, (jnp.int32(0), *carry))
    for ci, (u, h) in enumerate(chains):
        o_ref[rows(u), lanes(h)] = carry[2 + 2 * ci].astype(o_ref.dtype)


def _sb_attention(proj, b, s):
    tq = SB_TILES * TB_SB
    nq = s // tq
    pairs = WIDTH_B // LANES
    base = 3 * WIDTH_A // LANES
    return pl.pallas_call(
        _sb_attn_kernel,
        grid=(b, pairs, nq),
        in_specs=[pl.BlockSpec((None, tq, LANES), lambda bi, p, i: (bi, i, base + p)),
                  pl.BlockSpec((None, s, LANES), lambda bi, p, i: (bi, 0, base + pairs + p)),
                  pl.BlockSpec((None, s, LANES), lambda bi, p, i: (bi, 0, base + 2 * pairs + p))],
        out_specs=pl.BlockSpec((None, tq, LANES), lambda bi, p, i: (bi, i, p)),
        out_shape=jax.ShapeDtypeStruct((b, s, WIDTH_B), BF16),
        compiler_params=_params(3),
        name="sb_attn",
    )(proj, proj, proj)


def _mixer_out_kernel(oa_ref, ob_ref, x_ref, ga_ref, gb_ref, wout_ref, lng_ref, lnb_ref, wq_ref,
                      x1_ref, qm_ref):
    def rms(ref, g_ref):
        a = ref[...].astype(F32)
        return (a * lax.rsqrt(jnp.mean(a * a, axis=-1, keepdims=True) + RMS_EPS) * g_ref[...]).astype(BF16)

    y = jnp.dot(rms(oa_ref, ga_ref), wout_ref[:WIDTH_A, :], preferred_element_type=F32)
    y = y + jnp.dot(rms(ob_ref, gb_ref), wout_ref[WIDTH_A:, :], preferred_element_type=F32)
    x1 = _layer_norm(DEEPNORM_ALPHA * x_ref[...] + y, lng_ref[...], lnb_ref[...])
    x1_ref[...] = x1
    qm_ref[...] = jnp.dot(x1.astype(BF16), wq_ref[...], preferred_element_type=F32).astype(BF16)


def _mixer_out(out_a, out_b, x, g_a, g_b, w_out, ln_g, ln_b, w_q):
    n = x.shape[0]
    tm = TM_TOK
    row = lambda width: pl.BlockSpec((tm, width), lambda i: (i, 0))
    full = lambda r, c: pl.BlockSpec((r, c), lambda i: (0, 0))
    return pl.pallas_call(
        _mixer_out_kernel,
        grid=(n // tm,),
        in_specs=[row(WIDTH_A), row(WIDTH_B), row(D_MODEL), full(1, WIDTH_A), full(1, WIDTH_B),
                  full(MIX_WIDTH, D_MODEL), full(1, D_MODEL), full(1, D_MODEL), full(D_MODEL, D_MODEL)],
        out_specs=[row(D_MODEL), row(D_MODEL)],
        out_shape=[jax.ShapeDtypeStruct((n, D_MODEL), F32), jax.ShapeDtypeStruct((n, D_MODEL), BF16)],
        compiler_params=_params(1),
        name="mixer_out",
    )(out_a, out_b, x, g_a, g_b, w_out, ln_g, ln_b, w_q)


def _mem_attn_kernel(q_ref, kv_ref, x1_ref, wo_ref, lng_ref, lnb_ref, wr_ref, br_ref,
                     x2_ref, idx_ref, gate_ref):
    heads = []
    for h in range(N_HEADS_MEM):
        sl = slice(h * HEAD_DIM_MEM, (h + 1) * HEAD_DIM_MEM)
        q = q_ref[:, sl] * (HEAD_DIM_MEM ** -0.5)
        k = kv_ref[:, sl]
        v = kv_ref[:, D_MODEL + h * HEAD_DIM_MEM:D_MODEL + (h + 1) * HEAD_DIM_MEM]
        s = lax.dot_general(q, k, _NT, preferred_element_type=F32)
        m = jnp.max(s, axis=-1, keepdims=True)
        p = jnp.exp(s - m)
        l = jnp.sum(p, axis=-1, keepdims=True)
        heads.append((jnp.dot(p.astype(BF16), v, preferred_element_type=F32) / l).astype(BF16))
    o = jnp.concatenate(heads, axis=-1)
    y = jnp.dot(o, wo_ref[...], preferred_element_type=F32)
    x2 = _layer_norm(DEEPNORM_ALPHA * x1_ref[...] + y, lng_ref[...], lnb_ref[...])
    _store_token_major(x2_ref, x2)

    logits = jnp.dot(x2.astype(BF16), wr_ref[...], preferred_element_type=F32) + br_ref[...]
    tm = logits.shape[0]
    e_iota = lax.broadcasted_iota(jnp.int32, (tm, N_EXPERTS), 1)
    lane = lax.broadcasted_iota(jnp.int32, (tm, LANES), 1)
    idx_out = jnp.zeros((tm, LANES), jnp.int32)
    val_out = jnp.zeros((tm, LANES), F32)
    top = None
    denom = jnp.zeros((tm, 1), F32)
    for kk in range(TOP_K):
        m = jnp.max(logits, axis=-1, keepdims=True)
        sel = jnp.min(jnp.where(logits == m, e_iota, N_EXPERTS), axis=-1, keepdims=True)
        if top is None:
            top = m
        e = jnp.exp(m - top)
        denom = denom + e
        idx_out = jnp.where(lane == kk, sel, idx_out)
        val_out = jnp.where(lane == kk, e, val_out)
        logits = jnp.where(e_iota == sel, -jnp.inf, logits)
    idx_ref[...] = idx_out
    gate_ref[...] = val_out / denom


def _mem_attn(qm, kv, x1, w_o, ln_g, ln_b, w_r, b_r, b, s):
    tm = TM_TOK
    nt = s // tm
    mem_len = kv.shape[1]
    row = lambda width: pl.BlockSpec((tm, width), lambda bi, i: (bi * nt + i, 0))
    full = lambda r, c: pl.BlockSpec((r, c), lambda bi, i: (0, 0))
    n = b * s
    return pl.pallas_call(
        _mem_attn_kernel,
        grid=(b, nt),
        in_specs=[row(D_MODEL), pl.BlockSpec((None, mem_len, 2 * D_MODEL), lambda bi, i: (bi, 0, 0)),
                  row(D_MODEL), full(D_MODEL, D_MODEL), full(1, D_MODEL), full(1, D_MODEL),
                  full(D_MODEL, N_EXPERTS), full(1, N_EXPERTS)],
        out_specs=[pl.BlockSpec((tm * LANE_BLOCKS, LANES), lambda bi, i: (bi * nt + i, 0)), row(LANES), row(LANES)],
        out_shape=[jax.ShapeDtypeStruct((n * LANE_BLOCKS, LANES), F32),
                   jax.ShapeDtypeStruct((n, LANES), jnp.int32),
                   jax.ShapeDtypeStruct((n, LANES), F32)],
        compiler_params=_params(2),
        name="mem_attn",
    )(qm, kv, x1, w_o, ln_g, ln_b, w_r, b_r)


def _gather_rows(idx_ref, first, count, src_hbm, dst_ref, sem):
    for r in range(count):
        src = pl.ds(pl.multiple_of(idx_ref[first + r], LANE_BLOCKS), LANE_BLOCKS)
        pltpu.make_async_copy(src_hbm.at[src], dst_ref.at[pl.ds(r * LANE_BLOCKS, LANE_BLOCKS)], sem).start(
            priority=r % 2)


def _wait_rows(count, src_hbm, dst_ref, sem):
    pltpu.make_async_copy(src_hbm.at[pl.ds(0, count * LANE_BLOCKS)], dst_ref, sem).wait()


def _moe_kernel(bexp_ref, nused_ref, tok_ref, x_hbm, wgu_ref, bgu_ref, wd_ref, bd_ref,
                y_ref, xbuf, wgu_bf, wd_bf, sem):
    i = pl.program_id(0)
    slot = i % MOE_SLOTS
    ahead = MOE_SLOTS - 1
    n_used = nused_ref[0]

    @pl.when(i == 0)
    def _():
        for b in range(ahead):
            _gather_rows(tok_ref, b * TM_MOE, TM_MOE, x_hbm, xbuf.at[b], sem.at[b])

    @pl.when(i < n_used + ahead)
    def _():
        _wait_rows(TM_MOE, x_hbm, xbuf.at[slot], sem.at[slot])

    @pl.when((i < n_used) & ((i == 0) | (bexp_ref[i] != bexp_ref[jnp.maximum(i - 1, 0)])))
    def _():
        wgu_bf[...] = wgu_ref[0].astype(BF16)
        wd_bf[...] = wd_ref[0].astype(BF16)

    @pl.when(i < n_used)
    def _():
        x = _load_token_major(xbuf.at[slot], 0, TM_MOE).astype(BF16)
        nxt = (i + ahead) % MOE_SLOTS
        _gather_rows(tok_ref, (i + ahead) * TM_MOE, TM_MOE, x_hbm, xbuf.at[nxt], sem.at[nxt])
        gu = jnp.dot(x, wgu_bf[...], preferred_element_type=F32) + bgu_ref[0]
        gate = jnp.minimum(gu[:, :D_FF], SWIGLU_LIMIT)
        up = jnp.clip(gu[:, D_FF:], -SWIGLU_LIMIT, SWIGLU_LIMIT)
        glu = gate * jax.nn.sigmoid(gate * SWIGLU_ALPHA)
        hid = ((up + 1.0) * glu).astype(BF16)
        _store_token_major(y_ref, jnp.dot(hid, wd_bf[...], preferred_element_type=F32) + bd_ref[0])

    @pl.when(i >= n_used)
    def _():
        y_ref[...] = jnp.zeros_like(y_ref)


def _moe_experts(block_expert, n_used, row_token, x2, w_gu, b_gu, w_d, b_d):
    nb = block_expert.shape[0]
    grid_spec = pltpu.PrefetchScalarGridSpec(
        num_scalar_prefetch=3,
        grid=(nb,),
        in_specs=[pl.BlockSpec(memory_space=pl.ANY),
                  pl.BlockSpec((1, D_MODEL, 2 * D_FF), lambda i, be, nu, tok: (be[i], 0, 0)),
                  pl.BlockSpec((1, 1, 2 * D_FF), lambda i, be, nu, tok: (be[i], 0, 0)),
                  pl.BlockSpec((1, D_FF, D_MODEL), lambda i, be, nu, tok: (be[i], 0, 0)),
                  pl.BlockSpec((1, 1, D_MODEL), lambda i, be, nu, tok: (be[i], 0, 0))],
        out_specs=pl.BlockSpec((TM_MOE * LANE_BLOCKS, LANES), lambda i, be, nu, tok: (i, 0)),
        scratch_shapes=[pltpu.VMEM((MOE_SLOTS, TM_MOE * LANE_BLOCKS, LANES), F32),
                        pltpu.VMEM((D_MODEL, 2 * D_FF), BF16), pltpu.VMEM((D_FF, D_MODEL), BF16),
                        pltpu.SemaphoreType.DMA((MOE_SLOTS,))],
    )
    return pl.pallas_call(
        _moe_kernel,
        grid_spec=grid_spec,
        out_shape=jax.ShapeDtypeStruct((nb * TM_MOE * LANE_BLOCKS, LANES), F32),
        compiler_params=pltpu.CompilerParams(dimension_semantics=("arbitrary",),
                                             vmem_limit_bytes=VMEM_LIMIT_MOE),
        name="moe_experts",
    )(block_expert, n_used, row_token * LANE_BLOCKS, x2, w_gu, b_gu, w_d, b_d)


def _combine_kernel(idx_ref, y_hbm, x2_ref, gate_ref, lng_ref, lnb_ref, o_ref, *scratch):
    ybufs, sem = scratch[:COMB_SLOTS], scratch[COMB_SLOTS]
    i = pl.program_id(0)
    nt = pl.num_programs(0)
    ahead = COMB_SLOTS - 1
    rows = TOP_K * TM_COMB

    @pl.when(i == 0)
    def _():
        for t in range(ahead):
            _gather_rows(idx_ref, t * rows, rows, y_hbm, ybufs[t], sem.at[t])

    def compute(ybuf):
        g = gate_ref[...]
        ff = jnp.zeros((TM_COMB, D_MODEL), F32)
        for kk in range(TOP_K):
            ff = ff + g[:, kk:kk + 1] * _load_token_major(ybuf, kk * TM_COMB, TM_COMB)
        x2 = _load_token_major(x2_ref, 0, TM_COMB)
        o_ref[...] = _layer_norm(DEEPNORM_ALPHA * x2 + ff, lng_ref[...], lnb_ref[...])

    for k in range(COMB_SLOTS):
        nxt = (k + ahead) % COMB_SLOTS

        @pl.when((i % COMB_SLOTS == k) & (i + ahead < nt))
        def _():
            _wait_rows(rows, y_hbm, ybufs[k], sem.at[k])
            _gather_rows(idx_ref, (i + ahead) * rows, rows, y_hbm, ybufs[nxt], sem.at[nxt])
            compute(ybufs[k])

        @pl.when((i % COMB_SLOTS == k) & (i + ahead >= nt))
        def _():
            _wait_rows(rows, y_hbm, ybufs[k], sem.at[k])
            compute(ybufs[k])


def _combine(dest, y_rows, x2, gates, ln_g, ln_b):
    n = x2.shape[0] // LANE_BLOCKS
    nt = n // TM_COMB
    rows = TOP_K * TM_COMB
    idx = (dest * LANE_BLOCKS).reshape(nt, TM_COMB, TOP_K).transpose(0, 2, 1).reshape(nt * rows)
    row = lambda width: pl.BlockSpec((TM_COMB, width), lambda i, idx: (i, 0))
    full = lambda r, c: pl.BlockSpec((r, c), lambda i, idx: (0, 0))
    grid_spec = pltpu.PrefetchScalarGridSpec(
        num_scalar_prefetch=1,
        grid=(nt,),
        in_specs=[pl.BlockSpec(memory_space=pl.ANY),
                  pl.BlockSpec((TM_COMB * LANE_BLOCKS, LANES), lambda i, idx: (i, 0)), row(LANES),
                  full(1, D_MODEL), full(1, D_MODEL)],
        out_specs=row(D_MODEL),
        scratch_shapes=[pltpu.VMEM((rows * LANE_BLOCKS, LANES), F32)] * COMB_SLOTS
        + [pltpu.SemaphoreType.DMA((COMB_SLOTS,))],
    )
    return pl.pallas_call(
        _combine_kernel,
        grid_spec=grid_spec,
        out_shape=jax.ShapeDtypeStruct((n, D_MODEL), F32),
        compiler_params=_params(1),
        name="combine",
    )(idx, y_rows, x2, gates, ln_g, ln_b)


def _rank_kernel(idx_ref, dest_ref, counts_ref, running):
    p = pl.program_id(0)
    j = pl.program_id(1)
    t = TM_RANK
    idx = idx_ref[...]
    e_iota = lax.broadcasted_iota(jnp.int32, (t, LANES), 1)
    hot = [idx[:, k:k + 1] == e_iota for k in range(TOP_K)]
    chosen = sum(jnp.where(h, 1.0, 0.0) for h in hot)
    tile_counts = jnp.sum(chosen, axis=0, keepdims=True)

    @pl.when((p == 0) & (j == 0))
    def _():
        running[...] = jnp.zeros_like(running)

    @pl.when(p == 0)
    def _():
        running[...] += tile_counts
        dest_ref[...] = jnp.zeros_like(dest_ref)

    @pl.when((p == 1) & (j == 0))
    def _():
        counts = running[...]
        counts_ref[...] = counts
        padded = jnp.ceil(counts * (1.0 / TM_MOE)) * TM_MOE
        lane = lax.broadcasted_iota(jnp.int32, (1, LANES), 1)
        scan = padded
        shift = 1
        while shift < N_EXPERTS:
            scan = scan + jnp.where(lane >= shift, pltpu.roll(scan, shift, 1), 0.0)
            shift *= 2
        running[...] = scan - padded

    @pl.when(p == 1)
    def _():
        r_iota = lax.broadcasted_iota(jnp.int32, (t, t), 0)
        c_iota = lax.broadcasted_iota(jnp.int32, (t, t), 1)
        earlier = jnp.where(c_iota < r_iota, 1.0, 0.0).astype(BF16)
        base = jnp.dot(earlier, chosen.astype(BF16), preferred_element_type=F32) + running[...]
        out = jnp.zeros((t, LANES), jnp.int32)
        for k in range(TOP_K):
            row = jnp.sum(jnp.where(hot[k], base, 0.0), axis=-1, keepdims=True).astype(jnp.int32)
            out = jnp.where(e_iota == k, row, out)
        dest_ref[...] = out
        running[...] += tile_counts


def _row_token_kernel(dest_ref, gap_ref, tok_ref):
    unroll = 8

    def clear(row, carry):
        tok_ref[row] = 0
        return carry

    def place(c, carry):
        rows = [dest_ref[c * unroll + u] for u in range(unroll)]
        for u in range(unroll):
            tok_ref[rows[u]] = c * (unroll // TOP_K) + u // TOP_K
        return carry

    for g in range(gap_ref.shape[1]):
        lax.fori_loop(gap_ref[0, g], gap_ref[1, g], clear, 0)
    lax.fori_loop(0, dest_ref.shape[0] // unroll, place, 0)


def _dispatch_plan(top_idx, n):
    n_assign = n * TOP_K
    nb = n_assign // TM_MOE + N_EXPERTS + MOE_SLOTS - 2
    dest, counts = pl.pallas_call(
        _rank_kernel,
        grid=(2, n // TM_RANK),
        in_specs=[pl.BlockSpec((TM_RANK, LANES), lambda p, j: (j, 0))],
        out_specs=[pl.BlockSpec((TM_RANK, LANES), lambda p, j: (p * j, 0)),
                   pl.BlockSpec((1, LANES), lambda p, j: (0, 0))],
        out_shape=[jax.ShapeDtypeStruct((n, LANES), jnp.int32), jax.ShapeDtypeStruct((1, LANES), F32)],
        scratch_shapes=[pltpu.VMEM((1, LANES), F32)],
        compiler_params=_params(2),
        name="expert_rank",
    )(top_idx)
    dest = dest[:, :TOP_K].reshape(-1)
    counts = counts[0, :N_EXPERTS].astype(jnp.int32)
    padded = (counts + TM_MOE - 1) // TM_MOE * TM_MOE
    pad_end = jnp.cumsum(padded)
    tail = jnp.stack([pad_end[-1], jnp.int32(nb * TM_MOE)])[:, None]
    gaps = jnp.concatenate([jnp.stack([pad_end - padded + counts, pad_end]), tail], axis=1)
    row_token = pl.pallas_call(
        _row_token_kernel,
        in_specs=[pl.BlockSpec(memory_space=pltpu.SMEM), pl.BlockSpec(memory_space=pltpu.SMEM)],
        out_specs=pl.BlockSpec(memory_space=pltpu.SMEM),
        out_shape=jax.ShapeDtypeStruct((nb * TM_MOE,), jnp.int32),
        name="row_token",
    )(dest, gaps)
    block_first_row = jnp.arange(nb, dtype=jnp.int32) * TM_MOE
    block_expert = jnp.minimum(jnp.sum(pad_end[None, :] <= block_first_row[:, None], axis=1),
                               N_EXPERTS - 1).astype(jnp.int32)
    n_used = (pad_end[-1:] // TM_MOE).astype(jnp.int32)
    return dest, row_token, block_expert, n_used


def kernel(x, mem, w_in, rel_bias, g_group_a, g_group_b, w_out, w_q_mem, w_kv_mem, w_o_mem, w_router, b_router, w_gate_up, b_gate_up, w_down, b_down, ln_g, ln_b):
    b, s, d = x.shape
    n = b * s
    depth = w_in.shape[0]
    xf = x.reshape(n, d)
    for l in range(depth):
        q_scale = jnp.ones((3 * MIX_WIDTH,), F32).at[:WIDTH_A].set(HEAD_DIM ** -0.5)
        q_scale = q_scale.at[3 * WIDTH_A:3 * WIDTH_A + WIDTH_B].set(HEAD_DIM ** -0.5)
        proj = _matmul(xf, (w_in[l] * q_scale).astype(BF16), BF16, TM_TOK, 3 * MIX_WIDTH).reshape(b, s, 3 * MIX_WIDTH)
        out_a = _chunk_attention(proj, _chunk_bias(rel_bias[l]), b, s).reshape(n, WIDTH_A)
        out_b = _sb_attention(proj, b, s).reshape(n, WIDTH_B)
        x1, qm = _mixer_out(out_a, out_b, xf, g_group_a[l][None], g_group_b[l][None],
                            w_out[l].astype(BF16), ln_g[l, 0][None], ln_b[l, 0][None],
                            w_q_mem[l].astype(BF16))
        mem_len = mem.shape[1]
        kv = _matmul(mem.reshape(b * mem_len, d), w_kv_mem[l].astype(BF16), BF16, b * mem_len, 1024)
        x2, top_idx, gates = _mem_attn(qm, kv.reshape(b, mem_len, 2 * d), x1, w_o_mem[l].astype(BF16),
                                       ln_g[l, 1][None], ln_b[l, 1][None],
                                       w_router[l].astype(BF16), b_router[l][None], b, s)
        dest, row_token, block_expert, n_used = _dispatch_plan(top_idx, n)
        y_rows = _moe_experts(block_expert, n_used, row_token, x2,
                              w_gate_up[l], b_gate_up[l][:, None, :], w_down[l], b_down[l][:, None, :])
        xf = _combine(dest, y_rows, x2, gates, ln_g[l, 2][None], ln_b[l, 2][None])
    return xf.reshape(b, s, d)
```

```python
import functools

import jax
import jax.numpy as jnp
from jax import lax
from jax.experimental import pallas as pl
from jax.experimental.pallas import tpu as pltpu

D_MODEL = 1024
CHUNK = 64
LEFT_CHUNKS = 8
LEFT = LEFT_CHUNKS * CHUNK
HEAD_DIM = 64
N_HEADS_A = 8
N_HEADS_B = 8
WIDTH_A = N_HEADS_A * HEAD_DIM
WIDTH_B = N_HEADS_B * HEAD_DIM
MIX_WIDTH = WIDTH_A + WIDTH_B
REL_CLIP = 128
N_HEADS_MEM = 4
HEAD_DIM_MEM = D_MODEL // N_HEADS_MEM
N_EXPERTS = 32
TOP_K = 4
D_FF = D_MODEL
SWIGLU_LIMIT = 7.0
SWIGLU_ALPHA = 1.702
LN_EPS = 1e-5
RMS_EPS = 1e-6
DEEPNORM_ALPHA = 2.0 ** 0.25
NEG_INF = -1e30
LOG2E = 1.4426950408889634

LANES = 128
LANE_BLOCKS = D_MODEL // LANES
VMEM_LIMIT = 48 * 1024 * 1024
VMEM_LIMIT_MOE = 56 * 1024 * 1024

TQ_A = 512
SUB_A = 256
HEADS_A_STEP = 4
TB_SB = 256
SB_TILES = 4
SB_UNDERFLOW = 110.0
TM_TOK = 512
TM_MOE = 256
MOE_SLOTS = 3
TM_COMB = 256
COMB_SLOTS = 3
TM_RANK = 512

F32 = jnp.float32
BF16 = jnp.bfloat16
_NT = (((1,), (1,)), ((), ()))


def _params(n_axes):
    return pltpu.CompilerParams(dimension_semantics=("arbitrary",) * n_axes,
                                vmem_limit_bytes=VMEM_LIMIT)


def _store_token_major(ref, value):
    rows = value.shape[0]
    for c in range(LANE_BLOCKS):
        ref[pl.ds(c, rows, stride=LANE_BLOCKS), :] = value[:, c * LANES:(c + 1) * LANES]


def _load_token_major(ref, first_row, rows):
    return jnp.concatenate(
        [ref[pl.ds(first_row * LANE_BLOCKS + c, rows, stride=LANE_BLOCKS), :] for c in range(LANE_BLOCKS)],
        axis=1)


def _layer_norm(r, g, b):
    mu = jnp.mean(r, axis=-1, keepdims=True)
    d = r - mu
    var = jnp.mean(d * d, axis=-1, keepdims=True)
    return d * lax.rsqrt(var + LN_EPS) * g + b


def _matmul_kernel(x_ref, w_ref, o_ref):
    o_ref[...] = jnp.dot(x_ref[...].astype(BF16), w_ref[...],
                         preferred_element_type=F32).astype(o_ref.dtype)


def _matmul(x, w, out_dtype, tm, tn):
    m, k = x.shape
    n = w.shape[1]
    return pl.pallas_call(
        _matmul_kernel,
        grid=(m // tm, n // tn),
        in_specs=[pl.BlockSpec((tm, k), lambda i, j: (i, 0)),
                  pl.BlockSpec((k, tn), lambda i, j: (0, j))],
        out_specs=pl.BlockSpec((tm, tn), lambda i, j: (i, j)),
        out_shape=jax.ShapeDtypeStruct((m, n), out_dtype),
        compiler_params=_params(2),
        name="matmul",
    )(x, w)


def _chunk_attn_kernel(q_ref, kp_ref, kc_ref, vp_ref, vc_ref, bias0_ref, bias1_ref, o_ref):
    w = SUB_A + LEFT
    for h in range(HEADS_A_STEP):
        sl = slice(h * HEAD_DIM, (h + 1) * HEAD_DIM)
        k = jnp.concatenate([kp_ref[:, sl], kc_ref[:, sl]], axis=0)
        v = jnp.concatenate([vp_ref[:, sl], vc_ref[:, sl]], axis=0)
        for sub, bias_ref in enumerate((bias0_ref, bias1_ref)):
            rows = slice(sub * SUB_A, (sub + 1) * SUB_A)
            keys = slice(sub * SUB_A, sub * SUB_A + w)
            s = lax.dot_general(q_ref[rows, sl], k[keys], _NT, preferred_element_type=F32) + bias_ref[h]
            m = jnp.max(s, axis=-1, keepdims=True)
            p = jnp.exp(s - m)
            l = jnp.sum(p, axis=-1, keepdims=True)
            o = jnp.dot(p.astype(BF16), v[keys], preferred_element_type=F32) / l
            o_ref[rows, sl] = o.astype(o_ref.dtype)


def _chunk_bias(rel_bias):
    w = SUB_A + LEFT
    heads = rel_bias.shape[0]
    m = jnp.arange(2 * w)
    rel = jnp.where(m < w, LEFT - m, LEFT + 2 * w - m)
    diag = rel_bias[:, jnp.clip(rel, -REL_CLIP, REL_CLIP) + REL_CLIP].astype(F32)[:, None, :]
    n_tables = 1 + TQ_A // SUB_A
    return pl.pallas_call(
        _chunk_bias_kernel,
        grid=(heads,),
        in_specs=[pl.BlockSpec((1, 1, 2 * w), lambda h: (h, 0, 0))],
        out_specs=pl.BlockSpec((n_tables, 1, SUB_A, w), lambda h: (0, h, 0, 0)),
        out_shape=jax.ShapeDtypeStruct((n_tables, heads, SUB_A, w), F32),
        compiler_params=_params(1),
        name="chunk_bias",
    )(diag)


def _chunk_bias_kernel(diag_ref, o_ref):
    w = SUB_A + LEFT
    rolled = pltpu.roll(jnp.broadcast_to(diag_ref[0], (SUB_A, 2 * w)), 0, 1, stride=1, stride_axis=0)
    col = lax.broadcasted_iota(jnp.int32, (SUB_A, w), 1)
    qc = lax.broadcasted_iota(jnp.int32, (SUB_A, w), 0) // CHUNK
    kc = col // CHUNK
    band = (kc >= qc) & (kc <= qc + LEFT_CHUNKS)
    table = jnp.where(band, rolled[:, :w], NEG_INF)
    o_ref[0, 0] = table
    for sub in range(TQ_A // SUB_A):
        o_ref[1 + sub, 0] = jnp.where(col >= LEFT - sub * SUB_A, table, NEG_INF)


def _chunk_attention(proj, bias, b, s):
    nq = s // TQ_A
    width = HEADS_A_STEP * HEAD_DIM
    groups = WIDTH_A // width
    blk = lambda off, prev: pl.BlockSpec(
        (None, TQ_A, width),
        (lambda bi, p, i: (bi, jnp.maximum(i - 1, 0), off + p)) if prev
        else (lambda bi, p, i: (bi, i, off + p)))
    table = lambda sub: pl.BlockSpec((None, HEADS_A_STEP, SUB_A, SUB_A + LEFT),
                                     lambda bi, p, i: (jnp.where(i == 0, 1 + sub, 0), p, 0, 0))
    return pl.pallas_call(
        _chunk_attn_kernel,
        grid=(b, groups, nq),
        in_specs=[blk(0, False),
                  blk(groups, True), blk(groups, False),
                  blk(2 * groups, True), blk(2 * groups, False),
                  table(0), table(1)],
        out_specs=pl.BlockSpec((None, TQ_A, width), lambda bi, p, i: (bi, i, p)),
        out_shape=jax.ShapeDtypeStruct((b, s, WIDTH_A), BF16),
        compiler_params=_params(3),
        name="chunk_attn",
    )(proj, proj, proj, proj, proj, bias, bias)


def _sb_block(q, k_ref, v_ref, sl, j, later, tri, causal):
    t = TB_SB
    start = pl.multiple_of(j * t, t)
    k = k_ref[pl.ds(start, t), sl]
    v = v_ref[pl.ds(start, t), sl]
    z = lax.dot_general(q, k, _NT, preferred_element_type=F32)
    sp = jnp.maximum(z, 0.0) + jnp.log(1.0 + jnp.exp2(jnp.abs(z) * (-LOG2E)))
    if causal is not None:
        sp = jnp.where(causal, sp, 0.0)
    csum = jnp.dot(sp.astype(BF16), tri, preferred_element_type=F32) + later
    wgt = jnp.exp(z - csum)
    if causal is not None:
        wgt = jnp.where(causal, wgt, 0.0)
    pv = jnp.dot(wgt.astype(BF16), v, preferred_element_type=F32)
    return pv, csum[:, 0:1]


def _sb_attn_kernel(q_ref, k_ref, v_ref, o_ref):
    first = pl.program_id(2) * SB_TILES
    t = TB_SB
    row = lax.broadcasted_iota(jnp.int32, (t, t), 0)
    col = lax.broadcasted_iota(jnp.int32, (t, t), 1)
    tri = jnp.where(row >= col, 1.0, 0.0).astype(BF16)
    causal = col < row
    chains = [(u, h) for u in range(SB_TILES) for h in range(2)]
    rows = lambda u: slice(u * t, (u + 1) * t)
    lanes = lambda h: slice(h * HEAD_DIM, (h + 1) * HEAD_DIM)
    qs = [q_ref[rows(u), lanes(h)] for u, h in chains]

    def block(c, j, later, masked):
        return _sb_block(qs[c], k_ref, v_ref, lanes(chains[c][1]), j, later, tri, causal if masked else None)

    carry = []
    for c, (u, h) in enumerate(chains):
        tile = first + u
        pv, later = block(c, tile, jnp.zeros((t, 1), F32), True)
        pv_prev, later_prev = block(c, jnp.maximum(tile - 1, 0), later, False)
        has_prev = tile >= 1
        carry += [jnp.where(has_prev, later_prev, later), pv + jnp.where(has_prev, pv_prev, 0.0)]

    def pending(c):
        todo = False
        for ci, (u, h) in enumerate(chains):
            todo = todo | ((first + u - 2 - c[0] >= 0) & (jnp.min(c[1 + 2 * ci]) < SB_UNDERFLOW))
        return todo

    def step(c):
        out = [c[0] + 1]
        for ci, (u, h) in enumerate(chains):
            j = first + u - 2 - c[0]
            pv, later = block(ci, jnp.maximum(j, 0), c[1 + 2 * ci], False)
            out += [jnp.where(j >= 0, later, c[1 + 2 * ci]), c[2 + 2 * ci] + jnp.where(j >= 0, pv, 0.0)]
        return tuple(out)

    carry = lax.while_loop(pending, step, (jnp.int32(0), *carry))
    for ci, (u, h) in enumerate(chains):
        o_ref[rows(u), lanes(h)] = carry[2 + 2 * ci].astype(o_ref.dtype)


def _sb_attention(proj, b, s):
    tq = SB_TILES * TB_SB
    nq = s // tq
    pairs = WIDTH_B // LANES
    base = 3 * WIDTH_A // LANES
    return pl.pallas_call(
        _sb_attn_kernel,
        grid=(b, pairs, nq),
        in_specs=[pl.BlockSpec((None, tq, LANES), lambda bi, p, i: (bi, i, base + p)),
                  pl.BlockSpec((None, s, LANES), lambda bi, p, i: (bi, 0, base + pairs + p)),
                  pl.BlockSpec((None, s, LANES), lambda bi, p, i: (bi, 0, base + 2 * pairs + p))],
        out_specs=pl.BlockSpec((None, tq, LANES), lambda bi, p, i: (bi, i, p)),
        out_shape=jax.ShapeDtypeStruct((b, s, WIDTH_B), BF16),
        compiler_params=_params(3),
        name="sb_attn",
    )(proj, proj, proj)


def _mixer_out_kernel(oa_ref, ob_ref, x_ref, ga_ref, gb_ref, wout_ref, lng_ref, lnb_ref, wq_ref,
                      x1_ref, qm_ref):
    def rms(ref, g_ref):
        a = ref[...].astype(F32)
        return (a * lax.rsqrt(jnp.mean(a * a, axis=-1, keepdims=True) + RMS_EPS) * g_ref[...]).astype(BF16)

    y = jnp.dot(rms(oa_ref, ga_ref), wout_ref[:WIDTH_A, :], preferred_element_type=F32)
    y = y + jnp.dot(rms(ob_ref, gb_ref), wout_ref[WIDTH_A:, :], preferred_element_type=F32)
    x1 = _layer_norm(DEEPNORM_ALPHA * x_ref[...] + y, lng_ref[...], lnb_ref[...])
    x1_ref[...] = x1
    qm_ref[...] = jnp.dot(x1.astype(BF16), wq_ref[...], preferred_element_type=F32).astype(BF16)


def _mixer_out(out_a, out_b, x, g_a, g_b, w_out, ln_g, ln_b, w_q):
    n = x.shape[0]
    tm = TM_TOK
    row = lambda width: pl.BlockSpec((tm, width), lambda i: (i, 0))
    full = lambda r, c: pl.BlockSpec((r, c), lambda i: (0, 0))
    return pl.pallas_call(
        _mixer_out_kernel,
        grid=(n // tm,),
        in_specs=[row(WIDTH_A), row(WIDTH_B), row(D_MODEL), full(1, WIDTH_A), full(1, WIDTH_B),
                  full(MIX_WIDTH, D_MODEL), full(1, D_MODEL), full(1, D_MODEL), full(D_MODEL, D_MODEL)],
        out_specs=[row(D_MODEL), row(D_MODEL)],
        out_shape=[jax.ShapeDtypeStruct((n, D_MODEL), F32), jax.ShapeDtypeStruct((n, D_MODEL), BF16)],
        compiler_params=_params(1),
        name="mixer_out",
    )(out_a, out_b, x, g_a, g_b, w_out, ln_g, ln_b, w_q)


def _mem_attn_kernel(q_ref, kv_ref, x1_ref, wo_ref, lng_ref, lnb_ref, wr_ref, br_ref,
                     x2_ref, idx_ref, gate_ref):
    heads = []
    for h in range(N_HEADS_MEM):
        sl = slice(h * HEAD_DIM_MEM, (h + 1) * HEAD_DIM_MEM)
        q = q_ref[:, sl] * (HEAD_DIM_MEM ** -0.5)
        k = kv_ref[:, sl]
        v = kv_ref[:, D_MODEL + h * HEAD_DIM_MEM:D_MODEL + (h + 1) * HEAD_DIM_MEM]
        s = lax.dot_general(q, k, _NT, preferred_element_type=F32)
        m = jnp.max(s, axis=-1, keepdims=True)
        p = jnp.exp(s - m)
        l = jnp.sum(p, axis=-1, keepdims=True)
        heads.append((jnp.dot(p.astype(BF16), v, preferred_element_type=F32) / l).astype(BF16))
    o = jnp.concatenate(heads, axis=-1)
    y = jnp.dot(o, wo_ref[...], preferred_element_type=F32)
    x2 = _layer_norm(DEEPNORM_ALPHA * x1_ref[...] + y, lng_ref[...], lnb_ref[...])
    _store_token_major(x2_ref, x2)

    logits = jnp.dot(x2.astype(BF16), wr_ref[...], preferred_element_type=F32) + br_ref[...]
    tm = logits.shape[0]
    e_iota = lax.broadcasted_iota(jnp.int32, (tm, N_EXPERTS), 1)
    lane = lax.broadcasted_iota(jnp.int32, (tm, LANES), 1)
    idx_out = jnp.zeros((tm, LANES), jnp.int32)
    val_out = jnp.zeros((tm, LANES), F32)
    top = None
    denom = jnp.zeros((tm, 1), F32)
    for kk in range(TOP_K):
        m = jnp.max(logits, axis=-1, keepdims=True)
        sel = jnp.min(jnp.where(logits == m, e_iota, N_EXPERTS), axis=-1, keepdims=True)
        if top is None:
            top = m
        e = jnp.exp(m - top)
        denom = denom + e
        idx_out = jnp.where(lane == kk, sel, idx_out)
        val_out = jnp.where(lane == kk, e, val_out)
        logits = jnp.where(e_iota == sel, -jnp.inf, logits)
    idx_ref[...] = idx_out
    gate_ref[...] = val_out / denom


def _mem_attn(qm, kv, x1, w_o, ln_g, ln_b, w_r, b_r, b, s):
    tm = TM_TOK
    nt = s // tm
    mem_len = kv.shape[1]
    row = lambda width: pl.BlockSpec((tm, width), lambda bi, i: (bi * nt + i, 0))
    full = lambda r, c: pl.BlockSpec((r, c), lambda bi, i: (0, 0))
    n = b * s
    return pl.pallas_call(
        _mem_attn_kernel,
        grid=(b, nt),
        in_specs=[row(D_MODEL), pl.BlockSpec((None, mem_len, 2 * D_MODEL), lambda bi, i: (bi, 0, 0)),
                  row(D_MODEL), full(D_MODEL, D_MODEL), full(1, D_MODEL), full(1, D_MODEL),
                  full(D_MODEL, N_EXPERTS), full(1, N_EXPERTS)],
        out_specs=[pl.BlockSpec((tm * LANE_BLOCKS, LANES), lambda bi, i: (bi * nt + i, 0)), row(LANES), row(LANES)],
        out_shape=[jax.ShapeDtypeStruct((n * LANE_BLOCKS, LANES), F32),
                   jax.ShapeDtypeStruct((n, LANES), jnp.int32),
                   jax.ShapeDtypeStruct((n, LANES), F32)],
        compiler_params=_params(2),
        name="mem_attn",
    )(qm, kv, x1, w_o, ln_g, ln_b, w_r, b_r)


def _gather_rows(idx_ref, first, count, src_hbm, dst_ref, sem):
    for r in range(count):
        src = pl.ds(pl.multiple_of(idx_ref[first + r], LANE_BLOCKS), LANE_BLOCKS)
        pltpu.make_async_copy(src_hbm.at[src], dst_ref.at[pl.ds(r * LANE_BLOCKS, LANE_BLOCKS)], sem).start(
            priority=r % 2)


def _wait_rows(count, src_hbm, dst_ref, sem):
    pltpu.make_async_copy(src_hbm.at[pl.ds(0, count * LANE_BLOCKS)], dst_ref, sem).wait()


def _fill_row_table(dest_ref, gap_ref, tok_ref):
    unroll = 8

    def clear(row, carry):
        tok_ref[row] = 0
        return carry

    def place(c, carry):
        rows = [dest_ref[c * unroll + u] for u in range(unroll)]
        for u in range(unroll):
            tok_ref[rows[u]] = (c * (unroll // TOP_K) + u // TOP_K) * LANE_BLOCKS
        return carry

    for g in range(gap_ref.shape[1]):
        lax.fori_loop(gap_ref[0, g], gap_ref[1, g], clear, 0)
    lax.fori_loop(0, dest_ref.shape[0] // unroll, place, 0)


def _moe_kernel(bexp_ref, nused_ref, dest_ref, gap_ref, x_hbm, wgu_ref, bgu_ref, wd_ref, bd_ref,
                y_ref, xbuf, wgu_bf, wd_bf, tok_ref, sem):
    i = pl.program_id(0)
    slot = i % MOE_SLOTS
    ahead = MOE_SLOTS - 1
    n_used = nused_ref[0]

    @pl.when(i == 0)
    def _():
        _fill_row_table(dest_ref, gap_ref, tok_ref)
        for b in range(ahead):
            _gather_rows(tok_ref, b * TM_MOE, TM_MOE, x_hbm, xbuf.at[b], sem.at[b])

    @pl.when(i < n_used + ahead)
    def _():
        _wait_rows(TM_MOE, x_hbm, xbuf.at[slot], sem.at[slot])

    @pl.when((i < n_used) & ((i == 0) | (bexp_ref[i] != bexp_ref[jnp.maximum(i - 1, 0)])))
    def _():
        wgu_bf[...] = wgu_ref[0].astype(BF16)
        wd_bf[...] = wd_ref[0].astype(BF16)

    @pl.when(i < n_used)
    def _():
        x = _load_token_major(xbuf.at[slot], 0, TM_MOE).astype(BF16)
        nxt = (i + ahead) % MOE_SLOTS
        _gather_rows(tok_ref, (i + ahead) * TM_MOE, TM_MOE, x_hbm, xbuf.at[nxt], sem.at[nxt])
        gu = jnp.dot(x, wgu_bf[...], preferred_element_type=F32) + bgu_ref[0]
        gate = jnp.minimum(gu[:, :D_FF], SWIGLU_LIMIT)
        up = jnp.clip(gu[:, D_FF:], -SWIGLU_LIMIT, SWIGLU_LIMIT)
        glu = gate * jax.nn.sigmoid(gate * SWIGLU_ALPHA)
        hid = ((up + 1.0) * glu).astype(BF16)
        _store_token_major(y_ref, jnp.dot(hid, wd_bf[...], preferred_element_type=F32) + bd_ref[0])

    @pl.when(i >= n_used)
    def _():
        y_ref[...] = jnp.zeros_like(y_ref)


def _moe_experts(block_expert, n_used, dest, gaps, x2, w_gu, b_gu, w_d, b_d):
    nb = block_expert.shape[0]
    grid_spec = pltpu.PrefetchScalarGridSpec(
        num_scalar_prefetch=4,
        grid=(nb,),
        in_specs=[pl.BlockSpec(memory_space=pl.ANY),
                  pl.BlockSpec((1, D_MODEL, 2 * D_FF), lambda i, be, nu, de, ga: (be[i], 0, 0)),
                  pl.BlockSpec((1, 1, 2 * D_FF), lambda i, be, nu, de, ga: (be[i], 0, 0)),
                  pl.BlockSpec((1, D_FF, D_MODEL), lambda i, be, nu, de, ga: (be[i], 0, 0)),
                  pl.BlockSpec((1, 1, D_MODEL), lambda i, be, nu, de, ga: (be[i], 0, 0))],
        out_specs=pl.BlockSpec((TM_MOE * LANE_BLOCKS, LANES), lambda i, be, nu, de, ga: (i, 0)),
        scratch_shapes=[pltpu.VMEM((MOE_SLOTS, TM_MOE * LANE_BLOCKS, LANES), F32),
                        pltpu.VMEM((D_MODEL, 2 * D_FF), BF16), pltpu.VMEM((D_FF, D_MODEL), BF16),
                        pltpu.SMEM((nb * TM_MOE,), jnp.int32),
                        pltpu.SemaphoreType.DMA((MOE_SLOTS,))],
    )
    return pl.pallas_call(
        _moe_kernel,
        grid_spec=grid_spec,
        out_shape=jax.ShapeDtypeStruct((nb * TM_MOE * LANE_BLOCKS, LANES), F32),
        compiler_params=pltpu.CompilerParams(dimension_semantics=("arbitrary",),
                                             vmem_limit_bytes=VMEM_LIMIT_MOE),
        name="moe_experts",
    )(block_expert, n_used, dest, gaps, x2, w_gu, b_gu, w_d, b_d)


def _combine_kernel(idx_ref, y_hbm, x2_ref, gate_ref, lng_ref, lnb_ref, o_ref, *scratch):
    ybufs, sem = scratch[:COMB_SLOTS], scratch[COMB_SLOTS]
    i = pl.program_id(0)
    nt = pl.num_programs(0)
    ahead = COMB_SLOTS - 1
    rows = TOP_K * TM_COMB

    @pl.when(i == 0)
    def _():
        for t in range(ahead):
            _gather_rows(idx_ref, t * rows, rows, y_hbm, ybufs[t], sem.at[t])

    def compute(ybuf):
        g = gate_ref[...]
        ff = jnp.zeros((TM_COMB, D_MODEL), F32)
        for kk in range(TOP_K):
            ff = ff + g[:, kk:kk + 1] * _load_token_major(ybuf, kk * TM_COMB, TM_COMB)
        x2 = _load_token_major(x2_ref, 0, TM_COMB)
        o_ref[...] = _layer_norm(DEEPNORM_ALPHA * x2 + ff, lng_ref[...], lnb_ref[...])

    for k in range(COMB_SLOTS):
        nxt = (k + ahead) % COMB_SLOTS

        @pl.when((i % COMB_SLOTS == k) & (i + ahead < nt))
        def _():
            _wait_rows(rows, y_hbm, ybufs[k], sem.at[k])
            _gather_rows(idx_ref, (i + ahead) * rows, rows, y_hbm, ybufs[nxt], sem.at[nxt])
            compute(ybufs[k])

        @pl.when((i % COMB_SLOTS == k) & (i + ahead >= nt))
        def _():
            _wait_rows(rows, y_hbm, ybufs[k], sem.at[k])
            compute(ybufs[k])


def _combine(dest, y_rows, x2, gates, ln_g, ln_b):
    n = x2.shape[0] // LANE_BLOCKS
    nt = n // TM_COMB
    rows = TOP_K * TM_COMB
    idx = (dest * LANE_BLOCKS).reshape(nt, TM_COMB, TOP_K).transpose(0, 2, 1).reshape(nt * rows)
    row = lambda width: pl.BlockSpec((TM_COMB, width), lambda i, idx: (i, 0))
    full = lambda r, c: pl.BlockSpec((r, c), lambda i, idx: (0, 0))
    grid_spec = pltpu.PrefetchScalarGridSpec(
        num_scalar_prefetch=1,
        grid=(nt,),
        in_specs=[pl.BlockSpec(memory_space=pl.ANY),
                  pl.BlockSpec((TM_COMB * LANE_BLOCKS, LANES), lambda i, idx: (i, 0)), row(LANES),
                  full(1, D_MODEL), full(1, D_MODEL)],
        out_specs=row(D_MODEL),
        scratch_shapes=[pltpu.VMEM((rows * LANE_BLOCKS, LANES), F32)] * COMB_SLOTS
        + [pltpu.SemaphoreType.DMA((COMB_SLOTS,))],
    )
    return pl.pallas_call(
        _combine_kernel,
        grid_spec=grid_spec,
        out_shape=jax.ShapeDtypeStruct((n, D_MODEL), F32),
        compiler_params=_params(1),
        name="combine",
    )(idx, y_rows, x2, gates, ln_g, ln_b)


def _rank_kernel(idx_ref, dest_ref, counts_ref, running):
    p = pl.program_id(0)
    j = pl.program_id(1)
    t = TM_RANK
    idx = idx_ref[...]
    e_iota = lax.broadcasted_iota(jnp.int32, (t, LANES), 1)
    hot = [idx[:, k:k + 1] == e_iota for k in range(TOP_K)]
    chosen = sum(jnp.where(h, 1.0, 0.0) for h in hot)
    tile_counts = jnp.sum(chosen, axis=0, keepdims=True)

    @pl.when((p == 0) & (j == 0))
    def _():
        running[...] = jnp.zeros_like(running)

    @pl.when(p == 0)
    def _():
        running[...] += tile_counts
        dest_ref[...] = jnp.zeros_like(dest_ref)

    @pl.when((p == 1) & (j == 0))
    def _():
        counts = running[...]
        counts_ref[...] = counts
        padded = jnp.ceil(counts * (1.0 / TM_MOE)) * TM_MOE
        lane = lax.broadcasted_iota(jnp.int32, (1, LANES), 1)
        scan = padded
        shift = 1
        while shift < N_EXPERTS:
            scan = scan + jnp.where(lane >= shift, pltpu.roll(scan, shift, 1), 0.0)
            shift *= 2
        running[...] = scan - padded

    @pl.when(p == 1)
    def _():
        r_iota = lax.broadcasted_iota(jnp.int32, (t, t), 0)
        c_iota = lax.broadcasted_iota(jnp.int32, (t, t), 1)
        earlier = jnp.where(c_iota < r_iota, 1.0, 0.0).astype(BF16)
        base = jnp.dot(earlier, chosen.astype(BF16), preferred_element_type=F32) + running[...]
        out = jnp.zeros((t, LANES), jnp.int32)
        for k in range(TOP_K):
            row = jnp.sum(jnp.where(hot[k], base, 0.0), axis=-1, keepdims=True).astype(jnp.int32)
            out = jnp.where(e_iota == k, row, out)
        dest_ref[...] = out
        running[...] += tile_counts


def _dispatch_plan(top_idx, n):
    n_assign = n * TOP_K
    nb = n_assign // TM_MOE + N_EXPERTS + MOE_SLOTS - 2
    dest, counts = pl.pallas_call(
        _rank_kernel,
        grid=(2, n // TM_RANK),
        in_specs=[pl.BlockSpec((TM_RANK, LANES), lambda p, j: (j, 0))],
        out_specs=[pl.BlockSpec((TM_RANK, LANES), lambda p, j: (p * j, 0)),
                   pl.BlockSpec((1, LANES), lambda p, j: (0, 0))],
        out_shape=[jax.ShapeDtypeStruct((n, LANES), jnp.int32), jax.ShapeDtypeStruct((1, LANES), F32)],
        scratch_shapes=[pltpu.VMEM((1, LANES), F32)],
        compiler_params=_params(2),
        name="expert_rank",
    )(top_idx)
    dest = dest[:, :TOP_K].reshape(-1)
    counts = counts[0, :N_EXPERTS].astype(jnp.int32)
    padded = (counts + TM_MOE - 1) // TM_MOE * TM_MOE
    pad_end = jnp.cumsum(padded)
    tail = jnp.stack([pad_end[-1], jnp.int32(nb * TM_MOE)])[:, None]
    gaps = jnp.concatenate([jnp.stack([pad_end - padded + counts, pad_end]), tail], axis=1)
    block_first_row = jnp.arange(nb, dtype=jnp.int32) * TM_MOE
    block_expert = jnp.minimum(jnp.sum(pad_end[None, :] <= block_first_row[:, None], axis=1),
                               N_EXPERTS - 1).astype(jnp.int32)
    n_used = (pad_end[-1:] // TM_MOE).astype(jnp.int32)
    return dest, gaps, block_expert, n_used


def kernel(x, mem, w_in, rel_bias, g_group_a, g_group_b, w_out, w_q_mem, w_kv_mem, w_o_mem, w_router, b_router, w_gate_up, b_gate_up, w_down, b_down, ln_g, ln_b):
    b, s, d = x.shape
    n = b * s
    depth = w_in.shape[0]
    xf = x.reshape(n, d)
    for l in range(depth):
        q_scale = jnp.ones((3 * MIX_WIDTH,), F32).at[:WIDTH_A].set(HEAD_DIM ** -0.5)
        q_scale = q_scale.at[3 * WIDTH_A:3 * WIDTH_A + WIDTH_B].set(HEAD_DIM ** -0.5)
        proj = _matmul(xf, (w_in[l] * q_scale).astype(BF16), BF16, TM_TOK, 3 * MIX_WIDTH).reshape(b, s, 3 * MIX_WIDTH)
        out_a = _chunk_attention(proj, _chunk_bias(rel_bias[l]), b, s).reshape(n, WIDTH_A)
        out_b = _sb_attention(proj, b, s).reshape(n, WIDTH_B)
        x1, qm = _mixer_out(out_a, out_b, xf, g_group_a[l][None], g_group_b[l][None],
                            w_out[l].astype(BF16), ln_g[l, 0][None], ln_b[l, 0][None],
                            w_q_mem[l].astype(BF16))
        mem_len = mem.shape[1]
        kv = _matmul(mem.reshape(b * mem_len, d), w_kv_mem[l].astype(BF16), BF16, b * mem_len, 1024)
        x2, top_idx, gates = _mem_attn(qm, kv.reshape(b, mem_len, 2 * d), x1, w_o_mem[l].astype(BF16),
                                       ln_g[l, 1][None], ln_b[l, 1][None],
                                       w_router[l].astype(BF16), b_router[l][None], b, s)
        dest, gaps, block_expert, n_used = _dispatch_plan(top_idx, n)
        y_rows = _moe_experts(block_expert, n_used, dest, gaps, x2,
                              w_gate_up[l], b_gate_up[l][:, None, :], w_down[l], b_down[l][:, None, :])
        xf = _combine(dest, y_rows, x2, gates, ln_g[l, 2][None], ln_b[l, 2][None])
    return xf.reshape(b, s, d)
```

```python
import functools

import jax
import jax.numpy as jnp
from jax import lax
from jax.experimental import pallas as pl
from jax.experimental.pallas import tpu as pltpu

D_MODEL = 1024
CHUNK = 64
LEFT_CHUNKS = 8
LEFT = LEFT_CHUNKS * CHUNK
HEAD_DIM = 64
N_HEADS_A = 8
N_HEADS_B = 8
WIDTH_A = N_HEADS_A * HEAD_DIM
WIDTH_B = N_HEADS_B * HEAD_DIM
MIX_WIDTH = WIDTH_A + WIDTH_B
REL_CLIP = 128
N_HEADS_MEM = 4
HEAD_DIM_MEM = D_MODEL // N_HEADS_MEM
N_EXPERTS = 32
TOP_K = 4
D_FF = D_MODEL
SWIGLU_LIMIT = 7.0
SWIGLU_ALPHA = 1.702
LN_EPS = 1e-5
RMS_EPS = 1e-6
DEEPNORM_ALPHA = 2.0 ** 0.25
NEG_INF = -1e30
LOG2E = 1.4426950408889634

LANES = 128
LANE_BLOCKS = D_MODEL // LANES
VMEM_LIMIT = 48 * 1024 * 1024
VMEM_LIMIT_MOE = 56 * 1024 * 1024

TQ_A = 512
SUB_A = 256
HEADS_A_STEP = 4
TB_SB = 256
SB_TILES = 4
SB_UNDERFLOW = 110.0
TM_TOK = 512
TM_MOE = 256
MOE_SLOTS = 3
TM_COMB = 256
COMB_SLOTS = 3
TM_RANK = 512

F32 = jnp.float32
BF16 = jnp.bfloat16
_NT = (((1,), (1,)), ((), ()))


def _params(n_axes):
    return pltpu.CompilerParams(dimension_semantics=("arbitrary",) * n_axes,
                                vmem_limit_bytes=VMEM_LIMIT)


def _store_token_major(ref, value):
    rows = value.shape[0]
    for c in range(LANE_BLOCKS):
        ref[pl.ds(c, rows, stride=LANE_BLOCKS), :] = value[:, c * LANES:(c + 1) * LANES]


def _load_token_major(ref, first_row, rows):
    return jnp.concatenate(
        [ref[pl.ds(first_row * LANE_BLOCKS + c, rows, stride=LANE_BLOCKS), :] for c in range(LANE_BLOCKS)],
        axis=1)


def _layer_norm(r, g, b):
    mu = jnp.mean(r, axis=-1, keepdims=True)
    d = r - mu
    var = jnp.mean(d * d, axis=-1, keepdims=True)
    return d * lax.rsqrt(var + LN_EPS) * g + b


def _matmul_kernel(x_ref, w_ref, o_ref):
    o_ref[...] = jnp.dot(x_ref[...].astype(BF16), w_ref[...],
                         preferred_element_type=F32).astype(o_ref.dtype)


def _matmul(x, w, out_dtype, tm, tn):
    m, k = x.shape
    n = w.shape[1]
    return pl.pallas_call(
        _matmul_kernel,
        grid=(m // tm, n // tn),
        in_specs=[pl.BlockSpec((tm, k), lambda i, j: (i, 0)),
                  pl.BlockSpec((k, tn), lambda i, j: (0, j))],
        out_specs=pl.BlockSpec((tm, tn), lambda i, j: (i, j)),
        out_shape=jax.ShapeDtypeStruct((m, n), out_dtype),
        compiler_params=_params(2),
        name="matmul",
    )(x, w)


def _chunk_attn_kernel(q_ref, kp_ref, kc_ref, vp_ref, vc_ref, bias0_ref, bias1_ref, o_ref):
    w = SUB_A + LEFT
    for h in range(HEADS_A_STEP):
        sl = slice(h * HEAD_DIM, (h + 1) * HEAD_DIM)
        k = jnp.concatenate([kp_ref[:, sl], kc_ref[:, sl]], axis=0)
        v = jnp.concatenate([vp_ref[:, sl], vc_ref[:, sl]], axis=0)
        for sub, bias_ref in enumerate((bias0_ref, bias1_ref)):
            rows = slice(sub * SUB_A, (sub + 1) * SUB_A)
            keys = slice(sub * SUB_A, sub * SUB_A + w)
            s = lax.dot_general(q_ref[rows, sl], k[keys], _NT, preferred_element_type=F32) + bias_ref[h]
            m = jnp.max(s, axis=-1, keepdims=True)
            p = jnp.exp(s - m)
            l = jnp.sum(p, axis=-1, keepdims=True)
            o = jnp.dot(p.astype(BF16), v[keys], preferred_element_type=F32) / l
            o_ref[rows, sl] = o.astype(o_ref.dtype)


def _chunk_bias(rel_bias):
    w = SUB_A + LEFT
    heads = rel_bias.shape[0]
    m = jnp.arange(2 * w)
    rel = jnp.where(m < w, LEFT - m, LEFT + 2 * w - m)
    diag = rel_bias[:, jnp.clip(rel, -REL_CLIP, REL_CLIP) + REL_CLIP].astype(F32)[:, None, :]
    n_tables = 1 + TQ_A // SUB_A
    return pl.pallas_call(
        _chunk_bias_kernel,
        grid=(heads,),
        in_specs=[pl.BlockSpec((1, 1, 2 * w), lambda h: (h, 0, 0))],
        out_specs=pl.BlockSpec((n_tables, 1, SUB_A, w), lambda h: (0, h, 0, 0)),
        out_shape=jax.ShapeDtypeStruct((n_tables, heads, SUB_A, w), F32),
        compiler_params=_params(1),
        name="chunk_bias",
    )(diag)


def _chunk_bias_kernel(diag_ref, o_ref):
    w = SUB_A + LEFT
    rolled = pltpu.roll(jnp.broadcast_to(diag_ref[0], (SUB_A, 2 * w)), 0, 1, stride=1, stride_axis=0)
    col = lax.broadcasted_iota(jnp.int32, (SUB_A, w), 1)
    qc = lax.broadcasted_iota(jnp.int32, (SUB_A, w), 0) // CHUNK
    kc = col // CHUNK
    band = (kc >= qc) & (kc <= qc + LEFT_CHUNKS)
    table = jnp.where(band, rolled[:, :w], NEG_INF)
    o_ref[0, 0] = table
    for sub in range(TQ_A // SUB_A):
        o_ref[1 + sub, 0] = jnp.where(col >= LEFT - sub * SUB_A, table, NEG_INF)


def _chunk_attention(proj, bias, b, s):
    nq = s // TQ_A
    width = HEADS_A_STEP * HEAD_DIM
    groups = WIDTH_A // width
    blk = lambda off, prev: pl.BlockSpec(
        (None, TQ_A, width),
        (lambda bi, p, i: (bi, jnp.maximum(i - 1, 0), off + p)) if prev
        else (lambda bi, p, i: (bi, i, off + p)))
    table = lambda sub: pl.BlockSpec((None, HEADS_A_STEP, SUB_A, SUB_A + LEFT),
                                     lambda bi, p, i: (jnp.where(i == 0, 1 + sub, 0), p, 0, 0))
    return pl.pallas_call(
        _chunk_attn_kernel,
        grid=(b, groups, nq),
        in_specs=[blk(0, False),
                  blk(groups, True), blk(groups, False),
                  blk(2 * groups, True), blk(2 * groups, False),
                  table(0), table(1)],
        out_specs=pl.BlockSpec((None, TQ_A, width), lambda bi, p, i: (bi, i, p)),
        out_shape=jax.ShapeDtypeStruct((b, s, WIDTH_A), BF16),
        compiler_params=_params(3),
        name="chunk_attn",
    )(proj, proj, proj, proj, proj, bias, bias)


def _sb_block(q, k_ref, v_ref, sl, j, later, tri, causal):
    t = TB_SB
    start = pl.multiple_of(j * t, t)
    k = k_ref[pl.ds(start, t), sl]
    v = v_ref[pl.ds(start, t), sl]
    z = lax.dot_general(q, k, _NT, preferred_element_type=F32)
    sp = jnp.maximum(z, 0.0) + jnp.log(1.0 + jnp.exp2(jnp.abs(z) * (-LOG2E)))
    if causal is not None:
        sp = jnp.where(causal, sp, 0.0)
    csum = jnp.dot(sp.astype(BF16), tri, preferred_element_type=F32) + later
    wgt = jnp.exp(z - csum)
    if causal is not None:
        wgt = jnp.where(causal, wgt, 0.0)
    pv = jnp.dot(wgt.astype(BF16), v, preferred_element_type=F32)
    return pv, csum[:, 0:1]


def _sb_attn_kernel(q_ref, k_ref, v_ref, o_ref):
    first = pl.program_id(2) * SB_TILES
    t = TB_SB
    row = lax.broadcasted_iota(jnp.int32, (t, t), 0)
    col = lax.broadcasted_iota(jnp.int32, (t, t), 1)
    tri = jnp.where(row >= col, 1.0, 0.0).astype(BF16)
    causal = col < row
    chains = [(u, h) for u in range(SB_TILES) for h in range(2)]
    rows = lambda u: slice(u * t, (u + 1) * t)
    lanes = lambda h: slice(h * HEAD_DIM, (h + 1) * HEAD_DIM)
    qs = [q_ref[rows(u), lanes(h)] for u, h in chains]

    def block(c, j, later, masked):
        return _sb_block(qs[c], k_ref, v_ref, lanes(chains[c][1]), j, later, tri, causal if masked else None)

    carry = []
    for c, (u, h) in enumerate(chains):
        tile = first + u
        pv, later = block(c, tile, jnp.zeros((t, 1), F32), True)
        pv_prev, later_prev = block(c, jnp.maximum(tile - 1, 0), later, False)
        has_prev = tile >= 1
        carry += [jnp.where(has_prev, later_prev, later), pv + jnp.where(has_prev, pv_prev, 0.0)]

    def pending(c):
        todo = False
        for ci, (u, h) in enumerate(chains):
            todo = todo | ((first + u - 2 - c[0] >= 0) & (jnp.min(c[1 + 2 * ci]) < SB_UNDERFLOW))
        return todo

    def step(c):
        out = [c[0] + 1]
        for ci, (u, h) in enumerate(chains):
            j = first + u - 2 - c[0]
            pv, later = block(ci, jnp.maximum(j, 0), c[1 + 2 * ci], False)
            out += [jnp.where(j >= 0, later, c[1 + 2 * ci]), c[2 + 2 * ci] + jnp.where(j >= 0, pv, 0.0)]
        return tuple(out)

    carry = lax.while_loop(pending, step, (jnp.int32(0), *carry))
    for ci, (u, h) in enumerate(chains):
        o_ref[rows(u), lanes(h)] = carry[2 + 2 * ci].astype(o_ref.dtype)


def _sb_attention(proj, b, s):
    tq = SB_TILES * TB_SB
    nq = s // tq
    pairs = WIDTH_B // LANES
    base = 3 * WIDTH_A // LANES
    return pl.pallas_call(
        _sb_attn_kernel,
        grid=(b, pairs, nq),
        in_specs=[pl.BlockSpec((None, tq, LANES), lambda bi, p, i: (bi, i, base + p)),
                  pl.BlockSpec((None, s, LANES), lambda bi, p, i: (bi, 0, base + pairs + p)),
                  pl.BlockSpec((None, s, LANES), lambda bi, p, i: (bi, 0, base + 2 * pairs + p))],
        out_specs=pl.BlockSpec((None, tq, LANES), lambda bi, p, i: (bi, i, p)),
        out_shape=jax.ShapeDtypeStruct((b, s, WIDTH_B), BF16),
        compiler_params=_params(3),
        name="sb_attn",
    )(proj, proj, proj)


def _mixer_out_kernel(oa_ref, ob_ref, x_ref, ga_ref, gb_ref, wout_ref, lng_ref, lnb_ref, wq_ref,
                      x1_ref, qm_ref):
    def rms(ref, g_ref):
        a = ref[...].astype(F32)
        return (a * lax.rsqrt(jnp.mean(a * a, axis=-1, keepdims=True) + RMS_EPS) * g_ref[...]).astype(BF16)

    y = jnp.dot(rms(oa_ref, ga_ref), wout_ref[:WIDTH_A, :], preferred_element_type=F32)
    y = y + jnp.dot(rms(ob_ref, gb_ref), wout_ref[WIDTH_A:, :], preferred_element_type=F32)
    x1 = _layer_norm(DEEPNORM_ALPHA * x_ref[...] + y, lng_ref[...], lnb_ref[...])
    x1_ref[...] = x1
    qm_ref[...] = jnp.dot(x1.astype(BF16), wq_ref[...], preferred_element_type=F32).astype(BF16)


def _mixer_out(out_a, out_b, x, g_a, g_b, w_out, ln_g, ln_b, w_q):
    n = x.shape[0]
    tm = TM_TOK
    row = lambda width: pl.BlockSpec((tm, width), lambda i: (i, 0))
    full = lambda r, c: pl.BlockSpec((r, c), lambda i: (0, 0))
    return pl.pallas_call(
        _mixer_out_kernel,
        grid=(n // tm,),
        in_specs=[row(WIDTH_A), row(WIDTH_B), row(D_MODEL), full(1, WIDTH_A), full(1, WIDTH_B),
                  full(MIX_WIDTH, D_MODEL), full(1, D_MODEL), full(1, D_MODEL), full(D_MODEL, D_MODEL)],
        out_specs=[row(D_MODEL), row(D_MODEL)],
        out_shape=[jax.ShapeDtypeStruct((n, D_MODEL), F32), jax.ShapeDtypeStruct((n, D_MODEL), BF16)],
        compiler_params=_params(1),
        name="mixer_out",
    )(out_a, out_b, x, g_a, g_b, w_out, ln_g, ln_b, w_q)


def _mem_attn_kernel(q_ref, kv_ref, x1_ref, wo_ref, lng_ref, lnb_ref, wr_ref, br_ref,
                     x2_ref, idx_ref, gate_ref):
    heads = []
    for h in range(N_HEADS_MEM):
        sl = slice(h * HEAD_DIM_MEM, (h + 1) * HEAD_DIM_MEM)
        q = q_ref[:, sl] * (HEAD_DIM_MEM ** -0.5)
        k = kv_ref[:, sl]
        v = kv_ref[:, D_MODEL + h * HEAD_DIM_MEM:D_MODEL + (h + 1) * HEAD_DIM_MEM]
        s = lax.dot_general(q, k, _NT, preferred_element_type=F32)
        m = jnp.max(s, axis=-1, keepdims=True)
        p = jnp.exp(s - m)
        l = jnp.sum(p, axis=-1, keepdims=True)
        heads.append((jnp.dot(p.astype(BF16), v, preferred_element_type=F32) / l).astype(BF16))
    o = jnp.concatenate(heads, axis=-1)
    y = jnp.dot(o, wo_ref[...], preferred_element_type=F32)
    x2 = _layer_norm(DEEPNORM_ALPHA * x1_ref[...] + y, lng_ref[...], lnb_ref[...])
    _store_token_major(x2_ref, x2)

    logits = jnp.dot(x2.astype(BF16), wr_ref[...], preferred_element_type=F32) + br_ref[...]
    tm = logits.shape[0]
    e_iota = lax.broadcasted_iota(jnp.int32, (tm, N_EXPERTS), 1)
    lane = lax.broadcasted_iota(jnp.int32, (tm, LANES), 1)
    idx_out = jnp.zeros((tm, LANES), jnp.int32)
    val_out = jnp.zeros((tm, LANES), F32)
    top = None
    denom = jnp.zeros((tm, 1), F32)
    for kk in range(TOP_K):
        m = jnp.max(logits, axis=-1, keepdims=True)
        sel = jnp.min(jnp.where(logits == m, e_iota, N_EXPERTS), axis=-1, keepdims=True)
        if top is None:
            top = m
        e = jnp.exp(m - top)
        denom = denom + e
        idx_out = jnp.where(lane == kk, sel, idx_out)
        val_out = jnp.where(lane == kk, e, val_out)
        logits = jnp.where(e_iota == sel, -jnp.inf, logits)
    idx_ref[...] = idx_out
    gate_ref[...] = val_out / denom


def _mem_attn(qm, kv, x1, w_o, ln_g, ln_b, w_r, b_r, b, s):
    tm = TM_TOK
    nt = s // tm
    mem_len = kv.shape[1]
    row = lambda width: pl.BlockSpec((tm, width), lambda bi, i: (bi * nt + i, 0))
    full = lambda r, c: pl.BlockSpec((r, c), lambda bi, i: (0, 0))
    n = b * s
    return pl.pallas_call(
        _mem_attn_kernel,
        grid=(b, nt),
        in_specs=[row(D_MODEL), pl.BlockSpec((None, mem_len, 2 * D_MODEL), lambda bi, i: (bi, 0, 0)),
                  row(D_MODEL), full(D_MODEL, D_MODEL), full(1, D_MODEL), full(1, D_MODEL),
                  full(D_MODEL, N_EXPERTS), full(1, N_EXPERTS)],
        out_specs=[pl.BlockSpec((tm * LANE_BLOCKS, LANES), lambda bi, i: (bi * nt + i, 0)), row(LANES), row(LANES)],
        out_shape=[jax.ShapeDtypeStruct((n * LANE_BLOCKS, LANES), F32),
                   jax.ShapeDtypeStruct((n, LANES), jnp.int32),
                   jax.ShapeDtypeStruct((n, LANES), F32)],
        compiler_params=_params(2),
        name="mem_attn",
    )(qm, kv, x1, w_o, ln_g, ln_b, w_r, b_r)


def _gather_rows(idx_ref, first, count, src_hbm, dst_ref, sem):
    for r in range(count):
        src = pl.ds(pl.multiple_of(idx_ref[first + r], LANE_BLOCKS), LANE_BLOCKS)
        pltpu.make_async_copy(src_hbm.at[src], dst_ref.at[pl.ds(r * LANE_BLOCKS, LANE_BLOCKS)], sem).start(
            priority=r % 2)


def _wait_rows(count, src_hbm, dst_ref, sem):
    pltpu.make_async_copy(src_hbm.at[pl.ds(0, count * LANE_BLOCKS)], dst_ref, sem).wait()


def _fill_row_table(dest_ref, gap_ref, tok_ref):
    unroll = 16

    def clear(c, carry):
        for u in range(unroll):
            tok_ref[c * unroll + u] = 0
        return carry

    def place(c, carry):
        rows = [dest_ref[c * unroll + u] for u in range(unroll)]
        for u in range(unroll):
            tok_ref[rows[u]] = (c * (unroll // TOP_K) + u // TOP_K) * LANE_BLOCKS
        return carry

    for g in range(gap_ref.shape[1]):
        lax.fori_loop(gap_ref[0, g] // unroll, gap_ref[1, g] // unroll, clear, 0)
    lax.fori_loop(0, dest_ref.shape[0] // unroll, place, 0)


def _moe_kernel(bexp_ref, nused_ref, dest_ref, gap_ref, x_hbm, wgu_ref, bgu_ref, wd_ref, bd_ref,
                y_ref, xbuf, wgu_bf, wd_bf, tok_ref, sem):
    i = pl.program_id(0)
    slot = i % MOE_SLOTS
    ahead = MOE_SLOTS - 1
    n_used = nused_ref[0]

    @pl.when(i == 0)
    def _():
        _fill_row_table(dest_ref, gap_ref, tok_ref)
        for b in range(ahead):
            _gather_rows(tok_ref, b * TM_MOE, TM_MOE, x_hbm, xbuf.at[b], sem.at[b])

    @pl.when(i < n_used + ahead)
    def _():
        _wait_rows(TM_MOE, x_hbm, xbuf.at[slot], sem.at[slot])

    @pl.when((i < n_used) & ((i == 0) | (bexp_ref[i] != bexp_ref[jnp.maximum(i - 1, 0)])))
    def _():
        wgu_bf[...] = wgu_ref[0].astype(BF16)
        wd_bf[...] = wd_ref[0].astype(BF16)

    @pl.when(i < n_used)
    def _():
        x = _load_token_major(xbuf.at[slot], 0, TM_MOE).astype(BF16)
        nxt = (i + ahead) % MOE_SLOTS
        _gather_rows(tok_ref, (i + ahead) * TM_MOE, TM_MOE, x_hbm, xbuf.at[nxt], sem.at[nxt])
        gu = jnp.dot(x, wgu_bf[...], preferred_element_type=F32) + bgu_ref[0]
        gate = jnp.minimum(gu[:, :D_FF], SWIGLU_LIMIT)
        up = jnp.clip(gu[:, D_FF:], -SWIGLU_LIMIT, SWIGLU_LIMIT)
        glu = gate * jax.nn.sigmoid(gate * SWIGLU_ALPHA)
        hid = ((up + 1.0) * glu).astype(BF16)
        _store_token_major(y_ref, jnp.dot(hid, wd_bf[...], preferred_element_type=F32) + bd_ref[0])

    @pl.when(i >= n_used)
    def _():
        y_ref[...] = jnp.zeros_like(y_ref)


def _moe_experts(block_expert, n_used, dest, gaps, x2, w_gu, b_gu, w_d, b_d):
    nb = block_expert.shape[0]
    grid_spec = pltpu.PrefetchScalarGridSpec(
        num_scalar_prefetch=4,
        grid=(nb,),
        in_specs=[pl.BlockSpec(memory_space=pl.ANY),
                  pl.BlockSpec((1, D_MODEL, 2 * D_FF), lambda i, be, nu, de, ga: (be[i], 0, 0)),
                  pl.BlockSpec((1, 1, 2 * D_FF), lambda i, be, nu, de, ga: (be[i], 0, 0)),
                  pl.BlockSpec((1, D_FF, D_MODEL), lambda i, be, nu, de, ga: (be[i], 0, 0)),
                  pl.BlockSpec((1, 1, D_MODEL), lambda i, be, nu, de, ga: (be[i], 0, 0))],
        out_specs=pl.BlockSpec((TM_MOE * LANE_BLOCKS, LANES), lambda i, be, nu, de, ga: (i, 0)),
        scratch_shapes=[pltpu.VMEM((MOE_SLOTS, TM_MOE * LANE_BLOCKS, LANES), F32),
                        pltpu.VMEM((D_MODEL, 2 * D_FF), BF16), pltpu.VMEM((D_FF, D_MODEL), BF16),
                        pltpu.SMEM((nb * TM_MOE,), jnp.int32),
                        pltpu.SemaphoreType.DMA((MOE_SLOTS,))],
    )
    return pl.pallas_call(
        _moe_kernel,
        grid_spec=grid_spec,
        out_shape=jax.ShapeDtypeStruct((nb * TM_MOE * LANE_BLOCKS, LANES), F32),
        compiler_params=pltpu.CompilerParams(dimension_semantics=("arbitrary",),
                                             vmem_limit_bytes=VMEM_LIMIT_MOE),
        name="moe_experts",
    )(block_expert, n_used, dest, gaps, x2, w_gu, b_gu, w_d, b_d)


def _combine_kernel(idx_ref, y_hbm, x2_ref, gate_ref, lng_ref, lnb_ref, o_ref, *scratch):
    ybufs, sem = scratch[:COMB_SLOTS], scratch[COMB_SLOTS]
    i = pl.program_id(0)
    nt = pl.num_programs(0)
    ahead = COMB_SLOTS - 1
    rows = TOP_K * TM_COMB

    @pl.when(i == 0)
    def _():
        for t in range(ahead):
            _gather_rows(idx_ref, t * rows, rows, y_hbm, ybufs[t], sem.at[t])

    def compute(ybuf):
        g = gate_ref[...]
        ff = jnp.zeros((TM_COMB, D_MODEL), F32)
        for kk in range(TOP_K):
            ff = ff + g[:, kk:kk + 1] * _load_token_major(ybuf, kk * TM_COMB, TM_COMB)
        x2 = _load_token_major(x2_ref, 0, TM_COMB)
        o_ref[...] = _layer_norm(DEEPNORM_ALPHA * x2 + ff, lng_ref[...], lnb_ref[...])

    for k in range(COMB_SLOTS):
        nxt = (k + ahead) % COMB_SLOTS

        @pl.when((i % COMB_SLOTS == k) & (i + ahead < nt))
        def _():
            _wait_rows(rows, y_hbm, ybufs[k], sem.at[k])
            _gather_rows(idx_ref, (i + ahead) * rows, rows, y_hbm, ybufs[nxt], sem.at[nxt])
            compute(ybufs[k])

        @pl.when((i % COMB_SLOTS == k) & (i + ahead >= nt))
        def _():
            _wait_rows(rows, y_hbm, ybufs[k], sem.at[k])
            compute(ybufs[k])


def _combine(dest, y_rows, x2, gates, ln_g, ln_b):
    n = x2.shape[0] // LANE_BLOCKS
    nt = n // TM_COMB
    rows = TOP_K * TM_COMB
    idx = (dest * LANE_BLOCKS).reshape(nt, TM_COMB, TOP_K).transpose(0, 2, 1).reshape(nt * rows)
    row = lambda width: pl.BlockSpec((TM_COMB, width), lambda i, idx: (i, 0))
    full = lambda r, c: pl.BlockSpec((r, c), lambda i, idx: (0, 0))
    grid_spec = pltpu.PrefetchScalarGridSpec(
        num_scalar_prefetch=1,
        grid=(nt,),
        in_specs=[pl.BlockSpec(memory_space=pl.ANY),
                  pl.BlockSpec((TM_COMB * LANE_BLOCKS, LANES), lambda i, idx: (i, 0)), row(LANES),
                  full(1, D_MODEL), full(1, D_MODEL)],
        out_specs=row(D_MODEL),
        scratch_shapes=[pltpu.VMEM((rows * LANE_BLOCKS, LANES), F32)] * COMB_SLOTS
        + [pltpu.SemaphoreType.DMA((COMB_SLOTS,))],
    )
    return pl.pallas_call(
        _combine_kernel,
        grid_spec=grid_spec,
        out_shape=jax.ShapeDtypeStruct((n, D_MODEL), F32),
        compiler_params=_params(1),
        name="combine",
    )(idx, y_rows, x2, gates, ln_g, ln_b)


def _rank_kernel(idx_ref, dest_ref, counts_ref, running):
    p = pl.program_id(0)
    j = pl.program_id(1)
    t = TM_RANK
    idx = idx_ref[...]
    e_iota = lax.broadcasted_iota(jnp.int32, (t, LANES), 1)
    hot = [idx[:, k:k + 1] == e_iota for k in range(TOP_K)]
    chosen = sum(jnp.where(h, 1.0, 0.0) for h in hot)
    tile_counts = jnp.sum(chosen, axis=0, keepdims=True)

    @pl.when((p == 0) & (j == 0))
    def _():
        running[...] = jnp.zeros_like(running)

    @pl.when(p == 0)
    def _():
        running[...] += tile_counts
        dest_ref[...] = jnp.zeros_like(dest_ref)

    @pl.when((p == 1) & (j == 0))
    def _():
        counts = running[...]
        counts_ref[...] = counts
        padded = jnp.ceil(counts * (1.0 / TM_MOE)) * TM_MOE
        lane = lax.broadcasted_iota(jnp.int32, (1, LANES), 1)
        scan = padded
        shift = 1
        while shift < N_EXPERTS:
            scan = scan + jnp.where(lane >= shift, pltpu.roll(scan, shift, 1), 0.0)
            shift *= 2
        running[...] = scan - padded

    @pl.when(p == 1)
    def _():
        r_iota = lax.broadcasted_iota(jnp.int32, (t, t), 0)
        c_iota = lax.broadcasted_iota(jnp.int32, (t, t), 1)
        earlier = jnp.where(c_iota < r_iota, 1.0, 0.0).astype(BF16)
        base = jnp.dot(earlier, chosen.astype(BF16), preferred_element_type=F32) + running[...]
        out = jnp.zeros((t, LANES), jnp.int32)
        for k in range(TOP_K):
            row = jnp.sum(jnp.where(hot[k], base, 0.0), axis=-1, keepdims=True).astype(jnp.int32)
            out = jnp.where(e_iota == k, row, out)
        dest_ref[...] = out
        running[...] += tile_counts


def _dispatch_plan(top_idx, n):
    n_assign = n * TOP_K
    nb = n_assign // TM_MOE + N_EXPERTS + MOE_SLOTS - 2
    dest, counts = pl.pallas_call(
        _rank_kernel,
        grid=(2, n // TM_RANK),
        in_specs=[pl.BlockSpec((TM_RANK, LANES), lambda p, j: (j, 0))],
        out_specs=[pl.BlockSpec((TM_RANK, LANES), lambda p, j: (p * j, 0)),
                   pl.BlockSpec((1, LANES), lambda p, j: (0, 0))],
        out_shape=[jax.ShapeDtypeStruct((n, LANES), jnp.int32), jax.ShapeDtypeStruct((1, LANES), F32)],
        scratch_shapes=[pltpu.VMEM((1, LANES), F32)],
        compiler_params=_params(2),
        name="expert_rank",
    )(top_idx)
    dest = dest[:, :TOP_K].reshape(-1)
    counts = counts[0, :N_EXPERTS].astype(jnp.int32)
    padded = (counts + TM_MOE - 1) // TM_MOE * TM_MOE
    pad_end = jnp.cumsum(padded)
    tail = jnp.stack([pad_end[-1], jnp.int32(nb * TM_MOE)])[:, None]
    gaps = jnp.concatenate([jnp.stack([pad_end - padded + counts, pad_end]), tail], axis=1)
    block_first_row = jnp.arange(nb, dtype=jnp.int32) * TM_MOE
    block_expert = jnp.minimum(jnp.sum(pad_end[None, :] <= block_first_row[:, None], axis=1),
                               N_EXPERTS - 1).astype(jnp.int32)
    n_used = (pad_end[-1:] // TM_MOE).astype(jnp.int32)
    return dest, gaps, block_expert, n_used


def kernel(x, mem, w_in, rel_bias, g_group_a, g_group_b, w_out, w_q_mem, w_kv_mem, w_o_mem, w_router, b_router, w_gate_up, b_gate_up, w_down, b_down, ln_g, ln_b):
    b, s, d = x.shape
    n = b * s
    depth = w_in.shape[0]
    xf = x.reshape(n, d)
    for l in range(depth):
        q_scale = jnp.ones((3 * MIX_WIDTH,), F32).at[:WIDTH_A].set(HEAD_DIM ** -0.5)
        q_scale = q_scale.at[3 * WIDTH_A:3 * WIDTH_A + WIDTH_B].set(HEAD_DIM ** -0.5)
        proj = _matmul(xf, (w_in[l] * q_scale).astype(BF16), BF16, TM_TOK, 3 * MIX_WIDTH).reshape(b, s, 3 * MIX_WIDTH)
        out_a = _chunk_attention(proj, _chunk_bias(rel_bias[l]), b, s).reshape(n, WIDTH_A)
        out_b = _sb_attention(proj, b, s).reshape(n, WIDTH_B)
        x1, qm = _mixer_out(out_a, out_b, xf, g_group_a[l][None], g_group_b[l][None],
                            w_out[l].astype(BF16), ln_g[l, 0][None], ln_b[l, 0][None],
                            w_q_mem[l].astype(BF16))
        mem_len = mem.shape[1]
        kv = _matmul(mem.reshape(b * mem_len, d), w_kv_mem[l].astype(BF16), BF16, b * mem_len, 1024)
        x2, top_idx, gates = _mem_attn(qm, kv.reshape(b, mem_len, 2 * d), x1, w_o_mem[l].astype(BF16),
                                       ln_g[l, 1][None], ln_b[l, 1][None],
                                       w_router[l].astype(BF16), b_router[l][None], b, s)
        dest, gaps, block_expert, n_used = _dispatch_plan(top_idx, n)
        y_rows = _moe_experts(block_expert, n_used, dest, gaps, x2,
                              w_gate_up[l], b_gate_up[l][:, None, :], w_down[l], b_down[l][:, None, :])
        xf = _combine(dest, y_rows, x2, gates, ln_g[l, 2][None], ln_b[l, 2][None])
    return xf.reshape(b, s, d)
```

```python
import functools

import jax
import jax.numpy as jnp
from jax import lax
from jax.experimental import pallas as pl
from jax.experimental.pallas import tpu as pltpu

D_MODEL = 1024
CHUNK = 64
LEFT_CHUNKS = 8
LEFT = LEFT_CHUNKS * CHUNK
HEAD_DIM = 64
N_HEADS_A = 8
N_HEADS_B = 8
WIDTH_A = N_HEADS_A * HEAD_DIM
WIDTH_B = N_HEADS_B * HEAD_DIM
MIX_WIDTH = WIDTH_A + WIDTH_B
REL_CLIP = 128
N_HEADS_MEM = 4
HEAD_DIM_MEM = D_MODEL // N_HEADS_MEM
N_EXPERTS = 32
TOP_K = 4
D_FF = D_MODEL
SWIGLU_LIMIT = 7.0
SWIGLU_ALPHA = 1.702
LN_EPS = 1e-5
RMS_EPS = 1e-6
DEEPNORM_ALPHA = 2.0 ** 0.25
NEG_INF = -1e30
LOG2E = 1.4426950408889634

LANES = 128
LANE_BLOCKS = D_MODEL // LANES
VMEM_LIMIT = 48 * 1024 * 1024
VMEM_LIMIT_MOE = 56 * 1024 * 1024

TQ_A = 512
SUB_A = 256
HEADS_A_STEP = 4
TB_SB = 256
SB_TILES = 4
SB_UNDERFLOW = 110.0
TM_TOK = 512
TM_MOE = 256
MOE_SLOTS = 3
TM_COMB = 256
COMB_SLOTS = 3
TM_RANK = 512

F32 = jnp.float32
BF16 = jnp.bfloat16
_NT = (((1,), (1,)), ((), ()))


def _params(n_axes):
    return pltpu.CompilerParams(dimension_semantics=("arbitrary",) * n_axes,
                                vmem_limit_bytes=VMEM_LIMIT)


def _store_token_major(ref, value):
    rows = value.shape[0]
    for c in range(LANE_BLOCKS):
        ref[pl.ds(c, rows, stride=LANE_BLOCKS), :] = value[:, c * LANES:(c + 1) * LANES]


def _load_token_major(ref, first_row, rows):
    return jnp.concatenate(
        [ref[pl.ds(first_row * LANE_BLOCKS + c, rows, stride=LANE_BLOCKS), :] for c in range(LANE_BLOCKS)],
        axis=1)


def _layer_norm(r, g, b):
    mu = jnp.mean(r, axis=-1, keepdims=True)
    d = r - mu
    var = jnp.mean(d * d, axis=-1, keepdims=True)
    return d * lax.rsqrt(var + LN_EPS) * g + b


def _matmul_kernel(x_ref, w_ref, o_ref):
    o_ref[...] = jnp.dot(x_ref[...].astype(BF16), w_ref[...],
                         preferred_element_type=F32).astype(o_ref.dtype)


def _matmul(x, w, out_dtype, tm, tn):
    m, k = x.shape
    n = w.shape[1]
    return pl.pallas_call(
        _matmul_kernel,
        grid=(m // tm, n // tn),
        in_specs=[pl.BlockSpec((tm, k), lambda i, j: (i, 0)),
                  pl.BlockSpec((k, tn), lambda i, j: (0, j))],
        out_specs=pl.BlockSpec((tm, tn), lambda i, j: (i, j)),
        out_shape=jax.ShapeDtypeStruct((m, n), out_dtype),
        compiler_params=_params(2),
        name="matmul",
    )(x, w)


def _chunk_attn_kernel(q_ref, kp_ref, kc_ref, vp_ref, vc_ref, bias0_ref, bias1_ref, o_ref):
    w = SUB_A + LEFT
    for h in range(HEADS_A_STEP):
        sl = slice(h * HEAD_DIM, (h + 1) * HEAD_DIM)
        k = jnp.concatenate([kp_ref[:, sl], kc_ref[:, sl]], axis=0)
        v = jnp.concatenate([vp_ref[:, sl], vc_ref[:, sl]], axis=0)
        for sub, bias_ref in enumerate((bias0_ref, bias1_ref)):
            rows = slice(sub * SUB_A, (sub + 1) * SUB_A)
            keys = slice(sub * SUB_A, sub * SUB_A + w)
            s = lax.dot_general(q_ref[rows, sl], k[keys], _NT, preferred_element_type=F32) + bias_ref[h]
            m = jnp.max(s, axis=-1, keepdims=True)
            p = jnp.exp(s - m)
            l = jnp.sum(p, axis=-1, keepdims=True)
            o = jnp.dot(p.astype(BF16), v[keys], preferred_element_type=F32) / l
            o_ref[rows, sl] = o.astype(o_ref.dtype)


def _chunk_bias(rel_bias):
    w = SUB_A + LEFT
    heads = rel_bias.shape[0]
    m = jnp.arange(2 * w)
    rel = jnp.where(m < w, LEFT - m, LEFT + 2 * w - m)
    diag = rel_bias[:, jnp.clip(rel, -REL_CLIP, REL_CLIP) + REL_CLIP].astype(F32)[:, None, :]
    n_tables = 1 + TQ_A // SUB_A
    return pl.pallas_call(
        _chunk_bias_kernel,
        grid=(heads,),
        in_specs=[pl.BlockSpec((1, 1, 2 * w), lambda h: (h, 0, 0))],
        out_specs=pl.BlockSpec((n_tables, 1, SUB_A, w), lambda h: (0, h, 0, 0)),
        out_shape=jax.ShapeDtypeStruct((n_tables, heads, SUB_A, w), F32),
        compiler_params=_params(1),
        name="chunk_bias",
    )(diag)


def _chunk_bias_kernel(diag_ref, o_ref):
    w = SUB_A + LEFT
    rolled = pltpu.roll(jnp.broadcast_to(diag_ref[0], (SUB_A, 2 * w)), 0, 1, stride=1, stride_axis=0)
    col = lax.broadcasted_iota(jnp.int32, (SUB_A, w), 1)
    qc = lax.broadcasted_iota(jnp.int32, (SUB_A, w), 0) // CHUNK
    kc = col // CHUNK
    band = (kc >= qc) & (kc <= qc + LEFT_CHUNKS)
    table = jnp.where(band, rolled[:, :w], NEG_INF)
    o_ref[0, 0] = table
    for sub in range(TQ_A // SUB_A):
        o_ref[1 + sub, 0] = jnp.where(col >= LEFT - sub * SUB_A, table, NEG_INF)


def _chunk_attention(proj, bias, b, s):
    nq = s // TQ_A
    width = HEADS_A_STEP * HEAD_DIM
    groups = WIDTH_A // width
    blk = lambda off, prev: pl.BlockSpec(
        (None, TQ_A, width),
        (lambda bi, p, i: (bi, jnp.maximum(i - 1, 0), off + p)) if prev
        else (lambda bi, p, i: (bi, i, off + p)))
    table = lambda sub: pl.BlockSpec((None, HEADS_A_STEP, SUB_A, SUB_A + LEFT),
                                     lambda bi, p, i: (jnp.where(i == 0, 1 + sub, 0), p, 0, 0))
    return pl.pallas_call(
        _chunk_attn_kernel,
        grid=(b, groups, nq),
        in_specs=[blk(0, False),
                  blk(groups, True), blk(groups, False),
                  blk(2 * groups, True), blk(2 * groups, False),
                  table(0), table(1)],
        out_specs=pl.BlockSpec((None, TQ_A, width), lambda bi, p, i: (bi, i, p)),
        out_shape=jax.ShapeDtypeStruct((b, s, WIDTH_A), BF16),
        compiler_params=_params(3),
        name="chunk_attn",
    )(proj, proj, proj, proj, proj, bias, bias)


def _sb_block(q, k_ref, v_ref, sl, j, later, tri, causal):
    t = TB_SB
    start = pl.multiple_of(j * t, t)
    k = k_ref[pl.ds(start, t), sl]
    v = v_ref[pl.ds(start, t), sl]
    z = lax.dot_general(q, k, _NT, preferred_element_type=F32)
    sp = jnp.maximum(z, 0.0) + jnp.log(1.0 + jnp.exp2(jnp.abs(z) * (-LOG2E)))
    if causal is not None:
        sp = jnp.where(causal, sp, 0.0)
    csum = jnp.dot(sp.astype(BF16), tri, preferred_element_type=F32) + later
    wgt = jnp.exp(z - csum)
    if causal is not None:
        wgt = jnp.where(causal, wgt, 0.0)
    pv = jnp.dot(wgt.astype(BF16), v, preferred_element_type=F32)
    return pv, csum[:, 0:1]


def _sb_attn_kernel(q_ref, k_ref, v_ref, o_ref):
    first = pl.program_id(2) * SB_TILES
    t = TB_SB
    row = lax.broadcasted_iota(jnp.int32, (t, t), 0)
    col = lax.broadcasted_iota(jnp.int32, (t, t), 1)
    tri = jnp.where(row >= col, 1.0, 0.0).astype(BF16)
    causal = col < row
    chains = [(u, h) for u in range(SB_TILES) for h in range(2)]
    rows = lambda u: slice(u * t, (u + 1) * t)
    lanes = lambda h: slice(h * HEAD_DIM, (h + 1) * HEAD_DIM)
    qs = [q_ref[rows(u), lanes(h)] for u, h in chains]

    def block(c, j, later, masked):
        return _sb_block(qs[c], k_ref, v_ref, lanes(chains[c][1]), j, later, tri, causal if masked else None)

    carry = []
    for c, (u, h) in enumerate(chains):
        tile = first + u
        pv, later = block(c, tile, jnp.zeros((t, 1), F32), True)
        pv_prev, later_prev = block(c, jnp.maximum(tile - 1, 0), later, False)
        has_prev = tile >= 1
        carry += [jnp.where(has_prev, later_prev, later), pv + jnp.where(has_prev, pv_prev, 0.0)]

    def pending(c):
        todo = False
        for ci, (u, h) in enumerate(chains):
            todo = todo | ((first + u - 2 - c[0] >= 0) & (jnp.min(c[1 + 2 * ci]) < SB_UNDERFLOW))
        return todo

    def step(c):
        out = [c[0] + 1]
        for ci, (u, h) in enumerate(chains):
            j = first + u - 2 - c[0]
            pv, later = block(ci, jnp.maximum(j, 0), c[1 + 2 * ci], False)
            out += [jnp.where(j >= 0, later, c[1 + 2 * ci]), c[2 + 2 * ci] + jnp.where(j >= 0, pv, 0.0)]
        return tuple(out)

    carry = lax.while_loop(pending, step, (jnp.int32(0), *carry))
    for ci, (u, h) in enumerate(chains):
        o_ref[rows(u), lanes(h)] = carry[2 + 2 * ci].astype(o_ref.dtype)


def _sb_attention(proj, b, s):
    tq = SB_TILES * TB_SB
    nq = s // tq
    pairs = WIDTH_B // LANES
    base = 3 * WIDTH_A // LANES
    return pl.pallas_call(
        _sb_attn_kernel,
        grid=(b, pairs, nq),
        in_specs=[pl.BlockSpec((None, tq, LANES), lambda bi, p, i: (bi, i, base + p)),
                  pl.BlockSpec((None, s, LANES), lambda bi, p, i: (bi, 0, base + pairs + p)),
                  pl.BlockSpec((None, s, LANES), lambda bi, p, i: (bi, 0, base + 2 * pairs + p))],
        out_specs=pl.BlockSpec((None, tq, LANES), lambda bi, p, i: (bi, i, p)),
        out_shape=jax.ShapeDtypeStruct((b, s, WIDTH_B), BF16),
        compiler_params=_params(3),
        name="sb_attn",
    )(proj, proj, proj)


def _mixer_out_kernel(oa_ref, ob_ref, x_ref, ga_ref, gb_ref, wout_ref, lng_ref, lnb_ref, wq_ref,
                      x1_ref, qm_ref):
    def rms(ref, g_ref):
        a = ref[...].astype(F32)
        return (a * lax.rsqrt(jnp.mean(a * a, axis=-1, keepdims=True) + RMS_EPS) * g_ref[...]).astype(BF16)

    y = jnp.dot(rms(oa_ref, ga_ref), wout_ref[:WIDTH_A, :], preferred_element_type=F32)
    y = y + jnp.dot(rms(ob_ref, gb_ref), wout_ref[WIDTH_A:, :], preferred_element_type=F32)
    x1 = _layer_norm(DEEPNORM_ALPHA * x_ref[...] + y, lng_ref[...], lnb_ref[...])
    x1_ref[...] = x1
    qm_ref[...] = jnp.dot(x1.astype(BF16), wq_ref[...], preferred_element_type=F32).astype(BF16)


def _mixer_out(out_a, out_b, x, g_a, g_b, w_out, ln_g, ln_b, w_q):
    n = x.shape[0]
    tm = TM_TOK
    row = lambda width: pl.BlockSpec((tm, width), lambda i: (i, 0))
    full = lambda r, c: pl.BlockSpec((r, c), lambda i: (0, 0))
    return pl.pallas_call(
        _mixer_out_kernel,
        grid=(n // tm,),
        in_specs=[row(WIDTH_A), row(WIDTH_B), row(D_MODEL), full(1, WIDTH_A), full(1, WIDTH_B),
                  full(MIX_WIDTH, D_MODEL), full(1, D_MODEL), full(1, D_MODEL), full(D_MODEL, D_MODEL)],
        out_specs=[row(D_MODEL), row(D_MODEL)],
        out_shape=[jax.ShapeDtypeStruct((n, D_MODEL), F32), jax.ShapeDtypeStruct((n, D_MODEL), BF16)],
        compiler_params=_params(1),
        name="mixer_out",
    )(out_a, out_b, x, g_a, g_b, w_out, ln_g, ln_b, w_q)


def _mem_attn_kernel(q_ref, kv_ref, x1_ref, wo_ref, lng_ref, lnb_ref, wr_ref, br_ref,
                     x2_ref, idx_ref, gate_ref):
    heads = []
    for h in range(N_HEADS_MEM):
        sl = slice(h * HEAD_DIM_MEM, (h + 1) * HEAD_DIM_MEM)
        q = q_ref[:, sl] * (HEAD_DIM_MEM ** -0.5)
        k = kv_ref[:, sl]
        v = kv_ref[:, D_MODEL + h * HEAD_DIM_MEM:D_MODEL + (h + 1) * HEAD_DIM_MEM]
        s = lax.dot_general(q, k, _NT, preferred_element_type=F32)
        m = jnp.max(s, axis=-1, keepdims=True)
        p = jnp.exp(s - m)
        l = jnp.sum(p, axis=-1, keepdims=True)
        heads.append((jnp.dot(p.astype(BF16), v, preferred_element_type=F32) / l).astype(BF16))
    o = jnp.concatenate(heads, axis=-1)
    y = jnp.dot(o, wo_ref[...], preferred_element_type=F32)
    x2 = _layer_norm(DEEPNORM_ALPHA * x1_ref[...] + y, lng_ref[...], lnb_ref[...])
    _store_token_major(x2_ref, x2)

    logits = jnp.dot(x2.astype(BF16), wr_ref[...], preferred_element_type=F32) + br_ref[...]
    tm = logits.shape[0]
    e_iota = lax.broadcasted_iota(jnp.int32, (tm, N_EXPERTS), 1)
    lane = lax.broadcasted_iota(jnp.int32, (tm, LANES), 1)
    idx_out = jnp.zeros((tm, LANES), jnp.int32)
    val_out = jnp.zeros((tm, LANES), F32)
    top = None
    denom = jnp.zeros((tm, 1), F32)
    for kk in range(TOP_K):
        m = jnp.max(logits, axis=-1, keepdims=True)
        sel = jnp.min(jnp.where(logits == m, e_iota, N_EXPERTS), axis=-1, keepdims=True)
        if top is None:
            top = m
        e = jnp.exp(m - top)
        denom = denom + e
        idx_out = jnp.where(lane == kk, sel, idx_out)
        val_out = jnp.where(lane == kk, e, val_out)
        logits = jnp.where(e_iota == sel, -jnp.inf, logits)
    idx_ref[...] = idx_out
    gate_ref[...] = val_out / denom


def _mem_attn(qm, kv, x1, w_o, ln_g, ln_b, w_r, b_r, b, s):
    tm = TM_TOK
    nt = s // tm
    mem_len = kv.shape[1]
    row = lambda width: pl.BlockSpec((tm, width), lambda bi, i: (bi * nt + i, 0))
    full = lambda r, c: pl.BlockSpec((r, c), lambda bi, i: (0, 0))
    n = b * s
    return pl.pallas_call(
        _mem_attn_kernel,
        grid=(b, nt),
        in_specs=[row(D_MODEL), pl.BlockSpec((None, mem_len, 2 * D_MODEL), lambda bi, i: (bi, 0, 0)),
                  row(D_MODEL), full(D_MODEL, D_MODEL), full(1, D_MODEL), full(1, D_MODEL),
                  full(D_MODEL, N_EXPERTS), full(1, N_EXPERTS)],
        out_specs=[pl.BlockSpec((tm * LANE_BLOCKS, LANES), lambda bi, i: (bi * nt + i, 0)), row(LANES), row(LANES)],
        out_shape=[jax.ShapeDtypeStruct((n * LANE_BLOCKS, LANES), F32),
                   jax.ShapeDtypeStruct((n, LANES), jnp.int32),
                   jax.ShapeDtypeStruct((n, LANES), F32)],
        compiler_params=_params(2),
        name="mem_attn",
    )(qm, kv, x1, w_o, ln_g, ln_b, w_r, b_r)


def _gather_rows(idx_ref, first, count, src_hbm, dst_ref, sem):
    for r in range(count):
        src = pl.ds(pl.multiple_of(idx_ref[first + r], LANE_BLOCKS), LANE_BLOCKS)
        pltpu.make_async_copy(src_hbm.at[src], dst_ref.at[pl.ds(r * LANE_BLOCKS, LANE_BLOCKS)], sem).start(
            priority=r % 2)


def _wait_rows(count, src_hbm, dst_ref, sem):
    pltpu.make_async_copy(src_hbm.at[pl.ds(0, count * LANE_BLOCKS)], dst_ref, sem).wait()


def _fill_row_table(dest_ref, gap_ref, tok_ref):
    unroll = 16

    def clear(c, carry):
        for u in range(unroll):
            tok_ref[c * unroll + u] = 0
        return carry

    def place(c, carry):
        rows = [dest_ref[c * unroll + u] for u in range(unroll)]
        for u in range(unroll):
            tok_ref[rows[u]] = (c * (unroll // TOP_K) + u // TOP_K) * LANE_BLOCKS
        return carry

    for g in range(gap_ref.shape[1]):
        lax.fori_loop(gap_ref[0, g] // unroll, gap_ref[1, g] // unroll, clear, 0)
    lax.fori_loop(0, dest_ref.shape[0] // unroll, place, 0)


def _moe_kernel(bexp_ref, nused_ref, next_ref, dest_ref, gap_ref, x_hbm, wgu_hbm, bgu_ref, wd_hbm, bd_ref,
                y_ref, xbuf, wgu_f32, wd_f32, wgu_bf, wd_bf, tok_ref, sem, wsem):
    i = pl.program_id(0)
    slot = i % MOE_SLOTS
    ahead = MOE_SLOTS - 1
    n_used = nused_ref[0]

    def weight_copies(expert):
        return (pltpu.make_async_copy(wgu_hbm.at[expert], wgu_f32, wsem.at[0]),
                pltpu.make_async_copy(wd_hbm.at[expert], wd_f32, wsem.at[1]))

    @pl.when(i == 0)
    def _():
        for copy in weight_copies(bexp_ref[0]):
            copy.start()
        _fill_row_table(dest_ref, gap_ref, tok_ref)
        for b in range(ahead):
            _gather_rows(tok_ref, b * TM_MOE, TM_MOE, x_hbm, xbuf.at[b], sem.at[b])

    @pl.when(i < n_used + ahead)
    def _():
        _wait_rows(TM_MOE, x_hbm, xbuf.at[slot], sem.at[slot])

    @pl.when((i < n_used) & ((i == 0) | (bexp_ref[i] != bexp_ref[jnp.maximum(i - 1, 0)])))
    def _():
        for copy in weight_copies(bexp_ref[i]):
            copy.wait()
        wgu_bf[...] = wgu_f32[...].astype(BF16)
        wd_bf[...] = wd_f32[...].astype(BF16)

        @pl.when(next_ref[i] >= 0)
        def _():
            for copy in weight_copies(next_ref[i]):
                copy.start()

    @pl.when(i < n_used)
    def _():
        x = _load_token_major(xbuf.at[slot], 0, TM_MOE).astype(BF16)
        nxt = (i + ahead) % MOE_SLOTS
        _gather_rows(tok_ref, (i + ahead) * TM_MOE, TM_MOE, x_hbm, xbuf.at[nxt], sem.at[nxt])
        gu = jnp.dot(x, wgu_bf[...], preferred_element_type=F32) + bgu_ref[0]
        gate = jnp.minimum(gu[:, :D_FF], SWIGLU_LIMIT)
        up = jnp.clip(gu[:, D_FF:], -SWIGLU_LIMIT, SWIGLU_LIMIT)
        glu = gate * jax.nn.sigmoid(gate * SWIGLU_ALPHA)
        hid = ((up + 1.0) * glu).astype(BF16)
        _store_token_major(y_ref, jnp.dot(hid, wd_bf[...], preferred_element_type=F32) + bd_ref[0])

    @pl.when(i >= n_used)
    def _():
        y_ref[...] = jnp.zeros_like(y_ref)


def _moe_experts(block_expert, n_used, next_expert, dest, gaps, x2, w_gu, b_gu, w_d, b_d):
    nb = block_expert.shape[0]
    grid_spec = pltpu.PrefetchScalarGridSpec(
        num_scalar_prefetch=5,
        grid=(nb,),
        in_specs=[pl.BlockSpec(memory_space=pl.ANY),
                  pl.BlockSpec(memory_space=pl.ANY),
                  pl.BlockSpec((1, 1, 2 * D_FF), lambda i, be, nu, ne, de, ga: (be[i], 0, 0)),
                  pl.BlockSpec(memory_space=pl.ANY),
                  pl.BlockSpec((1, 1, D_MODEL), lambda i, be, nu, ne, de, ga: (be[i], 0, 0))],
        out_specs=pl.BlockSpec((TM_MOE * LANE_BLOCKS, LANES), lambda i, be, nu, ne, de, ga: (i, 0)),
        scratch_shapes=[pltpu.VMEM((MOE_SLOTS, TM_MOE * LANE_BLOCKS, LANES), F32),
                        pltpu.VMEM((D_MODEL, 2 * D_FF), F32), pltpu.VMEM((D_FF, D_MODEL), F32),
                        pltpu.VMEM((D_MODEL, 2 * D_FF), BF16), pltpu.VMEM((D_FF, D_MODEL), BF16),
                        pltpu.SMEM((nb * TM_MOE,), jnp.int32),
                        pltpu.SemaphoreType.DMA((MOE_SLOTS,)), pltpu.SemaphoreType.DMA((2,))],
    )
    return pl.pallas_call(
        _moe_kernel,
        grid_spec=grid_spec,
        out_shape=jax.ShapeDtypeStruct((nb * TM_MOE * LANE_BLOCKS, LANES), F32),
        compiler_params=pltpu.CompilerParams(dimension_semantics=("arbitrary",),
                                             vmem_limit_bytes=VMEM_LIMIT_MOE),
        name="moe_experts",
    )(block_expert, n_used, next_expert, dest, gaps, x2, w_gu, b_gu, w_d, b_d)


def _combine_kernel(idx_ref, y_hbm, x2_ref, gate_ref, lng_ref, lnb_ref, o_ref, *scratch):
    ybufs, sem = scratch[:COMB_SLOTS], scratch[COMB_SLOTS]
    i = pl.program_id(0)
    nt = pl.num_programs(0)
    ahead = COMB_SLOTS - 1
    rows = TOP_K * TM_COMB

    @pl.when(i == 0)
    def _():
        for t in range(ahead):
            _gather_rows(idx_ref, t * rows, rows, y_hbm, ybufs[t], sem.at[t])

    def compute(ybuf):
        g = gate_ref[...]
        ff = jnp.zeros((TM_COMB, D_MODEL), F32)
        for kk in range(TOP_K):
            ff = ff + g[:, kk:kk + 1] * _load_token_major(ybuf, kk * TM_COMB, TM_COMB)
        x2 = _load_token_major(x2_ref, 0, TM_COMB)
        o_ref[...] = _layer_norm(DEEPNORM_ALPHA * x2 + ff, lng_ref[...], lnb_ref[...])

    for k in range(COMB_SLOTS):
        nxt = (k + ahead) % COMB_SLOTS

        @pl.when((i % COMB_SLOTS == k) & (i + ahead < nt))
        def _():
            _wait_rows(rows, y_hbm, ybufs[k], sem.at[k])
            _gather_rows(idx_ref, (i + ahead) * rows, rows, y_hbm, ybufs[nxt], sem.at[nxt])
            compute(ybufs[k])

        @pl.when((i % COMB_SLOTS == k) & (i + ahead >= nt))
        def _():
            _wait_rows(rows, y_hbm, ybufs[k], sem.at[k])
            compute(ybufs[k])


def _combine(dest, y_rows, x2, gates, ln_g, ln_b):
    n = x2.shape[0] // LANE_BLOCKS
    nt = n // TM_COMB
    rows = TOP_K * TM_COMB
    idx = (dest * LANE_BLOCKS).reshape(nt, TM_COMB, TOP_K).transpose(0, 2, 1).reshape(nt * rows)
    row = lambda width: pl.BlockSpec((TM_COMB, width), lambda i, idx: (i, 0))
    full = lambda r, c: pl.BlockSpec((r, c), lambda i, idx: (0, 0))
    grid_spec = pltpu.PrefetchScalarGridSpec(
        num_scalar_prefetch=1,
        grid=(nt,),
        in_specs=[pl.BlockSpec(memory_space=pl.ANY),
                  pl.BlockSpec((TM_COMB * LANE_BLOCKS, LANES), lambda i, idx: (i, 0)), row(LANES),
                  full(1, D_MODEL), full(1, D_MODEL)],
        out_specs=row(D_MODEL),
        scratch_shapes=[pltpu.VMEM((rows * LANE_BLOCKS, LANES), F32)] * COMB_SLOTS
        + [pltpu.SemaphoreType.DMA((COMB_SLOTS,))],
    )
    return pl.pallas_call(
        _combine_kernel,
        grid_spec=grid_spec,
        out_shape=jax.ShapeDtypeStruct((n, D_MODEL), F32),
        compiler_params=_params(1),
        name="combine",
    )(idx, y_rows, x2, gates, ln_g, ln_b)


def _rank_kernel(idx_ref, dest_ref, counts_ref, running):
    p = pl.program_id(0)
    j = pl.program_id(1)
    t = TM_RANK
    idx = idx_ref[...]
    e_iota = lax.broadcasted_iota(jnp.int32, (t, LANES), 1)
    hot = [idx[:, k:k + 1] == e_iota for k in range(TOP_K)]
    chosen = sum(jnp.where(h, 1.0, 0.0) for h in hot)
    tile_counts = jnp.sum(chosen, axis=0, keepdims=True)

    @pl.when((p == 0) & (j == 0))
    def _():
        running[...] = jnp.zeros_like(running)

    @pl.when(p == 0)
    def _():
        running[...] += tile_counts
        dest_ref[...] = jnp.zeros_like(dest_ref)

    @pl.when((p == 1) & (j == 0))
    def _():
        counts = running[...]
        counts_ref[...] = counts
        padded = jnp.ceil(counts * (1.0 / TM_MOE)) * TM_MOE
        lane = lax.broadcasted_iota(jnp.int32, (1, LANES), 1)
        scan = padded
        shift = 1
        while shift < N_EXPERTS:
            scan = scan + jnp.where(lane >= shift, pltpu.roll(scan, shift, 1), 0.0)
            shift *= 2
        running[...] = scan - padded

    @pl.when(p == 1)
    def _():
        r_iota = lax.broadcasted_iota(jnp.int32, (t, t), 0)
        c_iota = lax.broadcasted_iota(jnp.int32, (t, t), 1)
        earlier = jnp.where(c_iota < r_iota, 1.0, 0.0).astype(BF16)
        base = jnp.dot(earlier, chosen.astype(BF16), preferred_element_type=F32) + running[...]
        out = jnp.zeros((t, LANES), jnp.int32)
        for k in range(TOP_K):
            row = jnp.sum(jnp.where(hot[k], base, 0.0), axis=-1, keepdims=True).astype(jnp.int32)
            out = jnp.where(e_iota == k, row, out)
        dest_ref[...] = out
        running[...] += tile_counts


def _dispatch_plan(top_idx, n):
    n_assign = n * TOP_K
    nb = n_assign // TM_MOE + N_EXPERTS + MOE_SLOTS - 2
    dest, counts = pl.pallas_call(
        _rank_kernel,
        grid=(2, n // TM_RANK),
        in_specs=[pl.BlockSpec((TM_RANK, LANES), lambda p, j: (j, 0))],
        out_specs=[pl.BlockSpec((TM_RANK, LANES), lambda p, j: (p * j, 0)),
                   pl.BlockSpec((1, LANES), lambda p, j: (0, 0))],
        out_shape=[jax.ShapeDtypeStruct((n, LANES), jnp.int32), jax.ShapeDtypeStruct((1, LANES), F32)],
        scratch_shapes=[pltpu.VMEM((1, LANES), F32)],
        compiler_params=_params(2),
        name="expert_rank",
    )(top_idx)
    dest = dest[:, :TOP_K].reshape(-1)
    counts = counts[0, :N_EXPERTS].astype(jnp.int32)
    padded = (counts + TM_MOE - 1) // TM_MOE * TM_MOE
    pad_end = jnp.cumsum(padded)
    tail = jnp.stack([pad_end[-1], jnp.int32(nb * TM_MOE)])[:, None]
    gaps = jnp.concatenate([jnp.stack([pad_end - padded + counts, pad_end]), tail], axis=1)
    block_first_row = jnp.arange(nb, dtype=jnp.int32) * TM_MOE
    block_expert = jnp.minimum(jnp.sum(pad_end[None, :] <= block_first_row[:, None], axis=1),
                               N_EXPERTS - 1).astype(jnp.int32)
    n_used = (pad_end[-1:] // TM_MOE).astype(jnp.int32)
    block = jnp.arange(nb, dtype=jnp.int32)
    starts = (block < n_used[0]) & (block > 0) & (block_expert != jnp.roll(block_expert, 1))
    later_start = jnp.min(jnp.where(starts[None, :] & (block[None, :] > block[:, None]), block[None, :], nb), axis=1)
    next_expert = jnp.where(later_start < nb, block_expert[jnp.minimum(later_start, nb - 1)], -1).astype(jnp.int32)
    return dest, gaps, block_expert, n_used, next_expert


def kernel(x, mem, w_in, rel_bias, g_group_a, g_group_b, w_out, w_q_mem, w_kv_mem, w_o_mem, w_router, b_router, w_gate_up, b_gate_up, w_down, b_down, ln_g, ln_b):
    b, s, d = x.shape
    n = b * s
    depth = w_in.shape[0]
    xf = x.reshape(n, d)
    for l in range(depth):
        q_scale = jnp.ones((3 * MIX_WIDTH,), F32).at[:WIDTH_A].set(HEAD_DIM ** -0.5)
        q_scale = q_scale.at[3 * WIDTH_A:3 * WIDTH_A + WIDTH_B].set(HEAD_DIM ** -0.5)
        proj = _matmul(xf, (w_in[l] * q_scale).astype(BF16), BF16, TM_TOK, 3 * MIX_WIDTH).reshape(b, s, 3 * MIX_WIDTH)
        out_a = _chunk_attention(proj, _chunk_bias(rel_bias[l]), b, s).reshape(n, WIDTH_A)
        out_b = _sb_attention(proj, b, s).reshape(n, WIDTH_B)
        x1, qm = _mixer_out(out_a, out_b, xf, g_group_a[l][None], g_group_b[l][None],
                            w_out[l].astype(BF16), ln_g[l, 0][None], ln_b[l, 0][None],
                            w_q_mem[l].astype(BF16))
        mem_len = mem.shape[1]
        kv = _matmul(mem.reshape(b * mem_len, d), w_kv_mem[l].astype(BF16), BF16, b * mem_len, 1024)
        x2, top_idx, gates = _mem_attn(qm, kv.reshape(b, mem_len, 2 * d), x1, w_o_mem[l].astype(BF16),
                                       ln_g[l, 1][None], ln_b[l, 1][None],
                                       w_router[l].astype(BF16), b_router[l][None], b, s)
        dest, gaps, block_expert, n_used, next_expert = _dispatch_plan(top_idx, n)
        y_rows = _moe_experts(block_expert, n_used, next_expert, dest, gaps, x2,
                              w_gate_up[l], b_gate_up[l][:, None, :], w_down[l], b_down[l][:, None, :])
        xf = _combine(dest, y_rows, x2, gates, ln_g[l, 2][None], ln_b[l, 2][None])
    return xf.reshape(b, s, d)
```

```python
import functools

import jax
import jax.numpy as jnp
from jax import lax
from jax.experimental import pallas as pl
from jax.experimental.pallas import tpu as pltpu

D_MODEL = 1024
CHUNK = 64
LEFT_CHUNKS = 8
LEFT = LEFT_CHUNKS * CHUNK
HEAD_DIM = 64
N_HEADS_A = 8
N_HEADS_B = 8
WIDTH_A = N_HEADS_A * HEAD_DIM
WIDTH_B = N_HEADS_B * HEAD_DIM
MIX_WIDTH = WIDTH_A + WIDTH_B
REL_CLIP = 128
N_HEADS_MEM = 4
HEAD_DIM_MEM = D_MODEL // N_HEADS_MEM
N_EXPERTS = 32
TOP_K = 4
D_FF = D_MODEL
SWIGLU_LIMIT = 7.0
SWIGLU_ALPHA = 1.702
LN_EPS = 1e-5
RMS_EPS = 1e-6
DEEPNORM_ALPHA = 2.0 ** 0.25
NEG_INF = -1e30
LOG2E = 1.4426950408889634

LANES = 128
LANE_BLOCKS = D_MODEL // LANES
VMEM_LIMIT = 48 * 1024 * 1024
VMEM_LIMIT_MOE = 56 * 1024 * 1024

TQ_A = 512
SUB_A = 256
HEADS_A_STEP = 4
TB_SB = 256
SB_TILES = 4
SB_UNDERFLOW = 110.0
TM_TOK = 512
TM_MOE = 256
MOE_SLOTS = 3
TM_COMB = 256
COMB_SLOTS = 3
TM_RANK = 512

F32 = jnp.float32
BF16 = jnp.bfloat16
_NT = (((1,), (1,)), ((), ()))


def _params(n_axes):
    return pltpu.CompilerParams(dimension_semantics=("arbitrary",) * n_axes,
                                vmem_limit_bytes=VMEM_LIMIT)


def _store_token_major(ref, value):
    rows = value.shape[0]
    for c in range(LANE_BLOCKS):
        ref[pl.ds(c, rows, stride=LANE_BLOCKS), :] = value[:, c * LANES:(c + 1) * LANES]


def _load_token_major(ref, first_row, rows):
    return jnp.concatenate(
        [ref[pl.ds(first_row * LANE_BLOCKS + c, rows, stride=LANE_BLOCKS), :] for c in range(LANE_BLOCKS)],
        axis=1)


def _layer_norm(r, g, b):
    mu = jnp.mean(r, axis=-1, keepdims=True)
    d = r - mu
    var = jnp.mean(d * d, axis=-1, keepdims=True)
    return d * lax.rsqrt(var + LN_EPS) * g + b


def _matmul_kernel(x_ref, w_ref, o_ref):
    o_ref[...] = jnp.dot(x_ref[...].astype(BF16), w_ref[...],
                         preferred_element_type=F32).astype(o_ref.dtype)


def _matmul(x, w, out_dtype, tm, tn):
    m, k = x.shape
    n = w.shape[1]
    return pl.pallas_call(
        _matmul_kernel,
        grid=(m // tm, n // tn),
        in_specs=[pl.BlockSpec((tm, k), lambda i, j: (i, 0)),
                  pl.BlockSpec((k, tn), lambda i, j: (0, j))],
        out_specs=pl.BlockSpec((tm, tn), lambda i, j: (i, j)),
        out_shape=jax.ShapeDtypeStruct((m, n), out_dtype),
        compiler_params=_params(2),
        name="matmul",
    )(x, w)


def _chunk_attn_kernel(q_ref, kp_ref, kc_ref, vp_ref, vc_ref, bias0_ref, bias1_ref, o_ref):
    w = SUB_A + LEFT
    for h in range(HEADS_A_STEP):
        sl = slice(h * HEAD_DIM, (h + 1) * HEAD_DIM)
        k = jnp.concatenate([kp_ref[:, sl], kc_ref[:, sl]], axis=0)
        v = jnp.concatenate([vp_ref[:, sl], vc_ref[:, sl]], axis=0)
        for sub, bias_ref in enumerate((bias0_ref, bias1_ref)):
            rows = slice(sub * SUB_A, (sub + 1) * SUB_A)
            keys = slice(sub * SUB_A, sub * SUB_A + w)
            s = lax.dot_general(q_ref[rows, sl], k[keys], _NT, preferred_element_type=F32) + bias_ref[h]
            m = jnp.max(s, axis=-1, keepdims=True)
            p = jnp.exp(s - m)
            l = jnp.sum(p, axis=-1, keepdims=True)
            o = jnp.dot(p.astype(BF16), v[keys], preferred_element_type=F32) / l
            o_ref[rows, sl] = o.astype(o_ref.dtype)


def _chunk_bias(rel_bias):
    w = SUB_A + LEFT
    heads = rel_bias.shape[0]
    m = jnp.arange(2 * w)
    rel = jnp.where(m < w, LEFT - m, LEFT + 2 * w - m)
    diag = rel_bias[:, jnp.clip(rel, -REL_CLIP, REL_CLIP) + REL_CLIP].astype(F32)[:, None, :]
    n_tables = 1 + TQ_A // SUB_A
    return pl.pallas_call(
        _chunk_bias_kernel,
        grid=(heads,),
        in_specs=[pl.BlockSpec((1, 1, 2 * w), lambda h: (h, 0, 0))],
        out_specs=pl.BlockSpec((n_tables, 1, SUB_A, w), lambda h: (0, h, 0, 0)),
        out_shape=jax.ShapeDtypeStruct((n_tables, heads, SUB_A, w), F32),
        compiler_params=_params(1),
        name="chunk_bias",
    )(diag)


def _chunk_bias_kernel(diag_ref, o_ref):
    w = SUB_A + LEFT
    rolled = pltpu.roll(jnp.broadcast_to(diag_ref[0], (SUB_A, 2 * w)), 0, 1, stride=1, stride_axis=0)
    col = lax.broadcasted_iota(jnp.int32, (SUB_A, w), 1)
    qc = lax.broadcasted_iota(jnp.int32, (SUB_A, w), 0) // CHUNK
    kc = col // CHUNK
    band = (kc >= qc) & (kc <= qc + LEFT_CHUNKS)
    table = jnp.where(band, rolled[:, :w], NEG_INF)
    o_ref[0, 0] = table
    for sub in range(TQ_A // SUB_A):
        o_ref[1 + sub, 0] = jnp.where(col >= LEFT - sub * SUB_A, table, NEG_INF)


def _chunk_attention(proj, bias, b, s):
    nq = s // TQ_A
    width = HEADS_A_STEP * HEAD_DIM
    groups = WIDTH_A // width
    blk = lambda off, prev: pl.BlockSpec(
        (None, TQ_A, width),
        (lambda bi, p, i: (bi, jnp.maximum(i - 1, 0), off + p)) if prev
        else (lambda bi, p, i: (bi, i, off + p)))
    table = lambda sub: pl.BlockSpec((None, HEADS_A_STEP, SUB_A, SUB_A + LEFT),
                                     lambda bi, p, i: (jnp.where(i == 0, 1 + sub, 0), p, 0, 0))
    return pl.pallas_call(
        _chunk_attn_kernel,
        grid=(b, groups, nq),
        in_specs=[blk(0, False),
                  blk(groups, True), blk(groups, False),
                  blk(2 * groups, True), blk(2 * groups, False),
                  table(0), table(1)],
        out_specs=pl.BlockSpec((None, TQ_A, width), lambda bi, p, i: (bi, i, p)),
        out_shape=jax.ShapeDtypeStruct((b, s, WIDTH_A), BF16),
        compiler_params=_params(3),
        name="chunk_attn",
    )(proj, proj, proj, proj, proj, bias, bias)


def _sb_block(q, k_ref, v_ref, sl, j, later, tri, causal):
    t = TB_SB
    start = pl.multiple_of(j * t, t)
    k = k_ref[pl.ds(start, t), sl]
    v = v_ref[pl.ds(start, t), sl]
    z = lax.dot_general(q, k, _NT, preferred_element_type=F32)
    sp = jnp.maximum(z, 0.0) + jnp.log(1.0 + jnp.exp2(jnp.abs(z) * (-LOG2E)))
    if causal is not None:
        sp = jnp.where(causal, sp, 0.0)
    csum = jnp.dot(sp.astype(BF16), tri, preferred_element_type=F32) + later
    wgt = jnp.exp(z - csum)
    if causal is not None:
        wgt = jnp.where(causal, wgt, 0.0)
    pv = jnp.dot(wgt.astype(BF16), v, preferred_element_type=F32)
    return pv, csum[:, 0:1]


def _sb_attn_kernel(q_ref, k_ref, v_ref, o_ref):
    first = pl.program_id(2) * SB_TILES
    t = TB_SB
    row = lax.broadcasted_iota(jnp.int32, (t, t), 0)
    col = lax.broadcasted_iota(jnp.int32, (t, t), 1)
    tri = jnp.where(row >= col, 1.0, 0.0).astype(BF16)
    causal = col < row
    chains = [(u, h) for u in range(SB_TILES) for h in range(2)]
    rows = lambda u: slice(u * t, (u + 1) * t)
    lanes = lambda h: slice(h * HEAD_DIM, (h + 1) * HEAD_DIM)
    qs = [q_ref[rows(u), lanes(h)] for u, h in chains]

    def block(c, j, later, masked):
        return _sb_block(qs[c], k_ref, v_ref, lanes(chains[c][1]), j, later, tri, causal if masked else None)

    carry = []
    for c, (u, h) in enumerate(chains):
        tile = first + u
        pv, later = block(c, tile, jnp.zeros((t, 1), F32), True)
        pv_prev, later_prev = block(c, jnp.maximum(tile - 1, 0), later, False)
        has_prev = tile >= 1
        carry += [jnp.where(has_prev, later_prev, later), pv + jnp.where(has_prev, pv_prev, 0.0)]

    def pending(c):
        todo = False
        for ci, (u, h) in enumerate(chains):
            todo = todo | ((first + u - 2 - c[0] >= 0) & (jnp.min(c[1 + 2 * ci]) < SB_UNDERFLOW))
        return todo

    def step(c):
        out = [c[0] + 1]
        for ci, (u, h) in enumerate(chains):
            j = first + u - 2 - c[0]
            pv, later = block(ci, jnp.maximum(j, 0), c[1 + 2 * ci], False)
            out += [jnp.where(j >= 0, later, c[1 + 2 * ci]), c[2 + 2 * ci] + jnp.where(j >= 0, pv, 0.0)]
        return tuple(out)

    carry = lax.while_loop(pending, step, (jnp.int32(0), *carry))
    for ci, (u, h) in enumerate(chains):
        o_ref[rows(u), lanes(h)] = carry[2 + 2 * ci].astype(o_ref.dtype)


def _sb_attention(proj, b, s):
    tq = SB_TILES * TB_SB
    nq = s // tq
    pairs = WIDTH_B // LANES
    base = 3 * WIDTH_A // LANES
    return pl.pallas_call(
        _sb_attn_kernel,
        grid=(b, pairs, nq),
        in_specs=[pl.BlockSpec((None, tq, LANES), lambda bi, p, i: (bi, i, base + p)),
                  pl.BlockSpec((None, s, LANES), lambda bi, p, i: (bi, 0, base + pairs + p)),
                  pl.BlockSpec((None, s, LANES), lambda bi, p, i: (bi, 0, base + 2 * pairs + p))],
        out_specs=pl.BlockSpec((None, tq, LANES), lambda bi, p, i: (bi, i, p)),
        out_shape=jax.ShapeDtypeStruct((b, s, WIDTH_B), BF16),
        compiler_params=_params(3),
        name="sb_attn",
    )(proj, proj, proj)


def _token_block_kernel(oa_ref, ob_ref, x_ref, ga_ref, gb_ref, wout_ref, ln1g_ref, ln1b_ref, wq_ref,
                        kv_ref, wo_ref, ln2g_ref, ln2b_ref, wr_ref, br_ref,
                        x2_ref, idx_ref, gate_ref):
    def rms(ref, g_ref):
        a = ref[...].astype(F32)
        return (a * lax.rsqrt(jnp.mean(a * a, axis=-1, keepdims=True) + RMS_EPS) * g_ref[...]).astype(BF16)

    y = jnp.dot(rms(oa_ref, ga_ref), wout_ref[:WIDTH_A, :], preferred_element_type=F32)
    y = y + jnp.dot(rms(ob_ref, gb_ref), wout_ref[WIDTH_A:, :], preferred_element_type=F32)
    x1 = _layer_norm(DEEPNORM_ALPHA * x_ref[...] + y, ln1g_ref[...], ln1b_ref[...])
    qm = jnp.dot(x1.astype(BF16), wq_ref[...], preferred_element_type=F32).astype(BF16)

    heads = []
    for h in range(N_HEADS_MEM):
        sl = slice(h * HEAD_DIM_MEM, (h + 1) * HEAD_DIM_MEM)
        q = qm[:, sl] * (HEAD_DIM_MEM ** -0.5)
        k = kv_ref[:, sl]
        v = kv_ref[:, D_MODEL + h * HEAD_DIM_MEM:D_MODEL + (h + 1) * HEAD_DIM_MEM]
        s = lax.dot_general(q, k, _NT, preferred_element_type=F32)
        m = jnp.max(s, axis=-1, keepdims=True)
        p = jnp.exp(s - m)
        l = jnp.sum(p, axis=-1, keepdims=True)
        heads.append((jnp.dot(p.astype(BF16), v, preferred_element_type=F32) / l).astype(BF16))
    o = jnp.concatenate(heads, axis=-1)
    y = jnp.dot(o, wo_ref[...], preferred_element_type=F32)
    x2 = _layer_norm(DEEPNORM_ALPHA * x1 + y, ln2g_ref[...], ln2b_ref[...])
    _store_token_major(x2_ref, x2)

    logits = jnp.dot(x2.astype(BF16), wr_ref[...], preferred_element_type=F32) + br_ref[...]
    tm = logits.shape[0]
    e_iota = lax.broadcasted_iota(jnp.int32, (tm, N_EXPERTS), 1)
    lane = lax.broadcasted_iota(jnp.int32, (tm, LANES), 1)
    idx_out = jnp.zeros((tm, LANES), jnp.int32)
    val_out = jnp.zeros((tm, LANES), F32)
    top = None
    denom = jnp.zeros((tm, 1), F32)
    for kk in range(TOP_K):
        m = jnp.max(logits, axis=-1, keepdims=True)
        sel = jnp.min(jnp.where(logits == m, e_iota, N_EXPERTS), axis=-1, keepdims=True)
        if top is None:
            top = m
        e = jnp.exp(m - top)
        denom = denom + e
        idx_out = jnp.where(lane == kk, sel, idx_out)
        val_out = jnp.where(lane == kk, e, val_out)
        logits = jnp.where(e_iota == sel, -jnp.inf, logits)
    idx_ref[...] = idx_out
    gate_ref[...] = val_out / denom


def _token_block(out_a, out_b, x, g_a, g_b, w_out, ln1_g, ln1_b, w_q, kv, w_o, ln2_g, ln2_b, w_r, b_r, b, s):
    tm = TM_TOK
    nt = s // tm
    mem_len = kv.shape[1]
    n = b * s
    row = lambda width: pl.BlockSpec((tm, width), lambda bi, i: (bi * nt + i, 0))
    full = lambda r, c: pl.BlockSpec((r, c), lambda bi, i: (0, 0))
    return pl.pallas_call(
        _token_block_kernel,
        grid=(b, nt),
        in_specs=[row(WIDTH_A), row(WIDTH_B), row(D_MODEL), full(1, WIDTH_A), full(1, WIDTH_B),
                  full(MIX_WIDTH, D_MODEL), full(1, D_MODEL), full(1, D_MODEL), full(D_MODEL, D_MODEL),
                  pl.BlockSpec((None, mem_len, 2 * D_MODEL), lambda bi, i: (bi, 0, 0)),
                  full(D_MODEL, D_MODEL), full(1, D_MODEL), full(1, D_MODEL),
                  full(D_MODEL, N_EXPERTS), full(1, N_EXPERTS)],
        out_specs=[pl.BlockSpec((tm * LANE_BLOCKS, LANES), lambda bi, i: (bi * nt + i, 0)), row(LANES), row(LANES)],
        out_shape=[jax.ShapeDtypeStruct((n * LANE_BLOCKS, LANES), F32),
                   jax.ShapeDtypeStruct((n, LANES), jnp.int32),
                   jax.ShapeDtypeStruct((n, LANES), F32)],
        compiler_params=_params(2),
        name="token_block",
    )(out_a, out_b, x, g_a, g_b, w_out, ln1_g, ln1_b, w_q, kv, w_o, ln2_g, ln2_b, w_r, b_r)


def _gather_rows(idx_ref, first, count, src_hbm, dst_ref, sem):
    for r in range(count):
        src = pl.ds(pl.multiple_of(idx_ref[first + r], LANE_BLOCKS), LANE_BLOCKS)
        pltpu.make_async_copy(src_hbm.at[src], dst_ref.at[pl.ds(r * LANE_BLOCKS, LANE_BLOCKS)], sem).start(
            priority=r % 2)


def _wait_rows(count, src_hbm, dst_ref, sem):
    pltpu.make_async_copy(src_hbm.at[pl.ds(0, count * LANE_BLOCKS)], dst_ref, sem).wait()


def _fill_row_table(dest_ref, gap_ref, tok_ref):
    unroll = 16

    def clear(c, carry):
        for u in range(unroll):
            tok_ref[c * unroll + u] = 0
        return carry

    def place(c, carry):
        rows = [dest_ref[c * unroll + u] for u in range(unroll)]
        for u in range(unroll):
            tok_ref[rows[u]] = (c * (unroll // TOP_K) + u // TOP_K) * LANE_BLOCKS
        return carry

    for g in range(gap_ref.shape[1]):
        lax.fori_loop(gap_ref[0, g] // unroll, gap_ref[1, g] // unroll, clear, 0)
    lax.fori_loop(0, dest_ref.shape[0] // unroll, place, 0)


def _moe_kernel(bexp_ref, nused_ref, next_ref, dest_ref, gap_ref, x_hbm, wgu_hbm, bgu_ref, wd_hbm, bd_ref,
                y_ref, xbuf, wgu_f32, wd_f32, wgu_bf, wd_bf, tok_ref, sem, wsem):
    i = pl.program_id(0)
    slot = i % MOE_SLOTS
    ahead = MOE_SLOTS - 1
    n_used = nused_ref[0]

    def weight_copies(expert):
        return (pltpu.make_async_copy(wgu_hbm.at[expert], wgu_f32, wsem.at[0]),
                pltpu.make_async_copy(wd_hbm.at[expert], wd_f32, wsem.at[1]))

    @pl.when(i == 0)
    def _():
        for copy in weight_copies(bexp_ref[0]):
            copy.start()
        _fill_row_table(dest_ref, gap_ref, tok_ref)
        for b in range(ahead):
            _gather_rows(tok_ref, b * TM_MOE, TM_MOE, x_hbm, xbuf.at[b], sem.at[b])

    @pl.when(i < n_used + ahead)
    def _():
        _wait_rows(TM_MOE, x_hbm, xbuf.at[slot], sem.at[slot])

    @pl.when((i < n_used) & ((i == 0) | (bexp_ref[i] != bexp_ref[jnp.maximum(i - 1, 0)])))
    def _():
        for copy in weight_copies(bexp_ref[i]):
            copy.wait()
        wgu_bf[...] = wgu_f32[...].astype(BF16)
        wd_bf[...] = wd_f32[...].astype(BF16)

        @pl.when(next_ref[i] >= 0)
        def _():
            for copy in weight_copies(next_ref[i]):
                copy.start()

    @pl.when(i < n_used)
    def _():
        x = _load_token_major(xbuf.at[slot], 0, TM_MOE).astype(BF16)
        nxt = (i + ahead) % MOE_SLOTS
        _gather_rows(tok_ref, (i + ahead) * TM_MOE, TM_MOE, x_hbm, xbuf.at[nxt], sem.at[nxt])
        gu = jnp.dot(x, wgu_bf[...], preferred_element_type=F32) + bgu_ref[0]
        gate = jnp.minimum(gu[:, :D_FF], SWIGLU_LIMIT)
        up = jnp.clip(gu[:, D_FF:], -SWIGLU_LIMIT, SWIGLU_LIMIT)
        glu = gate * jax.nn.sigmoid(gate * SWIGLU_ALPHA)
        hid = ((up + 1.0) * glu).astype(BF16)
        _store_token_major(y_ref, jnp.dot(hid, wd_bf[...], preferred_element_type=F32) + bd_ref[0])

    @pl.when(i >= n_used)
    def _():
        y_ref[...] = jnp.zeros_like(y_ref)


def _moe_experts(block_expert, n_used, next_expert, dest, gaps, x2, w_gu, b_gu, w_d, b_d):
    nb = block_expert.shape[0]
    grid_spec = pltpu.PrefetchScalarGridSpec(
        num_scalar_prefetch=5,
        grid=(nb,),
        in_specs=[pl.BlockSpec(memory_space=pl.ANY),
                  pl.BlockSpec(memory_space=pl.ANY),
                  pl.BlockSpec((1, 1, 2 * D_FF), lambda i, be, nu, ne, de, ga: (be[i], 0, 0)),
                  pl.BlockSpec(memory_space=pl.ANY),
                  pl.BlockSpec((1, 1, D_MODEL), lambda i, be, nu, ne, de, ga: (be[i], 0, 0))],
        out_specs=pl.BlockSpec((TM_MOE * LANE_BLOCKS, LANES), lambda i, be, nu, ne, de, ga: (i, 0)),
        scratch_shapes=[pltpu.VMEM((MOE_SLOTS, TM_MOE * LANE_BLOCKS, LANES), F32),
                        pltpu.VMEM((D_MODEL, 2 * D_FF), F32), pltpu.VMEM((D_FF, D_MODEL), F32),
                        pltpu.VMEM((D_MODEL, 2 * D_FF), BF16), pltpu.VMEM((D_FF, D_MODEL), BF16),
                        pltpu.SMEM((nb * TM_MOE,), jnp.int32),
                        pltpu.SemaphoreType.DMA((MOE_SLOTS,)), pltpu.SemaphoreType.DMA((2,))],
    )
    return pl.pallas_call(
        _moe_kernel,
        grid_spec=grid_spec,
        out_shape=jax.ShapeDtypeStruct((nb * TM_MOE * LANE_BLOCKS, LANES), F32),
        compiler_params=pltpu.CompilerParams(dimension_semantics=("arbitrary",),
                                             vmem_limit_bytes=VMEM_LIMIT_MOE),
        name="moe_experts",
    )(block_expert, n_used, next_expert, dest, gaps, x2, w_gu, b_gu, w_d, b_d)


def _combine_kernel(idx_ref, y_hbm, x2_ref, gate_ref, lng_ref, lnb_ref, o_ref, *scratch):
    ybufs, sem = scratch[:COMB_SLOTS], scratch[COMB_SLOTS]
    i = pl.program_id(0)
    nt = pl.num_programs(0)
    ahead = COMB_SLOTS - 1
    rows = TOP_K * TM_COMB

    @pl.when(i == 0)
    def _():
        for t in range(ahead):
            _gather_rows(idx_ref, t * rows, rows, y_hbm, ybufs[t], sem.at[t])

    def compute(ybuf):
        g = gate_ref[...]
        ff = jnp.zeros((TM_COMB, D_MODEL), F32)
        for kk in range(TOP_K):
            ff = ff + g[:, kk:kk + 1] * _load_token_major(ybuf, kk * TM_COMB, TM_COMB)
        x2 = _load_token_major(x2_ref, 0, TM_COMB)
        o_ref[...] = _layer_norm(DEEPNORM_ALPHA * x2 + ff, lng_ref[...], lnb_ref[...])

    for k in range(COMB_SLOTS):
        nxt = (k + ahead) % COMB_SLOTS

        @pl.when((i % COMB_SLOTS == k) & (i + ahead < nt))
        def _():
            _wait_rows(rows, y_hbm, ybufs[k], sem.at[k])
            _gather_rows(idx_ref, (i + ahead) * rows, rows, y_hbm, ybufs[nxt], sem.at[nxt])
            compute(ybufs[k])

        @pl.when((i % COMB_SLOTS == k) & (i + ahead >= nt))
        def _():
            _wait_rows(rows, y_hbm, ybufs[k], sem.at[k])
            compute(ybufs[k])


def _combine(dest, y_rows, x2, gates, ln_g, ln_b):
    n = x2.shape[0] // LANE_BLOCKS
    nt = n // TM_COMB
    rows = TOP_K * TM_COMB
    idx = (dest * LANE_BLOCKS).reshape(nt, TM_COMB, TOP_K).transpose(0, 2, 1).reshape(nt * rows)
    row = lambda width: pl.BlockSpec((TM_COMB, width), lambda i, idx: (i, 0))
    full = lambda r, c: pl.BlockSpec((r, c), lambda i, idx: (0, 0))
    grid_spec = pltpu.PrefetchScalarGridSpec(
        num_scalar_prefetch=1,
        grid=(nt,),
        in_specs=[pl.BlockSpec(memory_space=pl.ANY),
                  pl.BlockSpec((TM_COMB * LANE_BLOCKS, LANES), lambda i, idx: (i, 0)), row(LANES),
                  full(1, D_MODEL), full(1, D_MODEL)],
        out_specs=row(D_MODEL),
        scratch_shapes=[pltpu.VMEM((rows * LANE_BLOCKS, LANES), F32)] * COMB_SLOTS
        + [pltpu.SemaphoreType.DMA((COMB_SLOTS,))],
    )
    return pl.pallas_call(
        _combine_kernel,
        grid_spec=grid_spec,
        out_shape=jax.ShapeDtypeStruct((n, D_MODEL), F32),
        compiler_params=_params(1),
        name="combine",
    )(idx, y_rows, x2, gates, ln_g, ln_b)


def _rank_kernel(idx_ref, dest_ref, counts_ref, running):
    p = pl.program_id(0)
    j = pl.program_id(1)
    t = TM_RANK
    idx = idx_ref[...]
    e_iota = lax.broadcasted_iota(jnp.int32, (t, LANES), 1)
    hot = [idx[:, k:k + 1] == e_iota for k in range(TOP_K)]
    chosen = sum(jnp.where(h, 1.0, 0.0) for h in hot)
    tile_counts = jnp.sum(chosen, axis=0, keepdims=True)

    @pl.when((p == 0) & (j == 0))
    def _():
        running[...] = jnp.zeros_like(running)

    @pl.when(p == 0)
    def _():
        running[...] += tile_counts
        dest_ref[...] = jnp.zeros_like(dest_ref)

    @pl.when((p == 1) & (j == 0))
    def _():
        counts = running[...]
        counts_ref[...] = counts
        padded = jnp.ceil(counts * (1.0 / TM_MOE)) * TM_MOE
        lane = lax.broadcasted_iota(jnp.int32, (1, LANES), 1)
        scan = padded
        shift = 1
        while shift < N_EXPERTS:
            scan = scan + jnp.where(lane >= shift, pltpu.roll(scan, shift, 1), 0.0)
            shift *= 2
        running[...] = scan - padded

    @pl.when(p == 1)
    def _():
        r_iota = lax.broadcasted_iota(jnp.int32, (t, t), 0)
        c_iota = lax.broadcasted_iota(jnp.int32, (t, t), 1)
        earlier = jnp.where(c_iota < r_iota, 1.0, 0.0).astype(BF16)
        base = jnp.dot(earlier, chosen.astype(BF16), preferred_element_type=F32) + running[...]
        out = jnp.zeros((t, LANES), jnp.int32)
        for k in range(TOP_K):
            row = jnp.sum(jnp.where(hot[k], base, 0.0), axis=-1, keepdims=True).astype(jnp.int32)
            out = jnp.where(e_iota == k, row, out)
        dest_ref[...] = out
        running[...] += tile_counts


def _dispatch_plan(top_idx, n):
    n_assign = n * TOP_K
    nb = n_assign // TM_MOE + N_EXPERTS + MOE_SLOTS - 2
    dest, counts = pl.pallas_call(
        _rank_kernel,
        grid=(2, n // TM_RANK),
        in_specs=[pl.BlockSpec((TM_RANK, LANES), lambda p, j: (j, 0))],
        out_specs=[pl.BlockSpec((TM_RANK, LANES), lambda p, j: (p * j, 0)),
                   pl.BlockSpec((1, LANES), lambda p, j: (0, 0))],
        out_shape=[jax.ShapeDtypeStruct((n, LANES), jnp.int32), jax.ShapeDtypeStruct((1, LANES), F32)],
        scratch_shapes=[pltpu.VMEM((1, LANES), F32)],
        compiler_params=_params(2),
        name="expert_rank",
    )(top_idx)
    dest = dest[:, :TOP_K].reshape(-1)
    counts = counts[0, :N_EXPERTS].astype(jnp.int32)
    padded = (counts + TM_MOE - 1) // TM_MOE * TM_MOE
    pad_end = jnp.cumsum(padded)
    tail = jnp.stack([pad_end[-1], jnp.int32(nb * TM_MOE)])[:, None]
    gaps = jnp.concatenate([jnp.stack([pad_end - padded + counts, pad_end]), tail], axis=1)
    block_first_row = jnp.arange(nb, dtype=jnp.int32) * TM_MOE
    block_expert = jnp.minimum(jnp.sum(pad_end[None, :] <= block_first_row[:, None], axis=1),
                               N_EXPERTS - 1).astype(jnp.int32)
    n_used = (pad_end[-1:] // TM_MOE).astype(jnp.int32)
    block = jnp.arange(nb, dtype=jnp.int32)
    starts = (block < n_used[0]) & (block > 0) & (block_expert != jnp.roll(block_expert, 1))
    later_start = jnp.min(jnp.where(starts[None, :] & (block[None, :] > block[:, None]), block[None, :], nb), axis=1)
    next_expert = jnp.where(later_start < nb, block_expert[jnp.minimum(later_start, nb - 1)], -1).astype(jnp.int32)
    return dest, gaps, block_expert, n_used, next_expert


def kernel(x, mem, w_in, rel_bias, g_group_a, g_group_b, w_out, w_q_mem, w_kv_mem, w_o_mem, w_router, b_router, w_gate_up, b_gate_up, w_down, b_down, ln_g, ln_b):
    b, s, d = x.shape
    n = b * s
    depth = w_in.shape[0]
    xf = x.reshape(n, d)
    for l in range(depth):
        q_scale = jnp.ones((3 * MIX_WIDTH,), F32).at[:WIDTH_A].set(HEAD_DIM ** -0.5)
        q_scale = q_scale.at[3 * WIDTH_A:3 * WIDTH_A + WIDTH_B].set(HEAD_DIM ** -0.5)
        proj = _matmul(xf, (w_in[l] * q_scale).astype(BF16), BF16, TM_TOK, 3 * MIX_WIDTH).reshape(b, s, 3 * MIX_WIDTH)
        out_a = _chunk_attention(proj, _chunk_bias(rel_bias[l]), b, s).reshape(n, WIDTH_A)
        out_b = _sb_attention(proj, b, s).reshape(n, WIDTH_B)
        mem_len = mem.shape[1]
        kv = _matmul(mem.reshape(b * mem_len, d), w_kv_mem[l].astype(BF16), BF16, b * mem_len, 1024)
        x2, top_idx, gates = _token_block(
            out_a, out_b, xf, g_group_a[l][None], g_group_b[l][None], w_out[l].astype(BF16),
            ln_g[l, 0][None], ln_b[l, 0][None], w_q_mem[l].astype(BF16), kv.reshape(b, mem_len, 2 * d),
            w_o_mem[l].astype(BF16), ln_g[l, 1][None], ln_b[l, 1][None],
            w_router[l].astype(BF16), b_router[l][None], b, s)
        dest, gaps, block_expert, n_used, next_expert = _dispatch_plan(top_idx, n)
        y_rows = _moe_experts(block_expert, n_used, next_expert, dest, gaps, x2,
                              w_gate_up[l], b_gate_up[l][:, None, :], w_down[l], b_down[l][:, None, :])
        xf = _combine(dest, y_rows, x2, gates, ln_g[l, 2][None], ln_b[l, 2][None])
    return xf.reshape(b, s, d)
```

```python
import functools

import jax
import jax.numpy as jnp
from jax import lax
from jax.experimental import pallas as pl
from jax.experimental.pallas import tpu as pltpu

D_MODEL = 1024
CHUNK = 64
LEFT_CHUNKS = 8
LEFT = LEFT_CHUNKS * CHUNK
HEAD_DIM = 64
N_HEADS_A = 8
N_HEADS_B = 8
WIDTH_A = N_HEADS_A * HEAD_DIM
WIDTH_B = N_HEADS_B * HEAD_DIM
MIX_WIDTH = WIDTH_A + WIDTH_B
REL_CLIP = 128
N_HEADS_MEM = 4
HEAD_DIM_MEM = D_MODEL // N_HEADS_MEM
N_EXPERTS = 32
TOP_K = 4
D_FF = D_MODEL
SWIGLU_LIMIT = 7.0
SWIGLU_ALPHA = 1.702
LN_EPS = 1e-5
RMS_EPS = 1e-6
DEEPNORM_ALPHA = 2.0 ** 0.25
NEG_INF = -1e30
LOG2E = 1.4426950408889634

LANES = 128
LANE_BLOCKS = D_MODEL // LANES
VMEM_LIMIT = 48 * 1024 * 1024
VMEM_LIMIT_MOE = 56 * 1024 * 1024

TQ_A = 512
SUB_A = 256
HEADS_A_STEP = 4
TB_SB = 256
SB_TILES = 4
SB_UNDERFLOW = 110.0
TM_TOK = 512
TM_MOE = 256
MOE_SLOTS = 3
TM_COMB = 256
COMB_SLOTS = 3
TM_RANK = 512

F32 = jnp.float32
BF16 = jnp.bfloat16
_NT = (((1,), (1,)), ((), ()))


def _params(n_axes):
    return pltpu.CompilerParams(dimension_semantics=("arbitrary",) * n_axes,
                                vmem_limit_bytes=VMEM_LIMIT)


def _store_token_major(ref, value):
    rows = value.shape[0]
    for c in range(LANE_BLOCKS):
        ref[pl.ds(c, rows, stride=LANE_BLOCKS), :] = value[:, c * LANES:(c + 1) * LANES]


def _load_token_major(ref, first_row, rows):
    return jnp.concatenate(
        [ref[pl.ds(first_row * LANE_BLOCKS + c, rows, stride=LANE_BLOCKS), :] for c in range(LANE_BLOCKS)],
        axis=1)


def _layer_norm(r, g, b):
    mu = jnp.mean(r, axis=-1, keepdims=True)
    d = r - mu
    var = jnp.mean(d * d, axis=-1, keepdims=True)
    return d * lax.rsqrt(var + LN_EPS) * g + b


def _matmul_kernel(x_ref, w_ref, o_ref):
    o_ref[...] = jnp.dot(x_ref[...].astype(BF16), w_ref[...],
                         preferred_element_type=F32).astype(o_ref.dtype)


def _matmul(x, w, out_dtype, tm, tn):
    m, k = x.shape
    n = w.shape[1]
    return pl.pallas_call(
        _matmul_kernel,
        grid=(m // tm, n // tn),
        in_specs=[pl.BlockSpec((tm, k), lambda i, j: (i, 0)),
                  pl.BlockSpec((k, tn), lambda i, j: (0, j))],
        out_specs=pl.BlockSpec((tm, tn), lambda i, j: (i, j)),
        out_shape=jax.ShapeDtypeStruct((m, n), out_dtype),
        compiler_params=_params(2),
        name="matmul",
    )(x, w)


def _chunk_attn_kernel(q_ref, kp_ref, kc_ref, vp_ref, vc_ref, bias0_ref, bias1_ref, o_ref):
    w = SUB_A + LEFT
    for h in range(HEADS_A_STEP):
        sl = slice(h * HEAD_DIM, (h + 1) * HEAD_DIM)
        k = jnp.concatenate([kp_ref[:, sl], kc_ref[:, sl]], axis=0)
        v = jnp.concatenate([vp_ref[:, sl], vc_ref[:, sl]], axis=0)
        for sub, bias_ref in enumerate((bias0_ref, bias1_ref)):
            rows = slice(sub * SUB_A, (sub + 1) * SUB_A)
            keys = slice(sub * SUB_A, sub * SUB_A + w)
            s = lax.dot_general(q_ref[rows, sl], k[keys], _NT, preferred_element_type=F32) + bias_ref[h]
            m = jnp.max(s, axis=-1, keepdims=True)
            p = jnp.exp(s - m)
            l = jnp.sum(p, axis=-1, keepdims=True)
            o = jnp.dot(p.astype(BF16), v[keys], preferred_element_type=F32) / l
            o_ref[rows, sl] = o.astype(o_ref.dtype)


def _chunk_bias(rel_bias):
    w = SUB_A + LEFT
    heads = rel_bias.shape[0]
    m = jnp.arange(2 * w)
    rel = jnp.where(m < w, LEFT - m, LEFT + 2 * w - m)
    diag = rel_bias[:, jnp.clip(rel, -REL_CLIP, REL_CLIP) + REL_CLIP].astype(F32)[:, None, :]
    n_tables = 1 + TQ_A // SUB_A
    return pl.pallas_call(
        _chunk_bias_kernel,
        grid=(heads,),
        in_specs=[pl.BlockSpec((1, 1, 2 * w), lambda h: (h, 0, 0))],
        out_specs=pl.BlockSpec((n_tables, 1, SUB_A, w), lambda h: (0, h, 0, 0)),
        out_shape=jax.ShapeDtypeStruct((n_tables, heads, SUB_A, w), F32),
        compiler_params=_params(1),
        name="chunk_bias",
    )(diag)


def _chunk_bias_kernel(diag_ref, o_ref):
    w = SUB_A + LEFT
    rolled = pltpu.roll(jnp.broadcast_to(diag_ref[0], (SUB_A, 2 * w)), 0, 1, stride=1, stride_axis=0)
    col = lax.broadcasted_iota(jnp.int32, (SUB_A, w), 1)
    qc = lax.broadcasted_iota(jnp.int32, (SUB_A, w), 0) // CHUNK
    kc = col // CHUNK
    band = (kc >= qc) & (kc <= qc + LEFT_CHUNKS)
    table = jnp.where(band, rolled[:, :w], NEG_INF)
    o_ref[0, 0] = table
    for sub in range(TQ_A // SUB_A):
        o_ref[1 + sub, 0] = jnp.where(col >= LEFT - sub * SUB_A, table, NEG_INF)


def _chunk_attention(proj, bias, b, s):
    nq = s // TQ_A
    width = HEADS_A_STEP * HEAD_DIM
    groups = WIDTH_A // width
    blk = lambda off, prev: pl.BlockSpec(
        (None, TQ_A, width),
        (lambda bi, p, i: (bi, jnp.maximum(i - 1, 0), off + p)) if prev
        else (lambda bi, p, i: (bi, i, off + p)))
    table = lambda sub: pl.BlockSpec((None, HEADS_A_STEP, SUB_A, SUB_A + LEFT),
                                     lambda bi, p, i: (jnp.where(i == 0, 1 + sub, 0), p, 0, 0))
    return pl.pallas_call(
        _chunk_attn_kernel,
        grid=(b, groups, nq),
        in_specs=[blk(0, False),
                  blk(groups, True), blk(groups, False),
                  blk(2 * groups, True), blk(2 * groups, False),
                  table(0), table(1)],
        out_specs=pl.BlockSpec((None, TQ_A, width), lambda bi, p, i: (bi, i, p)),
        out_shape=jax.ShapeDtypeStruct((b, s, WIDTH_A), BF16),
        compiler_params=_params(3),
        name="chunk_attn",
    )(proj, proj, proj, proj, proj, bias, bias)


def _sb_block(q, k_ref, v_ref, sl, j, later, tri, causal):
    t = TB_SB
    start = pl.multiple_of(j * t, t)
    k = k_ref[pl.ds(start, t), sl]
    v = v_ref[pl.ds(start, t), sl]
    z = lax.dot_general(q, k, _NT, preferred_element_type=F32)
    sp = jnp.maximum(z, 0.0) + jnp.log(1.0 + jnp.exp2(jnp.abs(z) * (-LOG2E)))
    if causal is not None:
        sp = jnp.where(causal, sp, 0.0)
    csum = jnp.dot(sp.astype(BF16), tri, preferred_element_type=F32) + later
    wgt = jnp.exp(z - csum)
    if causal is not None:
        wgt = jnp.where(causal, wgt, 0.0)
    pv = jnp.dot(wgt.astype(BF16), v, preferred_element_type=F32)
    return pv, csum[:, 0:1]


def _sb_attn_kernel(q_ref, k_ref, v_ref, o_ref):
    first = pl.program_id(2) * SB_TILES
    t = TB_SB
    row = lax.broadcasted_iota(jnp.int32, (t, t), 0)
    col = lax.broadcasted_iota(jnp.int32, (t, t), 1)
    tri = jnp.where(row >= col, 1.0, 0.0).astype(BF16)
    causal = col < row
    chains = [(u, h) for u in range(SB_TILES) for h in range(2)]
    rows = lambda u: slice(u * t, (u + 1) * t)
    lanes = lambda h: slice(h * HEAD_DIM, (h + 1) * HEAD_DIM)
    qs = [q_ref[rows(u), lanes(h)] for u, h in chains]

    def block(c, j, later, masked):
        return _sb_block(qs[c], k_ref, v_ref, lanes(chains[c][1]), j, later, tri, causal if masked else None)

    carry = []
    for c, (u, h) in enumerate(chains):
        tile = first + u
        pv, later = block(c, tile, jnp.zeros((t, 1), F32), True)
        pv_prev, later_prev = block(c, jnp.maximum(tile - 1, 0), later, False)
        has_prev = tile >= 1
        carry += [jnp.where(has_prev, later_prev, later), pv + jnp.where(has_prev, pv_prev, 0.0)]

    def pending(c):
        todo = False
        for ci, (u, h) in enumerate(chains):
            todo = todo | ((first + u - 2 - c[0] >= 0) & (jnp.min(c[1 + 2 * ci]) < SB_UNDERFLOW))
        return todo

    def step(c):
        out = [c[0] + 1]
        for ci, (u, h) in enumerate(chains):
            j = first + u - 2 - c[0]
            pv, later = block(ci, jnp.maximum(j, 0), c[1 + 2 * ci], False)
            out += [jnp.where(j >= 0, later, c[1 + 2 * ci]), c[2 + 2 * ci] + jnp.where(j >= 0, pv, 0.0)]
        return tuple(out)

    carry = lax.while_loop(pending, step, (jnp.int32(0), *carry))
    for ci, (u, h) in enumerate(chains):
        o_ref[rows(u), lanes(h)] = carry[2 + 2 * ci].astype(o_ref.dtype)


def _sb_attention(proj, b, s):
    tq = SB_TILES * TB_SB
    nq = s // tq
    pairs = WIDTH_B // LANES
    base = 3 * WIDTH_A // LANES
    return pl.pallas_call(
        _sb_attn_kernel,
        grid=(b, pairs, nq),
        in_specs=[pl.BlockSpec((None, tq, LANES), lambda bi, p, i: (bi, i, base + p)),
                  pl.BlockSpec((None, s, LANES), lambda bi, p, i: (bi, 0, base + pairs + p)),
                  pl.BlockSpec((None, s, LANES), lambda bi, p, i: (bi, 0, base + 2 * pairs + p))],
        out_specs=pl.BlockSpec((None, tq, LANES), lambda bi, p, i: (bi, i, p)),
        out_shape=jax.ShapeDtypeStruct((b, s, WIDTH_B), BF16),
        compiler_params=_params(3),
        name="sb_attn",
    )(proj, proj, proj)


def _token_block_kernel(oa_ref, ob_ref, x_ref, ga_ref, gb_ref, wout_ref, ln1g_ref, ln1b_ref, wq_ref,
                        kv_ref, wo_ref, ln2g_ref, ln2b_ref, wr_ref, br_ref,
                        x2_ref, idx_ref, gate_ref, counts_ref):
    def rms(ref, g_ref):
        a = ref[...].astype(F32)
        return (a * lax.rsqrt(jnp.mean(a * a, axis=-1, keepdims=True) + RMS_EPS) * g_ref[...]).astype(BF16)

    y = jnp.dot(rms(oa_ref, ga_ref), wout_ref[:WIDTH_A, :], preferred_element_type=F32)
    y = y + jnp.dot(rms(ob_ref, gb_ref), wout_ref[WIDTH_A:, :], preferred_element_type=F32)
    x1 = _layer_norm(DEEPNORM_ALPHA * x_ref[...] + y, ln1g_ref[...], ln1b_ref[...])
    qm = jnp.dot(x1.astype(BF16), wq_ref[...], preferred_element_type=F32).astype(BF16)

    heads = []
    for h in range(N_HEADS_MEM):
        sl = slice(h * HEAD_DIM_MEM, (h + 1) * HEAD_DIM_MEM)
        q = qm[:, sl] * (HEAD_DIM_MEM ** -0.5)
        k = kv_ref[:, sl]
        v = kv_ref[:, D_MODEL + h * HEAD_DIM_MEM:D_MODEL + (h + 1) * HEAD_DIM_MEM]
        s = lax.dot_general(q, k, _NT, preferred_element_type=F32)
        m = jnp.max(s, axis=-1, keepdims=True)
        p = jnp.exp(s - m)
        l = jnp.sum(p, axis=-1, keepdims=True)
        heads.append((jnp.dot(p.astype(BF16), v, preferred_element_type=F32) / l).astype(BF16))
    o = jnp.concatenate(heads, axis=-1)
    y = jnp.dot(o, wo_ref[...], preferred_element_type=F32)
    x2 = _layer_norm(DEEPNORM_ALPHA * x1 + y, ln2g_ref[...], ln2b_ref[...])
    _store_token_major(x2_ref, x2)

    logits = jnp.dot(x2.astype(BF16), wr_ref[...], preferred_element_type=F32) + br_ref[...]
    tm = logits.shape[0]
    e_iota = lax.broadcasted_iota(jnp.int32, (tm, N_EXPERTS), 1)
    lane = lax.broadcasted_iota(jnp.int32, (tm, LANES), 1)
    idx_out = jnp.zeros((tm, LANES), jnp.int32)
    val_out = jnp.zeros((tm, LANES), F32)
    top = None
    denom = jnp.zeros((tm, 1), F32)
    chosen = jnp.zeros((1, LANES), F32)
    for kk in range(TOP_K):
        m = jnp.max(logits, axis=-1, keepdims=True)
        sel = jnp.min(jnp.where(logits == m, e_iota, N_EXPERTS), axis=-1, keepdims=True)
        if top is None:
            top = m
        e = jnp.exp(m - top)
        denom = denom + e
        idx_out = jnp.where(lane == kk, sel, idx_out)
        val_out = jnp.where(lane == kk, e, val_out)
        chosen = chosen + jnp.sum(jnp.where(lane == sel, 1.0, 0.0), axis=0, keepdims=True)
        logits = jnp.where(e_iota == sel, -jnp.inf, logits)
    idx_ref[...] = idx_out
    gate_ref[...] = val_out / denom

    @pl.when((pl.program_id(0) == 0) & (pl.program_id(1) == 0))
    def _():
        counts_ref[...] = jnp.zeros_like(counts_ref)

    counts_ref[...] += chosen


def _token_block(out_a, out_b, x, g_a, g_b, w_out, ln1_g, ln1_b, w_q, kv, w_o, ln2_g, ln2_b, w_r, b_r, b, s):
    tm = TM_TOK
    nt = s // tm
    mem_len = kv.shape[1]
    n = b * s
    row = lambda width: pl.BlockSpec((tm, width), lambda bi, i: (bi * nt + i, 0))
    full = lambda r, c: pl.BlockSpec((r, c), lambda bi, i: (0, 0))
    return pl.pallas_call(
        _token_block_kernel,
        grid=(b, nt),
        in_specs=[row(WIDTH_A), row(WIDTH_B), row(D_MODEL), full(1, WIDTH_A), full(1, WIDTH_B),
                  full(MIX_WIDTH, D_MODEL), full(1, D_MODEL), full(1, D_MODEL), full(D_MODEL, D_MODEL),
                  pl.BlockSpec((None, mem_len, 2 * D_MODEL), lambda bi, i: (bi, 0, 0)),
                  full(D_MODEL, D_MODEL), full(1, D_MODEL), full(1, D_MODEL),
                  full(D_MODEL, N_EXPERTS), full(1, N_EXPERTS)],
        out_specs=[pl.BlockSpec((tm * LANE_BLOCKS, LANES), lambda bi, i: (bi * nt + i, 0)), row(LANES), row(LANES),
                   full(1, LANES)],
        out_shape=[jax.ShapeDtypeStruct((n * LANE_BLOCKS, LANES), F32),
                   jax.ShapeDtypeStruct((n, LANES), jnp.int32),
                   jax.ShapeDtypeStruct((n, LANES), F32),
                   jax.ShapeDtypeStruct((1, LANES), F32)],
        compiler_params=_params(2),
        name="token_block",
    )(out_a, out_b, x, g_a, g_b, w_out, ln1_g, ln1_b, w_q, kv, w_o, ln2_g, ln2_b, w_r, b_r)


def _gather_rows(idx_ref, first, count, src_hbm, dst_ref, sem):
    for r in range(count):
        src = pl.ds(pl.multiple_of(idx_ref[first + r], LANE_BLOCKS), LANE_BLOCKS)
        pltpu.make_async_copy(src_hbm.at[src], dst_ref.at[pl.ds(r * LANE_BLOCKS, LANE_BLOCKS)], sem).start(
            priority=r % 2)


def _wait_rows(count, src_hbm, dst_ref, sem):
    pltpu.make_async_copy(src_hbm.at[pl.ds(0, count * LANE_BLOCKS)], dst_ref, sem).wait()


def _fill_row_table(dest_ref, gap_ref, tok_ref):
    unroll = 16

    def clear(c, carry):
        for u in range(unroll):
            tok_ref[c * unroll + u] = 0
        return carry

    def place(c, carry):
        rows = [dest_ref[c * unroll + u] for u in range(unroll)]
        for u in range(unroll):
            tok_ref[rows[u]] = (c * (unroll // TOP_K) + u // TOP_K) * LANE_BLOCKS
        return carry

    for g in range(gap_ref.shape[1]):
        lax.fori_loop(gap_ref[0, g] // unroll, gap_ref[1, g] // unroll, clear, 0)
    lax.fori_loop(0, dest_ref.shape[0] // unroll, place, 0)


def _moe_kernel(bexp_ref, nused_ref, next_ref, dest_ref, gap_ref, x_hbm, wgu_hbm, bgu_ref, wd_hbm, bd_ref,
                y_ref, xbuf, wgu_f32, wd_f32, wgu_bf, wd_bf, tok_ref, sem, wsem):
    i = pl.program_id(0)
    slot = i % MOE_SLOTS
    ahead = MOE_SLOTS - 1
    n_used = nused_ref[0]

    def weight_copies(expert):
        return (pltpu.make_async_copy(wgu_hbm.at[expert], wgu_f32, wsem.at[0]),
                pltpu.make_async_copy(wd_hbm.at[expert], wd_f32, wsem.at[1]))

    @pl.when(i == 0)
    def _():
        for copy in weight_copies(bexp_ref[0]):
            copy.start()
        _fill_row_table(dest_ref, gap_ref, tok_ref)
        for b in range(ahead):
            _gather_rows(tok_ref, b * TM_MOE, TM_MOE, x_hbm, xbuf.at[b], sem.at[b])

    @pl.when(i < n_used + ahead)
    def _():
        _wait_rows(TM_MOE, x_hbm, xbuf.at[slot], sem.at[slot])

    @pl.when((i < n_used) & ((i == 0) | (bexp_ref[i] != bexp_ref[jnp.maximum(i - 1, 0)])))
    def _():
        for copy in weight_copies(bexp_ref[i]):
            copy.wait()
        wgu_bf[...] = wgu_f32[...].astype(BF16)
        wd_bf[...] = wd_f32[...].astype(BF16)

        @pl.when(next_ref[i] >= 0)
        def _():
            for copy in weight_copies(next_ref[i]):
                copy.start()

    @pl.when(i < n_used)
    def _():
        x = _load_token_major(xbuf.at[slot], 0, TM_MOE).astype(BF16)
        nxt = (i + ahead) % MOE_SLOTS
        _gather_rows(tok_ref, (i + ahead) * TM_MOE, TM_MOE, x_hbm, xbuf.at[nxt], sem.at[nxt])
        gu = jnp.dot(x, wgu_bf[...], preferred_element_type=F32) + bgu_ref[0]
        gate = jnp.minimum(gu[:, :D_FF], SWIGLU_LIMIT)
        up = jnp.clip(gu[:, D_FF:], -SWIGLU_LIMIT, SWIGLU_LIMIT)
        glu = gate * jax.nn.sigmoid(gate * SWIGLU_ALPHA)
        hid = ((up + 1.0) * glu).astype(BF16)
        _store_token_major(y_ref, jnp.dot(hid, wd_bf[...], preferred_element_type=F32) + bd_ref[0])

    @pl.when(i >= n_used)
    def _():
        y_ref[...] = jnp.zeros_like(y_ref)


def _moe_experts(block_expert, n_used, next_expert, dest, gaps, x2, w_gu, b_gu, w_d, b_d):
    nb = block_expert.shape[0]
    grid_spec = pltpu.PrefetchScalarGridSpec(
        num_scalar_prefetch=5,
        grid=(nb,),
        in_specs=[pl.BlockSpec(memory_space=pl.ANY),
                  pl.BlockSpec(memory_space=pl.ANY),
                  pl.BlockSpec((1, 1, 2 * D_FF), lambda i, be, nu, ne, de, ga: (be[i], 0, 0)),
                  pl.BlockSpec(memory_space=pl.ANY),
                  pl.BlockSpec((1, 1, D_MODEL), lambda i, be, nu, ne, de, ga: (be[i], 0, 0))],
        out_specs=pl.BlockSpec((TM_MOE * LANE_BLOCKS, LANES), lambda i, be, nu, ne, de, ga: (i, 0)),
        scratch_shapes=[pltpu.VMEM((MOE_SLOTS, TM_MOE * LANE_BLOCKS, LANES), F32),
                        pltpu.VMEM((D_MODEL, 2 * D_FF), F32), pltpu.VMEM((D_FF, D_MODEL), F32),
                        pltpu.VMEM((D_MODEL, 2 * D_FF), BF16), pltpu.VMEM((D_FF, D_MODEL), BF16),
                        pltpu.SMEM((nb * TM_MOE,), jnp.int32),
                        pltpu.SemaphoreType.DMA((MOE_SLOTS,)), pltpu.SemaphoreType.DMA((2,))],
    )
    return pl.pallas_call(
        _moe_kernel,
        grid_spec=grid_spec,
        out_shape=jax.ShapeDtypeStruct((nb * TM_MOE * LANE_BLOCKS, LANES), F32),
        compiler_params=pltpu.CompilerParams(dimension_semantics=("arbitrary",),
                                             vmem_limit_bytes=VMEM_LIMIT_MOE),
        name="moe_experts",
    )(block_expert, n_used, next_expert, dest, gaps, x2, w_gu, b_gu, w_d, b_d)


def _combine_kernel(idx_ref, y_hbm, x2_ref, gate_ref, lng_ref, lnb_ref, o_ref, *scratch):
    ybufs, sem = scratch[:COMB_SLOTS], scratch[COMB_SLOTS]
    i = pl.program_id(0)
    nt = pl.num_programs(0)
    ahead = COMB_SLOTS - 1
    rows = TOP_K * TM_COMB

    @pl.when(i == 0)
    def _():
        for t in range(ahead):
            _gather_rows(idx_ref, t * rows, rows, y_hbm, ybufs[t], sem.at[t])

    def compute(ybuf):
        g = gate_ref[...]
        ff = jnp.zeros((TM_COMB, D_MODEL), F32)
        for kk in range(TOP_K):
            ff = ff + g[:, kk:kk + 1] * _load_token_major(ybuf, kk * TM_COMB, TM_COMB)
        x2 = _load_token_major(x2_ref, 0, TM_COMB)
        o_ref[...] = _layer_norm(DEEPNORM_ALPHA * x2 + ff, lng_ref[...], lnb_ref[...])

    for k in range(COMB_SLOTS):
        nxt = (k + ahead) % COMB_SLOTS

        @pl.when((i % COMB_SLOTS == k) & (i + ahead < nt))
        def _():
            _wait_rows(rows, y_hbm, ybufs[k], sem.at[k])
            _gather_rows(idx_ref, (i + ahead) * rows, rows, y_hbm, ybufs[nxt], sem.at[nxt])
            compute(ybufs[k])

        @pl.when((i % COMB_SLOTS == k) & (i + ahead >= nt))
        def _():
            _wait_rows(rows, y_hbm, ybufs[k], sem.at[k])
            compute(ybufs[k])


def _combine(dest, y_rows, x2, gates, ln_g, ln_b):
    n = x2.shape[0] // LANE_BLOCKS
    nt = n // TM_COMB
    rows = TOP_K * TM_COMB
    idx = (dest * LANE_BLOCKS).reshape(nt, TM_COMB, TOP_K).transpose(0, 2, 1).reshape(nt * rows)
    row = lambda width: pl.BlockSpec((TM_COMB, width), lambda i, idx: (i, 0))
    full = lambda r, c: pl.BlockSpec((r, c), lambda i, idx: (0, 0))
    grid_spec = pltpu.PrefetchScalarGridSpec(
        num_scalar_prefetch=1,
        grid=(nt,),
        in_specs=[pl.BlockSpec(memory_space=pl.ANY),
                  pl.BlockSpec((TM_COMB * LANE_BLOCKS, LANES), lambda i, idx: (i, 0)), row(LANES),
                  full(1, D_MODEL), full(1, D_MODEL)],
        out_specs=row(D_MODEL),
        scratch_shapes=[pltpu.VMEM((rows * LANE_BLOCKS, LANES), F32)] * COMB_SLOTS
        + [pltpu.SemaphoreType.DMA((COMB_SLOTS,))],
    )
    return pl.pallas_call(
        _combine_kernel,
        grid_spec=grid_spec,
        out_shape=jax.ShapeDtypeStruct((n, D_MODEL), F32),
        compiler_params=_params(1),
        name="combine",
    )(idx, y_rows, x2, gates, ln_g, ln_b)


def _rank_kernel(idx_ref, counts_ref, dest_ref, running):
    j = pl.program_id(0)
    t = TM_RANK
    idx = idx_ref[...]
    e_iota = lax.broadcasted_iota(jnp.int32, (t, LANES), 1)
    hot = [idx[:, k:k + 1] == e_iota for k in range(TOP_K)]
    chosen = sum(jnp.where(h, 1.0, 0.0) for h in hot)

    @pl.when(j == 0)
    def _():
        counts = counts_ref[...]
        padded = jnp.ceil(counts * (1.0 / TM_MOE)) * TM_MOE
        lane = lax.broadcasted_iota(jnp.int32, (1, LANES), 1)
        scan = padded
        shift = 1
        while shift < N_EXPERTS:
            scan = scan + jnp.where(lane >= shift, pltpu.roll(scan, shift, 1), 0.0)
            shift *= 2
        running[...] = scan - padded

    r_iota = lax.broadcasted_iota(jnp.int32, (t, t), 0)
    c_iota = lax.broadcasted_iota(jnp.int32, (t, t), 1)
    earlier = jnp.where(c_iota < r_iota, 1.0, 0.0).astype(BF16)
    base = jnp.dot(earlier, chosen.astype(BF16), preferred_element_type=F32) + running[...]
    out = jnp.zeros((t, LANES), jnp.int32)
    for k in range(TOP_K):
        row = jnp.sum(jnp.where(hot[k], base, 0.0), axis=-1, keepdims=True).astype(jnp.int32)
        out = jnp.where(e_iota == k, row, out)
    dest_ref[...] = out
    running[...] += jnp.sum(chosen, axis=0, keepdims=True)


def _dispatch_plan(top_idx, counts, n):
    n_assign = n * TOP_K
    nb = n_assign // TM_MOE + N_EXPERTS + MOE_SLOTS - 2
    dest = pl.pallas_call(
        _rank_kernel,
        grid=(n // TM_RANK,),
        in_specs=[pl.BlockSpec((TM_RANK, LANES), lambda j: (j, 0)), pl.BlockSpec((1, LANES), lambda j: (0, 0))],
        out_specs=pl.BlockSpec((TM_RANK, LANES), lambda j: (j, 0)),
        out_shape=jax.ShapeDtypeStruct((n, LANES), jnp.int32),
        scratch_shapes=[pltpu.VMEM((1, LANES), F32)],
        compiler_params=_params(1),
        name="expert_rank",
    )(top_idx, counts)
    dest = dest[:, :TOP_K].reshape(-1)
    counts = counts[0, :N_EXPERTS].astype(jnp.int32)
    padded = (counts + TM_MOE - 1) // TM_MOE * TM_MOE
    pad_end = jnp.cumsum(padded)
    tail = jnp.stack([pad_end[-1], jnp.int32(nb * TM_MOE)])[:, None]
    gaps = jnp.concatenate([jnp.stack([pad_end - padded + counts, pad_end]), tail], axis=1)
    block_first_row = jnp.arange(nb, dtype=jnp.int32) * TM_MOE
    block_expert = jnp.minimum(jnp.sum(pad_end[None, :] <= block_first_row[:, None], axis=1),
                               N_EXPERTS - 1).astype(jnp.int32)
    n_used = (pad_end[-1:] // TM_MOE).astype(jnp.int32)
    block = jnp.arange(nb, dtype=jnp.int32)
    starts = (block < n_used[0]) & (block > 0) & (block_expert != jnp.roll(block_expert, 1))
    later_start = jnp.min(jnp.where(starts[None, :] & (block[None, :] > block[:, None]), block[None, :], nb), axis=1)
    next_expert = jnp.where(later_start < nb, block_expert[jnp.minimum(later_start, nb - 1)], -1).astype(jnp.int32)
    return dest, gaps, block_expert, n_used, next_expert


def kernel(x, mem, w_in, rel_bias, g_group_a, g_group_b, w_out, w_q_mem, w_kv_mem, w_o_mem, w_router, b_router, w_gate_up, b_gate_up, w_down, b_down, ln_g, ln_b):
    b, s, d = x.shape
    n = b * s
    depth = w_in.shape[0]
    xf = x.reshape(n, d)
    for l in range(depth):
        q_scale = jnp.ones((3 * MIX_WIDTH,), F32).at[:WIDTH_A].set(HEAD_DIM ** -0.5)
        q_scale = q_scale.at[3 * WIDTH_A:3 * WIDTH_A + WIDTH_B].set(HEAD_DIM ** -0.5)
        proj = _matmul(xf, (w_in[l] * q_scale).astype(BF16), BF16, TM_TOK, 3 * MIX_WIDTH).reshape(b, s, 3 * MIX_WIDTH)
        out_a = _chunk_attention(proj, _chunk_bias(rel_bias[l]), b, s).reshape(n, WIDTH_A)
        out_b = _sb_attention(proj, b, s).reshape(n, WIDTH_B)
        mem_len = mem.shape[1]
        kv = _matmul(mem.reshape(b * mem_len, d), w_kv_mem[l].astype(BF16), BF16, b * mem_len, 1024)
        x2, top_idx, gates, counts = _token_block(
            out_a, out_b, xf, g_group_a[l][None], g_group_b[l][None], w_out[l].astype(BF16),
            ln_g[l, 0][None], ln_b[l, 0][None], w_q_mem[l].astype(BF16), kv.reshape(b, mem_len, 2 * d),
            w_o_mem[l].astype(BF16), ln_g[l, 1][None], ln_b[l, 1][None],
            w_router[l].astype(BF16), b_router[l][None], b, s)
        dest, gaps, block_expert, n_used, next_expert = _dispatch_plan(top_idx, counts, n)
        y_rows = _moe_experts(block_expert, n_used, next_expert, dest, gaps, x2,
                              w_gate_up[l], b_gate_up[l][:, None, :], w_down[l], b_down[l][:, None, :])
        xf = _combine(dest, y_rows, x2, gates, ln_g[l, 2][None], ln_b[l, 2][None])
    return xf.reshape(b, s, d)
```

```python
import functools

import jax
import jax.numpy as jnp
from jax import lax
from jax.experimental import pallas as pl
from jax.experimental.pallas import tpu as pltpu

D_MODEL = 1024
CHUNK = 64
LEFT_CHUNKS = 8
LEFT = LEFT_CHUNKS * CHUNK
HEAD_DIM = 64
N_HEADS_A = 8
N_HEADS_B = 8
WIDTH_A = N_HEADS_A * HEAD_DIM
WIDTH_B = N_HEADS_B * HEAD_DIM
MIX_WIDTH = WIDTH_A + WIDTH_B
REL_CLIP = 128
N_HEADS_MEM = 4
HEAD_DIM_MEM = D_MODEL // N_HEADS_MEM
N_EXPERTS = 32
TOP_K = 4
D_FF = D_MODEL
SWIGLU_LIMIT = 7.0
SWIGLU_ALPHA = 1.702
LN_EPS = 1e-5
RMS_EPS = 1e-6
DEEPNORM_ALPHA = 2.0 ** 0.25
NEG_INF = -1e30
LOG2E = 1.4426950408889634

LANES = 128
LANE_BLOCKS = D_MODEL // LANES
VMEM_LIMIT = 48 * 1024 * 1024
VMEM_LIMIT_MOE = 56 * 1024 * 1024

TQ_A = 512
SUB_A = 256
HEADS_A_STEP = 4
TB_SB = 256
SB_TILES = 4
SB_UNDERFLOW = 110.0
TM_TOK = 512
TM_MOE = 256
MOE_SLOTS = 3
TM_COMB = 512
COMB_SLOTS = 3
TM_RANK = 512

F32 = jnp.float32
BF16 = jnp.bfloat16
_NT = (((1,), (1,)), ((), ()))


def _params(n_axes):
    return pltpu.CompilerParams(dimension_semantics=("arbitrary",) * n_axes,
                                vmem_limit_bytes=VMEM_LIMIT)


def _store_token_major(ref, value):
    rows = value.shape[0]
    for c in range(LANE_BLOCKS):
        ref[pl.ds(c, rows, stride=LANE_BLOCKS), :] = value[:, c * LANES:(c + 1) * LANES]


def _load_token_major(ref, first_row, rows):
    return jnp.concatenate(
        [ref[pl.ds(first_row * LANE_BLOCKS + c, rows, stride=LANE_BLOCKS), :] for c in range(LANE_BLOCKS)],
        axis=1)


def _layer_norm(r, g, b):
    mu = jnp.mean(r, axis=-1, keepdims=True)
    d = r - mu
    var = jnp.mean(d * d, axis=-1, keepdims=True)
    return d * lax.rsqrt(var + LN_EPS) * g + b


def _matmul_kernel(x_ref, w_ref, o_ref):
    o_ref[...] = jnp.dot(x_ref[...].astype(BF16), w_ref[...],
                         preferred_element_type=F32).astype(o_ref.dtype)


def _matmul(x, w, out_dtype, tm, tn):
    m, k = x.shape
    n = w.shape[1]
    return pl.pallas_call(
        _matmul_kernel,
        grid=(m // tm, n // tn),
        in_specs=[pl.BlockSpec((tm, k), lambda i, j: (i, 0)),
                  pl.BlockSpec((k, tn), lambda i, j: (0, j))],
        out_specs=pl.BlockSpec((tm, tn), lambda i, j: (i, j)),
        out_shape=jax.ShapeDtypeStruct((m, n), out_dtype),
        compiler_params=_params(2),
        name="matmul",
    )(x, w)


def _chunk_attn_kernel(q_ref, kp_ref, kc_ref, vp_ref, vc_ref, bias0_ref, bias1_ref, o_ref):
    w = SUB_A + LEFT
    for h in range(HEADS_A_STEP):
        sl = slice(h * HEAD_DIM, (h + 1) * HEAD_DIM)
        k = jnp.concatenate([kp_ref[:, sl], kc_ref[:, sl]], axis=0)
        v = jnp.concatenate([vp_ref[:, sl], vc_ref[:, sl]], axis=0)
        for sub, bias_ref in enumerate((bias0_ref, bias1_ref)):
            rows = slice(sub * SUB_A, (sub + 1) * SUB_A)
            keys = slice(sub * SUB_A, sub * SUB_A + w)
            s = lax.dot_general(q_ref[rows, sl], k[keys], _NT, preferred_element_type=F32) + bias_ref[h]
            m = jnp.max(s, axis=-1, keepdims=True)
            p = jnp.exp(s - m)
            l = jnp.sum(p, axis=-1, keepdims=True)
            o = jnp.dot(p.astype(BF16), v[keys], preferred_element_type=F32) / l
            o_ref[rows, sl] = o.astype(o_ref.dtype)


def _chunk_bias(rel_bias):
    w = SUB_A + LEFT
    heads = rel_bias.shape[0]
    m = jnp.arange(2 * w)
    rel = jnp.where(m < w, LEFT - m, LEFT + 2 * w - m)
    diag = rel_bias[:, jnp.clip(rel, -REL_CLIP, REL_CLIP) + REL_CLIP].astype(F32)[:, None, :]
    n_tables = 1 + TQ_A // SUB_A
    return pl.pallas_call(
        _chunk_bias_kernel,
        grid=(heads,),
        in_specs=[pl.BlockSpec((1, 1, 2 * w), lambda h: (h, 0, 0))],
        out_specs=pl.BlockSpec((n_tables, 1, SUB_A, w), lambda h: (0, h, 0, 0)),
        out_shape=jax.ShapeDtypeStruct((n_tables, heads, SUB_A, w), F32),
        compiler_params=_params(1),
        name="chunk_bias",
    )(diag)


def _chunk_bias_kernel(diag_ref, o_ref):
    w = SUB_A + LEFT
    rolled = pltpu.roll(jnp.broadcast_to(diag_ref[0], (SUB_A, 2 * w)), 0, 1, stride=1, stride_axis=0)
    col = lax.broadcasted_iota(jnp.int32, (SUB_A, w), 1)
    qc = lax.broadcasted_iota(jnp.int32, (SUB_A, w), 0) // CHUNK
    kc = col // CHUNK
    band = (kc >= qc) & (kc <= qc + LEFT_CHUNKS)
    table = jnp.where(band, rolled[:, :w], NEG_INF)
    o_ref[0, 0] = table
    for sub in range(TQ_A // SUB_A):
        o_ref[1 + sub, 0] = jnp.where(col >= LEFT - sub * SUB_A, table, NEG_INF)


def _chunk_attention(proj, bias, b, s):
    nq = s // TQ_A
    width = HEADS_A_STEP * HEAD_DIM
    groups = WIDTH_A // width
    blk = lambda off, prev: pl.BlockSpec(
        (None, TQ_A, width),
        (lambda bi, p, i: (bi, jnp.maximum(i - 1, 0), off + p)) if prev
        else (lambda bi, p, i: (bi, i, off + p)))
    table = lambda sub: pl.BlockSpec((None, HEADS_A_STEP, SUB_A, SUB_A + LEFT),
                                     lambda bi, p, i: (jnp.where(i == 0, 1 + sub, 0), p, 0, 0))
    return pl.pallas_call(
        _chunk_attn_kernel,
        grid=(b, groups, nq),
        in_specs=[blk(0, False),
                  blk(groups, True), blk(groups, False),
                  blk(2 * groups, True), blk(2 * groups, False),
                  table(0), table(1)],
        out_specs=pl.BlockSpec((None, TQ_A, width), lambda bi, p, i: (bi, i, p)),
        out_shape=jax.ShapeDtypeStruct((b, s, WIDTH_A), BF16),
        compiler_params=_params(3),
        name="chunk_attn",
    )(proj, proj, proj, proj, proj, bias, bias)


def _sb_block(q, k_ref, v_ref, sl, j, later, tri, causal):
    t = TB_SB
    start = pl.multiple_of(j * t, t)
    k = k_ref[pl.ds(start, t), sl]
    v = v_ref[pl.ds(start, t), sl]
    z = lax.dot_general(q, k, _NT, preferred_element_type=F32)
    sp = jnp.maximum(z, 0.0) + jnp.log(1.0 + jnp.exp2(jnp.abs(z) * (-LOG2E)))
    if causal is not None:
        sp = jnp.where(causal, sp, 0.0)
    csum = jnp.dot(sp.astype(BF16), tri, preferred_element_type=F32) + later
    wgt = jnp.exp(z - csum)
    if causal is not None:
        wgt = jnp.where(causal, wgt, 0.0)
    pv = jnp.dot(wgt.astype(BF16), v, preferred_element_type=F32)
    return pv, csum[:, 0:1]


def _sb_attn_kernel(q_ref, k_ref, v_ref, o_ref):
    first = pl.program_id(2) * SB_TILES
    t = TB_SB
    row = lax.broadcasted_iota(jnp.int32, (t, t), 0)
    col = lax.broadcasted_iota(jnp.int32, (t, t), 1)
    tri = jnp.where(row >= col, 1.0, 0.0).astype(BF16)
    causal = col < row
    chains = [(u, h) for u in range(SB_TILES) for h in range(2)]
    rows = lambda u: slice(u * t, (u + 1) * t)
    lanes = lambda h: slice(h * HEAD_DIM, (h + 1) * HEAD_DIM)
    qs = [q_ref[rows(u), lanes(h)] for u, h in chains]

    def block(c, j, later, masked):
        return _sb_block(qs[c], k_ref, v_ref, lanes(chains[c][1]), j, later, tri, causal if masked else None)

    carry = []
    for c, (u, h) in enumerate(chains):
        tile = first + u
        pv, later = block(c, tile, jnp.zeros((t, 1), F32), True)
        pv_prev, later_prev = block(c, jnp.maximum(tile - 1, 0), later, False)
        has_prev = tile >= 1
        carry += [jnp.where(has_prev, later_prev, later), pv + jnp.where(has_prev, pv_prev, 0.0)]

    def pending(c):
        todo = False
        for ci, (u, h) in enumerate(chains):
            todo = todo | ((first + u - 2 - c[0] >= 0) & (jnp.min(c[1 + 2 * ci]) < SB_UNDERFLOW))
        return todo

    def step(c):
        out = [c[0] + 1]
        for ci, (u, h) in enumerate(chains):
            j = first + u - 2 - c[0]
            pv, later = block(ci, jnp.maximum(j, 0), c[1 + 2 * ci], False)
            out += [jnp.where(j >= 0, later, c[1 + 2 * ci]), c[2 + 2 * ci] + jnp.where(j >= 0, pv, 0.0)]
        return tuple(out)

    carry = lax.while_loop(pending, step, (jnp.int32(0), *carry))
    for ci, (u, h) in enumerate(chains):
        o_ref[rows(u), lanes(h)] = carry[2 + 2 * ci].astype(o_ref.dtype)


def _sb_attention(proj, b, s):
    tq = SB_TILES * TB_SB
    nq = s // tq
    pairs = WIDTH_B // LANES
    base = 3 * WIDTH_A // LANES
    return pl.pallas_call(
        _sb_attn_kernel,
        grid=(b, pairs, nq),
        in_specs=[pl.BlockSpec((None, tq, LANES), lambda bi, p, i: (bi, i, base + p)),
                  pl.BlockSpec((None, s, LANES), lambda bi, p, i: (bi, 0, base + pairs + p)),
                  pl.BlockSpec((None, s, LANES), lambda bi, p, i: (bi, 0, base + 2 * pairs + p))],
        out_specs=pl.BlockSpec((None, tq, LANES), lambda bi, p, i: (bi, i, p)),
        out_shape=jax.ShapeDtypeStruct((b, s, WIDTH_B), BF16),
        compiler_params=_params(3),
        name="sb_attn",
    )(proj, proj, proj)


def _token_block_kernel(oa_ref, ob_ref, x_ref, ga_ref, gb_ref, wout_ref, ln1g_ref, ln1b_ref, wq_ref,
                        kv_ref, wo_ref, ln2g_ref, ln2b_ref, wr_ref, br_ref,
                        x2_ref, idx_ref, gate_ref, counts_ref):
    def rms(ref, g_ref):
        a = ref[...].astype(F32)
        return (a * lax.rsqrt(jnp.mean(a * a, axis=-1, keepdims=True) + RMS_EPS) * g_ref[...]).astype(BF16)

    y = jnp.dot(rms(oa_ref, ga_ref), wout_ref[:WIDTH_A, :], preferred_element_type=F32)
    y = y + jnp.dot(rms(ob_ref, gb_ref), wout_ref[WIDTH_A:, :], preferred_element_type=F32)
    x1 = _layer_norm(DEEPNORM_ALPHA * x_ref[...] + y, ln1g_ref[...], ln1b_ref[...])
    qm = jnp.dot(x1.astype(BF16), wq_ref[...], preferred_element_type=F32).astype(BF16)

    heads = []
    for h in range(N_HEADS_MEM):
        sl = slice(h * HEAD_DIM_MEM, (h + 1) * HEAD_DIM_MEM)
        q = qm[:, sl] * (HEAD_DIM_MEM ** -0.5)
        k = kv_ref[:, sl]
        v = kv_ref[:, D_MODEL + h * HEAD_DIM_MEM:D_MODEL + (h + 1) * HEAD_DIM_MEM]
        s = lax.dot_general(q, k, _NT, preferred_element_type=F32)
        m = jnp.max(s, axis=-1, keepdims=True)
        p = jnp.exp(s - m)
        l = jnp.sum(p, axis=-1, keepdims=True)
        heads.append((jnp.dot(p.astype(BF16), v, preferred_element_type=F32) / l).astype(BF16))
    o = jnp.concatenate(heads, axis=-1)
    y = jnp.dot(o, wo_ref[...], preferred_element_type=F32)
    x2 = _layer_norm(DEEPNORM_ALPHA * x1 + y, ln2g_ref[...], ln2b_ref[...])
    _store_token_major(x2_ref, x2)

    logits = jnp.dot(x2.astype(BF16), wr_ref[...], preferred_element_type=F32) + br_ref[...]
    tm = logits.shape[0]
    e_iota = lax.broadcasted_iota(jnp.int32, (tm, N_EXPERTS), 1)
    lane = lax.broadcasted_iota(jnp.int32, (tm, LANES), 1)
    idx_out = jnp.zeros((tm, LANES), jnp.int32)
    val_out = jnp.zeros((tm, LANES), F32)
    top = None
    denom = jnp.zeros((tm, 1), F32)
    chosen = jnp.zeros((1, LANES), F32)
    for kk in range(TOP_K):
        m = jnp.max(logits, axis=-1, keepdims=True)
        sel = jnp.min(jnp.where(logits == m, e_iota, N_EXPERTS), axis=-1, keepdims=True)
        if top is None:
            top = m
        e = jnp.exp(m - top)
        denom = denom + e
        idx_out = jnp.where(lane == kk, sel, idx_out)
        val_out = jnp.where(lane == kk, e, val_out)
        chosen = chosen + jnp.sum(jnp.where(lane == sel, 1.0, 0.0), axis=0, keepdims=True)
        logits = jnp.where(e_iota == sel, -jnp.inf, logits)
    idx_ref[...] = idx_out
    gate_ref[...] = val_out / denom

    @pl.when((pl.program_id(0) == 0) & (pl.program_id(1) == 0))
    def _():
        counts_ref[...] = jnp.zeros_like(counts_ref)

    counts_ref[...] += chosen


def _token_block(out_a, out_b, x, g_a, g_b, w_out, ln1_g, ln1_b, w_q, kv, w_o, ln2_g, ln2_b, w_r, b_r, b, s):
    tm = TM_TOK
    nt = s // tm
    mem_len = kv.shape[1]
    n = b * s
    row = lambda width: pl.BlockSpec((tm, width), lambda bi, i: (bi * nt + i, 0))
    full = lambda r, c: pl.BlockSpec((r, c), lambda bi, i: (0, 0))
    return pl.pallas_call(
        _token_block_kernel,
        grid=(b, nt),
        in_specs=[row(WIDTH_A), row(WIDTH_B), row(D_MODEL), full(1, WIDTH_A), full(1, WIDTH_B),
                  full(MIX_WIDTH, D_MODEL), full(1, D_MODEL), full(1, D_MODEL), full(D_MODEL, D_MODEL),
                  pl.BlockSpec((None, mem_len, 2 * D_MODEL), lambda bi, i: (bi, 0, 0)),
                  full(D_MODEL, D_MODEL), full(1, D_MODEL), full(1, D_MODEL),
                  full(D_MODEL, N_EXPERTS), full(1, N_EXPERTS)],
        out_specs=[pl.BlockSpec((tm * LANE_BLOCKS, LANES), lambda bi, i: (bi * nt + i, 0)), row(LANES), row(LANES),
                   full(1, LANES)],
        out_shape=[jax.ShapeDtypeStruct((n * LANE_BLOCKS, LANES), F32),
                   jax.ShapeDtypeStruct((n, LANES), jnp.int32),
                   jax.ShapeDtypeStruct((n, LANES), F32),
                   jax.ShapeDtypeStruct((1, LANES), F32)],
        compiler_params=_params(2),
        name="token_block",
    )(out_a, out_b, x, g_a, g_b, w_out, ln1_g, ln1_b, w_q, kv, w_o, ln2_g, ln2_b, w_r, b_r)


def _gather_rows(idx_ref, first, count, src_hbm, dst_ref, sem):
    for r in range(count):
        src = pl.ds(pl.multiple_of(idx_ref[first + r], LANE_BLOCKS), LANE_BLOCKS)
        pltpu.make_async_copy(src_hbm.at[src], dst_ref.at[pl.ds(r * LANE_BLOCKS, LANE_BLOCKS)], sem).start(
            priority=r % 2)


def _wait_rows(count, src_hbm, dst_ref, sem):
    pltpu.make_async_copy(src_hbm.at[pl.ds(0, count * LANE_BLOCKS)], dst_ref, sem).wait()


def _fill_row_table(dest_ref, gap_ref, tok_ref):
    unroll = 16

    def clear(c, carry):
        for u in range(unroll):
            tok_ref[c * unroll + u] = 0
        return carry

    def place(c, carry):
        rows = [dest_ref[c * unroll + u] for u in range(unroll)]
        for u in range(unroll):
            tok_ref[rows[u]] = (c * (unroll // TOP_K) + u // TOP_K) * LANE_BLOCKS
        return carry

    for g in range(gap_ref.shape[1]):
        lax.fori_loop(gap_ref[0, g] // unroll, gap_ref[1, g] // unroll, clear, 0)
    lax.fori_loop(0, dest_ref.shape[0] // unroll, place, 0)


def _moe_kernel(bexp_ref, nused_ref, next_ref, dest_ref, gap_ref, x_hbm, wgu_hbm, bgu_ref, wd_hbm, bd_ref,
                y_ref, xbuf, wgu_f32, wd_f32, wgu_bf, wd_bf, tok_ref, sem, wsem):
    i = pl.program_id(0)
    slot = i % MOE_SLOTS
    ahead = MOE_SLOTS - 1
    n_used = nused_ref[0]

    def weight_copies(expert):
        return (pltpu.make_async_copy(wgu_hbm.at[expert], wgu_f32, wsem.at[0]),
                pltpu.make_async_copy(wd_hbm.at[expert], wd_f32, wsem.at[1]))

    @pl.when(i == 0)
    def _():
        for copy in weight_copies(bexp_ref[0]):
            copy.start()
        _fill_row_table(dest_ref, gap_ref, tok_ref)
        for b in range(ahead):
            _gather_rows(tok_ref, b * TM_MOE, TM_MOE, x_hbm, xbuf.at[b], sem.at[b])

    @pl.when(i < n_used + ahead)
    def _():
        _wait_rows(TM_MOE, x_hbm, xbuf.at[slot], sem.at[slot])

    @pl.when((i < n_used) & ((i == 0) | (bexp_ref[i] != bexp_ref[jnp.maximum(i - 1, 0)])))
    def _():
        for copy in weight_copies(bexp_ref[i]):
            copy.wait()
        wgu_bf[...] = wgu_f32[...].astype(BF16)
        wd_bf[...] = wd_f32[...].astype(BF16)

        @pl.when(next_ref[i] >= 0)
        def _():
            for copy in weight_copies(next_ref[i]):
                copy.start()

    @pl.when(i < n_used)
    def _():
        x = _load_token_major(xbuf.at[slot], 0, TM_MOE).astype(BF16)
        nxt = (i + ahead) % MOE_SLOTS
        _gather_rows(tok_ref, (i + ahead) * TM_MOE, TM_MOE, x_hbm, xbuf.at[nxt], sem.at[nxt])
        gu = jnp.dot(x, wgu_bf[...], preferred_element_type=F32) + bgu_ref[0]
        gate = jnp.minimum(gu[:, :D_FF], SWIGLU_LIMIT)
        up = jnp.clip(gu[:, D_FF:], -SWIGLU_LIMIT, SWIGLU_LIMIT)
        glu = gate * jax.nn.sigmoid(gate * SWIGLU_ALPHA)
        hid = ((up + 1.0) * glu).astype(BF16)
        _store_token_major(y_ref, jnp.dot(hid, wd_bf[...], preferred_element_type=F32) + bd_ref[0])

    @pl.when(i >= n_used)
    def _():
        y_ref[...] = jnp.zeros_like(y_ref)


def _moe_experts(block_expert, n_used, next_expert, dest, gaps, x2, w_gu, b_gu, w_d, b_d):
    nb = block_expert.shape[0]
    grid_spec = pltpu.PrefetchScalarGridSpec(
        num_scalar_prefetch=5,
        grid=(nb,),
        in_specs=[pl.BlockSpec(memory_space=pl.ANY),
                  pl.BlockSpec(memory_space=pl.ANY),
                  pl.BlockSpec((1, 1, 2 * D_FF), lambda i, be, nu, ne, de, ga: (be[i], 0, 0)),
                  pl.BlockSpec(memory_space=pl.ANY),
                  pl.BlockSpec((1, 1, D_MODEL), lambda i, be, nu, ne, de, ga: (be[i], 0, 0))],
        out_specs=pl.BlockSpec((TM_MOE * LANE_BLOCKS, LANES), lambda i, be, nu, ne, de, ga: (i, 0)),
        scratch_shapes=[pltpu.VMEM((MOE_SLOTS, TM_MOE * LANE_BLOCKS, LANES), F32),
                        pltpu.VMEM((D_MODEL, 2 * D_FF), F32), pltpu.VMEM((D_FF, D_MODEL), F32),
                        pltpu.VMEM((D_MODEL, 2 * D_FF), BF16), pltpu.VMEM((D_FF, D_MODEL), BF16),
                        pltpu.SMEM((nb * TM_MOE,), jnp.int32),
                        pltpu.SemaphoreType.DMA((MOE_SLOTS,)), pltpu.SemaphoreType.DMA((2,))],
    )
    return pl.pallas_call(
        _moe_kernel,
        grid_spec=grid_spec,
        out_shape=jax.ShapeDtypeStruct((nb * TM_MOE * LANE_BLOCKS, LANES), F32),
        compiler_params=pltpu.CompilerParams(dimension_semantics=("arbitrary",),
                                             vmem_limit_bytes=VMEM_LIMIT_MOE),
        name="moe_experts",
    )(block_expert, n_used, next_expert, dest, gaps, x2, w_gu, b_gu, w_d, b_d)


def _combine_kernel(idx_ref, y_hbm, x2_ref, gate_ref, lng_ref, lnb_ref, o_ref, *scratch):
    ybufs, sem = scratch[:COMB_SLOTS], scratch[COMB_SLOTS]
    i = pl.program_id(0)
    nt = pl.num_programs(0)
    ahead = COMB_SLOTS - 1
    rows = TOP_K * TM_COMB

    @pl.when(i == 0)
    def _():
        for t in range(ahead):
            _gather_rows(idx_ref, t * rows, rows, y_hbm, ybufs[t], sem.at[t])

    def compute(ybuf):
        g = gate_ref[...]
        ff = jnp.zeros((TM_COMB, D_MODEL), F32)
        for kk in range(TOP_K):
            ff = ff + g[:, kk:kk + 1] * _load_token_major(ybuf, kk * TM_COMB, TM_COMB)
        x2 = _load_token_major(x2_ref, 0, TM_COMB)
        o_ref[...] = _layer_norm(DEEPNORM_ALPHA * x2 + ff, lng_ref[...], lnb_ref[...])

    for k in range(COMB_SLOTS):
        nxt = (k + ahead) % COMB_SLOTS

        @pl.when((i % COMB_SLOTS == k) & (i + ahead < nt))
        def _():
            _wait_rows(rows, y_hbm, ybufs[k], sem.at[k])
            _gather_rows(idx_ref, (i + ahead) * rows, rows, y_hbm, ybufs[nxt], sem.at[nxt])
            compute(ybufs[k])

        @pl.when((i % COMB_SLOTS == k) & (i + ahead >= nt))
        def _():
            _wait_rows(rows, y_hbm, ybufs[k], sem.at[k])
            compute(ybufs[k])


def _combine(dest, y_rows, x2, gates, ln_g, ln_b):
    n = x2.shape[0] // LANE_BLOCKS
    nt = n // TM_COMB
    rows = TOP_K * TM_COMB
    idx = (dest * LANE_BLOCKS).reshape(nt, TM_COMB, TOP_K).transpose(0, 2, 1).reshape(nt * rows)
    row = lambda width: pl.BlockSpec((TM_COMB, width), lambda i, idx: (i, 0))
    full = lambda r, c: pl.BlockSpec((r, c), lambda i, idx: (0, 0))
    grid_spec = pltpu.PrefetchScalarGridSpec(
        num_scalar_prefetch=1,
        grid=(nt,),
        in_specs=[pl.BlockSpec(memory_space=pl.ANY),
                  pl.BlockSpec((TM_COMB * LANE_BLOCKS, LANES), lambda i, idx: (i, 0)), row(LANES),
                  full(1, D_MODEL), full(1, D_MODEL)],
        out_specs=row(D_MODEL),
        scratch_shapes=[pltpu.VMEM((rows * LANE_BLOCKS, LANES), F32)] * COMB_SLOTS
        + [pltpu.SemaphoreType.DMA((COMB_SLOTS,))],
    )
    return pl.pallas_call(
        _combine_kernel,
        grid_spec=grid_spec,
        out_shape=jax.ShapeDtypeStruct((n, D_MODEL), F32),
        compiler_params=_params(1),
        name="combine",
    )(idx, y_rows, x2, gates, ln_g, ln_b)


def _rank_kernel(idx_ref, counts_ref, dest_ref, running):
    j = pl.program_id(0)
    t = TM_RANK
    idx = idx_ref[...]
    e_iota = lax.broadcasted_iota(jnp.int32, (t, LANES), 1)
    hot = [idx[:, k:k + 1] == e_iota for k in range(TOP_K)]
    chosen = sum(jnp.where(h, 1.0, 0.0) for h in hot)

    @pl.when(j == 0)
    def _():
        counts = counts_ref[...]
        padded = jnp.ceil(counts * (1.0 / TM_MOE)) * TM_MOE
        lane = lax.broadcasted_iota(jnp.int32, (1, LANES), 1)
        scan = padded
        shift = 1
        while shift < N_EXPERTS:
            scan = scan + jnp.where(lane >= shift, pltpu.roll(scan, shift, 1), 0.0)
            shift *= 2
        running[...] = scan - padded

    r_iota = lax.broadcasted_iota(jnp.int32, (t, t), 0)
    c_iota = lax.broadcasted_iota(jnp.int32, (t, t), 1)
    earlier = jnp.where(c_iota < r_iota, 1.0, 0.0).astype(BF16)
    base = jnp.dot(earlier, chosen.astype(BF16), preferred_element_type=F32) + running[...]
    out = jnp.zeros((t, LANES), jnp.int32)
    for k in range(TOP_K):
        row = jnp.sum(jnp.where(hot[k], base, 0.0), axis=-1, keepdims=True).astype(jnp.int32)
        out = jnp.where(e_iota == k, row, out)
    dest_ref[...] = out
    running[...] += jnp.sum(chosen, axis=0, keepdims=True)


def _dispatch_plan(top_idx, counts, n):
    n_assign = n * TOP_K
    nb = n_assign // TM_MOE + N_EXPERTS + MOE_SLOTS - 2
    dest = pl.pallas_call(
        _rank_kernel,
        grid=(n // TM_RANK,),
        in_specs=[pl.BlockSpec((TM_RANK, LANES), lambda j: (j, 0)), pl.BlockSpec((1, LANES), lambda j: (0, 0))],
        out_specs=pl.BlockSpec((TM_RANK, LANES), lambda j: (j, 0)),
        out_shape=jax.ShapeDtypeStruct((n, LANES), jnp.int32),
        scratch_shapes=[pltpu.VMEM((1, LANES), F32)],
        compiler_params=_params(1),
        name="expert_rank",
    )(top_idx, counts)
    dest = dest[:, :TOP_K].reshape(-1)
    counts = counts[0, :N_EXPERTS].astype(jnp.int32)
    padded = (counts + TM_MOE - 1) // TM_MOE * TM_MOE
    pad_end = jnp.cumsum(padded)
    tail = jnp.stack([pad_end[-1], jnp.int32(nb * TM_MOE)])[:, None]
    gaps = jnp.concatenate([jnp.stack([pad_end - padded + counts, pad_end]), tail], axis=1)
    block_first_row = jnp.arange(nb, dtype=jnp.int32) * TM_MOE
    block_expert = jnp.minimum(jnp.sum(pad_end[None, :] <= block_first_row[:, None], axis=1),
                               N_EXPERTS - 1).astype(jnp.int32)
    n_used = (pad_end[-1:] // TM_MOE).astype(jnp.int32)
    block = jnp.arange(nb, dtype=jnp.int32)
    starts = (block < n_used[0]) & (block > 0) & (block_expert != jnp.roll(block_expert, 1))
    later_start = jnp.min(jnp.where(starts[None, :] & (block[None, :] > block[:, None]), block[None, :], nb), axis=1)
    next_expert = jnp.where(later_start < nb, block_expert[jnp.minimum(later_start, nb - 1)], -1).astype(jnp.int32)
    return dest, gaps, block_expert, n_used, next_expert


def kernel(x, mem, w_in, rel_bias, g_group_a, g_group_b, w_out, w_q_mem, w_kv_mem, w_o_mem, w_router, b_router, w_gate_up, b_gate_up, w_down, b_down, ln_g, ln_b):
    b, s, d = x.shape
    n = b * s
    depth = w_in.shape[0]
    xf = x.reshape(n, d)
    for l in range(depth):
        q_scale = jnp.ones((3 * MIX_WIDTH,), F32).at[:WIDTH_A].set(HEAD_DIM ** -0.5)
        q_scale = q_scale.at[3 * WIDTH_A:3 * WIDTH_A + WIDTH_B].set(HEAD_DIM ** -0.5)
        proj = _matmul(xf, (w_in[l] * q_scale).astype(BF16), BF16, TM_TOK, 3 * MIX_WIDTH).reshape(b, s, 3 * MIX_WIDTH)
        out_a = _chunk_attention(proj, _chunk_bias(rel_bias[l]), b, s).reshape(n, WIDTH_A)
        out_b = _sb_attention(proj, b, s).reshape(n, WIDTH_B)
        mem_len = mem.shape[1]
        kv = _matmul(mem.reshape(b * mem_len, d), w_kv_mem[l].astype(BF16), BF16, b * mem_len, 1024)
        x2, top_idx, gates, counts = _token_block(
            out_a, out_b, xf, g_group_a[l][None], g_group_b[l][None], w_out[l].astype(BF16),
            ln_g[l, 0][None], ln_b[l, 0][None], w_q_mem[l].astype(BF16), kv.reshape(b, mem_len, 2 * d),
            w_o_mem[l].astype(BF16), ln_g[l, 1][None], ln_b[l, 1][None],
            w_router[l].astype(BF16), b_router[l][None], b, s)
        dest, gaps, block_expert, n_used, next_expert = _dispatch_plan(top_idx, counts, n)
        y_rows = _moe_experts(block_expert, n_used, next_expert, dest, gaps, x2,
                              w_gate_up[l], b_gate_up[l][:, None, :], w_down[l], b_down[l][:, None, :])
        xf = _combine(dest, y_rows, x2, gates, ln_g[l, 2][None], ln_b[l, 2][None])
    return xf.reshape(b, s, d)
```

```python
import functools

import jax
import jax.numpy as jnp
from jax import lax
from jax.experimental import pallas as pl
from jax.experimental.pallas import tpu as pltpu

D_MODEL = 1024
CHUNK = 64
LEFT_CHUNKS = 8
LEFT = LEFT_CHUNKS * CHUNK
HEAD_DIM = 64
N_HEADS_A = 8
N_HEADS_B = 8
WIDTH_A = N_HEADS_A * HEAD_DIM
WIDTH_B = N_HEADS_B * HEAD_DIM
MIX_WIDTH = WIDTH_A + WIDTH_B
REL_CLIP = 128
N_HEADS_MEM = 4
HEAD_DIM_MEM = D_MODEL // N_HEADS_MEM
N_EXPERTS = 32
TOP_K = 4
D_FF = D_MODEL
SWIGLU_LIMIT = 7.0
SWIGLU_ALPHA = 1.702
LN_EPS = 1e-5
RMS_EPS = 1e-6
DEEPNORM_ALPHA = 2.0 ** 0.25
NEG_INF = -1e30
LOG2E = 1.4426950408889634

LANES = 128
LANE_BLOCKS = D_MODEL // LANES
VMEM_LIMIT = 48 * 1024 * 1024
VMEM_LIMIT_MOE = 56 * 1024 * 1024

TQ_A = 512
SUB_A = 256
HEADS_A_STEP = 8
TB_SB = 256
SB_TILES = 8
SB_UNDERFLOW = 110.0
TM_TOK = 512
TM_MOE = 256
MOE_SLOTS = 3
TM_COMB = 256
COMB_SLOTS = 3
TM_RANK = 512

F32 = jnp.float32
BF16 = jnp.bfloat16
_NT = (((1,), (1,)), ((), ()))


def _params(n_axes):
    return pltpu.CompilerParams(dimension_semantics=("arbitrary",) * n_axes,
                                vmem_limit_bytes=VMEM_LIMIT)


def _store_token_major(ref, value):
    rows = value.shape[0]
    for c in range(LANE_BLOCKS):
        ref[pl.ds(c, rows, stride=LANE_BLOCKS), :] = value[:, c * LANES:(c + 1) * LANES]


def _load_token_major(ref, first_row, rows):
    return jnp.concatenate(
        [ref[pl.ds(first_row * LANE_BLOCKS + c, rows, stride=LANE_BLOCKS), :] for c in range(LANE_BLOCKS)],
        axis=1)


def _layer_norm(r, g, b):
    mu = jnp.mean(r, axis=-1, keepdims=True)
    d = r - mu
    var = jnp.mean(d * d, axis=-1, keepdims=True)
    return d * lax.rsqrt(var + LN_EPS) * g + b


def _matmul_kernel(x_ref, w_ref, o_ref):
    o_ref[...] = jnp.dot(x_ref[...].astype(BF16), w_ref[...],
                         preferred_element_type=F32).astype(o_ref.dtype)


def _matmul(x, w, out_dtype, tm, tn):
    m, k = x.shape
    n = w.shape[1]
    return pl.pallas_call(
        _matmul_kernel,
        grid=(m // tm, n // tn),
        in_specs=[pl.BlockSpec((tm, k), lambda i, j: (i, 0)),
                  pl.BlockSpec((k, tn), lambda i, j: (0, j))],
        out_specs=pl.BlockSpec((tm, tn), lambda i, j: (i, j)),
        out_shape=jax.ShapeDtypeStruct((m, n), out_dtype),
        compiler_params=_params(2),
        name="matmul",
    )(x, w)


def _chunk_attn_kernel(q_ref, kp_ref, kc_ref, vp_ref, vc_ref, bias0_ref, bias1_ref, o_ref):
    w = SUB_A + LEFT
    for h in range(HEADS_A_STEP):
        sl = slice(h * HEAD_DIM, (h + 1) * HEAD_DIM)
        k = jnp.concatenate([kp_ref[:, sl], kc_ref[:, sl]], axis=0)
        v = jnp.concatenate([vp_ref[:, sl], vc_ref[:, sl]], axis=0)
        for sub, bias_ref in enumerate((bias0_ref, bias1_ref)):
            rows = slice(sub * SUB_A, (sub + 1) * SUB_A)
            keys = slice(sub * SUB_A, sub * SUB_A + w)
            s = lax.dot_general(q_ref[rows, sl], k[keys], _NT, preferred_element_type=F32) + bias_ref[h]
            m = jnp.max(s, axis=-1, keepdims=True)
            p = jnp.exp(s - m)
            l = jnp.sum(p, axis=-1, keepdims=True)
            o = jnp.dot(p.astype(BF16), v[keys], preferred_element_type=F32) / l
            o_ref[rows, sl] = o.astype(o_ref.dtype)


def _chunk_bias(rel_bias):
    w = SUB_A + LEFT
    heads = rel_bias.shape[0]
    m = jnp.arange(2 * w)
    rel = jnp.where(m < w, LEFT - m, LEFT + 2 * w - m)
    diag = rel_bias[:, jnp.clip(rel, -REL_CLIP, REL_CLIP) + REL_CLIP].astype(F32)[:, None, :]
    n_tables = 1 + TQ_A // SUB_A
    return pl.pallas_call(
        _chunk_bias_kernel,
        grid=(heads,),
        in_specs=[pl.BlockSpec((1, 1, 2 * w), lambda h: (h, 0, 0))],
        out_specs=pl.BlockSpec((n_tables, 1, SUB_A, w), lambda h: (0, h, 0, 0)),
        out_shape=jax.ShapeDtypeStruct((n_tables, heads, SUB_A, w), F32),
        compiler_params=_params(1),
        name="chunk_bias",
    )(diag)


def _chunk_bias_kernel(diag_ref, o_ref):
    w = SUB_A + LEFT
    rolled = pltpu.roll(jnp.broadcast_to(diag_ref[0], (SUB_A, 2 * w)), 0, 1, stride=1, stride_axis=0)
    col = lax.broadcasted_iota(jnp.int32, (SUB_A, w), 1)
    qc = lax.broadcasted_iota(jnp.int32, (SUB_A, w), 0) // CHUNK
    kc = col // CHUNK
    band = (kc >= qc) & (kc <= qc + LEFT_CHUNKS)
    table = jnp.where(band, rolled[:, :w], NEG_INF)
    o_ref[0, 0] = table
    for sub in range(TQ_A // SUB_A):
        o_ref[1 + sub, 0] = jnp.where(col >= LEFT - sub * SUB_A, table, NEG_INF)


def _chunk_attention(proj, bias, b, s):
    nq = s // TQ_A
    width = HEADS_A_STEP * HEAD_DIM
    groups = WIDTH_A // width
    blk = lambda off, prev: pl.BlockSpec(
        (None, TQ_A, width),
        (lambda bi, p, i: (bi, jnp.maximum(i - 1, 0), off + p)) if prev
        else (lambda bi, p, i: (bi, i, off + p)))
    table = lambda sub: pl.BlockSpec((None, HEADS_A_STEP, SUB_A, SUB_A + LEFT),
                                     lambda bi, p, i: (jnp.where(i == 0, 1 + sub, 0), p, 0, 0))
    return pl.pallas_call(
        _chunk_attn_kernel,
        grid=(b, groups, nq),
        in_specs=[blk(0, False),
                  blk(groups, True), blk(groups, False),
                  blk(2 * groups, True), blk(2 * groups, False),
                  table(0), table(1)],
        out_specs=pl.BlockSpec((None, TQ_A, width), lambda bi, p, i: (bi, i, p)),
        out_shape=jax.ShapeDtypeStruct((b, s, WIDTH_A), BF16),
        compiler_params=_params(3),
        name="chunk_attn",
    )(proj, proj, proj, proj, proj, bias, bias)


def _sb_block(q, k_ref, v_ref, sl, j, later, tri, causal):
    t = TB_SB
    start = pl.multiple_of(j * t, t)
    k = k_ref[pl.ds(start, t), sl]
    v = v_ref[pl.ds(start, t), sl]
    z = lax.dot_general(q, k, _NT, preferred_element_type=F32)
    sp = jnp.maximum(z, 0.0) + jnp.log(1.0 + jnp.exp2(jnp.abs(z) * (-LOG2E)))
    if causal is not None:
        sp = jnp.where(causal, sp, 0.0)
    csum = jnp.dot(sp.astype(BF16), tri, preferred_element_type=F32) + later
    wgt = jnp.exp(z - csum)
    if causal is not None:
        wgt = jnp.where(causal, wgt, 0.0)
    pv = jnp.dot(wgt.astype(BF16), v, preferred_element_type=F32)
    return pv, csum[:, 0:1]


def _sb_attn_kernel(q_ref, k_ref, v_ref, o_ref):
    first = pl.program_id(2) * SB_TILES
    t = TB_SB
    row = lax.broadcasted_iota(jnp.int32, (t, t), 0)
    col = lax.broadcasted_iota(jnp.int32, (t, t), 1)
    tri = jnp.where(row >= col, 1.0, 0.0).astype(BF16)
    causal = col < row
    chains = [(u, h) for u in range(SB_TILES) for h in range(2)]
    rows = lambda u: slice(u * t, (u + 1) * t)
    lanes = lambda h: slice(h * HEAD_DIM, (h + 1) * HEAD_DIM)
    qs = [q_ref[rows(u), lanes(h)] for u, h in chains]

    def block(c, j, later, masked):
        return _sb_block(qs[c], k_ref, v_ref, lanes(chains[c][1]), j, later, tri, causal if masked else None)

    carry = []
    for c, (u, h) in enumerate(chains):
        tile = first + u
        pv, later = block(c, tile, jnp.zeros((t, 1), F32), True)
        pv_prev, later_prev = block(c, jnp.maximum(tile - 1, 0), later, False)
        has_prev = tile >= 1
        carry += [jnp.where(has_prev, later_prev, later), pv + jnp.where(has_prev, pv_prev, 0.0)]

    def pending(c):
        todo = False
        for ci, (u, h) in enumerate(chains):
            todo = todo | ((first + u - 2 - c[0] >= 0) & (jnp.min(c[1 + 2 * ci]) < SB_UNDERFLOW))
        return todo

    def step(c):
        out = [c[0] + 1]
        for ci, (u, h) in enumerate(chains):
            j = first + u - 2 - c[0]
            pv, later = block(ci, jnp.maximum(j, 0), c[1 + 2 * ci], False)
            out += [jnp.where(j >= 0, later, c[1 + 2 * ci]), c[2 + 2 * ci] + jnp.where(j >= 0, pv, 0.0)]
        return tuple(out)

    carry = lax.while_loop(pending, step, (jnp.int32(0), *carry))
    for ci, (u, h) in enumerate(chains):
        o_ref[rows(u), lanes(h)] = carry[2 + 2 * ci].astype(o_ref.dtype)


def _sb_attention(proj, b, s):
    tq = SB_TILES * TB_SB
    nq = s // tq
    pairs = WIDTH_B // LANES
    base = 3 * WIDTH_A // LANES
    return pl.pallas_call(
        _sb_attn_kernel,
        grid=(b, pairs, nq),
        in_specs=[pl.BlockSpec((None, tq, LANES), lambda bi, p, i: (bi, i, base + p)),
                  pl.BlockSpec((None, s, LANES), lambda bi, p, i: (bi, 0, base + pairs + p)),
                  pl.BlockSpec((None, s, LANES), lambda bi, p, i: (bi, 0, base + 2 * pairs + p))],
        out_specs=pl.BlockSpec((None, tq, LANES), lambda bi, p, i: (bi, i, p)),
        out_shape=jax.ShapeDtypeStruct((b, s, WIDTH_B), BF16),
        compiler_params=_params(3),
        name="sb_attn",
    )(proj, proj, proj)


def _token_block_kernel(oa_ref, ob_ref, x_ref, ga_ref, gb_ref, wout_ref, ln1g_ref, ln1b_ref, wq_ref,
                        kv_ref, wo_ref, ln2g_ref, ln2b_ref, wr_ref, br_ref,
                        x2_ref, idx_ref, gate_ref, counts_ref):
    def rms(ref, g_ref):
        a = ref[...].astype(F32)
        return (a * lax.rsqrt(jnp.mean(a * a, axis=-1, keepdims=True) + RMS_EPS) * g_ref[...]).astype(BF16)

    y = jnp.dot(rms(oa_ref, ga_ref), wout_ref[:WIDTH_A, :], preferred_element_type=F32)
    y = y + jnp.dot(rms(ob_ref, gb_ref), wout_ref[WIDTH_A:, :], preferred_element_type=F32)
    x1 = _layer_norm(DEEPNORM_ALPHA * x_ref[...] + y, ln1g_ref[...], ln1b_ref[...])
    qm = jnp.dot(x1.astype(BF16), wq_ref[...], preferred_element_type=F32).astype(BF16)

    heads = []
    for h in range(N_HEADS_MEM):
        sl = slice(h * HEAD_DIM_MEM, (h + 1) * HEAD_DIM_MEM)
        q = qm[:, sl] * (HEAD_DIM_MEM ** -0.5)
        k = kv_ref[:, sl]
        v = kv_ref[:, D_MODEL + h * HEAD_DIM_MEM:D_MODEL + (h + 1) * HEAD_DIM_MEM]
        s = lax.dot_general(q, k, _NT, preferred_element_type=F32)
        m = jnp.max(s, axis=-1, keepdims=True)
        p = jnp.exp(s - m)
        l = jnp.sum(p, axis=-1, keepdims=True)
        heads.append((jnp.dot(p.astype(BF16), v, preferred_element_type=F32) / l).astype(BF16))
    o = jnp.concatenate(heads, axis=-1)
    y = jnp.dot(o, wo_ref[...], preferred_element_type=F32)
    x2 = _layer_norm(DEEPNORM_ALPHA * x1 + y, ln2g_ref[...], ln2b_ref[...])
    _store_token_major(x2_ref, x2)

    logits = jnp.dot(x2.astype(BF16), wr_ref[...], preferred_element_type=F32) + br_ref[...]
    tm = logits.shape[0]
    e_iota = lax.broadcasted_iota(jnp.int32, (tm, N_EXPERTS), 1)
    lane = lax.broadcasted_iota(jnp.int32, (tm, LANES), 1)
    idx_out = jnp.zeros((tm, LANES), jnp.int32)
    val_out = jnp.zeros((tm, LANES), F32)
    top = None
    denom = jnp.zeros((tm, 1), F32)
    chosen = jnp.zeros((1, LANES), F32)
    for kk in range(TOP_K):
        m = jnp.max(logits, axis=-1, keepdims=True)
        sel = jnp.min(jnp.where(logits == m, e_iota, N_EXPERTS), axis=-1, keepdims=True)
        if top is None:
            top = m
        e = jnp.exp(m - top)
        denom = denom + e
        idx_out = jnp.where(lane == kk, sel, idx_out)
        val_out = jnp.where(lane == kk, e, val_out)
        chosen = chosen + jnp.sum(jnp.where(lane == sel, 1.0, 0.0), axis=0, keepdims=True)
        logits = jnp.where(e_iota == sel, -jnp.inf, logits)
    idx_ref[...] = idx_out
    gate_ref[...] = val_out / denom

    @pl.when((pl.program_id(0) == 0) & (pl.program_id(1) == 0))
    def _():
        counts_ref[...] = jnp.zeros_like(counts_ref)

    counts_ref[...] += chosen


def _token_block(out_a, out_b, x, g_a, g_b, w_out, ln1_g, ln1_b, w_q, kv, w_o, ln2_g, ln2_b, w_r, b_r, b, s):
    tm = TM_TOK
    nt = s // tm
    mem_len = kv.shape[1]
    n = b * s
    row = lambda width: pl.BlockSpec((tm, width), lambda bi, i: (bi * nt + i, 0))
    full = lambda r, c: pl.BlockSpec((r, c), lambda bi, i: (0, 0))
    return pl.pallas_call(
        _token_block_kernel,
        grid=(b, nt),
        in_specs=[row(WIDTH_A), row(WIDTH_B), row(D_MODEL), full(1, WIDTH_A), full(1, WIDTH_B),
                  full(MIX_WIDTH, D_MODEL), full(1, D_MODEL), full(1, D_MODEL), full(D_MODEL, D_MODEL),
                  pl.BlockSpec((None, mem_len, 2 * D_MODEL), lambda bi, i: (bi, 0, 0)),
                  full(D_MODEL, D_MODEL), full(1, D_MODEL), full(1, D_MODEL),
                  full(D_MODEL, N_EXPERTS), full(1, N_EXPERTS)],
        out_specs=[pl.BlockSpec((tm * LANE_BLOCKS, LANES), lambda bi, i: (bi * nt + i, 0)), row(LANES), row(LANES),
                   full(1, LANES)],
        out_shape=[jax.ShapeDtypeStruct((n * LANE_BLOCKS, LANES), F32),
                   jax.ShapeDtypeStruct((n, LANES), jnp.int32),
                   jax.ShapeDtypeStruct((n, LANES), F32),
                   jax.ShapeDtypeStruct((1, LANES), F32)],
        compiler_params=_params(2),
        name="token_block",
    )(out_a, out_b, x, g_a, g_b, w_out, ln1_g, ln1_b, w_q, kv, w_o, ln2_g, ln2_b, w_r, b_r)


def _gather_rows(idx_ref, first, count, src_hbm, dst_ref, sem):
    for r in range(count):
        src = pl.ds(pl.multiple_of(idx_ref[first + r], LANE_BLOCKS), LANE_BLOCKS)
        pltpu.make_async_copy(src_hbm.at[src], dst_ref.at[pl.ds(r * LANE_BLOCKS, LANE_BLOCKS)], sem).start(
            priority=r % 2)


def _wait_rows(count, src_hbm, dst_ref, sem):
    pltpu.make_async_copy(src_hbm.at[pl.ds(0, count * LANE_BLOCKS)], dst_ref, sem).wait()


def _fill_row_table(dest_ref, gap_ref, tok_ref):
    unroll = 16

    def clear(c, carry):
        for u in range(unroll):
            tok_ref[c * unroll + u] = 0
        return carry

    def place(c, carry):
        rows = [dest_ref[c * unroll + u] for u in range(unroll)]
        for u in range(unroll):
            tok_ref[rows[u]] = (c * (unroll // TOP_K) + u // TOP_K) * LANE_BLOCKS
        return carry

    for g in range(gap_ref.shape[1]):
        lax.fori_loop(gap_ref[0, g] // unroll, gap_ref[1, g] // unroll, clear, 0)
    lax.fori_loop(0, dest_ref.shape[0] // unroll, place, 0)


def _moe_kernel(bexp_ref, nused_ref, next_ref, dest_ref, gap_ref, x_hbm, wgu_hbm, bgu_ref, wd_hbm, bd_ref,
                y_ref, xbuf, wgu_f32, wd_f32, wgu_bf, wd_bf, tok_ref, sem, wsem):
    i = pl.program_id(0)
    slot = i % MOE_SLOTS
    ahead = MOE_SLOTS - 1
    n_used = nused_ref[0]

    def weight_copies(expert):
        return (pltpu.make_async_copy(wgu_hbm.at[expert], wgu_f32, wsem.at[0]),
                pltpu.make_async_copy(wd_hbm.at[expert], wd_f32, wsem.at[1]))

    @pl.when(i == 0)
    def _():
        for copy in weight_copies(bexp_ref[0]):
            copy.start()
        _fill_row_table(dest_ref, gap_ref, tok_ref)
        for b in range(ahead):
            _gather_rows(tok_ref, b * TM_MOE, TM_MOE, x_hbm, xbuf.at[b], sem.at[b])

    @pl.when(i < n_used + ahead)
    def _():
        _wait_rows(TM_MOE, x_hbm, xbuf.at[slot], sem.at[slot])

    @pl.when((i < n_used) & ((i == 0) | (bexp_ref[i] != bexp_ref[jnp.maximum(i - 1, 0)])))
    def _():
        for copy in weight_copies(bexp_ref[i]):
            copy.wait()
        wgu_bf[...] = wgu_f32[...].astype(BF16)
        wd_bf[...] = wd_f32[...].astype(BF16)

        @pl.when(next_ref[i] >= 0)
        def _():
            for copy in weight_copies(next_ref[i]):
                copy.start()

    @pl.when(i < n_used)
    def _():
        x = _load_token_major(xbuf.at[slot], 0, TM_MOE).astype(BF16)
        nxt = (i + ahead) % MOE_SLOTS
        _gather_rows(tok_ref, (i + ahead) * TM_MOE, TM_MOE, x_hbm, xbuf.at[nxt], sem.at[nxt])
        gu = jnp.dot(x, wgu_bf[...], preferred_element_type=F32) + bgu_ref[0]
        gate = jnp.minimum(gu[:, :D_FF], SWIGLU_LIMIT)
        up = jnp.clip(gu[:, D_FF:], -SWIGLU_LIMIT, SWIGLU_LIMIT)
        glu = gate * jax.nn.sigmoid(gate * SWIGLU_ALPHA)
        hid = ((up + 1.0) * glu).astype(BF16)
        _store_token_major(y_ref, jnp.dot(hid, wd_bf[...], preferred_element_type=F32) + bd_ref[0])

    @pl.when(i >= n_used)
    def _():
        y_ref[...] = jnp.zeros_like(y_ref)


def _moe_experts(block_expert, n_used, next_expert, dest, gaps, x2, w_gu, b_gu, w_d, b_d):
    nb = block_expert.shape[0]
    grid_spec = pltpu.PrefetchScalarGridSpec(
        num_scalar_prefetch=5,
        grid=(nb,),
        in_specs=[pl.BlockSpec(memory_space=pl.ANY),
                  pl.BlockSpec(memory_space=pl.ANY),
                  pl.BlockSpec((1, 1, 2 * D_FF), lambda i, be, nu, ne, de, ga: (be[i], 0, 0)),
                  pl.BlockSpec(memory_space=pl.ANY),
                  pl.BlockSpec((1, 1, D_MODEL), lambda i, be, nu, ne, de, ga: (be[i], 0, 0))],
        out_specs=pl.BlockSpec((TM_MOE * LANE_BLOCKS, LANES), lambda i, be, nu, ne, de, ga: (i, 0)),
        scratch_shapes=[pltpu.VMEM((MOE_SLOTS, TM_MOE * LANE_BLOCKS, LANES), F32),
                        pltpu.VMEM((D_MODEL, 2 * D_FF), F32), pltpu.VMEM((D_FF, D_MODEL), F32),
                        pltpu.VMEM((D_MODEL, 2 * D_FF), BF16), pltpu.VMEM((D_FF, D_MODEL), BF16),
                        pltpu.SMEM((nb * TM_MOE,), jnp.int32),
                        pltpu.SemaphoreType.DMA((MOE_SLOTS,)), pltpu.SemaphoreType.DMA((2,))],
    )
    return pl.pallas_call(
        _moe_kernel,
        grid_spec=grid_spec,
        out_shape=jax.ShapeDtypeStruct((nb * TM_MOE * LANE_BLOCKS, LANES), F32),
        compiler_params=pltpu.CompilerParams(dimension_semantics=("arbitrary",),
                                             vmem_limit_bytes=VMEM_LIMIT_MOE),
        name="moe_experts",
    )(block_expert, n_used, next_expert, dest, gaps, x2, w_gu, b_gu, w_d, b_d)


def _combine_kernel(idx_ref, y_hbm, x2_ref, gate_ref, lng_ref, lnb_ref, o_ref, *scratch):
    ybufs, sem = scratch[:COMB_SLOTS], scratch[COMB_SLOTS]
    i = pl.program_id(0)
    nt = pl.num_programs(0)
    ahead = COMB_SLOTS - 1
    rows = TOP_K * TM_COMB

    @pl.when(i == 0)
    def _():
        for t in range(ahead):
            _gather_rows(idx_ref, t * rows, rows, y_hbm, ybufs[t], sem.at[t])

    def compute(ybuf):
        g = gate_ref[...]
        ff = jnp.zeros((TM_COMB, D_MODEL), F32)
        for kk in range(TOP_K):
            ff = ff + g[:, kk:kk + 1] * _load_token_major(ybuf, kk * TM_COMB, TM_COMB)
        x2 = _load_token_major(x2_ref, 0, TM_COMB)
        o_ref[...] = _layer_norm(DEEPNORM_ALPHA * x2 + ff, lng_ref[...], lnb_ref[...])

    for k in range(COMB_SLOTS):
        nxt = (k + ahead) % COMB_SLOTS

        @pl.when((i % COMB_SLOTS == k) & (i + ahead < nt))
        def _():
            _wait_rows(rows, y_hbm, ybufs[k], sem.at[k])
            _gather_rows(idx_ref, (i + ahead) * rows, rows, y_hbm, ybufs[nxt], sem.at[nxt])
            compute(ybufs[k])

        @pl.when((i % COMB_SLOTS == k) & (i + ahead >= nt))
        def _():
            _wait_rows(rows, y_hbm, ybufs[k], sem.at[k])
            compute(ybufs[k])


def _combine(dest, y_rows, x2, gates, ln_g, ln_b):
    n = x2.shape[0] // LANE_BLOCKS
    nt = n // TM_COMB
    rows = TOP_K * TM_COMB
    idx = (dest * LANE_BLOCKS).reshape(nt, TM_COMB, TOP_K).transpose(0, 2, 1).reshape(nt * rows)
    row = lambda width: pl.BlockSpec((TM_COMB, width), lambda i, idx: (i, 0))
    full = lambda r, c: pl.BlockSpec((r, c), lambda i, idx: (0, 0))
    grid_spec = pltpu.PrefetchScalarGridSpec(
        num_scalar_prefetch=1,
        grid=(nt,),
        in_specs=[pl.BlockSpec(memory_space=pl.ANY),
                  pl.BlockSpec((TM_COMB * LANE_BLOCKS, LANES), lambda i, idx: (i, 0)), row(LANES),
                  full(1, D_MODEL), full(1, D_MODEL)],
        out_specs=row(D_MODEL),
        scratch_shapes=[pltpu.VMEM((rows * LANE_BLOCKS, LANES), F32)] * COMB_SLOTS
        + [pltpu.SemaphoreType.DMA((COMB_SLOTS,))],
    )
    return pl.pallas_call(
        _combine_kernel,
        grid_spec=grid_spec,
        out_shape=jax.ShapeDtypeStruct((n, D_MODEL), F32),
        compiler_params=_params(1),
        name="combine",
    )(idx, y_rows, x2, gates, ln_g, ln_b)


def _rank_kernel(idx_ref, counts_ref, dest_ref, running):
    j = pl.program_id(0)
    t = TM_RANK
    idx = idx_ref[...]
    e_iota = lax.broadcasted_iota(jnp.int32, (t, LANES), 1)
    hot = [idx[:, k:k + 1] == e_iota for k in range(TOP_K)]
    chosen = sum(jnp.where(h, 1.0, 0.0) for h in hot)

    @pl.when(j == 0)
    def _():
        counts = counts_ref[...]
        padded = jnp.ceil(counts * (1.0 / TM_MOE)) * TM_MOE
        lane = lax.broadcasted_iota(jnp.int32, (1, LANES), 1)
        scan = padded
        shift = 1
        while shift < N_EXPERTS:
            scan = scan + jnp.where(lane >= shift, pltpu.roll(scan, shift, 1), 0.0)
            shift *= 2
        running[...] = scan - padded

    r_iota = lax.broadcasted_iota(jnp.int32, (t, t), 0)
    c_iota = lax.broadcasted_iota(jnp.int32, (t, t), 1)
    earlier = jnp.where(c_iota < r_iota, 1.0, 0.0).astype(BF16)
    base = jnp.dot(earlier, chosen.astype(BF16), preferred_element_type=F32) + running[...]
    out = jnp.zeros((t, LANES), jnp.int32)
    for k in range(TOP_K):
        row = jnp.sum(jnp.where(hot[k], base, 0.0), axis=-1, keepdims=True).astype(jnp.int32)
        out = jnp.where(e_iota == k, row, out)
    dest_ref[...] = out
    running[...] += jnp.sum(chosen, axis=0, keepdims=True)


def _dispatch_plan(top_idx, counts, n):
    n_assign = n * TOP_K
    nb = n_assign // TM_MOE + N_EXPERTS + MOE_SLOTS - 2
    dest = pl.pallas_call(
        _rank_kernel,
        grid=(n // TM_RANK,),
        in_specs=[pl.BlockSpec((TM_RANK, LANES), lambda j: (j, 0)), pl.BlockSpec((1, LANES), lambda j: (0, 0))],
        out_specs=pl.BlockSpec((TM_RANK, LANES), lambda j: (j, 0)),
        out_shape=jax.ShapeDtypeStruct((n, LANES), jnp.int32),
        scratch_shapes=[pltpu.VMEM((1, LANES), F32)],
        compiler_params=_params(1),
        name="expert_rank",
    )(top_idx, counts)
    dest = dest[:, :TOP_K].reshape(-1)
    counts = counts[0, :N_EXPERTS].astype(jnp.int32)
    padded = (counts + TM_MOE - 1) // TM_MOE * TM_MOE
    pad_end = jnp.cumsum(padded)
    tail = jnp.stack([pad_end[-1], jnp.int32(nb * TM_MOE)])[:, None]
    gaps = jnp.concatenate([jnp.stack([pad_end - padded + counts, pad_end]), tail], axis=1)
    block_first_row = jnp.arange(nb, dtype=jnp.int32) * TM_MOE
    block_expert = jnp.minimum(jnp.sum(pad_end[None, :] <= block_first_row[:, None], axis=1),
                               N_EXPERTS - 1).astype(jnp.int32)
    n_used = (pad_end[-1:] // TM_MOE).astype(jnp.int32)
    block = jnp.arange(nb, dtype=jnp.int32)
    starts = (block < n_used[0]) & (block > 0) & (block_expert != jnp.roll(block_expert, 1))
    later_start = jnp.min(jnp.where(starts[None, :] & (block[None, :] > block[:, None]), block[None, :], nb), axis=1)
    next_expert = jnp.where(later_start < nb, block_expert[jnp.minimum(later_start, nb - 1)], -1).astype(jnp.int32)
    return dest, gaps, block_expert, n_used, next_expert


def kernel(x, mem, w_in, rel_bias, g_group_a, g_group_b, w_out, w_q_mem, w_kv_mem, w_o_mem, w_router, b_router, w_gate_up, b_gate_up, w_down, b_down, ln_g, ln_b):
    b, s, d = x.shape
    n = b * s
    depth = w_in.shape[0]
    xf = x.reshape(n, d)
    for l in range(depth):
        q_scale = jnp.ones((3 * MIX_WIDTH,), F32).at[:WIDTH_A].set(HEAD_DIM ** -0.5)
        q_scale = q_scale.at[3 * WIDTH_A:3 * WIDTH_A + WIDTH_B].set(HEAD_DIM ** -0.5)
        proj = _matmul(xf, (w_in[l] * q_scale).astype(BF16), BF16, TM_TOK, 3 * MIX_WIDTH).reshape(b, s, 3 * MIX_WIDTH)
        out_a = _chunk_attention(proj, _chunk_bias(rel_bias[l]), b, s).reshape(n, WIDTH_A)
        out_b = _sb_attention(proj, b, s).reshape(n, WIDTH_B)
        mem_len = mem.shape[1]
        kv = _matmul(mem.reshape(b * mem_len, d), w_kv_mem[l].astype(BF16), BF16, b * mem_len, 1024)
        x2, top_idx, gates, counts = _token_block(
            out_a, out_b, xf, g_group_a[l][None], g_group_b[l][None], w_out[l].astype(BF16),
            ln_g[l, 0][None], ln_b[l, 0][None], w_q_mem[l].astype(BF16), kv.reshape(b, mem_len, 2 * d),
            w_o_mem[l].astype(BF16), ln_g[l, 1][None], ln_b[l, 1][None],
            w_router[l].astype(BF16), b_router[l][None], b, s)
        dest, gaps, block_expert, n_used, next_expert = _dispatch_plan(top_idx, counts, n)
        y_rows = _moe_experts(block_expert, n_used, next_expert, dest, gaps, x2,
                              w_gate_up[l], b_gate_up[l][:, None, :], w_down[l], b_down[l][:, None, :])
        xf = _combine(dest, y_rows, x2, gates, ln_g[l, 2][None], ln_b[l, 2][None])
    return xf.reshape(b, s, d)
```

```python
import jax
import jax.numpy as jnp
from jax import lax
from jax.experimental import pallas as pl
from jax.experimental.pallas import tpu as pltpu

D_MODEL = 1024
CHUNK = 64
LEFT_CHUNKS = 8
LEFT = LEFT_CHUNKS * CHUNK
HEAD_DIM = 64
N_HEADS_A = 8
N_HEADS_B = 8
WIDTH_A = N_HEADS_A * HEAD_DIM
WIDTH_B = N_HEADS_B * HEAD_DIM
MIX_WIDTH = WIDTH_A + WIDTH_B
REL_CLIP = 128
N_HEADS_MEM = 4
HEAD_DIM_MEM = D_MODEL // N_HEADS_MEM
N_EXPERTS = 32
TOP_K = 4
D_FF = D_MODEL
SWIGLU_LIMIT = 7.0
SWIGLU_ALPHA = 1.702
LN_EPS = 1e-5
RMS_EPS = 1e-6
DEEPNORM_ALPHA = 2.0 ** 0.25
NEG_INF = -1e30
LOG2E = 1.4426950408889634

LANES = 128
LANE_BLOCKS = D_MODEL // LANES
VMEM_LIMIT = 48 * 1024 * 1024
VMEM_LIMIT_MOE = 56 * 1024 * 1024

TQ_A = 512
SUB_A = 256
HEADS_A_STEP = 4
TB_SB = 256
SB_TILES = 4
SB_UNDERFLOW = 110.0
TM_TOK = 512
TM_MOE = 256
MOE_SLOTS = 3
TM_COMB = 256
COMB_SLOTS = 3
TM_RANK = 512

F32 = jnp.float32
BF16 = jnp.bfloat16
_NT = (((1,), (1,)), ((), ()))


def _params(n_axes):
    return pltpu.CompilerParams(dimension_semantics=("arbitrary",) * n_axes,
                                vmem_limit_bytes=VMEM_LIMIT)


def _store_token_major(ref, value):
    rows = value.shape[0]
    for c in range(LANE_BLOCKS):
        ref[pl.ds(c, rows, stride=LANE_BLOCKS), :] = value[:, c * LANES:(c + 1) * LANES]


def _load_token_major(ref, first_row, rows):
    return jnp.concatenate(
        [ref[pl.ds(first_row * LANE_BLOCKS + c, rows, stride=LANE_BLOCKS), :] for c in range(LANE_BLOCKS)],
        axis=1)


def _layer_norm(r, g, b):
    mu = jnp.mean(r, axis=-1, keepdims=True)
    d = r - mu
    var = jnp.mean(d * d, axis=-1, keepdims=True)
    return d * lax.rsqrt(var + LN_EPS) * g + b


def _matmul_kernel(x_ref, w_ref, o_ref):
    o_ref[...] = jnp.dot(x_ref[...].astype(BF16), w_ref[...],
                         preferred_element_type=F32).astype(o_ref.dtype)


def _matmul(x, w, out_dtype, tm, tn):
    m, k = x.shape
    n = w.shape[1]
    return pl.pallas_call(
        _matmul_kernel,
        grid=(m // tm, n // tn),
        in_specs=[pl.BlockSpec((tm, k), lambda i, j: (i, 0)),
                  pl.BlockSpec((k, tn), lambda i, j: (0, j))],
        out_specs=pl.BlockSpec((tm, tn), lambda i, j: (i, j)),
        out_shape=jax.ShapeDtypeStruct((m, n), out_dtype),
        compiler_params=_params(2),
        name="matmul",
    )(x, w)


def _chunk_attn_kernel(q_ref, kp_ref, kc_ref, vp_ref, vc_ref, bias0_ref, bias1_ref, o_ref):
    w = SUB_A + LEFT
    for h in range(HEADS_A_STEP):
        sl = slice(h * HEAD_DIM, (h + 1) * HEAD_DIM)
        k = jnp.concatenate([kp_ref[:, sl], kc_ref[:, sl]], axis=0)
        v = jnp.concatenate([vp_ref[:, sl], vc_ref[:, sl]], axis=0)
        for sub, bias_ref in enumerate((bias0_ref, bias1_ref)):
            rows = slice(sub * SUB_A, (sub + 1) * SUB_A)
            keys = slice(sub * SUB_A, sub * SUB_A + w)
            s = lax.dot_general(q_ref[rows, sl], k[keys], _NT, preferred_element_type=F32) + bias_ref[h]
            m = jnp.max(s, axis=-1, keepdims=True)
            p = jnp.exp(s - m)
            l = jnp.sum(p, axis=-1, keepdims=True)
            o = jnp.dot(p.astype(BF16), v[keys], preferred_element_type=F32) / l
            o_ref[rows, sl] = o.astype(o_ref.dtype)


def _chunk_bias(rel_bias):
    w = SUB_A + LEFT
    heads = rel_bias.shape[0]
    m = jnp.arange(2 * w)
    rel = jnp.where(m < w, LEFT - m, LEFT + 2 * w - m)
    diag = rel_bias[:, jnp.clip(rel, -REL_CLIP, REL_CLIP) + REL_CLIP].astype(F32)[:, None, :]
    n_tables = 1 + TQ_A // SUB_A
    return pl.pallas_call(
        _chunk_bias_kernel,
        grid=(heads,),
        in_specs=[pl.BlockSpec((1, 1, 2 * w), lambda h: (h, 0, 0))],
        out_specs=pl.BlockSpec((n_tables, 1, SUB_A, w), lambda h: (0, h, 0, 0)),
        out_shape=jax.ShapeDtypeStruct((n_tables, heads, SUB_A, w), F32),
        compiler_params=_params(1),
        name="chunk_bias",
    )(diag)


def _chunk_bias_kernel(diag_ref, o_ref):
    w = SUB_A + LEFT
    rolled = pltpu.roll(jnp.broadcast_to(diag_ref[0], (SUB_A, 2 * w)), 0, 1, stride=1, stride_axis=0)
    col = lax.broadcasted_iota(jnp.int32, (SUB_A, w), 1)
    qc = lax.broadcasted_iota(jnp.int32, (SUB_A, w), 0) // CHUNK
    kc = col // CHUNK
    band = (kc >= qc) & (kc <= qc + LEFT_CHUNKS)
    table = jnp.where(band, rolled[:, :w], NEG_INF)
    o_ref[0, 0] = table
    for sub in range(TQ_A // SUB_A):
        o_ref[1 + sub, 0] = jnp.where(col >= LEFT - sub * SUB_A, table, NEG_INF)


def _chunk_attention(proj, bias, b, s):
    nq = s // TQ_A
    width = HEADS_A_STEP * HEAD_DIM
    groups = WIDTH_A // width
    blk = lambda off, prev: pl.BlockSpec(
        (None, TQ_A, width),
        (lambda bi, p, i: (bi, jnp.maximum(i - 1, 0), off + p)) if prev
        else (lambda bi, p, i: (bi, i, off + p)))
    table = lambda sub: pl.BlockSpec((None, HEADS_A_STEP, SUB_A, SUB_A + LEFT),
                                     lambda bi, p, i: (jnp.where(i == 0, 1 + sub, 0), p, 0, 0))
    return pl.pallas_call(
        _chunk_attn_kernel,
        grid=(b, groups, nq),
        in_specs=[blk(0, False),
                  blk(groups, True), blk(groups, False),
                  blk(2 * groups, True), blk(2 * groups, False),
                  table(0), table(1)],
        out_specs=pl.BlockSpec((None, TQ_A, width), lambda bi, p, i: (bi, i, p)),
        out_shape=jax.ShapeDtypeStruct((b, s, WIDTH_A), BF16),
        compiler_params=_params(3),
        name="chunk_attn",
    )(proj, proj, proj, proj, proj, bias, bias)


def _sb_block(q, k_ref, v_ref, sl, j, later, tri, causal):
    t = TB_SB
    start = pl.multiple_of(j * t, t)
    k = k_ref[pl.ds(start, t), sl]
    v = v_ref[pl.ds(start, t), sl]
    z = lax.dot_general(q, k, _NT, preferred_element_type=F32)
    sp = jnp.maximum(z, 0.0) + jnp.log(1.0 + jnp.exp2(jnp.abs(z) * (-LOG2E)))
    if causal is not None:
        sp = jnp.where(causal, sp, 0.0)
    csum = jnp.dot(sp.astype(BF16), tri, preferred_element_type=F32) + later
    wgt = jnp.exp(z - csum)
    if causal is not None:
        wgt = jnp.where(causal, wgt, 0.0)
    pv = jnp.dot(wgt.astype(BF16), v, preferred_element_type=F32)
    return pv, csum[:, 0:1]


def _sb_attn_kernel(q_ref, k_ref, v_ref, o_ref):
    first = pl.program_id(2) * SB_TILES
    t = TB_SB
    row = lax.broadcasted_iota(jnp.int32, (t, t), 0)
    col = lax.broadcasted_iota(jnp.int32, (t, t), 1)
    tri = jnp.where(row >= col, 1.0, 0.0).astype(BF16)
    causal = col < row
    chains = [(u, h) for u in range(SB_TILES) for h in range(2)]
    rows = lambda u: slice(u * t, (u + 1) * t)
    lanes = lambda h: slice(h * HEAD_DIM, (h + 1) * HEAD_DIM)
    qs = [q_ref[rows(u), lanes(h)] for u, h in chains]

    def block(c, j, later, masked):
        return _sb_block(qs[c], k_ref, v_ref, lanes(chains[c][1]), j, later, tri, causal if masked else None)

    carry = []
    for c, (u, h) in enumerate(chains):
        tile = first + u
        pv, later = block(c, tile, jnp.zeros((t, 1), F32), True)
        pv_prev, later_prev = block(c, jnp.maximum(tile - 1, 0), later, False)
        has_prev = tile >= 1
        carry += [jnp.where(has_prev, later_prev, later), pv + jnp.where(has_prev, pv_prev, 0.0)]

    def pending(c):
        todo = False
        for ci, (u, h) in enumerate(chains):
            todo = todo | ((first + u - 2 - c[0] >= 0) & (jnp.min(c[1 + 2 * ci]) < SB_UNDERFLOW))
        return todo

    def step(c):
        out = [c[0] + 1]
        for ci, (u, h) in enumerate(chains):
            j = first + u - 2 - c[0]
            pv, later = block(ci, jnp.maximum(j, 0), c[1 + 2 * ci], False)
            out += [jnp.where(j >= 0, later, c[1 + 2 * ci]), c[2 + 2 * ci] + jnp.where(j >= 0, pv, 0.0)]
        return tuple(out)

    carry = lax.while_loop(pending, step, (jnp.int32(0), *carry))
    for ci, (u, h) in enumerate(chains):
        o_ref[rows(u), lanes(h)] = carry[2 + 2 * ci].astype(o_ref.dtype)


def _sb_attention(proj, b, s):
    tq = SB_TILES * TB_SB
    nq = s // tq
    pairs = WIDTH_B // LANES
    base = 3 * WIDTH_A // LANES
    return pl.pallas_call(
        _sb_attn_kernel,
        grid=(b, pairs, nq),
        in_specs=[pl.BlockSpec((None, tq, LANES), lambda bi, p, i: (bi, i, base + p)),
                  pl.BlockSpec((None, s, LANES), lambda bi, p, i: (bi, 0, base + pairs + p)),
                  pl.BlockSpec((None, s, LANES), lambda bi, p, i: (bi, 0, base + 2 * pairs + p))],
        out_specs=pl.BlockSpec((None, tq, LANES), lambda bi, p, i: (bi, i, p)),
        out_shape=jax.ShapeDtypeStruct((b, s, WIDTH_B), BF16),
        compiler_params=_params(3),
        name="sb_attn",
    )(proj, proj, proj)


def _token_block_kernel(oa_ref, ob_ref, x_ref, ga_ref, gb_ref, wout_ref, ln1g_ref, ln1b_ref, wq_ref,
                        kv_ref, wo_ref, ln2g_ref, ln2b_ref, wr_ref, br_ref,
                        x2_ref, idx_ref, gate_ref, counts_ref):
    def rms(ref, g_ref):
        a = ref[...].astype(F32)
        return (a * lax.rsqrt(jnp.mean(a * a, axis=-1, keepdims=True) + RMS_EPS) * g_ref[...]).astype(BF16)

    y = jnp.dot(rms(oa_ref, ga_ref), wout_ref[:WIDTH_A, :], preferred_element_type=F32)
    y = y + jnp.dot(rms(ob_ref, gb_ref), wout_ref[WIDTH_A:, :], preferred_element_type=F32)
    x1 = _layer_norm(DEEPNORM_ALPHA * x_ref[...] + y, ln1g_ref[...], ln1b_ref[...])
    qm = jnp.dot(x1.astype(BF16), wq_ref[...], preferred_element_type=F32).astype(BF16)

    heads = []
    for h in range(N_HEADS_MEM):
        sl = slice(h * HEAD_DIM_MEM, (h + 1) * HEAD_DIM_MEM)
        q = qm[:, sl] * (HEAD_DIM_MEM ** -0.5)
        k = kv_ref[:, sl]
        v = kv_ref[:, D_MODEL + h * HEAD_DIM_MEM:D_MODEL + (h + 1) * HEAD_DIM_MEM]
        s = lax.dot_general(q, k, _NT, preferred_element_type=F32)
        m = jnp.max(s, axis=-1, keepdims=True)
        p = jnp.exp(s - m)
        l = jnp.sum(p, axis=-1, keepdims=True)
        heads.append((jnp.dot(p.astype(BF16), v, preferred_element_type=F32) / l).astype(BF16))
    o = jnp.concatenate(heads, axis=-1)
    y = jnp.dot(o, wo_ref[...], preferred_element_type=F32)
    x2 = _layer_norm(DEEPNORM_ALPHA * x1 + y, ln2g_ref[...], ln2b_ref[...])
    _store_token_major(x2_ref, x2)

    logits = jnp.dot(x2.astype(BF16), wr_ref[...], preferred_element_type=F32) + br_ref[...]
    tm = logits.shape[0]
    e_iota = lax.broadcasted_iota(jnp.int32, (tm, N_EXPERTS), 1)
    lane = lax.broadcasted_iota(jnp.int32, (tm, LANES), 1)
    idx_out = jnp.zeros((tm, LANES), jnp.int32)
    val_out = jnp.zeros((tm, LANES), F32)
    top = None
    denom = jnp.zeros((tm, 1), F32)
    chosen = jnp.zeros((1, LANES), F32)
    for kk in range(TOP_K):
        m = jnp.max(logits, axis=-1, keepdims=True)
        sel = jnp.min(jnp.where(logits == m, e_iota, N_EXPERTS), axis=-1, keepdims=True)
        if top is None:
            top = m
        e = jnp.exp(m - top)
        denom = denom + e
        idx_out = jnp.where(lane == kk, sel, idx_out)
        val_out = jnp.where(lane == kk, e, val_out)
        chosen = chosen + jnp.sum(jnp.where(lane == sel, 1.0, 0.0), axis=0, keepdims=True)
        logits = jnp.where(e_iota == sel, -jnp.inf, logits)
    idx_ref[...] = idx_out
    gate_ref[...] = val_out / denom

    @pl.when((pl.program_id(0) == 0) & (pl.program_id(1) == 0))
    def _():
        counts_ref[...] = jnp.zeros_like(counts_ref)

    counts_ref[...] += chosen


def _token_block(out_a, out_b, x, g_a, g_b, w_out, ln1_g, ln1_b, w_q, kv, w_o, ln2_g, ln2_b, w_r, b_r, b, s):
    tm = TM_TOK
    nt = s // tm
    mem_len = kv.shape[1]
    n = b * s
    row = lambda width: pl.BlockSpec((tm, width), lambda bi, i: (bi * nt + i, 0))
    full = lambda r, c: pl.BlockSpec((r, c), lambda bi, i: (0, 0))
    return pl.pallas_call(
        _token_block_kernel,
        grid=(b, nt),
        in_specs=[row(WIDTH_A), row(WIDTH_B), row(D_MODEL), full(1, WIDTH_A), full(1, WIDTH_B),
                  full(MIX_WIDTH, D_MODEL), full(1, D_MODEL), full(1, D_MODEL), full(D_MODEL, D_MODEL),
                  pl.BlockSpec((None, mem_len, 2 * D_MODEL), lambda bi, i: (bi, 0, 0)),
                  full(D_MODEL, D_MODEL), full(1, D_MODEL), full(1, D_MODEL),
                  full(D_MODEL, N_EXPERTS), full(1, N_EXPERTS)],
        out_specs=[pl.BlockSpec((tm * LANE_BLOCKS, LANES), lambda bi, i: (bi * nt + i, 0)), row(LANES), row(LANES),
                   full(1, LANES)],
        out_shape=[jax.ShapeDtypeStruct((n * LANE_BLOCKS, LANES), F32),
                   jax.ShapeDtypeStruct((n, LANES), jnp.int32),
                   jax.ShapeDtypeStruct((n, LANES), F32),
                   jax.ShapeDtypeStruct((1, LANES), F32)],
        compiler_params=_params(2),
        name="token_block",
    )(out_a, out_b, x, g_a, g_b, w_out, ln1_g, ln1_b, w_q, kv, w_o, ln2_g, ln2_b, w_r, b_r)


def _gather_rows(idx_ref, first, count, src_hbm, dst_ref, sem):
    for r in range(count):
        src = pl.ds(pl.multiple_of(idx_ref[first + r], LANE_BLOCKS), LANE_BLOCKS)
        pltpu.make_async_copy(src_hbm.at[src], dst_ref.at[pl.ds(r * LANE_BLOCKS, LANE_BLOCKS)], sem).start(
            priority=r % 2)


def _wait_rows(count, src_hbm, dst_ref, sem):
    pltpu.make_async_copy(src_hbm.at[pl.ds(0, count * LANE_BLOCKS)], dst_ref, sem).wait()


def _fill_row_table(dest_ref, gap_ref, tok_ref):
    unroll = 16

    def clear(c, carry):
        for u in range(unroll):
            tok_ref[c * unroll + u] = 0
        return carry

    def place(c, carry):
        rows = [dest_ref[c * unroll + u] for u in range(unroll)]
        for u in range(unroll):
            tok_ref[rows[u]] = (c * (unroll // TOP_K) + u // TOP_K) * LANE_BLOCKS
        return carry

    for g in range(gap_ref.shape[1]):
        lax.fori_loop(gap_ref[0, g] // unroll, gap_ref[1, g] // unroll, clear, 0)
    lax.fori_loop(0, dest_ref.shape[0] // unroll, place, 0)


def _moe_kernel(bexp_ref, nused_ref, next_ref, dest_ref, gap_ref, x_hbm, wgu_hbm, bgu_ref, wd_hbm, bd_ref,
                y_ref, xbuf, wgu_f32, wd_f32, wgu_bf, wd_bf, tok_ref, sem, wsem):
    i = pl.program_id(0)
    slot = i % MOE_SLOTS
    ahead = MOE_SLOTS - 1
    n_used = nused_ref[0]

    def weight_copies(expert):
        return (pltpu.make_async_copy(wgu_hbm.at[expert], wgu_f32, wsem.at[0]),
                pltpu.make_async_copy(wd_hbm.at[expert], wd_f32, wsem.at[1]))

    @pl.when(i == 0)
    def _():
        for copy in weight_copies(bexp_ref[0]):
            copy.start()
        _fill_row_table(dest_ref, gap_ref, tok_ref)
        for b in range(ahead):
            _gather_rows(tok_ref, b * TM_MOE, TM_MOE, x_hbm, xbuf.at[b], sem.at[b])

    @pl.when(i < n_used + ahead)
    def _():
        _wait_rows(TM_MOE, x_hbm, xbuf.at[slot], sem.at[slot])

    @pl.when((i < n_used) & ((i == 0) | (bexp_ref[i] != bexp_ref[jnp.maximum(i - 1, 0)])))
    def _():
        for copy in weight_copies(bexp_ref[i]):
            copy.wait()
        wgu_bf[...] = wgu_f32[...].astype(BF16)
        wd_bf[...] = wd_f32[...].astype(BF16)

        @pl.when(next_ref[i] >= 0)
        def _():
            for copy in weight_copies(next_ref[i]):
                copy.start()

    @pl.when(i < n_used)
    def _():
        x = _load_token_major(xbuf.at[slot], 0, TM_MOE).astype(BF16)
        nxt = (i + ahead) % MOE_SLOTS
        _gather_rows(tok_ref, (i + ahead) * TM_MOE, TM_MOE, x_hbm, xbuf.at[nxt], sem.at[nxt])
        gu = jnp.dot(x, wgu_bf[...], preferred_element_type=F32) + bgu_ref[0]
        gate = jnp.minimum(gu[:, :D_FF], SWIGLU_LIMIT)
        up = jnp.clip(gu[:, D_FF:], -SWIGLU_LIMIT, SWIGLU_LIMIT)
        glu = gate * jax.nn.sigmoid(gate * SWIGLU_ALPHA)
        hid = ((up + 1.0) * glu).astype(BF16)
        _store_token_major(y_ref, jnp.dot(hid, wd_bf[...], preferred_element_type=F32) + bd_ref[0])

    @pl.when(i >= n_used)
    def _():
        y_ref[...] = jnp.zeros_like(y_ref)


def _moe_experts(block_expert, n_used, next_expert, dest, gaps, x2, w_gu, b_gu, w_d, b_d):
    nb = block_expert.shape[0]
    grid_spec = pltpu.PrefetchScalarGridSpec(
        num_scalar_prefetch=5,
        grid=(nb,),
        in_specs=[pl.BlockSpec(memory_space=pl.ANY),
                  pl.BlockSpec(memory_space=pl.ANY),
                  pl.BlockSpec((1, 1, 2 * D_FF), lambda i, be, nu, ne, de, ga: (be[i], 0, 0)),
                  pl.BlockSpec(memory_space=pl.ANY),
                  pl.BlockSpec((1, 1, D_MODEL), lambda i, be, nu, ne, de, ga: (be[i], 0, 0))],
        out_specs=pl.BlockSpec((TM_MOE * LANE_BLOCKS, LANES), lambda i, be, nu, ne, de, ga: (i, 0)),
        scratch_shapes=[pltpu.VMEM((MOE_SLOTS, TM_MOE * LANE_BLOCKS, LANES), F32),
                        pltpu.VMEM((D_MODEL, 2 * D_FF), F32), pltpu.VMEM((D_FF, D_MODEL), F32),
                        pltpu.VMEM((D_MODEL, 2 * D_FF), BF16), pltpu.VMEM((D_FF, D_MODEL), BF16),
                        pltpu.SMEM((nb * TM_MOE,), jnp.int32),
                        pltpu.SemaphoreType.DMA((MOE_SLOTS,)), pltpu.SemaphoreType.DMA((2,))],
    )
    return pl.pallas_call(
        _moe_kernel,
        grid_spec=grid_spec,
        out_shape=jax.ShapeDtypeStruct((nb * TM_MOE * LANE_BLOCKS, LANES), F32),
        compiler_params=pltpu.CompilerParams(dimension_semantics=("arbitrary",),
                                             vmem_limit_bytes=VMEM_LIMIT_MOE),
        name="moe_experts",
    )(block_expert, n_used, next_expert, dest, gaps, x2, w_gu, b_gu, w_d, b_d)


def _combine_kernel(idx_ref, y_hbm, x2_ref, gate_ref, lng_ref, lnb_ref, o_ref, *scratch):
    ybufs, sem = scratch[:COMB_SLOTS], scratch[COMB_SLOTS]
    i = pl.program_id(0)
    nt = pl.num_programs(0)
    ahead = COMB_SLOTS - 1
    rows = TOP_K * TM_COMB

    @pl.when(i == 0)
    def _():
        for t in range(ahead):
            _gather_rows(idx_ref, t * rows, rows, y_hbm, ybufs[t], sem.at[t])

    def compute(ybuf):
        g = gate_ref[...]
        ff = jnp.zeros((TM_COMB, D_MODEL), F32)
        for kk in range(TOP_K):
            ff = ff + g[:, kk:kk + 1] * _load_token_major(ybuf, kk * TM_COMB, TM_COMB)
        x2 = _load_token_major(x2_ref, 0, TM_COMB)
        o_ref[...] = _layer_norm(DEEPNORM_ALPHA * x2 + ff, lng_ref[...], lnb_ref[...])

    for k in range(COMB_SLOTS):
        nxt = (k + ahead) % COMB_SLOTS

        @pl.when((i % COMB_SLOTS == k) & (i + ahead < nt))
        def _():
            _wait_rows(rows, y_hbm, ybufs[k], sem.at[k])
            _gather_rows(idx_ref, (i + ahead) * rows, rows, y_hbm, ybufs[nxt], sem.at[nxt])
            compute(ybufs[k])

        @pl.when((i % COMB_SLOTS == k) & (i + ahead >= nt))
        def _():
            _wait_rows(rows, y_hbm, ybufs[k], sem.at[k])
            compute(ybufs[k])


def _combine(dest, y_rows, x2, gates, ln_g, ln_b):
    n = x2.shape[0] // LANE_BLOCKS
    nt = n // TM_COMB
    rows = TOP_K * TM_COMB
    idx = (dest * LANE_BLOCKS).reshape(nt, TM_COMB, TOP_K).transpose(0, 2, 1).reshape(nt * rows)
    row = lambda width: pl.BlockSpec((TM_COMB, width), lambda i, idx: (i, 0))
    full = lambda r, c: pl.BlockSpec((r, c), lambda i, idx: (0, 0))
    grid_spec = pltpu.PrefetchScalarGridSpec(
        num_scalar_prefetch=1,
        grid=(nt,),
        in_specs=[pl.BlockSpec(memory_space=pl.ANY),
                  pl.BlockSpec((TM_COMB * LANE_BLOCKS, LANES), lambda i, idx: (i, 0)), row(LANES),
                  full(1, D_MODEL), full(1, D_MODEL)],
        out_specs=row(D_MODEL),
        scratch_shapes=[pltpu.VMEM((rows * LANE_BLOCKS, LANES), F32)] * COMB_SLOTS
        + [pltpu.SemaphoreType.DMA((COMB_SLOTS,))],
    )
    return pl.pallas_call(
        _combine_kernel,
        grid_spec=grid_spec,
        out_shape=jax.ShapeDtypeStruct((n, D_MODEL), F32),
        compiler_params=_params(1),
        name="combine",
    )(idx, y_rows, x2, gates, ln_g, ln_b)


def _rank_kernel(idx_ref, counts_ref, dest_ref, running):
    j = pl.program_id(0)
    t = TM_RANK
    idx = idx_ref[...]
    e_iota = lax.broadcasted_iota(jnp.int32, (t, LANES), 1)
    hot = [idx[:, k:k + 1] == e_iota for k in range(TOP_K)]
    chosen = sum(jnp.where(h, 1.0, 0.0) for h in hot)

    @pl.when(j == 0)
    def _():
        counts = counts_ref[...]
        padded = jnp.ceil(counts * (1.0 / TM_MOE)) * TM_MOE
        lane = lax.broadcasted_iota(jnp.int32, (1, LANES), 1)
        scan = padded
        shift = 1
        while shift < N_EXPERTS:
            scan = scan + jnp.where(lane >= shift, pltpu.roll(scan, shift, 1), 0.0)
            shift *= 2
        running[...] = scan - padded

    r_iota = lax.broadcasted_iota(jnp.int32, (t, t), 0)
    c_iota = lax.broadcasted_iota(jnp.int32, (t, t), 1)
    earlier = jnp.where(c_iota < r_iota, 1.0, 0.0).astype(BF16)
    base = jnp.dot(earlier, chosen.astype(BF16), preferred_element_type=F32) + running[...]
    out = jnp.zeros((t, LANES), jnp.int32)
    for k in range(TOP_K):
        row = jnp.sum(jnp.where(hot[k], base, 0.0), axis=-1, keepdims=True).astype(jnp.int32)
        out = jnp.where(e_iota == k, row, out)
    dest_ref[...] = out
    running[...] += jnp.sum(chosen, axis=0, keepdims=True)


def _dispatch_plan(top_idx, counts, n):
    n_assign = n * TOP_K
    nb = n_assign // TM_MOE + N_EXPERTS + MOE_SLOTS - 2
    dest = pl.pallas_call(
        _rank_kernel,
        grid=(n // TM_RANK,),
        in_specs=[pl.BlockSpec((TM_RANK, LANES), lambda j: (j, 0)), pl.BlockSpec((1, LANES), lambda j: (0, 0))],
        out_specs=pl.BlockSpec((TM_RANK, LANES), lambda j: (j, 0)),
        out_shape=jax.ShapeDtypeStruct((n, LANES), jnp.int32),
        scratch_shapes=[pltpu.VMEM((1, LANES), F32)],
        compiler_params=_params(1),
        name="expert_rank",
    )(top_idx, counts)
    dest = dest[:, :TOP_K].reshape(-1)
    counts = counts[0, :N_EXPERTS].astype(jnp.int32)
    padded = (counts + TM_MOE - 1) // TM_MOE * TM_MOE
    pad_end = jnp.cumsum(padded)
    tail = jnp.stack([pad_end[-1], jnp.int32(nb * TM_MOE)])[:, None]
    gaps = jnp.concatenate([jnp.stack([pad_end - padded + counts, pad_end]), tail], axis=1)
    block_first_row = jnp.arange(nb, dtype=jnp.int32) * TM_MOE
    block_expert = jnp.minimum(jnp.sum(pad_end[None, :] <= block_first_row[:, None], axis=1),
                               N_EXPERTS - 1).astype(jnp.int32)
    n_used = (pad_end[-1:] // TM_MOE).astype(jnp.int32)
    block = jnp.arange(nb, dtype=jnp.int32)
    starts = (block < n_used[0]) & (block > 0) & (block_expert != jnp.roll(block_expert, 1))
    later_start = jnp.min(jnp.where(starts[None, :] & (block[None, :] > block[:, None]), block[None, :], nb), axis=1)
    next_expert = jnp.where(later_start < nb, block_expert[jnp.minimum(later_start, nb - 1)], -1).astype(jnp.int32)
    return dest, gaps, block_expert, n_used, next_expert


def kernel(x, mem, w_in, rel_bias, g_group_a, g_group_b, w_out, w_q_mem, w_kv_mem, w_o_mem, w_router, b_router, w_gate_up, b_gate_up, w_down, b_down, ln_g, ln_b):
    b, s, d = x.shape
    n = b * s
    depth = w_in.shape[0]
    xf = x.reshape(n, d)
    for l in range(depth):
        q_scale = jnp.ones((3 * MIX_WIDTH,), F32).at[:WIDTH_A].set(HEAD_DIM ** -0.5)
        q_scale = q_scale.at[3 * WIDTH_A:3 * WIDTH_A + WIDTH_B].set(HEAD_DIM ** -0.5)
        proj = _matmul(xf, (w_in[l] * q_scale).astype(BF16), BF16, TM_TOK, 3 * MIX_WIDTH).reshape(b, s, 3 * MIX_WIDTH)
        out_a = _chunk_attention(proj, _chunk_bias(rel_bias[l]), b, s).reshape(n, WIDTH_A)
        out_b = _sb_attention(proj, b, s).reshape(n, WIDTH_B)
        mem_len = mem.shape[1]
        kv = _matmul(mem.reshape(b * mem_len, d), w_kv_mem[l].astype(BF16), BF16, b * mem_len, 1024)
        x2, top_idx, gates, counts = _token_block(
            out_a, out_b, xf, g_group_a[l][None], g_group_b[l][None], w_out[l].astype(BF16),
            ln_g[l, 0][None], ln_b[l, 0][None], w_q_mem[l].astype(BF16), kv.reshape(b, mem_len, 2 * d),
            w_o_mem[l].astype(BF16), ln_g[l, 1][None], ln_b[l, 1][None],
            w_router[l].astype(BF16), b_router[l][None], b, s)
        dest, gaps, block_expert, n_used, next_expert = _dispatch_plan(top_idx, counts, n)
        y_rows = _moe_experts(block_expert, n_used, next_expert, dest, gaps, x2,
                              w_gate_up[l], b_gate_up[l][:, None, :], w_down[l], b_down[l][:, None, :])
        xf = _combine(dest, y_rows, x2, gates, ln_g[l, 2][None], ln_b[l, 2][None])
    return xf.reshape(b, s, d)
```

```python
import jax
import jax.numpy as jnp
from jax import lax
from jax.experimental import pallas as pl
from jax.experimental.pallas import tpu as pltpu

D_MODEL = 1024
CHUNK = 64
LEFT_CHUNKS = 8
LEFT = LEFT_CHUNKS * CHUNK
HEAD_DIM = 64
N_HEADS_A = 8
N_HEADS_B = 8
WIDTH_A = N_HEADS_A * HEAD_DIM
WIDTH_B = N_HEADS_B * HEAD_DIM
MIX_WIDTH = WIDTH_A + WIDTH_B
REL_CLIP = 128
N_HEADS_MEM = 4
HEAD_DIM_MEM = D_MODEL // N_HEADS_MEM
N_EXPERTS = 32
TOP_K = 4
D_FF = D_MODEL
SWIGLU_LIMIT = 7.0
SWIGLU_ALPHA = 1.702
LN_EPS = 1e-5
RMS_EPS = 1e-6
DEEPNORM_ALPHA = 2.0 ** 0.25
NEG_INF = -1e30
LOG2E = 1.4426950408889634

LANES = 128
LANE_BLOCKS = D_MODEL // LANES
VMEM_LIMIT = 48 * 1024 * 1024
VMEM_LIMIT_MOE = 56 * 1024 * 1024

TQ_A = 512
SUB_A = 256
HEADS_A_STEP = 4
TB_SB = 256
SB_TILES = 4
SB_UNDERFLOW = 110.0
TM_TOK = 512
TM_MOE = 256
MOE_SLOTS = 4
TM_COMB = 256
COMB_SLOTS = 3
TM_RANK = 512

F32 = jnp.float32
BF16 = jnp.bfloat16
_NT = (((1,), (1,)), ((), ()))


def _params(n_axes):
    return pltpu.CompilerParams(dimension_semantics=("arbitrary",) * n_axes,
                                vmem_limit_bytes=VMEM_LIMIT)


def _store_token_major(ref, value):
    rows = value.shape[0]
    for c in range(LANE_BLOCKS):
        ref[pl.ds(c, rows, stride=LANE_BLOCKS), :] = value[:, c * LANES:(c + 1) * LANES]


def _load_token_major(ref, first_row, rows):
    return jnp.concatenate(
        [ref[pl.ds(first_row * LANE_BLOCKS + c, rows, stride=LANE_BLOCKS), :] for c in range(LANE_BLOCKS)],
        axis=1)


def _layer_norm(r, g, b):
    mu = jnp.mean(r, axis=-1, keepdims=True)
    d = r - mu
    var = jnp.mean(d * d, axis=-1, keepdims=True)
    return d * lax.rsqrt(var + LN_EPS) * g + b


def _matmul_kernel(x_ref, w_ref, o_ref):
    o_ref[...] = jnp.dot(x_ref[...].astype(BF16), w_ref[...],
                         preferred_element_type=F32).astype(o_ref.dtype)


def _matmul(x, w, out_dtype, tm, tn):
    m, k = x.shape
    n = w.shape[1]
    return pl.pallas_call(
        _matmul_kernel,
        grid=(m // tm, n // tn),
        in_specs=[pl.BlockSpec((tm, k), lambda i, j: (i, 0)),
                  pl.BlockSpec((k, tn), lambda i, j: (0, j))],
        out_specs=pl.BlockSpec((tm, tn), lambda i, j: (i, j)),
        out_shape=jax.ShapeDtypeStruct((m, n), out_dtype),
        compiler_params=_params(2),
        name="matmul",
    )(x, w)


def _chunk_attn_kernel(q_ref, kp_ref, kc_ref, vp_ref, vc_ref, bias0_ref, bias1_ref, o_ref):
    w = SUB_A + LEFT
    for h in range(HEADS_A_STEP):
        sl = slice(h * HEAD_DIM, (h + 1) * HEAD_DIM)
        k = jnp.concatenate([kp_ref[:, sl], kc_ref[:, sl]], axis=0)
        v = jnp.concatenate([vp_ref[:, sl], vc_ref[:, sl]], axis=0)
        for sub, bias_ref in enumerate((bias0_ref, bias1_ref)):
            rows = slice(sub * SUB_A, (sub + 1) * SUB_A)
            keys = slice(sub * SUB_A, sub * SUB_A + w)
            s = lax.dot_general(q_ref[rows, sl], k[keys], _NT, preferred_element_type=F32) + bias_ref[h]
            m = jnp.max(s, axis=-1, keepdims=True)
            p = jnp.exp(s - m)
            l = jnp.sum(p, axis=-1, keepdims=True)
            o = jnp.dot(p.astype(BF16), v[keys], preferred_element_type=F32) / l
            o_ref[rows, sl] = o.astype(o_ref.dtype)


def _chunk_bias(rel_bias):
    w = SUB_A + LEFT
    heads = rel_bias.shape[0]
    m = jnp.arange(2 * w)
    rel = jnp.where(m < w, LEFT - m, LEFT + 2 * w - m)
    diag = rel_bias[:, jnp.clip(rel, -REL_CLIP, REL_CLIP) + REL_CLIP].astype(F32)[:, None, :]
    n_tables = 1 + TQ_A // SUB_A
    return pl.pallas_call(
        _chunk_bias_kernel,
        grid=(heads,),
        in_specs=[pl.BlockSpec((1, 1, 2 * w), lambda h: (h, 0, 0))],
        out_specs=pl.BlockSpec((n_tables, 1, SUB_A, w), lambda h: (0, h, 0, 0)),
        out_shape=jax.ShapeDtypeStruct((n_tables, heads, SUB_A, w), F32),
        compiler_params=_params(1),
        name="chunk_bias",
    )(diag)


def _chunk_bias_kernel(diag_ref, o_ref):
    w = SUB_A + LEFT
    rolled = pltpu.roll(jnp.broadcast_to(diag_ref[0], (SUB_A, 2 * w)), 0, 1, stride=1, stride_axis=0)
    col = lax.broadcasted_iota(jnp.int32, (SUB_A, w), 1)
    qc = lax.broadcasted_iota(jnp.int32, (SUB_A, w), 0) // CHUNK
    kc = col // CHUNK
    band = (kc >= qc) & (kc <= qc + LEFT_CHUNKS)
    table = jnp.where(band, rolled[:, :w], NEG_INF)
    o_ref[0, 0] = table
    for sub in range(TQ_A // SUB_A):
        o_ref[1 + sub, 0] = jnp.where(col >= LEFT - sub * SUB_A, table, NEG_INF)


def _chunk_attention(proj, bias, b, s):
    nq = s // TQ_A
    width = HEADS_A_STEP * HEAD_DIM
    groups = WIDTH_A // width
    blk = lambda off, prev: pl.BlockSpec(
        (None, TQ_A, width),
        (lambda bi, p, i: (bi, jnp.maximum(i - 1, 0), off + p)) if prev
        else (lambda bi, p, i: (bi, i, off + p)))
    table = lambda sub: pl.BlockSpec((None, HEADS_A_STEP, SUB_A, SUB_A + LEFT),
                                     lambda bi, p, i: (jnp.where(i == 0, 1 + sub, 0), p, 0, 0))
    return pl.pallas_call(
        _chunk_attn_kernel,
        grid=(b, groups, nq),
        in_specs=[blk(0, False),
                  blk(groups, True), blk(groups, False),
                  blk(2 * groups, True), blk(2 * groups, False),
                  table(0), table(1)],
        out_specs=pl.BlockSpec((None, TQ_A, width), lambda bi, p, i: (bi, i, p)),
        out_shape=jax.ShapeDtypeStruct((b, s, WIDTH_A), BF16),
        compiler_params=_params(3),
        name="chunk_attn",
    )(proj, proj, proj, proj, proj, bias, bias)


def _sb_block(q, k_ref, v_ref, sl, j, later, tri, causal):
    t = TB_SB
    start = pl.multiple_of(j * t, t)
    k = k_ref[pl.ds(start, t), sl]
    v = v_ref[pl.ds(start, t), sl]
    z = lax.dot_general(q, k, _NT, preferred_element_type=F32)
    sp = jnp.maximum(z, 0.0) + jnp.log(1.0 + jnp.exp2(jnp.abs(z) * (-LOG2E)))
    if causal is not None:
        sp = jnp.where(causal, sp, 0.0)
    csum = jnp.dot(sp.astype(BF16), tri, preferred_element_type=F32) + later
    wgt = jnp.exp(z - csum)
    if causal is not None:
        wgt = jnp.where(causal, wgt, 0.0)
    pv = jnp.dot(wgt.astype(BF16), v, preferred_element_type=F32)
    return pv, csum[:, 0:1]


def _sb_attn_kernel(q_ref, k_ref, v_ref, o_ref):
    first = pl.program_id(2) * SB_TILES
    t = TB_SB
    row = lax.broadcasted_iota(jnp.int32, (t, t), 0)
    col = lax.broadcasted_iota(jnp.int32, (t, t), 1)
    tri = jnp.where(row >= col, 1.0, 0.0).astype(BF16)
    causal = col < row
    chains = [(u, h) for u in range(SB_TILES) for h in range(2)]
    rows = lambda u: slice(u * t, (u + 1) * t)
    lanes = lambda h: slice(h * HEAD_DIM, (h + 1) * HEAD_DIM)
    qs = [q_ref[rows(u), lanes(h)] for u, h in chains]

    def block(c, j, later, masked):
        return _sb_block(qs[c], k_ref, v_ref, lanes(chains[c][1]), j, later, tri, causal if masked else None)

    carry = []
    for c, (u, h) in enumerate(chains):
        tile = first + u
        pv, later = block(c, tile, jnp.zeros((t, 1), F32), True)
        pv_prev, later_prev = block(c, jnp.maximum(tile - 1, 0), later, False)
        has_prev = tile >= 1
        carry += [jnp.where(has_prev, later_prev, later), pv + jnp.where(has_prev, pv_prev, 0.0)]

    def pending(c):
        todo = False
        for ci, (u, h) in enumerate(chains):
            todo = todo | ((first + u - 2 - c[0] >= 0) & (jnp.min(c[1 + 2 * ci]) < SB_UNDERFLOW))
        return todo

    def step(c):
        out = [c[0] + 1]
        for ci, (u, h) in enumerate(chains):
            j = first + u - 2 - c[0]
            pv, later = block(ci, jnp.maximum(j, 0), c[1 + 2 * ci], False)
            out += [jnp.where(j >= 0, later, c[1 + 2 * ci]), c[2 + 2 * ci] + jnp.where(j >= 0, pv, 0.0)]
        return tuple(out)

    carry = lax.while_loop(pending, step, (jnp.int32(0), *carry))
    for ci, (u, h) in enumerate(chains):
        o_ref[rows(u), lanes(h)] = carry[2 + 2 * ci].astype(o_ref.dtype)


def _sb_attention(proj, b, s):
    tq = SB_TILES * TB_SB
    nq = s // tq
    pairs = WIDTH_B // LANES
    base = 3 * WIDTH_A // LANES
    return pl.pallas_call(
        _sb_attn_kernel,
        grid=(b, pairs, nq),
        in_specs=[pl.BlockSpec((None, tq, LANES), lambda bi, p, i: (bi, i, base + p)),
                  pl.BlockSpec((None, s, LANES), lambda bi, p, i: (bi, 0, base + pairs + p)),
                  pl.BlockSpec((None, s, LANES), lambda bi, p, i: (bi, 0, base + 2 * pairs + p))],
        out_specs=pl.BlockSpec((None, tq, LANES), lambda bi, p, i: (bi, i, p)),
        out_shape=jax.ShapeDtypeStruct((b, s, WIDTH_B), BF16),
        compiler_params=_params(3),
        name="sb_attn",
    )(proj, proj, proj)


def _token_block_kernel(oa_ref, ob_ref, x_ref, ga_ref, gb_ref, wout_ref, ln1g_ref, ln1b_ref, wq_ref,
                        kv_ref, wo_ref, ln2g_ref, ln2b_ref, wr_ref, br_ref,
                        x2_ref, idx_ref, gate_ref, counts_ref):
    def rms(ref, g_ref):
        a = ref[...].astype(F32)
        return (a * lax.rsqrt(jnp.mean(a * a, axis=-1, keepdims=True) + RMS_EPS) * g_ref[...]).astype(BF16)

    y = jnp.dot(rms(oa_ref, ga_ref), wout_ref[:WIDTH_A, :], preferred_element_type=F32)
    y = y + jnp.dot(rms(ob_ref, gb_ref), wout_ref[WIDTH_A:, :], preferred_element_type=F32)
    x1 = _layer_norm(DEEPNORM_ALPHA * x_ref[...] + y, ln1g_ref[...], ln1b_ref[...])
    qm = jnp.dot(x1.astype(BF16), wq_ref[...], preferred_element_type=F32).astype(BF16)

    heads = []
    for h in range(N_HEADS_MEM):
        sl = slice(h * HEAD_DIM_MEM, (h + 1) * HEAD_DIM_MEM)
        q = qm[:, sl] * (HEAD_DIM_MEM ** -0.5)
        k = kv_ref[:, sl]
        v = kv_ref[:, D_MODEL + h * HEAD_DIM_MEM:D_MODEL + (h + 1) * HEAD_DIM_MEM]
        s = lax.dot_general(q, k, _NT, preferred_element_type=F32)
        m = jnp.max(s, axis=-1, keepdims=True)
        p = jnp.exp(s - m)
        l = jnp.sum(p, axis=-1, keepdims=True)
        heads.append((jnp.dot(p.astype(BF16), v, preferred_element_type=F32) / l).astype(BF16))
    o = jnp.concatenate(heads, axis=-1)
    y = jnp.dot(o, wo_ref[...], preferred_element_type=F32)
    x2 = _layer_norm(DEEPNORM_ALPHA * x1 + y, ln2g_ref[...], ln2b_ref[...])
    _store_token_major(x2_ref, x2)

    logits = jnp.dot(x2.astype(BF16), wr_ref[...], preferred_element_type=F32) + br_ref[...]
    tm = logits.shape[0]
    e_iota = lax.broadcasted_iota(jnp.int32, (tm, N_EXPERTS), 1)
    lane = lax.broadcasted_iota(jnp.int32, (tm, LANES), 1)
    idx_out = jnp.zeros((tm, LANES), jnp.int32)
    val_out = jnp.zeros((tm, LANES), F32)
    top = None
    denom = jnp.zeros((tm, 1), F32)
    chosen = jnp.zeros((1, LANES), F32)
    for kk in range(TOP_K):
        m = jnp.max(logits, axis=-1, keepdims=True)
        sel = jnp.min(jnp.where(logits == m, e_iota, N_EXPERTS), axis=-1, keepdims=True)
        if top is None:
            top = m
        e = jnp.exp(m - top)
        denom = denom + e
        idx_out = jnp.where(lane == kk, sel, idx_out)
        val_out = jnp.where(lane == kk, e, val_out)
        chosen = chosen + jnp.sum(jnp.where(lane == sel, 1.0, 0.0), axis=0, keepdims=True)
        logits = jnp.where(e_iota == sel, -jnp.inf, logits)
    idx_ref[...] = idx_out
    gate_ref[...] = val_out / denom

    @pl.when((pl.program_id(0) == 0) & (pl.program_id(1) == 0))
    def _():
        counts_ref[...] = jnp.zeros_like(counts_ref)

    counts_ref[...] += chosen


def _token_block(out_a, out_b, x, g_a, g_b, w_out, ln1_g, ln1_b, w_q, kv, w_o, ln2_g, ln2_b, w_r, b_r, b, s):
    tm = TM_TOK
    nt = s // tm
    mem_len = kv.shape[1]
    n = b * s
    row = lambda width: pl.BlockSpec((tm, width), lambda bi, i: (bi * nt + i, 0))
    full = lambda r, c: pl.BlockSpec((r, c), lambda bi, i: (0, 0))
    return pl.pallas_call(
        _token_block_kernel,
        grid=(b, nt),
        in_specs=[row(WIDTH_A), row(WIDTH_B), row(D_MODEL), full(1, WIDTH_A), full(1, WIDTH_B),
                  full(MIX_WIDTH, D_MODEL), full(1, D_MODEL), full(1, D_MODEL), full(D_MODEL, D_MODEL),
                  pl.BlockSpec((None, mem_len, 2 * D_MODEL), lambda bi, i: (bi, 0, 0)),
                  full(D_MODEL, D_MODEL), full(1, D_MODEL), full(1, D_MODEL),
                  full(D_MODEL, N_EXPERTS), full(1, N_EXPERTS)],
        out_specs=[pl.BlockSpec((tm * LANE_BLOCKS, LANES), lambda bi, i: (bi * nt + i, 0)), row(LANES), row(LANES),
                   full(1, LANES)],
        out_shape=[jax.ShapeDtypeStruct((n * LANE_BLOCKS, LANES), F32),
                   jax.ShapeDtypeStruct((n, LANES), jnp.int32),
                   jax.ShapeDtypeStruct((n, LANES), F32),
                   jax.ShapeDtypeStruct((1, LANES), F32)],
        compiler_params=_params(2),
        name="token_block",
    )(out_a, out_b, x, g_a, g_b, w_out, ln1_g, ln1_b, w_q, kv, w_o, ln2_g, ln2_b, w_r, b_r)


def _gather_rows(idx_ref, first, count, src_hbm, dst_ref, sem):
    for r in range(count):
        src = pl.ds(pl.multiple_of(idx_ref[first + r], LANE_BLOCKS), LANE_BLOCKS)
        pltpu.make_async_copy(src_hbm.at[src], dst_ref.at[pl.ds(r * LANE_BLOCKS, LANE_BLOCKS)], sem).start(
            priority=r % 2)


def _wait_rows(count, src_hbm, dst_ref, sem):
    pltpu.make_async_copy(src_hbm.at[pl.ds(0, count * LANE_BLOCKS)], dst_ref, sem).wait()


def _fill_row_table(dest_ref, gap_ref, tok_ref):
    unroll = 16

    def clear(c, carry):
        for u in range(unroll):
            tok_ref[c * unroll + u] = 0
        return carry

    def place(c, carry):
        rows = [dest_ref[c * unroll + u] for u in range(unroll)]
        for u in range(unroll):
            tok_ref[rows[u]] = (c * (unroll // TOP_K) + u // TOP_K) * LANE_BLOCKS
        return carry

    for g in range(gap_ref.shape[1]):
        lax.fori_loop(gap_ref[0, g] // unroll, gap_ref[1, g] // unroll, clear, 0)
    lax.fori_loop(0, dest_ref.shape[0] // unroll, place, 0)


def _moe_kernel(bexp_ref, nused_ref, next_ref, dest_ref, gap_ref, x_hbm, wgu_hbm, bgu_ref, wd_hbm, bd_ref,
                y_ref, xbuf, wgu_f32, wd_f32, wgu_bf, wd_bf, tok_ref, sem, wsem):
    i = pl.program_id(0)
    slot = i % MOE_SLOTS
    ahead = MOE_SLOTS - 1
    n_used = nused_ref[0]

    def weight_copies(expert):
        return (pltpu.make_async_copy(wgu_hbm.at[expert], wgu_f32, wsem.at[0]),
                pltpu.make_async_copy(wd_hbm.at[expert], wd_f32, wsem.at[1]))

    @pl.when(i == 0)
    def _():
        for copy in weight_copies(bexp_ref[0]):
            copy.start()
        _fill_row_table(dest_ref, gap_ref, tok_ref)
        for b in range(ahead):
            _gather_rows(tok_ref, b * TM_MOE, TM_MOE, x_hbm, xbuf.at[b], sem.at[b])

    @pl.when(i < n_used + ahead)
    def _():
        _wait_rows(TM_MOE, x_hbm, xbuf.at[slot], sem.at[slot])

    @pl.when((i < n_used) & ((i == 0) | (bexp_ref[i] != bexp_ref[jnp.maximum(i - 1, 0)])))
    def _():
        for copy in weight_copies(bexp_ref[i]):
            copy.wait()
        wgu_bf[...] = wgu_f32[...].astype(BF16)
        wd_bf[...] = wd_f32[...].astype(BF16)

        @pl.when(next_ref[i] >= 0)
        def _():
            for copy in weight_copies(next_ref[i]):
                copy.start()

    @pl.when(i < n_used)
    def _():
        x = _load_token_major(xbuf.at[slot], 0, TM_MOE).astype(BF16)
        nxt = (i + ahead) % MOE_SLOTS
        _gather_rows(tok_ref, (i + ahead) * TM_MOE, TM_MOE, x_hbm, xbuf.at[nxt], sem.at[nxt])
        gu = jnp.dot(x, wgu_bf[...], preferred_element_type=F32) + bgu_ref[0]
        gate = jnp.minimum(gu[:, :D_FF], SWIGLU_LIMIT)
        up = jnp.clip(gu[:, D_FF:], -SWIGLU_LIMIT, SWIGLU_LIMIT)
        glu = gate * jax.nn.sigmoid(gate * SWIGLU_ALPHA)
        hid = ((up + 1.0) * glu).astype(BF16)
        _store_token_major(y_ref, jnp.dot(hid, wd_bf[...], preferred_element_type=F32) + bd_ref[0])

    @pl.when(i >= n_used)
    def _():
        y_ref[...] = jnp.zeros_like(y_ref)


def _moe_experts(block_expert, n_used, next_expert, dest, gaps, x2, w_gu, b_gu, w_d, b_d):
    nb = block_expert.shape[0]
    grid_spec = pltpu.PrefetchScalarGridSpec(
        num_scalar_prefetch=5,
        grid=(nb,),
        in_specs=[pl.BlockSpec(memory_space=pl.ANY),
                  pl.BlockSpec(memory_space=pl.ANY),
                  pl.BlockSpec((1, 1, 2 * D_FF), lambda i, be, nu, ne, de, ga: (be[i], 0, 0)),
                  pl.BlockSpec(memory_space=pl.ANY),
                  pl.BlockSpec((1, 1, D_MODEL), lambda i, be, nu, ne, de, ga: (be[i], 0, 0))],
        out_specs=pl.BlockSpec((TM_MOE * LANE_BLOCKS, LANES), lambda i, be, nu, ne, de, ga: (i, 0)),
        scratch_shapes=[pltpu.VMEM((MOE_SLOTS, TM_MOE * LANE_BLOCKS, LANES), F32),
                        pltpu.VMEM((D_MODEL, 2 * D_FF), F32), pltpu.VMEM((D_FF, D_MODEL), F32),
                        pltpu.VMEM((D_MODEL, 2 * D_FF), BF16), pltpu.VMEM((D_FF, D_MODEL), BF16),
                        pltpu.SMEM((nb * TM_MOE,), jnp.int32),
                        pltpu.SemaphoreType.DMA((MOE_SLOTS,)), pltpu.SemaphoreType.DMA((2,))],
    )
    return pl.pallas_call(
        _moe_kernel,
        grid_spec=grid_spec,
        out_shape=jax.ShapeDtypeStruct((nb * TM_MOE * LANE_BLOCKS, LANES), F32),
        compiler_params=pltpu.CompilerParams(dimension_semantics=("arbitrary",),
                                             vmem_limit_bytes=VMEM_LIMIT_MOE),
        name="moe_experts",
    )(block_expert, n_used, next_expert, dest, gaps, x2, w_gu, b_gu, w_d, b_d)


def _combine_kernel(idx_ref, y_hbm, x2_ref, gate_ref, lng_ref, lnb_ref, o_ref, *scratch):
    ybufs, sem = scratch[:COMB_SLOTS], scratch[COMB_SLOTS]
    i = pl.program_id(0)
    nt = pl.num_programs(0)
    ahead = COMB_SLOTS - 1
    rows = TOP_K * TM_COMB

    @pl.when(i == 0)
    def _():
        for t in range(ahead):
            _gather_rows(idx_ref, t * rows, rows, y_hbm, ybufs[t], sem.at[t])

    def compute(ybuf):
        g = gate_ref[...]
        ff = jnp.zeros((TM_COMB, D_MODEL), F32)
        for kk in range(TOP_K):
            ff = ff + g[:, kk:kk + 1] * _load_token_major(ybuf, kk * TM_COMB, TM_COMB)
        x2 = _load_token_major(x2_ref, 0, TM_COMB)
        o_ref[...] = _layer_norm(DEEPNORM_ALPHA * x2 + ff, lng_ref[...], lnb_ref[...])

    for k in range(COMB_SLOTS):
        nxt = (k + ahead) % COMB_SLOTS

        @pl.when((i % COMB_SLOTS == k) & (i + ahead < nt))
        def _():
            _wait_rows(rows, y_hbm, ybufs[k], sem.at[k])
            _gather_rows(idx_ref, (i + ahead) * rows, rows, y_hbm, ybufs[nxt], sem.at[nxt])
            compute(ybufs[k])

        @pl.when((i % COMB_SLOTS == k) & (i + ahead >= nt))
        def _():
            _wait_rows(rows, y_hbm, ybufs[k], sem.at[k])
            compute(ybufs[k])


def _combine(dest, y_rows, x2, gates, ln_g, ln_b):
    n = x2.shape[0] // LANE_BLOCKS
    nt = n // TM_COMB
    rows = TOP_K * TM_COMB
    idx = (dest * LANE_BLOCKS).reshape(nt, TM_COMB, TOP_K).transpose(0, 2, 1).reshape(nt * rows)
    row = lambda width: pl.BlockSpec((TM_COMB, width), lambda i, idx: (i, 0))
    full = lambda r, c: pl.BlockSpec((r, c), lambda i, idx: (0, 0))
    grid_spec = pltpu.PrefetchScalarGridSpec(
        num_scalar_prefetch=1,
        grid=(nt,),
        in_specs=[pl.BlockSpec(memory_space=pl.ANY),
                  pl.BlockSpec((TM_COMB * LANE_BLOCKS, LANES), lambda i, idx: (i, 0)), row(LANES),
                  full(1, D_MODEL), full(1, D_MODEL)],
        out_specs=row(D_MODEL),
        scratch_shapes=[pltpu.VMEM((rows * LANE_BLOCKS, LANES), F32)] * COMB_SLOTS
        + [pltpu.SemaphoreType.DMA((COMB_SLOTS,))],
    )
    return pl.pallas_call(
        _combine_kernel,
        grid_spec=grid_spec,
        out_shape=jax.ShapeDtypeStruct((n, D_MODEL), F32),
        compiler_params=_params(1),
        name="combine",
    )(idx, y_rows, x2, gates, ln_g, ln_b)


def _rank_kernel(idx_ref, counts_ref, dest_ref, running):
    j = pl.program_id(0)
    t = TM_RANK
    idx = idx_ref[...]
    e_iota = lax.broadcasted_iota(jnp.int32, (t, LANES), 1)
    hot = [idx[:, k:k + 1] == e_iota for k in range(TOP_K)]
    chosen = sum(jnp.where(h, 1.0, 0.0) for h in hot)

    @pl.when(j == 0)
    def _():
        counts = counts_ref[...]
        padded = jnp.ceil(counts * (1.0 / TM_MOE)) * TM_MOE
        lane = lax.broadcasted_iota(jnp.int32, (1, LANES), 1)
        scan = padded
        shift = 1
        while shift < N_EXPERTS:
            scan = scan + jnp.where(lane >= shift, pltpu.roll(scan, shift, 1), 0.0)
            shift *= 2
        running[...] = scan - padded

    r_iota = lax.broadcasted_iota(jnp.int32, (t, t), 0)
    c_iota = lax.broadcasted_iota(jnp.int32, (t, t), 1)
    earlier = jnp.where(c_iota < r_iota, 1.0, 0.0).astype(BF16)
    base = jnp.dot(earlier, chosen.astype(BF16), preferred_element_type=F32) + running[...]
    out = jnp.zeros((t, LANES), jnp.int32)
    for k in range(TOP_K):
        row = jnp.sum(jnp.where(hot[k], base, 0.0), axis=-1, keepdims=True).astype(jnp.int32)
        out = jnp.where(e_iota == k, row, out)
    dest_ref[...] = out
    running[...] += jnp.sum(chosen, axis=0, keepdims=True)


def _dispatch_plan(top_idx, counts, n):
    n_assign = n * TOP_K
    nb = n_assign // TM_MOE + N_EXPERTS + MOE_SLOTS - 2
    dest = pl.pallas_call(
        _rank_kernel,
        grid=(n // TM_RANK,),
        in_specs=[pl.BlockSpec((TM_RANK, LANES), lambda j: (j, 0)), pl.BlockSpec((1, LANES), lambda j: (0, 0))],
        out_specs=pl.BlockSpec((TM_RANK, LANES), lambda j: (j, 0)),
        out_shape=jax.ShapeDtypeStruct((n, LANES), jnp.int32),
        scratch_shapes=[pltpu.VMEM((1, LANES), F32)],
        compiler_params=_params(1),
        name="expert_rank",
    )(top_idx, counts)
    dest = dest[:, :TOP_K].reshape(-1)
    counts = counts[0, :N_EXPERTS].astype(jnp.int32)
    padded = (counts + TM_MOE - 1) // TM_MOE * TM_MOE
    pad_end = jnp.cumsum(padded)
    tail = jnp.stack([pad_end[-1], jnp.int32(nb * TM_MOE)])[:, None]
    gaps = jnp.concatenate([jnp.stack([pad_end - padded + counts, pad_end]), tail], axis=1)
    block_first_row = jnp.arange(nb, dtype=jnp.int32) * TM_MOE
    block_expert = jnp.minimum(jnp.sum(pad_end[None, :] <= block_first_row[:, None], axis=1),
                               N_EXPERTS - 1).astype(jnp.int32)
    n_used = (pad_end[-1:] // TM_MOE).astype(jnp.int32)
    block = jnp.arange(nb, dtype=jnp.int32)
    starts = (block < n_used[0]) & (block > 0) & (block_expert != jnp.roll(block_expert, 1))
    later_start = jnp.min(jnp.where(starts[None, :] & (block[None, :] > block[:, None]), block[None, :], nb), axis=1)
    next_expert = jnp.where(later_start < nb, block_expert[jnp.minimum(later_start, nb - 1)], -1).astype(jnp.int32)
    return dest, gaps, block_expert, n_used, next_expert


def kernel(x, mem, w_in, rel_bias, g_group_a, g_group_b, w_out, w_q_mem, w_kv_mem, w_o_mem, w_router, b_router, w_gate_up, b_gate_up, w_down, b_down, ln_g, ln_b):
    b, s, d = x.shape
    n = b * s
    depth = w_in.shape[0]
    xf = x.reshape(n, d)
    for l in range(depth):
        q_scale = jnp.ones((3 * MIX_WIDTH,), F32).at[:WIDTH_A].set(HEAD_DIM ** -0.5)
        q_scale = q_scale.at[3 * WIDTH_A:3 * WIDTH_A + WIDTH_B].set(HEAD_DIM ** -0.5)
        proj = _matmul(xf, (w_in[l] * q_scale).astype(BF16), BF16, TM_TOK, 3 * MIX_WIDTH).reshape(b, s, 3 * MIX_WIDTH)
        out_a = _chunk_attention(proj, _chunk_bias(rel_bias[l]), b, s).reshape(n, WIDTH_A)
        out_b = _sb_attention(proj, b, s).reshape(n, WIDTH_B)
        mem_len = mem.shape[1]
        kv = _matmul(mem.reshape(b * mem_len, d), w_kv_mem[l].astype(BF16), BF16, b * mem_len, 1024)
        x2, top_idx, gates, counts = _token_block(
            out_a, out_b, xf, g_group_a[l][None], g_group_b[l][None], w_out[l].astype(BF16),
            ln_g[l, 0][None], ln_b[l, 0][None], w_q_mem[l].astype(BF16), kv.reshape(b, mem_len, 2 * d),
            w_o_mem[l].astype(BF16), ln_g[l, 1][None], ln_b[l, 1][None],
            w_router[l].astype(BF16), b_router[l][None], b, s)
        dest, gaps, block_expert, n_used, next_expert = _dispatch_plan(top_idx, counts, n)
        y_rows = _moe_experts(block_expert, n_used, next_expert, dest, gaps, x2,
                              w_gate_up[l], b_gate_up[l][:, None, :], w_down[l], b_down[l][:, None, :])
        xf = _combine(dest, y_rows, x2, gates, ln_g[l, 2][None], ln_b[l, 2][None])
    return xf.reshape(b, s, d)
```

```python
import jax
import jax.numpy as jnp
from jax import lax
from jax.experimental import pallas as pl
from jax.experimental.pallas import tpu as pltpu

D_MODEL = 1024
CHUNK = 64
LEFT_CHUNKS = 8
LEFT = LEFT_CHUNKS * CHUNK
HEAD_DIM = 64
N_HEADS_A = 8
N_HEADS_B = 8
WIDTH_A = N_HEADS_A * HEAD_DIM
WIDTH_B = N_HEADS_B * HEAD_DIM
MIX_WIDTH = WIDTH_A + WIDTH_B
REL_CLIP = 128
N_HEADS_MEM = 4
HEAD_DIM_MEM = D_MODEL // N_HEADS_MEM
N_EXPERTS = 32
TOP_K = 4
D_FF = D_MODEL
SWIGLU_LIMIT = 7.0
SWIGLU_ALPHA = 1.702
LN_EPS = 1e-5
RMS_EPS = 1e-6
DEEPNORM_ALPHA = 2.0 ** 0.25
NEG_INF = -1e30
LOG2E = 1.4426950408889634

LANES = 128
LANE_BLOCKS = D_MODEL // LANES
VMEM_LIMIT = 48 * 1024 * 1024
VMEM_LIMIT_MOE = 56 * 1024 * 1024

TQ_A = 512
SUB_A = 256
HEADS_A_STEP = 4
TB_SB = 256
SB_TILES = 4
SB_UNDERFLOW = 110.0
TM_TOK = 512
TM_MOE = 256
MOE_SLOTS = 5
TM_COMB = 256
COMB_SLOTS = 3
TM_RANK = 512

F32 = jnp.float32
BF16 = jnp.bfloat16
_NT = (((1,), (1,)), ((), ()))


def _params(n_axes):
    return pltpu.CompilerParams(dimension_semantics=("arbitrary",) * n_axes,
                                vmem_limit_bytes=VMEM_LIMIT)


def _store_token_major(ref, value):
    rows = value.shape[0]
    for c in range(LANE_BLOCKS):
        ref[pl.ds(c, rows, stride=LANE_BLOCKS), :] = value[:, c * LANES:(c + 1) * LANES]


def _load_token_major(ref, first_row, rows):
    return jnp.concatenate(
        [ref[pl.ds(first_row * LANE_BLOCKS + c, rows, stride=LANE_BLOCKS), :] for c in range(LANE_BLOCKS)],
        axis=1)


def _layer_norm(r, g, b):
    mu = jnp.mean(r, axis=-1, keepdims=True)
    d = r - mu
    var = jnp.mean(d * d, axis=-1, keepdims=True)
    return d * lax.rsqrt(var + LN_EPS) * g + b


def _matmul_kernel(x_ref, w_ref, o_ref):
    o_ref[...] = jnp.dot(x_ref[...].astype(BF16), w_ref[...],
                         preferred_element_type=F32).astype(o_ref.dtype)


def _matmul(x, w, out_dtype, tm, tn):
    m, k = x.shape
    n = w.shape[1]
    return pl.pallas_call(
        _matmul_kernel,
        grid=(m // tm, n // tn),
        in_specs=[pl.BlockSpec((tm, k), lambda i, j: (i, 0)),
                  pl.BlockSpec((k, tn), lambda i, j: (0, j))],
        out_specs=pl.BlockSpec((tm, tn), lambda i, j: (i, j)),
        out_shape=jax.ShapeDtypeStruct((m, n), out_dtype),
        compiler_params=_params(2),
        name="matmul",
    )(x, w)


def _chunk_attn_kernel(q_ref, kp_ref, kc_ref, vp_ref, vc_ref, bias0_ref, bias1_ref, o_ref):
    w = SUB_A + LEFT
    for h in range(HEADS_A_STEP):
        sl = slice(h * HEAD_DIM, (h + 1) * HEAD_DIM)
        k = jnp.concatenate([kp_ref[:, sl], kc_ref[:, sl]], axis=0)
        v = jnp.concatenate([vp_ref[:, sl], vc_ref[:, sl]], axis=0)
        for sub, bias_ref in enumerate((bias0_ref, bias1_ref)):
            rows = slice(sub * SUB_A, (sub + 1) * SUB_A)
            keys = slice(sub * SUB_A, sub * SUB_A + w)
            s = lax.dot_general(q_ref[rows, sl], k[keys], _NT, preferred_element_type=F32) + bias_ref[h]
            m = jnp.max(s, axis=-1, keepdims=True)
            p = jnp.exp(s - m)
            l = jnp.sum(p, axis=-1, keepdims=True)
            o = jnp.dot(p.astype(BF16), v[keys], preferred_element_type=F32) / l
            o_ref[rows, sl] = o.astype(o_ref.dtype)


def _chunk_bias(rel_bias):
    w = SUB_A + LEFT
    heads = rel_bias.shape[0]
    m = jnp.arange(2 * w)
    rel = jnp.where(m < w, LEFT - m, LEFT + 2 * w - m)
    diag = rel_bias[:, jnp.clip(rel, -REL_CLIP, REL_CLIP) + REL_CLIP].astype(F32)[:, None, :]
    n_tables = 1 + TQ_A // SUB_A
    return pl.pallas_call(
        _chunk_bias_kernel,
        grid=(heads,),
        in_specs=[pl.BlockSpec((1, 1, 2 * w), lambda h: (h, 0, 0))],
        out_specs=pl.BlockSpec((n_tables, 1, SUB_A, w), lambda h: (0, h, 0, 0)),
        out_shape=jax.ShapeDtypeStruct((n_tables, heads, SUB_A, w), F32),
        compiler_params=_params(1),
        name="chunk_bias",
    )(diag)


def _chunk_bias_kernel(diag_ref, o_ref):
    w = SUB_A + LEFT
    rolled = pltpu.roll(jnp.broadcast_to(diag_ref[0], (SUB_A, 2 * w)), 0, 1, stride=1, stride_axis=0)
    col = lax.broadcasted_iota(jnp.int32, (SUB_A, w), 1)
    qc = lax.broadcasted_iota(jnp.int32, (SUB_A, w), 0) // CHUNK
    kc = col // CHUNK
    band = (kc >= qc) & (kc <= qc + LEFT_CHUNKS)
    table = jnp.where(band, rolled[:, :w], NEG_INF)
    o_ref[0, 0] = table
    for sub in range(TQ_A // SUB_A):
        o_ref[1 + sub, 0] = jnp.where(col >= LEFT - sub * SUB_A, table, NEG_INF)


def _chunk_attention(proj, bias, b, s):
    nq = s // TQ_A
    width = HEADS_A_STEP * HEAD_DIM
    groups = WIDTH_A // width
    blk = lambda off, prev: pl.BlockSpec(
        (None, TQ_A, width),
        (lambda bi, p, i: (bi, jnp.maximum(i - 1, 0), off + p)) if prev
        else (lambda bi, p, i: (bi, i, off + p)))
    table = lambda sub: pl.BlockSpec((None, HEADS_A_STEP, SUB_A, SUB_A + LEFT),
                                     lambda bi, p, i: (jnp.where(i == 0, 1 + sub, 0), p, 0, 0))
    return pl.pallas_call(
        _chunk_attn_kernel,
        grid=(b, groups, nq),
        in_specs=[blk(0, False),
                  blk(groups, True), blk(groups, False),
                  blk(2 * groups, True), blk(2 * groups, False),
                  table(0), table(1)],
        out_specs=pl.BlockSpec((None, TQ_A, width), lambda bi, p, i: (bi, i, p)),
        out_shape=jax.ShapeDtypeStruct((b, s, WIDTH_A), BF16),
        compiler_params=_params(3),
        name="chunk_attn",
    )(proj, proj, proj, proj, proj, bias, bias)


def _sb_block(q, k_ref, v_ref, sl, j, later, tri, causal):
    t = TB_SB
    start = pl.multiple_of(j * t, t)
    k = k_ref[pl.ds(start, t), sl]
    v = v_ref[pl.ds(start, t), sl]
    z = lax.dot_general(q, k, _NT, preferred_element_type=F32)
    sp = jnp.maximum(z, 0.0) + jnp.log(1.0 + jnp.exp2(jnp.abs(z) * (-LOG2E)))
    if causal is not None:
        sp = jnp.where(causal, sp, 0.0)
    csum = jnp.dot(sp.astype(BF16), tri, preferred_element_type=F32) + later
    wgt = jnp.exp(z - csum)
    if causal is not None:
        wgt = jnp.where(causal, wgt, 0.0)
    pv = jnp.dot(wgt.astype(BF16), v, preferred_element_type=F32)
    return pv, csum[:, 0:1]


def _sb_attn_kernel(q_ref, k_ref, v_ref, o_ref):
    first = pl.program_id(2) * SB_TILES
    t = TB_SB
    row = lax.broadcasted_iota(jnp.int32, (t, t), 0)
    col = lax.broadcasted_iota(jnp.int32, (t, t), 1)
    tri = jnp.where(row >= col, 1.0, 0.0).astype(BF16)
    causal = col < row
    chains = [(u, h) for u in range(SB_TILES) for h in range(2)]
    rows = lambda u: slice(u * t, (u + 1) * t)
    lanes = lambda h: slice(h * HEAD_DIM, (h + 1) * HEAD_DIM)
    qs = [q_ref[rows(u), lanes(h)] for u, h in chains]

    def block(c, j, later, masked):
        return _sb_block(qs[c], k_ref, v_ref, lanes(chains[c][1]), j, later, tri, causal if masked else None)

    carry = []
    for c, (u, h) in enumerate(chains):
        tile = first + u
        pv, later = block(c, tile, jnp.zeros((t, 1), F32), True)
        pv_prev, later_prev = block(c, jnp.maximum(tile - 1, 0), later, False)
        has_prev = tile >= 1
        carry += [jnp.where(has_prev, later_prev, later), pv + jnp.where(has_prev, pv_prev, 0.0)]

    def pending(c):
        todo = False
        for ci, (u, h) in enumerate(chains):
            todo = todo | ((first + u - 2 - c[0] >= 0) & (jnp.min(c[1 + 2 * ci]) < SB_UNDERFLOW))
        return todo

    def step(c):
        out = [c[0] + 1]
        for ci, (u, h) in enumerate(chains):
            j = first + u - 2 - c[0]
            pv, later = block(ci, jnp.maximum(j, 0), c[1 + 2 * ci], False)
            out += [jnp.where(j >= 0, later, c[1 + 2 * ci]), c[2 + 2 * ci] + jnp.where(j >= 0, pv, 0.0)]
        return tuple(out)

    carry = lax.while_loop(pending, step, (jnp.int32(0), *carry))
    for ci, (u, h) in enumerate(chains):
        o_ref[rows(u), lanes(h)] = carry[2 + 2 * ci].astype(o_ref.dtype)


def _sb_attention(proj, b, s):
    tq = SB_TILES * TB_SB
    nq = s // tq
    pairs = WIDTH_B // LANES
    base = 3 * WIDTH_A // LANES
    return pl.pallas_call(
        _sb_attn_kernel,
        grid=(b, pairs, nq),
        in_specs=[pl.BlockSpec((None, tq, LANES), lambda bi, p, i: (bi, i, base + p)),
                  pl.BlockSpec((None, s, LANES), lambda bi, p, i: (bi, 0, base + pairs + p)),
                  pl.BlockSpec((None, s, LANES), lambda bi, p, i: (bi, 0, base + 2 * pairs + p))],
        out_specs=pl.BlockSpec((None, tq, LANES), lambda bi, p, i: (bi, i, p)),
        out_shape=jax.ShapeDtypeStruct((b, s, WIDTH_B), BF16),
        compiler_params=_params(3),
        name="sb_attn",
    )(proj, proj, proj)


def _token_block_kernel(oa_ref, ob_ref, x_ref, ga_ref, gb_ref, wout_ref, ln1g_ref, ln1b_ref, wq_ref,
                        kv_ref, wo_ref, ln2g_ref, ln2b_ref, wr_ref, br_ref,
                        x2_ref, idx_ref, gate_ref, counts_ref):
    def rms(ref, g_ref):
        a = ref[...].astype(F32)
        return (a * lax.rsqrt(jnp.mean(a * a, axis=-1, keepdims=True) + RMS_EPS) * g_ref[...]).astype(BF16)

    y = jnp.dot(rms(oa_ref, ga_ref), wout_ref[:WIDTH_A, :], preferred_element_type=F32)
    y = y + jnp.dot(rms(ob_ref, gb_ref), wout_ref[WIDTH_A:, :], preferred_element_type=F32)
    x1 = _layer_norm(DEEPNORM_ALPHA * x_ref[...] + y, ln1g_ref[...], ln1b_ref[...])
    qm = jnp.dot(x1.astype(BF16), wq_ref[...], preferred_element_type=F32).astype(BF16)

    heads = []
    for h in range(N_HEADS_MEM):
        sl = slice(h * HEAD_DIM_MEM, (h + 1) * HEAD_DIM_MEM)
        q = qm[:, sl] * (HEAD_DIM_MEM ** -0.5)
        k = kv_ref[:, sl]
        v = kv_ref[:, D_MODEL + h * HEAD_DIM_MEM:D_MODEL + (h + 1) * HEAD_DIM_MEM]
        s = lax.dot_general(q, k, _NT, preferred_element_type=F32)
        m = jnp.max(s, axis=-1, keepdims=True)
        p = jnp.exp(s - m)
        l = jnp.sum(p, axis=-1, keepdims=True)
        heads.append((jnp.dot(p.astype(BF16), v, preferred_element_type=F32) / l).astype(BF16))
    o = jnp.concatenate(heads, axis=-1)
    y = jnp.dot(o, wo_ref[...], preferred_element_type=F32)
    x2 = _layer_norm(DEEPNORM_ALPHA * x1 + y, ln2g_ref[...], ln2b_ref[...])
    _store_token_major(x2_ref, x2)

    logits = jnp.dot(x2.astype(BF16), wr_ref[...], preferred_element_type=F32) + br_ref[...]
    tm = logits.shape[0]
    e_iota = lax.broadcasted_iota(jnp.int32, (tm, N_EXPERTS), 1)
    lane = lax.broadcasted_iota(jnp.int32, (tm, LANES), 1)
    idx_out = jnp.zeros((tm, LANES), jnp.int32)
    val_out = jnp.zeros((tm, LANES), F32)
    top = None
    denom = jnp.zeros((tm, 1), F32)
    chosen = jnp.zeros((1, LANES), F32)
    for kk in range(TOP_K):
        m = jnp.max(logits, axis=-1, keepdims=True)
        sel = jnp.min(jnp.where(logits == m, e_iota, N_EXPERTS), axis=-1, keepdims=True)
        if top is None:
            top = m
        e = jnp.exp(m - top)
        denom = denom + e
        idx_out = jnp.where(lane == kk, sel, idx_out)
        val_out = jnp.where(lane == kk, e, val_out)
        chosen = chosen + jnp.sum(jnp.where(lane == sel, 1.0, 0.0), axis=0, keepdims=True)
        logits = jnp.where(e_iota == sel, -jnp.inf, logits)
    idx_ref[...] = idx_out
    gate_ref[...] = val_out / denom

    @pl.when((pl.program_id(0) == 0) & (pl.program_id(1) == 0))
    def _():
        counts_ref[...] = jnp.zeros_like(counts_ref)

    counts_ref[...] += chosen


def _token_block(out_a, out_b, x, g_a, g_b, w_out, ln1_g, ln1_b, w_q, kv, w_o, ln2_g, ln2_b, w_r, b_r, b, s):
    tm = TM_TOK
    nt = s // tm
    mem_len = kv.shape[1]
    n = b * s
    row = lambda width: pl.BlockSpec((tm, width), lambda bi, i: (bi * nt + i, 0))
    full = lambda r, c: pl.BlockSpec((r, c), lambda bi, i: (0, 0))
    return pl.pallas_call(
        _token_block_kernel,
        grid=(b, nt),
        in_specs=[row(WIDTH_A), row(WIDTH_B), row(D_MODEL), full(1, WIDTH_A), full(1, WIDTH_B),
                  full(MIX_WIDTH, D_MODEL), full(1, D_MODEL), full(1, D_MODEL), full(D_MODEL, D_MODEL),
                  pl.BlockSpec((None, mem_len, 2 * D_MODEL), lambda bi, i: (bi, 0, 0)),
                  full(D_MODEL, D_MODEL), full(1, D_MODEL), full(1, D_MODEL),
                  full(D_MODEL, N_EXPERTS), full(1, N_EXPERTS)],
        out_specs=[pl.BlockSpec((tm * LANE_BLOCKS, LANES), lambda bi, i: (bi * nt + i, 0)), row(LANES), row(LANES),
                   full(1, LANES)],
        out_shape=[jax.ShapeDtypeStruct((n * LANE_BLOCKS, LANES), F32),
                   jax.ShapeDtypeStruct((n, LANES), jnp.int32),
                   jax.ShapeDtypeStruct((n, LANES), F32),
                   jax.ShapeDtypeStruct((1, LANES), F32)],
        compiler_params=_params(2),
        name="token_block",
    )(out_a, out_b, x, g_a, g_b, w_out, ln1_g, ln1_b, w_q, kv, w_o, ln2_g, ln2_b, w_r, b_r)


def _gather_rows(idx_ref, first, count, src_hbm, dst_ref, sem):
    for r in range(count):
        src = pl.ds(pl.multiple_of(idx_ref[first + r], LANE_BLOCKS), LANE_BLOCKS)
        pltpu.make_async_copy(src_hbm.at[src], dst_ref.at[pl.ds(r * LANE_BLOCKS, LANE_BLOCKS)], sem).start(
            priority=r % 2)


def _wait_rows(count, src_hbm, dst_ref, sem):
    pltpu.make_async_copy(src_hbm.at[pl.ds(0, count * LANE_BLOCKS)], dst_ref, sem).wait()


def _fill_row_table(dest_ref, gap_ref, tok_ref):
    unroll = 16

    def clear(c, carry):
        for u in range(unroll):
            tok_ref[c * unroll + u] = 0
        return carry

    def place(c, carry):
        rows = [dest_ref[c * unroll + u] for u in range(unroll)]
        for u in range(unroll):
            tok_ref[rows[u]] = (c * (unroll // TOP_K) + u // TOP_K) * LANE_BLOCKS
        return carry

    for g in range(gap_ref.shape[1]):
        lax.fori_loop(gap_ref[0, g] // unroll, gap_ref[1, g] // unroll, clear, 0)
    lax.fori_loop(0, dest_ref.shape[0] // unroll, place, 0)


def _moe_kernel(bexp_ref, nused_ref, next_ref, dest_ref, gap_ref, x_hbm, wgu_hbm, bgu_ref, wd_hbm, bd_ref,
                y_ref, xbuf, wgu_f32, wd_f32, wgu_bf, wd_bf, tok_ref, sem, wsem):
    i = pl.program_id(0)
    slot = i % MOE_SLOTS
    ahead = MOE_SLOTS - 1
    n_used = nused_ref[0]

    def weight_copies(expert):
        return (pltpu.make_async_copy(wgu_hbm.at[expert], wgu_f32, wsem.at[0]),
                pltpu.make_async_copy(wd_hbm.at[expert], wd_f32, wsem.at[1]))

    @pl.when(i == 0)
    def _():
        for copy in weight_copies(bexp_ref[0]):
            copy.start()
        _fill_row_table(dest_ref, gap_ref, tok_ref)
        for b in range(ahead):
            _gather_rows(tok_ref, b * TM_MOE, TM_MOE, x_hbm, xbuf.at[b], sem.at[b])

    @pl.when(i < n_used + ahead)
    def _():
        _wait_rows(TM_MOE, x_hbm, xbuf.at[slot], sem.at[slot])

    @pl.when((i < n_used) & ((i == 0) | (bexp_ref[i] != bexp_ref[jnp.maximum(i - 1, 0)])))
    def _():
        for copy in weight_copies(bexp_ref[i]):
            copy.wait()
        wgu_bf[...] = wgu_f32[...].astype(BF16)
        wd_bf[...] = wd_f32[...].astype(BF16)

        @pl.when(next_ref[i] >= 0)
        def _():
            for copy in weight_copies(next_ref[i]):
                copy.start()

    @pl.when(i < n_used)
    def _():
        x = _load_token_major(xbuf.at[slot], 0, TM_MOE).astype(BF16)
        nxt = (i + ahead) % MOE_SLOTS
        _gather_rows(tok_ref, (i + ahead) * TM_MOE, TM_MOE, x_hbm, xbuf.at[nxt], sem.at[nxt])
        gu = jnp.dot(x, wgu_bf[...], preferred_element_type=F32) + bgu_ref[0]
        gate = jnp.minimum(gu[:, :D_FF], SWIGLU_LIMIT)
        up = jnp.clip(gu[:, D_FF:], -SWIGLU_LIMIT, SWIGLU_LIMIT)
        glu = gate * jax.nn.sigmoid(gate * SWIGLU_ALPHA)
        hid = ((up + 1.0) * glu).astype(BF16)
        _store_token_major(y_ref, jnp.dot(hid, wd_bf[...], preferred_element_type=F32) + bd_ref[0])

    @pl.when(i >= n_used)
    def _():
        y_ref[...] = jnp.zeros_like(y_ref)


def _moe_experts(block_expert, n_used, next_expert, dest, gaps, x2, w_gu, b_gu, w_d, b_d):
    nb = block_expert.shape[0]
    grid_spec = pltpu.PrefetchScalarGridSpec(
        num_scalar_prefetch=5,
        grid=(nb,),
        in_specs=[pl.BlockSpec(memory_space=pl.ANY),
                  pl.BlockSpec(memory_space=pl.ANY),
                  pl.BlockSpec((1, 1, 2 * D_FF), lambda i, be, nu, ne, de, ga: (be[i], 0, 0)),
                  pl.BlockSpec(memory_space=pl.ANY),
                  pl.BlockSpec((1, 1, D_MODEL), lambda i, be, nu, ne, de, ga: (be[i], 0, 0))],
        out_specs=pl.BlockSpec((TM_MOE * LANE_BLOCKS, LANES), lambda i, be, nu, ne, de, ga: (i, 0)),
        scratch_shapes=[pltpu.VMEM((MOE_SLOTS, TM_MOE * LANE_BLOCKS, LANES), F32),
                        pltpu.VMEM((D_MODEL, 2 * D_FF), F32), pltpu.VMEM((D_FF, D_MODEL), F32),
                        pltpu.VMEM((D_MODEL, 2 * D_FF), BF16), pltpu.VMEM((D_FF, D_MODEL), BF16),
                        pltpu.SMEM((nb * TM_MOE,), jnp.int32),
                        pltpu.SemaphoreType.DMA((MOE_SLOTS,)), pltpu.SemaphoreType.DMA((2,))],
    )
    return pl.pallas_call(
        _moe_kernel,
        grid_spec=grid_spec,
        out_shape=jax.ShapeDtypeStruct((nb * TM_MOE * LANE_BLOCKS, LANES), F32),
        compiler_params=pltpu.CompilerParams(dimension_semantics=("arbitrary",),
                                             vmem_limit_bytes=VMEM_LIMIT_MOE),
        name="moe_experts",
    )(block_expert, n_used, next_expert, dest, gaps, x2, w_gu, b_gu, w_d, b_d)


def _combine_kernel(idx_ref, y_hbm, x2_ref, gate_ref, lng_ref, lnb_ref, o_ref, *scratch):
    ybufs, sem = scratch[:COMB_SLOTS], scratch[COMB_SLOTS]
    i = pl.program_id(0)
    nt = pl.num_programs(0)
    ahead = COMB_SLOTS - 1
    rows = TOP_K * TM_COMB

    @pl.when(i == 0)
    def _():
        for t in range(ahead):
            _gather_rows(idx_ref, t * rows, rows, y_hbm, ybufs[t], sem.at[t])

    def compute(ybuf):
        g = gate_ref[...]
        ff = jnp.zeros((TM_COMB, D_MODEL), F32)
        for kk in range(TOP_K):
            ff = ff + g[:, kk:kk + 1] * _load_token_major(ybuf, kk * TM_COMB, TM_COMB)
        x2 = _load_token_major(x2_ref, 0, TM_COMB)
        o_ref[...] = _layer_norm(DEEPNORM_ALPHA * x2 + ff, lng_ref[...], lnb_ref[...])

    for k in range(COMB_SLOTS):
        nxt = (k + ahead) % COMB_SLOTS

        @pl.when((i % COMB_SLOTS == k) & (i + ahead < nt))
        def _():
            _wait_rows(rows, y_hbm, ybufs[k], sem.at[k])
            _gather_rows(idx_ref, (i + ahead) * rows, rows, y_hbm, ybufs[nxt], sem.at[nxt])
            compute(ybufs[k])

        @pl.when((i % COMB_SLOTS == k) & (i + ahead >= nt))
        def _():
            _wait_rows(rows, y_hbm, ybufs[k], sem.at[k])
            compute(ybufs[k])


def _combine(dest, y_rows, x2, gates, ln_g, ln_b):
    n = x2.shape[0] // LANE_BLOCKS
    nt = n // TM_COMB
    rows = TOP_K * TM_COMB
    idx = (dest * LANE_BLOCKS).reshape(nt, TM_COMB, TOP_K).transpose(0, 2, 1).reshape(nt * rows)
    row = lambda width: pl.BlockSpec((TM_COMB, width), lambda i, idx: (i, 0))
    full = lambda r, c: pl.BlockSpec((r, c), lambda i, idx: (0, 0))
    grid_spec = pltpu.PrefetchScalarGridSpec(
        num_scalar_prefetch=1,
        grid=(nt,),
        in_specs=[pl.BlockSpec(memory_space=pl.ANY),
                  pl.BlockSpec((TM_COMB * LANE_BLOCKS, LANES), lambda i, idx: (i, 0)), row(LANES),
                  full(1, D_MODEL), full(1, D_MODEL)],
        out_specs=row(D_MODEL),
        scratch_shapes=[pltpu.VMEM((rows * LANE_BLOCKS, LANES), F32)] * COMB_SLOTS
        + [pltpu.SemaphoreType.DMA((COMB_SLOTS,))],
    )
    return pl.pallas_call(
        _combine_kernel,
        grid_spec=grid_spec,
        out_shape=jax.ShapeDtypeStruct((n, D_MODEL), F32),
        compiler_params=_params(1),
        name="combine",
    )(idx, y_rows, x2, gates, ln_g, ln_b)


def _rank_kernel(idx_ref, counts_ref, dest_ref, running):
    j = pl.program_id(0)
    t = TM_RANK
    idx = idx_ref[...]
    e_iota = lax.broadcasted_iota(jnp.int32, (t, LANES), 1)
    hot = [idx[:, k:k + 1] == e_iota for k in range(TOP_K)]
    chosen = sum(jnp.where(h, 1.0, 0.0) for h in hot)

    @pl.when(j == 0)
    def _():
        counts = counts_ref[...]
        padded = jnp.ceil(counts * (1.0 / TM_MOE)) * TM_MOE
        lane = lax.broadcasted_iota(jnp.int32, (1, LANES), 1)
        scan = padded
        shift = 1
        while shift < N_EXPERTS:
            scan = scan + jnp.where(lane >= shift, pltpu.roll(scan, shift, 1), 0.0)
            shift *= 2
        running[...] = scan - padded

    r_iota = lax.broadcasted_iota(jnp.int32, (t, t), 0)
    c_iota = lax.broadcasted_iota(jnp.int32, (t, t), 1)
    earlier = jnp.where(c_iota < r_iota, 1.0, 0.0).astype(BF16)
    base = jnp.dot(earlier, chosen.astype(BF16), preferred_element_type=F32) + running[...]
    out = jnp.zeros((t, LANES), jnp.int32)
    for k in range(TOP_K):
        row = jnp.sum(jnp.where(hot[k], base, 0.0), axis=-1, keepdims=True).astype(jnp.int32)
        out = jnp.where(e_iota == k, row, out)
    dest_ref[...] = out
    running[...] += jnp.sum(chosen, axis=0, keepdims=True)


def _dispatch_plan(top_idx, counts, n):
    n_assign = n * TOP_K
    nb = n_assign // TM_MOE + N_EXPERTS + MOE_SLOTS - 2
    dest = pl.pallas_call(
        _rank_kernel,
        grid=(n // TM_RANK,),
        in_specs=[pl.BlockSpec((TM_RANK, LANES), lambda j: (j, 0)), pl.BlockSpec((1, LANES), lambda j: (0, 0))],
        out_specs=pl.BlockSpec((TM_RANK, LANES), lambda j: (j, 0)),
        out_shape=jax.ShapeDtypeStruct((n, LANES), jnp.int32),
        scratch_shapes=[pltpu.VMEM((1, LANES), F32)],
        compiler_params=_params(1),
        name="expert_rank",
    )(top_idx, counts)
    dest = dest[:, :TOP_K].reshape(-1)
    counts = counts[0, :N_EXPERTS].astype(jnp.int32)
    padded = (counts + TM_MOE - 1) // TM_MOE * TM_MOE
    pad_end = jnp.cumsum(padded)
    tail = jnp.stack([pad_end[-1], jnp.int32(nb * TM_MOE)])[:, None]
    gaps = jnp.concatenate([jnp.stack([pad_end - padded + counts, pad_end]), tail], axis=1)
    block_first_row = jnp.arange(nb, dtype=jnp.int32) * TM_MOE
    block_expert = jnp.minimum(jnp.sum(pad_end[None, :] <= block_first_row[:, None], axis=1),
                               N_EXPERTS - 1).astype(jnp.int32)
    n_used = (pad_end[-1:] // TM_MOE).astype(jnp.int32)
    block = jnp.arange(nb, dtype=jnp.int32)
    starts = (block < n_used[0]) & (block > 0) & (block_expert != jnp.roll(block_expert, 1))
    later_start = jnp.min(jnp.where(starts[None, :] & (block[None, :] > block[:, None]), block[None, :], nb), axis=1)
    next_expert = jnp.where(later_start < nb, block_expert[jnp.minimum(later_start, nb - 1)], -1).astype(jnp.int32)
    return dest, gaps, block_expert, n_used, next_expert


def kernel(x, mem, w_in, rel_bias, g_group_a, g_group_b, w_out, w_q_mem, w_kv_mem, w_o_mem, w_router, b_router, w_gate_up, b_gate_up, w_down, b_down, ln_g, ln_b):
    b, s, d = x.shape
    n = b * s
    depth = w_in.shape[0]
    xf = x.reshape(n, d)
    for l in range(depth):
        q_scale = jnp.ones((3 * MIX_WIDTH,), F32).at[:WIDTH_A].set(HEAD_DIM ** -0.5)
        q_scale = q_scale.at[3 * WIDTH_A:3 * WIDTH_A + WIDTH_B].set(HEAD_DIM ** -0.5)
        proj = _matmul(xf, (w_in[l] * q_scale).astype(BF16), BF16, TM_TOK, 3 * MIX_WIDTH).reshape(b, s, 3 * MIX_WIDTH)
        out_a = _chunk_attention(proj, _chunk_bias(rel_bias[l]), b, s).reshape(n, WIDTH_A)
        out_b = _sb_attention(proj, b, s).reshape(n, WIDTH_B)
        mem_len = mem.shape[1]
        kv = _matmul(mem.reshape(b * mem_len, d), w_kv_mem[l].astype(BF16), BF16, b * mem_len, 1024)
        x2, top_idx, gates, counts = _token_block(
            out_a, out_b, xf, g_group_a[l][None], g_group_b[l][None], w_out[l].astype(BF16),
            ln_g[l, 0][None], ln_b[l, 0][None], w_q_mem[l].astype(BF16), kv.reshape(b, mem_len, 2 * d),
            w_o_mem[l].astype(BF16), ln_g[l, 1][None], ln_b[l, 1][None],
            w_router[l].astype(BF16), b_router[l][None], b, s)
        dest, gaps, block_expert, n_used, next_expert = _dispatch_plan(top_idx, counts, n)
        y_rows = _moe_experts(block_expert, n_used, next_expert, dest, gaps, x2,
                              w_gate_up[l], b_gate_up[l][:, None, :], w_down[l], b_down[l][:, None, :])
        xf = _combine(dest, y_rows, x2, gates, ln_g[l, 2][None], ln_b[l, 2][None])
    return xf.reshape(b, s, d)
```
